```python
import math, functools
import jax, jax.numpy as jnp
from jax import lax
import numpy as np

D_MODEL = 1024
BATCH = 8
SEQ = 2048
DEPTH = 2
DEC_BATCH = 128
DEC_SEQ = 4
PAST_LEN = 16384
PAGE_SIZE = 128

N_META = 16
CHUNK = 64
HA = 4
DK_A = 128
DV_A = 128
HB = 4
DK_B = 128
DV_B = 128
HC = 4
DK_C = 128
DV_C = 128
HD = 4
DK_D = 64
DV_D = 128
CONV_W = 4
GLA_RANK = 16
GLA_TAU = 16.0
D_FF = 2816
N_EXPERTS = 8
TOP_K = 2
MOE_FF = 1408
ROPE_BASE = 10000.0
LN_EPS = 1e-5
NORM_EPS = 1e-6
ALPHA = (2 * DEPTH) ** 0.25
BETA = (8 * DEPTH) ** -0.25
MIX0 = HA * DV_A + HB * DV_B
MIX1 = HC * DV_C + HD * DV_D
QKV_C = 2 * HC * DK_C + HC * DV_C
SPLITS0 = (HA * DK_A, HA * DK_A, HA * DV_A, HA * DV_A, HB * DK_B, HB * DK_B, HB * DV_B, HB * DV_B, HB, HB)
SPLITS1 = (QKV_C, HC * DV_C, HC, HC, HD * DK_D, HD * DK_D, HD * DV_D, HD * DV_D, GLA_RANK)
P0 = sum(SPLITS0)
P1 = sum(SPLITS1)

kernel_name = 'hybrid_retention_mlstm_gdn_gla_step'


def _split(a, sizes):
    idx = [int(i) for i in np.cumsum(sizes)[:-1]]
    return jnp.split(a, idx, axis=-1)


def _heads(a, h):
    b, t, w = a.shape
    return a.reshape(b, t, h, w // h).transpose(0, 2, 1, 3)


def _merge(a):
    b, h, t, d = a.shape
    return a.transpose(0, 2, 1, 3).reshape(b, t, h * d)


def _layer_norm(x, g, b):
    xf = x.astype(jnp.float32)
    mu = xf.mean(-1, keepdims=True)
    var = jnp.square(xf - mu).mean(-1, keepdims=True)
    return ((xf - mu) * lax.rsqrt(var + LN_EPS) * g.astype(jnp.float32) + b.astype(jnp.float32)).astype(x.dtype)


def _head_norm(o, g, rms=False):
    h, d = o.shape[1], o.shape[3]
    if rms:
        on = o * lax.rsqrt(jnp.square(o).mean(-1, keepdims=True) + NORM_EPS)
    else:
        oc = o - o.mean(-1, keepdims=True)
        on = oc * lax.rsqrt(jnp.square(oc).mean(-1, keepdims=True) + NORM_EPS)
    return on * g.astype(jnp.float32).reshape(h, 1, d)


def _l2norm(a):
    return a * lax.rsqrt(jnp.sum(jnp.square(a), -1, keepdims=True) + NORM_EPS)


def _rotary(a, pos):
    half = a.shape[-1] // 2
    inv = ROPE_BASE ** (-jnp.arange(half, dtype=jnp.float32) / half)
    ang = pos.astype(jnp.float32)[:, None] * inv[None, :]
    cos, sin = jnp.cos(ang), jnp.sin(ang)
    a1, a2 = a[..., :half], a[..., half:]
    return jnp.concatenate([a1 * cos - a2 * sin, a1 * sin + a2 * cos], -1)


def _causal(c, strict=False):
    t = jnp.arange(c)
    return (t[:, None] > t[None, :]) if strict else (t[:, None] >= t[None, :])


def _chunk_size(n):
    return CHUNK if n % CHUNK == 0 else n


def _to_chunks(a, c):
    n = a.shape[2] // c
    return jnp.moveaxis(a.reshape(a.shape[:2] + (n, c) + a.shape[3:]), 2, 0)


def _from_chunks(a):
    a = jnp.moveaxis(a, 0, 2)
    return a.reshape(a.shape[:2] + (a.shape[2] * a.shape[3],) + a.shape[4:])


def _run_segments(step, state, xs, segments):
    outs = []
    for start, stop, c in segments:
        seg = tuple(_to_chunks(a[:, :, start:stop], c) for a in xs)
        state, ys = lax.scan(step, state, seg)
        outs.append(_from_chunks(ys))
    return jnp.concatenate(outs, axis=2), state


def _retention_step(S, inp, log_gamma):
    q, k, v = inp
    c = q.shape[2]
    t = jnp.arange(c, dtype=jnp.float32)
    diff = t[:, None] - t[None, :]
    mask = diff >= 0
    dec = jnp.where(mask, jnp.exp(jnp.where(mask, diff, 0.0) * log_gamma[:, None, None]), 0.0)
    att = jnp.einsum('bhtd,bhsd->bhts', q, k) * dec
    carry_dec = jnp.exp((t[None, :] + 1.0) * log_gamma[:, None])[:, :, None]
    o = jnp.einsum('bhts,bhse->bhte', att, v) + carry_dec * jnp.einsum('bhtd,bhde->bhte', q, S)
    k_dec = k * jnp.exp((c - 1.0 - t[None, :]) * log_gamma[:, None])[:, :, None]
    S = jnp.exp(c * log_gamma)[:, None, None] * S + jnp.einsum('bhsd,bhse->bhde', k_dec, v)
    return S, o


def _mlstm_step(carry, inp):
    C, n, m = carry
    q, k, v, it, lf = inp
    c = q.shape[2]
    b = jnp.cumsum(lf, axis=-1)
    logw = jnp.where(_causal(c), b[..., :, None] - b[..., None, :] + it[..., None, :], -jnp.inf)
    m_t = jnp.maximum(b + m[..., None], logw.max(-1))
    w = jnp.exp(logw - m_t[..., None])
    carry_scale = jnp.exp(b + m[..., None] - m_t)
    qk = jnp.einsum('bhtd,bhsd->bhts', q, k) * w
    num = jnp.einsum('bhts,bhse->bhte', qk, v) + carry_scale[..., None] * jnp.einsum('bhtd,bhde->bhte', q, C)
    den = qk.sum(-1) + carry_scale * jnp.einsum('bhtd,bhd->bht', q, n)
    h = num / jnp.maximum(jnp.abs(den), jnp.exp(-m_t))[..., None]
    m_new = m_t[..., -1]
    w_last = jnp.exp(b[..., -1:] - b + it - m_new[..., None])
    decay = jnp.exp(b[..., -1] + m - m_new)
    C = decay[..., None, None] * C + jnp.einsum('bhs,bhsd,bhse->bhde', w_last, k, v)
    n = decay[..., None] * n + jnp.einsum('bhs,bhsd->bhd', w_last, k)
    return (C, n, m_new), h


def _gdn_step(S, inp):
    q, k, v, beta, g = inp
    c = q.shape[2]
    gc = jnp.cumsum(g, axis=-1)
    G = gc[..., :, None] - gc[..., None, :]
    strict, incl = _causal(c, True), _causal(c)
    dec_strict = jnp.exp(jnp.where(strict, G, -jnp.inf))
    dec_incl = jnp.exp(jnp.where(incl, G, -jnp.inf))
    A = beta[..., :, None] * jnp.einsum('bhtd,bhsd->bhts', k, k) * dec_strict
    rhs = jnp.concatenate([beta[..., None] * v, (beta * jnp.exp(gc))[..., None] * k], -1)
    sol = lax.linalg.triangular_solve(A + jnp.eye(c, dtype=A.dtype), rhs, left_side=True, lower=True, unit_diagonal=True)
    u_t, w = sol[..., :v.shape[-1]], sol[..., v.shape[-1]:]
    u = u_t - jnp.einsum('bhsd,bhde->bhse', w, S)
    qk = jnp.einsum('bhtd,bhsd->bhts', q, k) * dec_incl
    o = jnp.exp(gc)[..., None] * jnp.einsum('bhtd,bhde->bhte', q, S) + jnp.einsum('bhts,bhse->bhte', qk, u)
    gl = gc[..., -1]
    S = jnp.exp(gl)[..., None, None] * S + jnp.einsum('bhsd,bhse->bhde', k * jnp.exp(gl[..., None] - gc)[..., None], u)
    return S, o


def _gla_step(S, inp):
    q, k, v, la = inp
    c = q.shape[2]
    bc = jnp.cumsum(la, axis=2)
    diff = jnp.where(_causal(c)[:, :, None], bc[:, :, :, None, :] - bc[:, :, None, :, :], -jnp.inf)
    att = jnp.einsum('bhtd,bhsd,bhtsd->bhts', q, k, jnp.exp(diff))
    o = jnp.einsum('bhts,bhse->bhte', att, v) + jnp.einsum('bhtd,bhde->bhte', q * jnp.exp(bc), S)
    bl = bc[:, :, -1]
    S = jnp.exp(bl)[..., None] * S + jnp.einsum('bhsd,bhse->bhde', k * jnp.exp(bl[:, :, None, :] - bc), v)
    return S, o


def _swiglu(x, wg, wu, wd):
    return (jax.nn.silu(x @ wg) * (x @ wu)) @ wd


def _moe(x, w_router, b_router, w_gate, w_up, w_down):
    logits = (x @ w_router).astype(jnp.float32) + b_router.astype(jnp.float32)
    probs = jax.nn.softmax(logits, -1)
    top_p, top_i = lax.top_k(probs, TOP_K)
    top_p = top_p / top_p.sum(-1, keepdims=True)
    gates = jnp.sum(jax.nn.one_hot(top_i, N_EXPERTS, dtype=jnp.float32) * top_p[..., None], -2).astype(x.dtype)
    y = jnp.zeros_like(x)
    for e in range(N_EXPERTS):
        h = jax.nn.silu(x @ w_gate[e]) * (x @ w_up[e])
        y = y + gates[..., e:e + 1] * (h @ w_down[e])
    return y


def _mixer_even(x, state, pos, segments, p):
    f32 = jnp.float32
    s_ret, s_c, s_n, s_m = (s.astype(f32) for s in state)
    proj = (x @ p['w_in0']).astype(f32)
    qa, ka, va, ga, qb, kb, vb, ob, ib, fb = _split(proj, SPLITS0)
    qa = _rotary(_heads(qa, HA), pos)
    ka = _rotary(_heads(ka, HA), pos) * DK_A ** -0.5
    va = _heads(va, HA)
    log_gamma = jnp.log(1.0 - 2.0 ** (-5.0 - jnp.arange(HA, dtype=f32)))
    o_a, s_ret = _run_segments(functools.partial(_retention_step, log_gamma=log_gamma), s_ret, (qa, ka, va), segments)
    y_a = _merge(_head_norm(o_a, p['ret_norm_g'])) * jax.nn.silu(ga)
    gb = p['mlstm_gate_bias'].astype(f32)
    qb = _heads(qb, HB)
    kb = _heads(kb, HB) * DK_B ** -0.5
    vb = _heads(vb, HB)
    i_pre = (ib + gb[:HB]).transpose(0, 2, 1)
    log_f = jax.nn.log_sigmoid(fb + gb[HB:]).transpose(0, 2, 1)
    h_b, (s_c, s_n, s_m) = _run_segments(_mlstm_step, (s_c, s_n, s_m), (qb, kb, vb, i_pre, log_f), segments)
    h_b = jax.nn.sigmoid(_heads(ob, HB)) * h_b
    y_b = _merge(_head_norm(h_b, p['mlstm_norm_g']))
    y = jnp.concatenate([y_a, y_b], -1).astype(x.dtype) @ p['w_out0']
    return y, (s_ret, s_c, s_n, s_m)


def _mixer_odd(x, state, segments, p):
    f32 = jnp.float32
    s_gdn, s_conv, s_gla = state
    s_gdn = s_gdn.astype(f32)
    s_gla = s_gla.astype(f32)
    proj = x @ p['w_in1']
    qkv_c, z_c, b_c, a_c, q_d, k_d, v_d, r_d, lr_d = _split(proj, SPLITS1)
    xc = jnp.concatenate([s_conv.astype(proj.dtype), qkv_c], 1)
    new_conv = xc[:, xc.shape[1] - (CONV_W - 1):]
    conv = lax.conv_general_dilated(xc.astype(f32), p['gdn_conv_w'].astype(f32)[:, None, :], window_strides=(1,), padding='VALID', dimension_numbers=('NWC', 'WIO', 'NWC'), feature_group_count=QKV_C)
    qc, kc, vc = _split(jax.nn.silu(conv), (HC * DK_C, HC * DK_C, HC * DV_C))
    qc = _l2norm(_heads(qc, HC)) * DK_C ** -0.5
    kc = _l2norm(_heads(kc, HC))
    vc = _heads(vc, HC)
    beta = jax.nn.sigmoid(b_c.astype(f32)).transpose(0, 2, 1)
    g = (-jnp.exp(p['gdn_a_log'].astype(f32)) * jax.nn.softplus(a_c.astype(f32) + p['gdn_dt_bias'].astype(f32))).transpose(0, 2, 1)
    o_c, s_gdn = _run_segments(_gdn_step, s_gdn, (qc, kc, vc, beta, g), segments)
    y_c = _merge(_head_norm(o_c, p['gdn_norm_g'], rms=True)) * jax.nn.silu(z_c.astype(f32))
    qd = _heads(q_d.astype(f32), HD) * DK_D ** -0.5
    kd = _heads(k_d.astype(f32), HD)
    vd = _heads(v_d.astype(f32), HD)
    log_alpha = jax.nn.log_sigmoid(lr_d.astype(f32) @ p['gla_w_alpha'].astype(f32) + p['gla_b_alpha'].astype(f32)) / GLA_TAU
    o_d, s_gla = _run_segments(_gla_step, s_gla, (qd, kd, vd, _heads(log_alpha, HD)), segments)
    y_d = _merge(_head_norm(o_d, p['gla_norm_g'])) * jax.nn.silu(r_d.astype(f32))
    y = jnp.concatenate([y_c, y_d], -1).astype(x.dtype) @ p['w_out1']
    return y, (s_gdn, new_conv, s_gla)


def _trunk(x, st_even, st_odd, pos, segments, p):
    for layer in range(DEPTH):
        if layer % 2 == 0:
            h, st_even = _mixer_even(x, st_even, pos, segments, p)
            x = _layer_norm(ALPHA * x + h, p['ln0_mix_g'], p['ln0_mix_b'])
            f = _swiglu(x, p['ffn0_w_gate'], p['ffn0_w_up'], p['ffn0_w_down'])
            x = _layer_norm(ALPHA * x + f, p['ln0_ffn_g'], p['ln0_ffn_b'])
        else:
            h, st_odd = _mixer_odd(x, st_odd, segments, p)
            x = _layer_norm(ALPHA * x + h, p['ln1_mix_g'], p['ln1_mix_b'])
            f = _moe(x, p['moe_w_router'], p['moe_b_router'], p['moe_w_gate'], p['moe_w_up'], p['moe_w_down'])
            x = _layer_norm(ALPHA * x + f, p['ln1_ffn_g'], p['ln1_ffn_b'])
    return x, st_even, st_odd


def setup_inputs(seed: int = 0) -> dict:
    key = jax.random.key(seed)
    ks = jax.random.split(key, 64)
    cnt = [0]
    f32 = jnp.float32

    def nk():
        cnt[0] += 1
        return ks[cnt[0] - 1]

    def nrm(shape, scale):
        return jax.random.normal(nk(), shape, f32) * scale

    def gain(n):
        return 1.0 + nrm((n,), 0.02)

    def bias(n):
        return nrm((n,), 0.02)

    dt = jnp.exp(jax.random.uniform(nk(), (HC,), f32, math.log(1e-3), math.log(1e-1)))
    inp = {}
    inp['x_prompt'] = nrm((BATCH, SEQ, D_MODEL), 1.0)
    inp['x_sample'] = nrm((DEC_BATCH, DEC_SEQ, D_MODEL), 1.0)
    inp['state_ret'] = nrm((DEC_BATCH, HA, DK_A, DV_A), 0.5)
    inp['state_mlstm_c'] = nrm((DEC_BATCH, HB, DK_B, DV_B), 0.5)
    inp['state_mlstm_n'] = nrm((DEC_BATCH, HB, DK_B), 0.5)
    inp['state_mlstm_m'] = nrm((DEC_BATCH, HB), 1.0)
    inp['state_gdn'] = nrm((DEC_BATCH, HC, DK_C, DV_C), 0.3)
    inp['state_gdn_conv'] = nrm((DEC_BATCH, CONV_W - 1, QKV_C), 1.0)
    inp['state_gla'] = nrm((DEC_BATCH, HD, DK_D, DV_D), 0.5)
    inp['meta_tokens'] = nrm((N_META, D_MODEL), 1.0)
    inp['w_in0'] = nrm((D_MODEL, P0), D_MODEL ** -0.5)
    inp['ret_norm_g'] = gain(HA * DV_A)
    inp['mlstm_gate_bias'] = jnp.concatenate([nrm((HB,), 0.1), jnp.linspace(3.0, 6.0, HB, dtype=f32) + nrm((HB,), 0.1)])
    inp['mlstm_norm_g'] = gain(HB * DV_B)
    inp['w_out0'] = nrm((MIX0, D_MODEL), MIX0 ** -0.5 * BETA)
    inp['ln0_mix_g'] = gain(D_MODEL)
    inp['ln0_mix_b'] = bias(D_MODEL)
    inp['ffn0_w_gate'] = nrm((D_MODEL, D_FF), D_MODEL ** -0.5)
    inp['ffn0_w_up'] = nrm((D_MODEL, D_FF), D_MODEL ** -0.5)
    inp['ffn0_w_down'] = nrm((D_FF, D_MODEL), D_FF ** -0.5 * BETA)
    inp['ln0_ffn_g'] = gain(D_MODEL)
    inp['ln0_ffn_b'] = bias(D_MODEL)
    inp['w_in1'] = nrm((D_MODEL, P1), D_MODEL ** -0.5)
    inp['gdn_conv_w'] = nrm((CONV_W, QKV_C), CONV_W ** -0.5)
    inp['gdn_a_log'] = jnp.log(jax.random.uniform(nk(), (HC,), f32, 1.0, 16.0))
    inp['gdn_dt_bias'] = dt + jnp.log(-jnp.expm1(-dt))
    inp['gdn_norm_g'] = gain(HC * DV_C)
    inp['gla_w_alpha'] = nrm((GLA_RANK, HD * DK_D), GLA_RANK ** -0.5)
    inp['gla_b_alpha'] = bias(HD * DK_D)
    inp['gla_norm_g'] = gain(HD * DV_D)
    inp['w_out1'] = nrm((MIX1, D_MODEL), MIX1 ** -0.5 * BETA)
    inp['ln1_mix_g'] = gain(D_MODEL)
    inp['ln1_mix_b'] = bias(D_MODEL)
    inp['moe_w_router'] = nrm((D_MODEL, N_EXPERTS), D_MODEL ** -0.5)
    inp['moe_b_router'] = nrm((N_EXPERTS,), 0.01)
    inp['moe_w_gate'] = nrm((N_EXPERTS, D_MODEL, MOE_FF), D_MODEL ** -0.5)
    inp['moe_w_up'] = nrm((N_EXPERTS, D_MODEL, MOE_FF), D_MODEL ** -0.5)
    inp['moe_w_down'] = nrm((N_EXPERTS, MOE_FF, D_MODEL), MOE_FF ** -0.5 * BETA)
    inp['ln1_ffn_g'] = gain(D_MODEL)
    inp['ln1_ffn_b'] = bias(D_MODEL)
    return inp


def reference(x_prompt, x_sample, state_ret, state_mlstm_c, state_mlstm_n, state_mlstm_m, state_gdn, state_gdn_conv, state_gla, meta_tokens, w_in0, ret_norm_g, mlstm_gate_bias, mlstm_norm_g, w_out0, ln0_mix_g, ln0_mix_b, ffn0_w_gate, ffn0_w_up, ffn0_w_down, ln0_ffn_g, ln0_ffn_b, w_in1, gdn_conv_w, gdn_a_log, gdn_dt_bias, gdn_norm_g, gla_w_alpha, gla_b_alpha, gla_norm_g, w_out1, ln1_mix_g, ln1_mix_b, moe_w_router, moe_b_router, moe_w_gate, moe_w_up, moe_w_down, ln1_ffn_g, ln1_ffn_b):
    p = dict(w_in0=w_in0, ret_norm_g=ret_norm_g, mlstm_gate_bias=mlstm_gate_bias, mlstm_norm_g=mlstm_norm_g, w_out0=w_out0, ln0_mix_g=ln0_mix_g, ln0_mix_b=ln0_mix_b, ffn0_w_gate=ffn0_w_gate, ffn0_w_up=ffn0_w_up, ffn0_w_down=ffn0_w_down, ln0_ffn_g=ln0_ffn_g, ln0_ffn_b=ln0_ffn_b, w_in1=w_in1, gdn_conv_w=gdn_conv_w, gdn_a_log=gdn_a_log, gdn_dt_bias=gdn_dt_bias, gdn_norm_g=gdn_norm_g, gla_w_alpha=gla_w_alpha, gla_b_alpha=gla_b_alpha, gla_norm_g=gla_norm_g, w_out1=w_out1, ln1_mix_g=ln1_mix_g, ln1_mix_b=ln1_mix_b, moe_w_router=moe_w_router, moe_b_router=moe_b_router, moe_w_gate=moe_w_gate, moe_w_up=moe_w_up, moe_w_down=moe_w_down, ln1_ffn_g=ln1_ffn_g, ln1_ffn_b=ln1_ffn_b)
    f32 = jnp.float32
    bp, _, d = x_prompt.shape
    meta = jnp.broadcast_to(meta_tokens.astype(x_prompt.dtype)[None], (bp, N_META, d))
    xp = jnp.concatenate([meta, x_prompt], 1)
    tp = xp.shape[1]
    st_even_p = (jnp.zeros((bp, HA, DK_A, DV_A), f32), jnp.zeros((bp, HB, DK_B, DV_B), f32), jnp.zeros((bp, HB, DK_B), f32), jnp.zeros((bp, HB), f32))
    st_odd_p = (jnp.zeros((bp, HC, DK_C, DV_C), f32), jnp.zeros((bp, CONV_W - 1, QKV_C), xp.dtype), jnp.zeros((bp, HD, DK_D, DV_D), f32))
    segs_p = ((0, N_META, N_META), (N_META, tp, _chunk_size(tp - N_META)))
    yp, (p_ret, p_mc, p_mn, p_mm), (p_gdn, p_conv, p_gla) = _trunk(xp, st_even_p, st_odd_p, jnp.arange(tp), segs_p, p)
    y_prompt = yp[:, N_META:]
    ts = x_sample.shape[1]
    segs_s = ((0, ts, _chunk_size(ts)),)
    pos_s = PAST_LEN + jnp.arange(ts)
    y_sample, (s_ret, s_mc, s_mn, s_mm), (s_gdn, s_conv, s_gla) = _trunk(x_sample, (state_ret, state_mlstm_c, state_mlstm_n, state_mlstm_m), (state_gdn, state_gdn_conv, state_gla), pos_s, segs_s, p)
    return (y_prompt, y_sample, p_ret, p_mc, p_mn, p_mm, p_gdn, p_conv, p_gla, s_ret, s_mc, s_mn, s_mm, s_gdn, s_conv, s_gla)
```

```python
import functools
import math

import jax
import jax.numpy as jnp
import numpy as np
from jax import lax
from jax.experimental import pallas as pl
from jax.experimental.pallas import tpu as pltpu

D_MODEL = 1024
BATCH = 8
SEQ = 2048
DEC_BATCH = 128
DEC_SEQ = 4
PAST_LEN = 16384
N_META = 16
CHUNK = 64
NH = 4
DH = 128
DK_D = 64
CONV_W = 4
GLA_RANK = 16
GLA_TAU = 16.0
D_FF = 2816
N_EXPERTS = 8
MOE_FF = 1408
ROPE_BASE = 10000.0
LN_EPS = 1e-5
NORM_EPS = 1e-6
DEPTH = 2
ALPHA = (2 * DEPTH) ** 0.25
QKV_C = 3 * NH * DH

LANES = 128
N_PROMPT = BATCH * SEQ
N_SAMPLE = DEC_BATCH * DEC_SEQ
ROW_SAMPLE = N_PROMPT
ROW_META = N_PROMPT + N_SAMPLE
TM = 512
R_ROWS = ((ROW_META + N_META + TM - 1) // TM) * TM

P0_PAD = 33 * LANES
COL_GATE0 = 32
P1_PAD = 30 * LANES
COL_SMALL1 = 28

VMEM_LIMIT = 56 * 1024 * 1024

_bf16 = jnp.bfloat16
_f32 = jnp.float32
_HI = lax.Precision.HIGHEST


def _dot(a, b):
    return jnp.dot(a.astype(_bf16), b.astype(_bf16), preferred_element_type=_f32)


def _dot_nt(a, b):
    return lax.dot_general(a.astype(_bf16), b.astype(_bf16), (((1,), (1,)), ((), ())), preferred_element_type=_f32)


def _dot_tn(a, b):
    return lax.dot_general(a.astype(_bf16), b.astype(_bf16), (((0,), (0,)), ((), ())), preferred_element_type=_f32)


def _dot_hi(a, b):
    return jnp.dot(a, b, preferred_element_type=_f32, precision=_HI)


def _sigmoid(x):
    return 1.0 / (1.0 + jnp.exp(-x))


def _silu(x):
    return x * _sigmoid(x)


def _softplus(x):
    return jnp.maximum(x, 0.0) + jnp.log1p(jnp.exp(-jnp.abs(x)))


def _group_norm(o, g_row, rms):
    if not rms:
        o = o - jnp.mean(o, axis=-1, keepdims=True)
    return o * lax.rsqrt(jnp.mean(o * o, axis=-1, keepdims=True) + NORM_EPS) * g_row


def _layer_norm(x, g_row, b_row):
    mu = jnp.mean(x, axis=-1, keepdims=True)
    xc = x - mu
    var = jnp.mean(xc * xc, axis=-1, keepdims=True)
    return xc * lax.rsqrt(var + LN_EPS) * g_row + b_row


def _tri(c):
    t = lax.broadcasted_iota(jnp.int32, (c, c), 0)
    s = lax.broadcasted_iota(jnp.int32, (c, c), 1)
    return t, s


def _cumsum_col_row(x_col, x_row, c):
    t, s = _tri(c)
    col = jnp.sum(jnp.where(t >= s, x_row, 0.0), axis=1, keepdims=True)
    row = jnp.sum(jnp.where(t <= s, x_col, 0.0), axis=0, keepdims=True)
    return col, row


def _proj_kernel(x_ref, w_ref, o_ref, xb_ref):
    @pl.when(pl.program_id(1) == 0)
    def _():
        xb_ref[...] = x_ref[...].astype(_bf16)

    o_ref[...] = jnp.dot(xb_ref[...], w_ref[...], preferred_element_type=_f32)


def _proj(x, w, tn):
    rows, k = x.shape
    n = w.shape[1]
    return pl.pallas_call(
        _proj_kernel,
        grid=(rows // TM, n // tn),
        in_specs=[pl.BlockSpec((TM, k), lambda i, j: (i, 0)), pl.BlockSpec((k, tn), lambda i, j: (0, j))],
        out_specs=pl.BlockSpec((TM, tn), lambda i, j: (i, j)),
        out_shape=jax.ShapeDtypeStruct((rows, n), _f32),
        scratch_shapes=[pltpu.VMEM((TM, k), _bf16)],
        compiler_params=pltpu.CompilerParams(dimension_semantics=("parallel", "arbitrary"), vmem_limit_bytes=VMEM_LIMIT),
        name="in_proj",
    )(x, w)


def _out_ln_kernel(y_ref, w_ref, x_ref, g_ref, b_ref, o_ref):
    h = jnp.dot(y_ref[...], w_ref[...], preferred_element_type=_f32)
    o_ref[...] = _layer_norm(ALPHA * x_ref[...] + h, g_ref[...], b_ref[...])


def _out_ln(y, w, x, g, b):
    rows, k = y.shape
    d = w.shape[1]
    return pl.pallas_call(
        _out_ln_kernel,
        grid=(rows // TM,),
        in_specs=[
            pl.BlockSpec((TM, k), lambda i: (i, 0)),
            pl.BlockSpec((k, d), lambda i: (0, 0)),
            pl.BlockSpec((TM, d), lambda i: (i, 0)),
            pl.BlockSpec((1, d), lambda i: (0, 0)),
            pl.BlockSpec((1, d), lambda i: (0, 0)),
        ],
        out_specs=pl.BlockSpec((TM, d), lambda i: (i, 0)),
        out_shape=jax.ShapeDtypeStruct((rows, d), _f32),
        compiler_params=pltpu.CompilerParams(dimension_semantics=("parallel",), vmem_limit_bytes=VMEM_LIMIT),
        name="out_proj_ln",
    )(y, w, x, g, b)


FF_SUB = (0, 512, 1024, 1408)


def _swiglu_partial(xb, wg_ref, wu_ref, wd_ref):
    out = None
    for lo, hi in zip(FF_SUB[:-1], FF_SUB[1:]):
        hg = jnp.dot(xb, wg_ref[:, lo:hi], preferred_element_type=_f32)
        hu = jnp.dot(xb, wu_ref[:, lo:hi], preferred_element_type=_f32)
        a = (_silu(hg) * hu).astype(_bf16)
        part = jnp.dot(a, wd_ref[lo:hi, :], preferred_element_type=_f32)
        out = part if out is None else out + part
    return out


def _ffn_ln_kernel(x_ref, wg_ref, wu_ref, wd_ref, g_ref, b_ref, o_ref, xb_ref, acc_ref):
    f = pl.program_id(1)

    @pl.when(f == 0)
    def _():
        xb_ref[...] = x_ref[...].astype(_bf16)
        acc_ref[...] = jnp.zeros_like(acc_ref)

    acc_ref[...] += _swiglu_partial(xb_ref[...], wg_ref, wu_ref, wd_ref)

    @pl.when(f == pl.num_programs(1) - 1)
    def _():
        o_ref[...] = _layer_norm(ALPHA * x_ref[...] + acc_ref[...], g_ref[...], b_ref[...])


def _ffn_ln(x, wg, wu, wd, g, b):
    rows, d = x.shape
    ff = wg.shape[1]
    tf = MOE_FF
    return pl.pallas_call(
        _ffn_ln_kernel,
        grid=(rows // TM, ff // tf),
        in_specs=[
            pl.BlockSpec((TM, d), lambda i, f: (i, 0)),
            pl.BlockSpec((d, tf), lambda i, f: (0, f)),
            pl.BlockSpec((d, tf), lambda i, f: (0, f)),
            pl.BlockSpec((tf, d), lambda i, f: (f, 0)),
            pl.BlockSpec((1, d), lambda i, f: (0, 0)),
            pl.BlockSpec((1, d), lambda i, f: (0, 0)),
        ],
        out_specs=pl.BlockSpec((TM, d), lambda i, f: (i, 0)),
        out_shape=jax.ShapeDtypeStruct((rows, d), _f32),
        scratch_shapes=[pltpu.VMEM((TM, d), _bf16), pltpu.VMEM((TM, d), _f32)],
        compiler_params=pltpu.CompilerParams(dimension_semantics=("parallel", "arbitrary"), vmem_limit_bytes=VMEM_LIMIT),
        name="ffn_ln",
    )(x, wg, wu, wd, g, b)


def _moe_ln_kernel(x_ref, wr_ref, br_ref, wg_ref, wu_ref, wd_ref, g_ref, b_ref, o_ref, xb_ref, acc_ref, gate_ref):
    e = pl.program_id(1)

    @pl.when(e == 0)
    def _():
        x = x_ref[...]
        xb_ref[...] = x.astype(_bf16)
        acc_ref[...] = jnp.zeros_like(acc_ref)
        logits = _dot_hi(x, wr_ref[...]) + br_ref[...]
        lane = lax.broadcasted_iota(jnp.int32, logits.shape, 1)
        ex = jnp.exp(logits - jnp.max(logits, axis=-1, keepdims=True))
        probs = ex / jnp.sum(ex, axis=-1, keepdims=True)
        p1 = jnp.max(probs, axis=-1, keepdims=True)
        i1 = jnp.min(jnp.where(probs == p1, lane, LANES), axis=-1, keepdims=True)
        rest = jnp.where(lane == i1, -1.0, probs)
        p2 = jnp.max(rest, axis=-1, keepdims=True)
        i2 = jnp.min(jnp.where(rest == p2, lane, LANES), axis=-1, keepdims=True)
        tot = p1 + p2
        gate_ref[...] = jnp.where(lane == i1, p1 / tot, 0.0) + jnp.where(lane == i2, p2 / tot, 0.0)

    gates = gate_ref[...]
    lane = lax.broadcasted_iota(jnp.int32, gates.shape, 1)
    gate_col = jnp.sum(jnp.where(lane == e, gates, 0.0), axis=-1, keepdims=True)
    acc_ref[...] += gate_col * _swiglu_partial(xb_ref[...], wg_ref, wu_ref, wd_ref)

    @pl.when(e == pl.num_programs(1) - 1)
    def _():
        o_ref[...] = _layer_norm(ALPHA * x_ref[...] + acc_ref[...], g_ref[...], b_ref[...])


def _moe_ln(x, wr, br, wg, wu, wd, g, b):
    rows, d = x.shape
    ne, _, ff = wg.shape
    return pl.pallas_call(
        _moe_ln_kernel,
        grid=(rows // TM, ne),
        in_specs=[
            pl.BlockSpec((TM, d), lambda i, e: (i, 0)),
            pl.BlockSpec((d, LANES), lambda i, e: (0, 0)),
            pl.BlockSpec((1, LANES), lambda i, e: (0, 0)),
            pl.BlockSpec((None, d, ff), lambda i, e: (e, 0, 0)),
            pl.BlockSpec((None, d, ff), lambda i, e: (e, 0, 0)),
            pl.BlockSpec((None, ff, d), lambda i, e: (e, 0, 0)),
            pl.BlockSpec((1, d), lambda i, e: (0, 0)),
            pl.BlockSpec((1, d), lambda i, e: (0, 0)),
        ],
        out_specs=pl.BlockSpec((TM, d), lambda i, e: (i, 0)),
        out_shape=jax.ShapeDtypeStruct((rows, d), _f32),
        scratch_shapes=[pltpu.VMEM((TM, d), _bf16), pltpu.VMEM((TM, d), _f32), pltpu.VMEM((TM, LANES), _f32)],
        compiler_params=pltpu.CompilerParams(dimension_semantics=("parallel", "arbitrary"), vmem_limit_bytes=VMEM_LIMIT),
        name="moe_ln",
    )(x, wr, br, wg, wu, wd, g, b)


def _retention_chunk(q, k, v, s_prev, log_gamma, c):
    t, s = _tri(c)
    diff = (t - s).astype(_f32)
    dec = jnp.where(t >= s, jnp.exp(jnp.where(t >= s, diff, 0.0) * log_gamma), 0.0)
    tcol = lax.broadcasted_iota(jnp.int32, (c, 1), 0).astype(_f32)
    att = _dot_nt(q, k) * dec
    o = _dot(att, v) + jnp.exp((tcol + 1.0) * log_gamma) * _dot(q, s_prev)
    k_dec = k * jnp.exp((c - 1.0 - tcol) * log_gamma)
    s_new = math.exp(c * log_gamma) * s_prev + _dot_tn(k_dec, v)
    return o, s_new


def _mlstm_chunk(q, k, v, it_col, it_row, lf_col, lf_row, c_prev, n_prev, m_prev, c):
    t, s = _tri(c)
    b_col, b_row = _cumsum_col_row(lf_col, lf_row, c)
    logw = jnp.where(t >= s, b_col - b_row + it_row, -jnp.inf)
    m_t = jnp.maximum(b_col + m_prev, jnp.max(logw, axis=1, keepdims=True))
    w = jnp.exp(logw - m_t)
    carry = jnp.exp(b_col + m_prev - m_t)
    qk = _dot_nt(q, k) * w
    num = _dot(qk, v) + carry * _dot(q, c_prev)
    den = jnp.sum(qk, axis=1, keepdims=True) + carry * jnp.sum(q * n_prev, axis=1, keepdims=True)
    h = num / jnp.maximum(jnp.abs(den), jnp.exp(-m_t))
    m_new = m_t[c - 1 : c, :]
    b_last = b_col[c - 1 : c, :]
    w_last = jnp.exp(b_last - b_col + it_col - m_new)
    decay = jnp.exp(b_last + m_prev - m_new)
    kw = k * w_last
    c_new = decay * c_prev + _dot_tn(kw, v)
    n_new = decay * n_prev + jnp.sum(kw, axis=0, keepdims=True)
    return h, c_new, n_new, m_new


def _mixer0_kernel(
    qa_ref, ka_ref, va_ref, ga_ref, qb_ref, kb_ref, vb_ref, ob_ref, gate_ref, cos_ref, sin_ref,
    gb_ref, ng_ref, s0_ref, c0_ref, n0_ref, m0_ref, ybuf_ref,
    y_ref, s_ref, c_ref, n_ref, m_ref, *, c, bb,
):
    del ybuf_ref

    @pl.when(pl.program_id(1) == 0)
    def _():
        s_ref[...] = jnp.broadcast_to(s0_ref[...], s_ref.shape)
        c_ref[...] = jnp.broadcast_to(c0_ref[...], c_ref.shape)
        n_ref[...] = jnp.broadcast_to(n0_ref[...], n_ref.shape)
        m_ref[...] = jnp.broadcast_to(m0_ref[...], m_ref.shape)

    cosf = cos_ref[...]
    sinf = sin_ref[...]
    gates = gate_ref[...] + gb_ref[...]
    lane = lax.broadcasted_iota(jnp.int32, gates.shape, 1)
    gates = jnp.where(lane < NH, gates, -_softplus(-gates))
    gates_t = jnp.transpose(gates)

    for j in range(bb):
        r0, r1 = j * c, (j + 1) * c
        for h in range(NH):
            l0, l1 = h * DH, (h + 1) * DH
            q = qa_ref[r0:r1, l0:l1]
            k = ka_ref[r0:r1, l0:l1]
            q = q * cosf[r0:r1] + pltpu.roll(q, DH // 2, axis=1) * sinf[r0:r1]
            k = (k * cosf[r0:r1] + pltpu.roll(k, DH // 2, axis=1) * sinf[r0:r1]) * DH**-0.5
            log_gamma = math.log(1.0 - 2.0 ** (-5.0 - h))
            o, s_new = _retention_chunk(q, k, va_ref[r0:r1, l0:l1], s_ref[j, h], log_gamma, c)
            s_ref[j, h] = s_new
            y_a = _group_norm(o, ng_ref[0:1, l0:l1], rms=False) * _silu(ga_ref[r0:r1, l0:l1])
            y_ref[r0:r1, l0:l1] = y_a.astype(y_ref.dtype)
            h_b, c_new, n_new, m_new = _mlstm_chunk(
                qb_ref[r0:r1, l0:l1], kb_ref[r0:r1, l0:l1] * DH**-0.5, vb_ref[r0:r1, l0:l1],
                gates[r0:r1, h : h + 1], gates_t[h : h + 1, r0:r1],
                gates[r0:r1, NH + h : NH + h + 1], gates_t[NH + h : NH + h + 1, r0:r1],
                c_ref[j, h], n_ref[j, h : h + 1, :], m_ref[j, h : h + 1, 0:1], c,
            )
            c_ref[j, h] = c_new
            n_ref[j, h : h + 1, :] = n_new
            m_ref[j, h : h + 1, :] = jnp.broadcast_to(m_new, (1, LANES))
            h_b = _sigmoid(ob_ref[r0:r1, l0:l1]) * h_b
            y_b = _group_norm(h_b, ng_ref[1:2, l0:l1], rms=False)
            y_ref[r0:r1, NH * DH + l0 : NH * DH + l1] = y_b.astype(y_ref.dtype)


def _mixer0(proj, ybuf, cosf, sinf, gb_row, norm_g, init, *, row_off, nb, nchunk, c, bb, bcast_init):
    rows = bb * c
    blk0 = row_off // rows
    w4 = NH * DH

    def row_map(col):
        return lambda i, ci: (blk0 + i * nchunk + ci, col)

    def st_map(*zeros):
        return (lambda i, ci: (0,) + zeros) if bcast_init else (lambda i, ci: (i,) + zeros)

    ib = 1 if bcast_init else bb
    tab_map = (lambda i, ci: (ci, 0)) if nchunk > 1 else (lambda i, ci: (0, 0))
    in_specs = [pl.BlockSpec((rows, w4), row_map(col)) for col in range(8)]
    in_specs += [
        pl.BlockSpec((rows, LANES), row_map(COL_GATE0)),
        pl.BlockSpec((rows, DH), tab_map),
        pl.BlockSpec((rows, DH), tab_map),
        pl.BlockSpec((1, LANES), lambda i, ci: (0, 0)),
        pl.BlockSpec((2, w4), lambda i, ci: (0, 0)),
        pl.BlockSpec((ib, NH, DH, DH), st_map(0, 0, 0)),
        pl.BlockSpec((ib, NH, DH, DH), st_map(0, 0, 0)),
        pl.BlockSpec((ib, NH, DH), st_map(0, 0)),
        pl.BlockSpec((ib, NH, LANES), st_map(0, 0)),
        pl.BlockSpec(memory_space=pl.ANY),
    ]
    out_specs = [
        pl.BlockSpec((rows, 2 * w4), row_map(0)),
        pl.BlockSpec((bb, NH, DH, DH), lambda i, ci: (i, 0, 0, 0)),
        pl.BlockSpec((bb, NH, DH, DH), lambda i, ci: (i, 0, 0, 0)),
        pl.BlockSpec((bb, NH, DH), lambda i, ci: (i, 0, 0)),
        pl.BlockSpec((bb, NH, LANES), lambda i, ci: (i, 0, 0)),
    ]
    out_shape = [
        jax.ShapeDtypeStruct(ybuf.shape, ybuf.dtype),
        jax.ShapeDtypeStruct((nb, NH, DH, DH), _f32),
        jax.ShapeDtypeStruct((nb, NH, DH, DH), _f32),
        jax.ShapeDtypeStruct((nb, NH, DH), _f32),
        jax.ShapeDtypeStruct((nb, NH, LANES), _f32),
    ]
    args = [proj] * 9 + [cosf, sinf, gb_row, norm_g, *init, ybuf]
    return pl.pallas_call(
        functools.partial(_mixer0_kernel, c=c, bb=bb),
        grid=(nb // bb, nchunk),
        in_specs=in_specs,
        out_specs=out_specs,
        out_shape=out_shape,
        input_output_aliases={len(args) - 1: 0},
        compiler_params=pltpu.CompilerParams(dimension_semantics=("parallel", "arbitrary"), vmem_limit_bytes=VMEM_LIMIT),
        name=f"mixer0_c{c}",
    )(*args)


def _unit_lower_inverse(a, c):
    t, s = _tri(c)
    eye = jnp.where(t == s, 1.0, 0.0)
    inv = eye - a
    p = a
    span = 2
    while span < c:
        p = _dot_hi(p, p)
        inv = inv + _dot_hi(inv, p)
        span *= 2
    return inv


def _gdn_chunk(q, k, v, beta_col, g_col, g_row, s_prev, c):
    t, s = _tri(c)
    gc_col, gc_row = _cumsum_col_row(g_col, g_row, c)
    gdiff = gc_col - gc_row
    dec_strict = jnp.exp(jnp.where(t > s, gdiff, -jnp.inf))
    dec_incl = jnp.exp(jnp.where(t >= s, gdiff, -jnp.inf))
    e_col = jnp.exp(gc_col)
    a = beta_col * _dot_nt(k, k) * dec_strict
    rhs = jnp.concatenate([beta_col * v, (beta_col * e_col) * k], axis=-1)
    sol = _dot_hi(_unit_lower_inverse(a, c), rhs)
    u = sol[:, :DH] - _dot(sol[:, DH:], s_prev)
    qk = _dot_nt(q, k) * dec_incl
    o = e_col * _dot(q, s_prev) + _dot(qk, u)
    gl = gc_col[c - 1 : c, :]
    s_new = jnp.exp(gl) * s_prev + _dot_tn(k * jnp.exp(gl - gc_col), u)
    return o, s_new


def _gla_chunk(q, k, v, la, s_prev, c):
    t, s = _tri(c)
    bc = _dot_hi(jnp.where(t >= s, 1.0, 0.0), la)
    qe = q * jnp.exp(bc)
    att = jnp.where(t >= s, _dot_nt(qe, k * jnp.exp(-bc)), 0.0)
    o = _dot(att, v) + _dot(qe, s_prev)
    bl = bc[c - 1 : c, :]
    ti, si = _tri(DK_D)
    el_col = jnp.sum(jnp.where(ti == si, jnp.exp(bl), 0.0), axis=1, keepdims=True)
    s_new = el_col * s_prev + _dot_tn(k * jnp.exp(bl - bc), v)
    return o, s_new


def _mixer1_kernel(
    qkv_ref, z_ref, qkd_ref, vd_ref, rd_ref, small_ref, cw_ref, gp_ref, wa_ref, ba_ref, ng_ref,
    s0_ref, cv0_ref, d0_ref, ybuf_ref,
    y_ref, s_ref, cv_ref, d_ref, *, c, bb,
):
    del ybuf_ref

    @pl.when(pl.program_id(1) == 0)
    def _():
        s_ref[...] = jnp.broadcast_to(s0_ref[...], s_ref.shape)
        cv_ref[...] = jnp.broadcast_to(cv0_ref[...], cv_ref.shape)
        d_ref[...] = jnp.broadcast_to(d0_ref[...], d_ref.shape)

    w4 = NH * DH
    small = small_ref[...]
    beta_all = _sigmoid(small)
    g_all = -jnp.exp(gp_ref[0:1, :]) * _softplus(small + gp_ref[1:2, :])
    g_all_t = jnp.transpose(g_all)
    lr = small[:, 0:GLA_RANK]
    log_alpha = -_softplus(-(_dot_hi(lr, wa_ref[...]) + ba_ref[...])) * (1.0 / GLA_TAU)
    cw = cw_ref[...]

    for j in range(bb):
        r0, r1 = j * c, (j + 1) * c
        ext = jnp.concatenate([cv_ref[j], qkv_ref[r0:r1, :]], axis=0)
        conv = cw[3:4] * ext[8 : 8 + c] + cw[2:3] * ext[7 : 7 + c] + cw[1:2] * ext[6 : 6 + c] + cw[0:1] * ext[5 : 5 + c]
        cv_ref[j] = ext[c : c + 8]
        act = _silu(conv)
        for h in range(NH):
            l0, l1 = h * DH, (h + 1) * DH
            qc = act[:, l0:l1]
            kc = act[:, w4 + l0 : w4 + l1]
            qc = qc * lax.rsqrt(jnp.sum(qc * qc, axis=-1, keepdims=True) + NORM_EPS) * DH**-0.5
            kc = kc * lax.rsqrt(jnp.sum(kc * kc, axis=-1, keepdims=True) + NORM_EPS)
            o, s_new = _gdn_chunk(
                qc, kc, act[:, 2 * w4 + l0 : 2 * w4 + l1],
                beta_all[r0:r1, GLA_RANK + h : GLA_RANK + h + 1],
                g_all[r0:r1, GLA_RANK + NH + h : GLA_RANK + NH + h + 1],
                g_all_t[GLA_RANK + NH + h : GLA_RANK + NH + h + 1, r0:r1],
                s_ref[j, h], c,
            )
            s_ref[j, h] = s_new
            y_c = _group_norm(o, ng_ref[0:1, l0:l1], rms=True) * _silu(z_ref[r0:r1, l0:l1])
            y_ref[r0:r1, l0:l1] = y_c.astype(y_ref.dtype)
            k0, k1 = h * DK_D, (h + 1) * DK_D
            o, d_new = _gla_chunk(
                qkd_ref[r0:r1, k0:k1] * DK_D**-0.5, qkd_ref[r0:r1, NH * DK_D + k0 : NH * DK_D + k1],
                vd_ref[r0:r1, l0:l1], log_alpha[r0:r1, k0:k1], d_ref[j, h], c,
            )
            d_ref[j, h] = d_new
            y_d = _group_norm(o, ng_ref[1:2, l0:l1], rms=False) * _silu(rd_ref[r0:r1, l0:l1])
            y_ref[r0:r1, w4 + l0 : w4 + l1] = y_d.astype(y_ref.dtype)


def _mixer1(proj, ybuf, conv_w, gdn_par, w_alpha, b_alpha, norm_g, init, *, row_off, nb, nchunk, c, bb, bcast_init):
    rows = bb * c
    blk0 = row_off // rows
    w4 = NH * DH

    def row_map(col):
        return lambda i, ci: (blk0 + i * nchunk + ci, col)

    def st_map(*zeros):
        return (lambda i, ci: (0,) + zeros) if bcast_init else (lambda i, ci: (i,) + zeros)

    def const2(i, ci):
        return (0, 0)

    ib = 1 if bcast_init else bb
    in_specs = [
        pl.BlockSpec((rows, QKV_C), row_map(0)),
        pl.BlockSpec((rows, w4), row_map(3)),
        pl.BlockSpec((rows, w4), row_map(4)),
        pl.BlockSpec((rows, w4), row_map(5)),
        pl.BlockSpec((rows, w4), row_map(6)),
        pl.BlockSpec((rows, LANES), row_map(COL_SMALL1)),
        pl.BlockSpec((CONV_W, QKV_C), const2),
        pl.BlockSpec((2, LANES), const2),
        pl.BlockSpec((GLA_RANK, NH * DK_D), const2),
        pl.BlockSpec((1, NH * DK_D), const2),
        pl.BlockSpec((2, w4), const2),
        pl.BlockSpec((ib, NH, DH, DH), st_map(0, 0, 0)),
        pl.BlockSpec((ib, 8, QKV_C), st_map(0, 0)),
        pl.BlockSpec((ib, NH, DK_D, DH), st_map(0, 0, 0)),
        pl.BlockSpec(memory_space=pl.ANY),
    ]
    out_specs = [
        pl.BlockSpec((rows, 2 * w4), row_map(0)),
        pl.BlockSpec((bb, NH, DH, DH), lambda i, ci: (i, 0, 0, 0)),
        pl.BlockSpec((bb, 8, QKV_C), lambda i, ci: (i, 0, 0)),
        pl.BlockSpec((bb, NH, DK_D, DH), lambda i, ci: (i, 0, 0, 0)),
    ]
    out_shape = [
        jax.ShapeDtypeStruct(ybuf.shape, ybuf.dtype),
        jax.ShapeDtypeStruct((nb, NH, DH, DH), _f32),
        jax.ShapeDtypeStruct((nb, 8, QKV_C), _f32),
        jax.ShapeDtypeStruct((nb, NH, DK_D, DH), _f32),
    ]
    args = [proj] * 6 + [conv_w, gdn_par, w_alpha, b_alpha, norm_g, *init, ybuf]
    return pl.pallas_call(
        functools.partial(_mixer1_kernel, c=c, bb=bb),
        grid=(nb // bb, nchunk),
        in_specs=in_specs,
        out_specs=out_specs,
        out_shape=out_shape,
        input_output_aliases={len(args) - 1: 0},
        compiler_params=pltpu.CompilerParams(dimension_semantics=("parallel", "arbitrary"), vmem_limit_bytes=VMEM_LIMIT),
        name=f"mixer1_c{c}",
    )(*args)


_GROUP_META = dict(row_off=ROW_META, nb=1, nchunk=1, c=N_META, bb=1)
_GROUP_PROMPT = dict(row_off=0, nb=BATCH, nchunk=SEQ // CHUNK, c=CHUNK, bb=1)
_GROUP_SAMPLE = dict(row_off=ROW_SAMPLE, nb=DEC_BATCH, nchunk=1, c=DEC_SEQ, bb=8)


def _rotary_tables(pos, reps):
    half = DH // 2
    inv = ROPE_BASE ** (-jnp.arange(half, dtype=_f32) / half)
    ang = pos.astype(_f32)[:, None] * inv[None, :]
    cos, sin = jnp.cos(ang), jnp.sin(ang)
    cosf = jnp.concatenate([cos, cos], -1)
    sinf = jnp.concatenate([-sin, sin], -1)
    return jnp.tile(cosf, (reps, 1)), jnp.tile(sinf, (reps, 1))


def _lanes(m):
    return jnp.broadcast_to(m.astype(_f32)[..., None], m.shape + (LANES,))


def _conv_rows(s):
    return jnp.pad(s.astype(_f32), ((0, 0), (8 - (CONV_W - 1), 0), (0, 0)))


def kernel(x_prompt, x_sample, state_ret, state_mlstm_c, state_mlstm_n, state_mlstm_m, state_gdn, state_gdn_conv, state_gla, meta_tokens, w_in0, ret_norm_g, mlstm_gate_bias, mlstm_norm_g, w_out0, ln0_mix_g, ln0_mix_b, ffn0_w_gate, ffn0_w_up, ffn0_w_down, ln0_ffn_g, ln0_ffn_b, w_in1, gdn_conv_w, gdn_a_log, gdn_dt_bias, gdn_norm_g, gla_w_alpha, gla_b_alpha, gla_norm_g, w_out1, ln1_mix_g, ln1_mix_b, moe_w_router, moe_b_router, moe_w_gate, moe_w_up, moe_w_down, ln1_ffn_g, ln1_ffn_b):
    w4 = NH * DH
    x = jnp.concatenate(
        [
            x_prompt.reshape(N_PROMPT, D_MODEL),
            x_sample.reshape(N_SAMPLE, D_MODEL),
            meta_tokens.astype(x_prompt.dtype),
            jnp.zeros((R_ROWS - ROW_META - N_META, D_MODEL), x_prompt.dtype),
        ],
        0,
    )

    w_in0_p = jnp.pad(w_in0, ((0, 0), (0, P0_PAD - w_in0.shape[1]))).astype(_bf16)
    c_qkv, c_z, c_b, c_a = QKV_C, QKV_C + w4, QKV_C + w4 + NH, QKV_C + w4 + 2 * NH
    c_qd = c_a
    c_lr = c_qd + 2 * NH * DK_D + 2 * w4
    w_in1_p = jnp.concatenate(
        [
            w_in1[:, :c_z],
            w_in1[:, c_qd:c_lr],
            w_in1[:, c_lr:],
            w_in1[:, c_z:c_a],
        ],
        1,
    )
    w_in1_p = jnp.pad(w_in1_p, ((0, 0), (0, P1_PAD - w_in1_p.shape[1]))).astype(_bf16)
    del c_qkv, c_b
    gb_row = jnp.pad(mlstm_gate_bias.astype(_f32), (0, LANES - 2 * NH))[None]
    norm0 = jnp.stack([ret_norm_g, mlstm_norm_g]).astype(_f32)
    norm1 = jnp.stack([gdn_norm_g, gla_norm_g]).astype(_f32)
    lo = GLA_RANK + NH
    gdn_par = jnp.stack(
        [
            jnp.pad(gdn_a_log.astype(_f32), (lo, LANES - lo - NH)),
            jnp.pad(gdn_dt_bias.astype(_f32), (lo, LANES - lo - NH)),
        ]
    )
    w_router = jnp.pad(moe_w_router.astype(_f32), ((0, 0), (0, LANES - N_EXPERTS)))
    b_router = jnp.pad(moe_b_router.astype(_f32), (0, LANES - N_EXPERTS), constant_values=-jnp.inf)[None]

    def row(v):
        return v.astype(_f32)[None]

    proj0 = _proj(x, w_in0_p, tn=P0_PAD // 3)
    ybuf = jnp.zeros((R_ROWS, 2 * w4), _bf16)
    zeros_even = (
        jnp.zeros((1, NH, DH, DH), _f32), jnp.zeros((1, NH, DH, DH), _f32),
        jnp.zeros((1, NH, DH), _f32), jnp.zeros((1, NH, LANES), _f32),
    )
    cos_m, sin_m = _rotary_tables(jnp.arange(N_META), 1)
    cos_p, sin_p = _rotary_tables(N_META + jnp.arange(SEQ), 1)
    cos_s, sin_s = _rotary_tables(PAST_LEN + jnp.arange(DEC_SEQ), _GROUP_SAMPLE["bb"])
    ybuf, *meta_even = _mixer0(proj0, ybuf, cos_m, sin_m, gb_row, norm0, zeros_even, bcast_init=True, **_GROUP_META)
    ybuf, p_ret, p_mc, p_mn, p_mm = _mixer0(proj0, ybuf, cos_p, sin_p, gb_row, norm0, meta_even, bcast_init=True, **_GROUP_PROMPT)
    init_s = (state_ret.astype(_f32), state_mlstm_c.astype(_f32), state_mlstm_n.astype(_f32), _lanes(state_mlstm_m))
    ybuf, s_ret, s_mc, s_mn, s_mm = _mixer0(proj0, ybuf, cos_s, sin_s, gb_row, norm0, init_s, bcast_init=False, **_GROUP_SAMPLE)
    x = _out_ln(ybuf, w_out0.astype(_bf16), x, row(ln0_mix_g), row(ln0_mix_b))
    x = _ffn_ln(x, ffn0_w_gate.astype(_bf16), ffn0_w_up.astype(_bf16), ffn0_w_down.astype(_bf16), row(ln0_ffn_g), row(ln0_ffn_b))

    proj1 = _proj(x, w_in1_p, tn=P1_PAD // 3)
    ybuf = jnp.zeros((R_ROWS, 2 * w4), _bf16)
    zeros_odd = (jnp.zeros((1, NH, DH, DH), _f32), jnp.zeros((1, 8, QKV_C), _f32), jnp.zeros((1, NH, DK_D, DH), _f32))
    m1_par = (gdn_conv_w.astype(_f32), gdn_par, gla_w_alpha.astype(_f32), row(gla_b_alpha), norm1)
    ybuf, *meta_odd = _mixer1(proj1, ybuf, *m1_par, zeros_odd, bcast_init=True, **_GROUP_META)
    ybuf, p_gdn, p_conv, p_gla = _mixer1(proj1, ybuf, *m1_par, meta_odd, bcast_init=True, **_GROUP_PROMPT)
    init_s = (state_gdn.astype(_f32), _conv_rows(state_gdn_conv), state_gla.astype(_f32))
    ybuf, s_gdn, s_conv, s_gla = _mixer1(proj1, ybuf, *m1_par, init_s, bcast_init=False, **_GROUP_SAMPLE)
    x = _out_ln(ybuf, w_out1.astype(_bf16), x, row(ln1_mix_g), row(ln1_mix_b))
    x = _moe_ln(x, w_router, b_router, moe_w_gate.astype(_bf16), moe_w_up.astype(_bf16), moe_w_down.astype(_bf16), row(ln1_ffn_g), row(ln1_ffn_b))

    y_prompt = x[:N_PROMPT].reshape(BATCH, SEQ, D_MODEL)
    y_sample = x[ROW_SAMPLE:ROW_META].reshape(DEC_BATCH, DEC_SEQ, D_MODEL)
    tail = 8 - (CONV_W - 1)
    return (
        y_prompt, y_sample,
        p_ret, p_mc, p_mn, p_mm[..., 0], p_gdn, p_conv[:, tail:], p_gla,
        s_ret, s_mc, s_mn, s_mm[..., 0], s_gdn, s_conv[:, tail:], s_gla,
    )
```

```python
import functools
import math

import jax
import jax.numpy as jnp
import numpy as np
from jax import lax
from jax.experimental import pallas as pl
from jax.experimental.pallas import tpu as pltpu

D_MODEL = 1024
BATCH = 8
SEQ = 2048
DEC_BATCH = 128
DEC_SEQ = 4
PAST_LEN = 16384
N_META = 16
CHUNK = 64
NH = 4
DH = 128
DK_D = 64
CONV_W = 4
GLA_RANK = 16
GLA_TAU = 16.0
D_FF = 2816
N_EXPERTS = 8
MOE_FF = 1408
ROPE_BASE = 10000.0
LN_EPS = 1e-5
NORM_EPS = 1e-6
DEPTH = 2
ALPHA = (2 * DEPTH) ** 0.25
QKV_C = 3 * NH * DH

LANES = 128
N_PROMPT = BATCH * SEQ
N_SAMPLE = DEC_BATCH * DEC_SEQ
ROW_SAMPLE = N_PROMPT
ROW_META = N_PROMPT + N_SAMPLE
TM = 512
R_ROWS = ((ROW_META + N_META + TM - 1) // TM) * TM

P0_PAD = 33 * LANES
COL_GATE0 = 32
P1_PAD = 30 * LANES
COL_SMALL1 = 28
SOLVE_BLOCK = 16

VMEM_LIMIT = 56 * 1024 * 1024

_bf16 = jnp.bfloat16
_f32 = jnp.float32
_HI = lax.Precision.HIGHEST


def _dot_hi(a, b):
    return jnp.dot(a, b, preferred_element_type=_f32, precision=_HI)


def _bmm(a, b):
    return jnp.einsum("nmk,nkp->nmp", a.astype(_bf16), b.astype(_bf16), preferred_element_type=_f32)


def _bmm_nt(a, b):
    return jnp.einsum("nmk,npk->nmp", a.astype(_bf16), b.astype(_bf16), preferred_element_type=_f32)


def _bmm_tn(a, b):
    return jnp.einsum("nkm,nkp->nmp", a.astype(_bf16), b.astype(_bf16), preferred_element_type=_f32)


def _split2(a):
    hi = a.astype(_bf16)
    lo = (a - hi.astype(_f32)).astype(_bf16)
    return hi, lo


def _bmm_x3(a, b):
    ah, al = _split2(a)
    bh, bl = _split2(b)
    mm = functools.partial(jnp.einsum, "nmk,nkp->nmp", preferred_element_type=_f32)
    return mm(ah, bh) + (mm(ah, bl) + mm(al, bh))


def _sigmoid(x):
    return 1.0 / (1.0 + jnp.exp(-x))


def _silu(x):
    return x * _sigmoid(x)


def _softplus(x):
    return jnp.maximum(x, 0.0) + jnp.log1p(jnp.exp(-jnp.abs(x)))


def _group_norm(o, g_row, rms):
    if not rms:
        o = o - jnp.mean(o, axis=-1, keepdims=True)
    return o * lax.rsqrt(jnp.mean(o * o, axis=-1, keepdims=True) + NORM_EPS) * g_row


def _layer_norm(x, g_row, b_row):
    mu = jnp.mean(x, axis=-1, keepdims=True)
    xc = x - mu
    var = jnp.mean(xc * xc, axis=-1, keepdims=True)
    return xc * lax.rsqrt(var + LN_EPS) * g_row + b_row


def _tri3(c):
    t = lax.broadcasted_iota(jnp.int32, (1, c, c), 1)
    s = lax.broadcasted_iota(jnp.int32, (1, c, c), 2)
    return t, s


def _cumsum_col_row(x_col, x_row, c):
    t, s = _tri3(c)
    col = jnp.sum(jnp.where(t >= s, x_row, 0.0), axis=2, keepdims=True)
    row = jnp.sum(jnp.where(t <= s, x_col, 0.0), axis=1, keepdims=True)
    return col, row


def _proj_kernel(x_ref, w_ref, o_ref, xb_ref):
    @pl.when(pl.program_id(1) == 0)
    def _():
        xb_ref[...] = x_ref[...].astype(_bf16)

    o_ref[...] = jnp.dot(xb_ref[...], w_ref[...], preferred_element_type=_f32)


def _proj(x, w, tn):
    rows, k = x.shape
    n = w.shape[1]
    return pl.pallas_call(
        _proj_kernel,
        grid=(rows // TM, n // tn),
        in_specs=[pl.BlockSpec((TM, k), lambda i, j: (i, 0)), pl.BlockSpec((k, tn), lambda i, j: (0, j))],
        out_specs=pl.BlockSpec((TM, tn), lambda i, j: (i, j)),
        out_shape=jax.ShapeDtypeStruct((rows, n), _f32),
        scratch_shapes=[pltpu.VMEM((TM, k), _bf16)],
        compiler_params=pltpu.CompilerParams(dimension_semantics=("parallel", "arbitrary"), vmem_limit_bytes=VMEM_LIMIT),
        name="in_proj",
    )(x, w)


def _out_ln_kernel(y_ref, w_ref, x_ref, g_ref, b_ref, o_ref):
    h = jnp.dot(y_ref[...], w_ref[...], preferred_element_type=_f32)
    o_ref[...] = _layer_norm(ALPHA * x_ref[...] + h, g_ref[...], b_ref[...])


def _out_ln(y, w, x, g, b):
    rows, k = y.shape
    d = w.shape[1]
    return pl.pallas_call(
        _out_ln_kernel,
        grid=(rows // TM,),
        in_specs=[
            pl.BlockSpec((TM, k), lambda i: (i, 0)),
            pl.BlockSpec((k, d), lambda i: (0, 0)),
            pl.BlockSpec((TM, d), lambda i: (i, 0)),
            pl.BlockSpec((1, d), lambda i: (0, 0)),
            pl.BlockSpec((1, d), lambda i: (0, 0)),
        ],
        out_specs=pl.BlockSpec((TM, d), lambda i: (i, 0)),
        out_shape=jax.ShapeDtypeStruct((rows, d), _f32),
        compiler_params=pltpu.CompilerParams(dimension_semantics=("parallel",), vmem_limit_bytes=VMEM_LIMIT),
        name="out_proj_ln",
    )(y, w, x, g, b)


FF_SUB = (0, 512, 1024, 1408)


def _swiglu_partial(xb, wg_ref, wu_ref, wd_ref):
    out = None
    for lo, hi in zip(FF_SUB[:-1], FF_SUB[1:]):
        hg = jnp.dot(xb, wg_ref[:, lo:hi], preferred_element_type=_f32)
        hu = jnp.dot(xb, wu_ref[:, lo:hi], preferred_element_type=_f32)
        a = (_silu(hg) * hu).astype(_bf16)
        part = jnp.dot(a, wd_ref[lo:hi, :], preferred_element_type=_f32)
        out = part if out is None else out + part
    return out


def _ffn_ln_kernel(x_ref, wg_ref, wu_ref, wd_ref, g_ref, b_ref, o_ref, xb_ref, acc_ref):
    f = pl.program_id(1)

    @pl.when(f == 0)
    def _():
        xb_ref[...] = x_ref[...].astype(_bf16)
        acc_ref[...] = jnp.zeros_like(acc_ref)

    acc_ref[...] += _swiglu_partial(xb_ref[...], wg_ref, wu_ref, wd_ref)

    @pl.when(f == pl.num_programs(1) - 1)
    def _():
        o_ref[...] = _layer_norm(ALPHA * x_ref[...] + acc_ref[...], g_ref[...], b_ref[...])


def _ffn_ln(x, wg, wu, wd, g, b):
    rows, d = x.shape
    ff = wg.shape[1]
    tf = MOE_FF
    return pl.pallas_call(
        _ffn_ln_kernel,
        grid=(rows // TM, ff // tf),
        in_specs=[
            pl.BlockSpec((TM, d), lambda i, f: (i, 0)),
            pl.BlockSpec((d, tf), lambda i, f: (0, f)),
            pl.BlockSpec((d, tf), lambda i, f: (0, f)),
            pl.BlockSpec((tf, d), lambda i, f: (f, 0)),
            pl.BlockSpec((1, d), lambda i, f: (0, 0)),
            pl.BlockSpec((1, d), lambda i, f: (0, 0)),
        ],
        out_specs=pl.BlockSpec((TM, d), lambda i, f: (i, 0)),
        out_shape=jax.ShapeDtypeStruct((rows, d), _f32),
        scratch_shapes=[pltpu.VMEM((TM, d), _bf16), pltpu.VMEM((TM, d), _f32)],
        compiler_params=pltpu.CompilerParams(dimension_semantics=("parallel", "arbitrary"), vmem_limit_bytes=VMEM_LIMIT),
        name="ffn_ln",
    )(x, wg, wu, wd, g, b)


def _moe_ln_kernel(x_ref, wr_ref, br_ref, wg_ref, wu_ref, wd_ref, g_ref, b_ref, o_ref, xb_ref, acc_ref, gate_ref):
    e = pl.program_id(1)

    @pl.when(e == 0)
    def _():
        x = x_ref[...]
        xb_ref[...] = x.astype(_bf16)
        acc_ref[...] = jnp.zeros_like(acc_ref)
        logits = _dot_hi(x, wr_ref[...]) + br_ref[...]
        lane = lax.broadcasted_iota(jnp.int32, logits.shape, 1)
        ex = jnp.exp(logits - jnp.max(logits, axis=-1, keepdims=True))
        probs = ex / jnp.sum(ex, axis=-1, keepdims=True)
        p1 = jnp.max(probs, axis=-1, keepdims=True)
        i1 = jnp.min(jnp.where(probs == p1, lane, LANES), axis=-1, keepdims=True)
        rest = jnp.where(lane == i1, -1.0, probs)
        p2 = jnp.max(rest, axis=-1, keepdims=True)
        i2 = jnp.min(jnp.where(rest == p2, lane, LANES), axis=-1, keepdims=True)
        tot = p1 + p2
        gate_ref[...] = jnp.where(lane == i1, p1 / tot, 0.0) + jnp.where(lane == i2, p2 / tot, 0.0)

    gates = gate_ref[...]
    lane = lax.broadcasted_iota(jnp.int32, gates.shape, 1)
    gate_col = jnp.sum(jnp.where(lane == e, gates, 0.0), axis=-1, keepdims=True)
    acc_ref[...] += gate_col * _swiglu_partial(xb_ref[...], wg_ref, wu_ref, wd_ref)

    @pl.when(e == pl.num_programs(1) - 1)
    def _():
        o_ref[...] = _layer_norm(ALPHA * x_ref[...] + acc_ref[...], g_ref[...], b_ref[...])


def _moe_ln(x, wr, br, wg, wu, wd, g, b):
    rows, d = x.shape
    ne, _, ff = wg.shape
    return pl.pallas_call(
        _moe_ln_kernel,
        grid=(rows // TM, ne),
        in_specs=[
            pl.BlockSpec((TM, d), lambda i, e: (i, 0)),
            pl.BlockSpec((d, LANES), lambda i, e: (0, 0)),
            pl.BlockSpec((1, LANES), lambda i, e: (0, 0)),
            pl.BlockSpec((None, d, ff), lambda i, e: (e, 0, 0)),
            pl.BlockSpec((None, d, ff), lambda i, e: (e, 0, 0)),
            pl.BlockSpec((None, ff, d), lambda i, e: (e, 0, 0)),
            pl.BlockSpec((1, d), lambda i, e: (0, 0)),
            pl.BlockSpec((1, d), lambda i, e: (0, 0)),
        ],
        out_specs=pl.BlockSpec((TM, d), lambda i, e: (i, 0)),
        out_shape=jax.ShapeDtypeStruct((rows, d), _f32),
        scratch_shapes=[pltpu.VMEM((TM, d), _bf16), pltpu.VMEM((TM, d), _f32), pltpu.VMEM((TM, LANES), _f32)],
        compiler_params=pltpu.CompilerParams(dimension_semantics=("parallel", "arbitrary"), vmem_limit_bytes=VMEM_LIMIT),
        name="moe_ln",
    )(x, wr, br, wg, wu, wd, g, b)


def _chains(ref, c, bb, width, off=0):
    return jnp.stack([ref[j * c : (j + 1) * c, off + h * width : off + (h + 1) * width] for j in range(bb) for h in range(NH)])


def _chain_cols(a, c, bb, lane0):
    return jnp.stack([a[j * c : (j + 1) * c, lane0 + h : lane0 + h + 1] for j in range(bb) for h in range(NH)])


def _chain_rows(a_t, c, bb, lane0):
    return jnp.stack([a_t[lane0 + h : lane0 + h + 1, j * c : (j + 1) * c] for j in range(bb) for h in range(NH)])


def _head_rows(ref, row, bb):
    return jnp.stack([ref[row : row + 1, h * DH : (h + 1) * DH] for _ in range(bb) for h in range(NH)])


def _store_chains(y_ref, y, c, bb, off):
    for j in range(bb):
        for h in range(NH):
            y_ref[j * c : (j + 1) * c, off + h * DH : off + (h + 1) * DH] = y[j * NH + h].astype(y_ref.dtype)


def _retention_chunk(q, k, v, s_prev, dec, vec):
    att = _bmm_nt(q, k) * dec
    o = _bmm(att, v) + vec[:, :, 0:1] * _bmm(q, s_prev)
    s_new = vec[:, 0:1, 2:3] * s_prev + _bmm_tn(k * vec[:, :, 1:2], v)
    return o, s_new


def _mlstm_chunk(q, k, v, it_col, it_row, lf_col, lf_row, c_prev, n_prev, m_prev, c):
    t, s = _tri3(c)
    b_col, b_row = _cumsum_col_row(lf_col, lf_row, c)
    logw = jnp.where(t >= s, b_col - b_row + it_row, -jnp.inf)
    m_t = jnp.maximum(b_col + m_prev, jnp.max(logw, axis=2, keepdims=True))
    w = jnp.exp(logw - m_t)
    carry = jnp.exp(b_col + m_prev - m_t)
    qk = _bmm_nt(q, k) * w
    num = _bmm(qk, v) + carry * _bmm(q, c_prev)
    den = jnp.sum(qk, axis=2, keepdims=True) + carry * jnp.sum(q * n_prev, axis=2, keepdims=True)
    h = num / jnp.maximum(jnp.abs(den), jnp.exp(-m_t))
    m_new = m_t[:, c - 1 : c, :]
    b_last = b_col[:, c - 1 : c, :]
    w_last = jnp.exp(b_last - b_col + it_col - m_new)
    decay = jnp.exp(b_last + m_prev - m_new)
    kw = k * w_last
    c_new = decay * c_prev + _bmm_tn(kw, v)
    n_new = decay * n_prev + jnp.sum(kw, axis=1, keepdims=True)
    return h, c_new, n_new, m_new


def _mixer0_kernel(
    qa_ref, ka_ref, va_ref, ga_ref, qb_ref, kb_ref, vb_ref, ob_ref, gate_ref, cos_ref, sin_ref,
    dec_ref, vec_ref, gb_ref, ng_ref, s0_ref, c0_ref, n0_ref, m0_ref, ybuf_ref,
    y_ref, s_ref, c_ref, n_ref, m_ref, *, c, bb,
):
    del ybuf_ref
    n = bb * NH

    @pl.when(pl.program_id(1) == 0)
    def _():
        s_ref[...] = jnp.broadcast_to(s0_ref[...], s_ref.shape)
        c_ref[...] = jnp.broadcast_to(c0_ref[...], c_ref.shape)
        n_ref[...] = jnp.broadcast_to(n0_ref[...], n_ref.shape)
        m_ref[...] = jnp.broadcast_to(m0_ref[...], m_ref.shape)

    cosf = cos_ref[...][None]
    sinf = sin_ref[...][None]
    gates = gate_ref[...] + gb_ref[...]
    lane = lax.broadcasted_iota(jnp.int32, gates.shape, 1)
    gates = jnp.where(lane < NH, gates, -_softplus(-gates))
    gates_t = jnp.transpose(gates)

    q = _chains(qa_ref, c, bb, DH)
    k = _chains(ka_ref, c, bb, DH)
    q = q * cosf + pltpu.roll(q, DH // 2, axis=2) * sinf
    k = (k * cosf + pltpu.roll(k, DH // 2, axis=2) * sinf) * DH**-0.5
    dec = jnp.concatenate([dec_ref[...]] * bb, axis=0)
    vec = jnp.concatenate([vec_ref[...]] * bb, axis=0)
    o, s_new = _retention_chunk(q, k, _chains(va_ref, c, bb, DH), s_ref[...].reshape(n, DH, DH), dec, vec)
    s_ref[...] = s_new.reshape(s_ref.shape)
    y_a = _group_norm(o, _head_rows(ng_ref, 0, bb), rms=False) * _silu(_chains(ga_ref, c, bb, DH))
    _store_chains(y_ref, y_a, c, bb, 0)

    n_prev = jnp.stack([n_ref[j, h : h + 1, :] for j in range(bb) for h in range(NH)])
    m_prev = jnp.stack([m_ref[j, h : h + 1, 0:1] for j in range(bb) for h in range(NH)])
    h_b, c_new, n_new, m_new = _mlstm_chunk(
        _chains(qb_ref, c, bb, DH), _chains(kb_ref, c, bb, DH) * DH**-0.5, _chains(vb_ref, c, bb, DH),
        _chain_cols(gates, c, bb, 0), _chain_rows(gates_t, c, bb, 0),
        _chain_cols(gates, c, bb, NH), _chain_rows(gates_t, c, bb, NH),
        c_ref[...].reshape(n, DH, DH), n_prev, m_prev, c,
    )
    c_ref[...] = c_new.reshape(c_ref.shape)
    for j in range(bb):
        for h in range(NH):
            n_ref[j, h : h + 1, :] = n_new[j * NH + h]
            m_ref[j, h : h + 1, :] = jnp.broadcast_to(m_new[j * NH + h], (1, LANES))
    h_b = _sigmoid(_chains(ob_ref, c, bb, DH)) * h_b
    y_b = _group_norm(h_b, _head_rows(ng_ref, 1, bb), rms=False)
    _store_chains(y_ref, y_b, c, bb, NH * DH)


def _retention_tables(c):
    lg = np.log(1.0 - 2.0 ** (-5.0 - np.arange(NH, dtype=np.float64)))[:, None, None]
    t = np.arange(c, dtype=np.float64)
    diff = t[None, :, None] - t[None, None, :]
    dec = np.where(diff >= 0, np.exp(np.maximum(diff, 0.0) * lg), 0.0)
    vec = np.zeros((NH, c, LANES))
    vec[:, :, 0] = np.exp((t[None, :] + 1.0) * lg[:, 0])
    vec[:, :, 1] = np.exp((c - 1.0 - t[None, :]) * lg[:, 0])
    vec[:, :, 2] = np.exp(c * lg[:, 0])
    return jnp.asarray(dec, _f32), jnp.asarray(vec, _f32)


def _group_maps(row_off, nb, nchunk, c, bb, bcast_init):
    rows = bb * c
    blk0 = row_off // rows
    nbb = nb // bb

    def row_map(col):
        return lambda i, ci: (blk0 + ci * nbb + i, col)

    def st_map(*zeros):
        return (lambda i, ci: (0,) + zeros) if bcast_init else (lambda i, ci: (i,) + zeros)

    return rows, row_map, st_map, (1 if bcast_init else bb)


def _mixer0(proj, ybuf, cosf, sinf, gb_row, norm_g, init, *, row_off, nb, nchunk, c, bb, bcast_init):
    rows, row_map, st_map, ib = _group_maps(row_off, nb, nchunk, c, bb, bcast_init)
    w4 = NH * DH
    dec, vec = _retention_tables(c)

    def const(nd):
        return lambda i, ci: (0,) * nd

    in_specs = [pl.BlockSpec((rows, w4), row_map(col)) for col in range(8)]
    in_specs += [
        pl.BlockSpec((rows, LANES), row_map(COL_GATE0)),
        pl.BlockSpec((c, DH), lambda i, ci: (ci, 0)),
        pl.BlockSpec((c, DH), lambda i, ci: (ci, 0)),
        pl.BlockSpec((NH, c, c), const(3)),
        pl.BlockSpec((NH, c, LANES), const(3)),
        pl.BlockSpec((1, LANES), const(2)),
        pl.BlockSpec((2, w4), const(2)),
        pl.BlockSpec((ib, NH, DH, DH), st_map(0, 0, 0)),
        pl.BlockSpec((ib, NH, DH, DH), st_map(0, 0, 0)),
        pl.BlockSpec((ib, NH, DH), st_map(0, 0)),
        pl.BlockSpec((ib, NH, LANES), st_map(0, 0)),
        pl.BlockSpec(memory_space=pl.ANY),
    ]
    out_specs = [
        pl.BlockSpec((rows, 2 * w4), row_map(0)),
        pl.BlockSpec((bb, NH, DH, DH), lambda i, ci: (i, 0, 0, 0)),
        pl.BlockSpec((bb, NH, DH, DH), lambda i, ci: (i, 0, 0, 0)),
        pl.BlockSpec((bb, NH, DH), lambda i, ci: (i, 0, 0)),
        pl.BlockSpec((bb, NH, LANES), lambda i, ci: (i, 0, 0)),
    ]
    out_shape = [
        jax.ShapeDtypeStruct(ybuf.shape, ybuf.dtype),
        jax.ShapeDtypeStruct((nb, NH, DH, DH), _f32),
        jax.ShapeDtypeStruct((nb, NH, DH, DH), _f32),
        jax.ShapeDtypeStruct((nb, NH, DH), _f32),
        jax.ShapeDtypeStruct((nb, NH, LANES), _f32),
    ]
    args = [proj] * 9 + [cosf, sinf, dec, vec, gb_row, norm_g, *init, ybuf]
    return pl.pallas_call(
        functools.partial(_mixer0_kernel, c=c, bb=bb),
        grid=(nb // bb, nchunk),
        in_specs=in_specs,
        out_specs=out_specs,
        out_shape=out_shape,
        input_output_aliases={len(args) - 1: 0},
        compiler_params=pltpu.CompilerParams(dimension_semantics=("parallel", "arbitrary"), vmem_limit_bytes=VMEM_LIMIT),
        name=f"mixer0_c{c}",
    )(*args)


def _unit_lower_solve(a, rhs, c):
    bs = min(SOLVE_BLOCK, c)
    t, s = _tri3(c)
    if c > bs:
        shift = bs.bit_length() - 1
        same = jnp.right_shift(t, shift) == jnp.right_shift(s, shift)
        d = jnp.where(same, a, 0.0)
    else:
        d = a
    inv = jnp.where(t == s, 1.0, 0.0) - d
    p = d
    span = 2
    while span < bs:
        p = _bmm_x3(p, p)
        inv = inv + _bmm_x3(inv, p)
        span *= 2
    y = _bmm(inv, rhs)
    if c == bs:
        return y
    b = _bmm(inv, jnp.where(same, 0.0, a))
    y = y - _bmm(b, y)
    p = b
    span = 2
    while span < c // bs:
        p = _bmm(p, p)
        y = y + _bmm(p, y)
        span *= 2
    return y


def _gdn_chunk(q, k, v, beta_col, g_col, g_row, s_prev, c):
    t, s = _tri3(c)
    gc_col, gc_row = _cumsum_col_row(g_col, g_row, c)
    dec_incl = jnp.exp(jnp.where(t >= s, gc_col - gc_row, -jnp.inf))
    dec_strict = jnp.where(t > s, dec_incl, 0.0)
    e_col = jnp.exp(gc_col)
    a = beta_col * _bmm_nt(k, k) * dec_strict
    rhs = jnp.concatenate([beta_col * v, (beta_col * e_col) * k], axis=-1)
    sol = _unit_lower_solve(a, rhs, c)
    u = sol[:, :, :DH] - _bmm(sol[:, :, DH:], s_prev)
    qk = _bmm_nt(q, k) * dec_incl
    o = e_col * _bmm(q, s_prev) + _bmm(qk, u)
    gl = gc_col[:, c - 1 : c, :]
    s_new = jnp.exp(gl) * s_prev + _bmm_tn(k * jnp.exp(gl - gc_col), u)
    return o, s_new


def _gla_chunk(q, k, v, bc, s_prev, c):
    t, s = _tri3(c)
    qe = q * jnp.exp(bc)
    att = jnp.where(t >= s, _bmm_nt(qe, k * jnp.exp(-bc)), 0.0)
    o = _bmm(att, v) + _bmm(qe, s_prev)
    bl = bc[:, c - 1 : c, :]
    ti, si = _tri3(DK_D)
    el_col = jnp.sum(jnp.where(ti == si, jnp.exp(bl), 0.0), axis=2, keepdims=True)
    s_new = el_col * s_prev + _bmm_tn(k * jnp.exp(bl - bc), v)
    return o, s_new


def _mixer1_kernel(
    qkv_ref, z_ref, qkd_ref, vd_ref, rd_ref, small_ref, cw_ref, gp_ref, wa_ref, ba_ref, ng_ref,
    s0_ref, cv0_ref, d0_ref, ybuf_ref,
    y_ref, s_ref, cv_ref, d_ref, *, c, bb,
):
    del ybuf_ref
    n = bb * NH
    w4 = NH * DH

    @pl.when(pl.program_id(1) == 0)
    def _():
        s_ref[...] = jnp.broadcast_to(s0_ref[...], s_ref.shape)
        cv_ref[...] = jnp.broadcast_to(cv0_ref[...], cv_ref.shape)
        d_ref[...] = jnp.broadcast_to(d0_ref[...], d_ref.shape)

    small = small_ref[...]
    beta_all = _sigmoid(small)
    g_all = -jnp.exp(gp_ref[0:1, :]) * _softplus(small + gp_ref[1:2, :])
    g_all_t = jnp.transpose(g_all)
    log_alpha = -_softplus(-(_dot_hi(small[:, 0:GLA_RANK], wa_ref[...]) + ba_ref[...])) * (1.0 / GLA_TAU)
    cw = cw_ref[...]

    acts = []
    for j in range(bb):
        ext = jnp.concatenate([cv_ref[j], qkv_ref[j * c : (j + 1) * c, :]], axis=0)
        conv = cw[3:4] * ext[8 : 8 + c] + cw[2:3] * ext[7 : 7 + c] + cw[1:2] * ext[6 : 6 + c] + cw[0:1] * ext[5 : 5 + c]
        cv_ref[j] = ext[c : c + 8]
        acts.append(_silu(conv))

    def act_chains(off):
        return jnp.stack([acts[j][:, off + h * DH : off + (h + 1) * DH] for j in range(bb) for h in range(NH)])

    qc = act_chains(0)
    kc = act_chains(w4)
    qc = qc * lax.rsqrt(jnp.sum(qc * qc, axis=-1, keepdims=True) + NORM_EPS) * DH**-0.5
    kc = kc * lax.rsqrt(jnp.sum(kc * kc, axis=-1, keepdims=True) + NORM_EPS)
    o, s_new = _gdn_chunk(
        qc, kc, act_chains(2 * w4),
        _chain_cols(beta_all, c, bb, GLA_RANK), _chain_cols(g_all, c, bb, GLA_RANK + NH),
        _chain_rows(g_all_t, c, bb, GLA_RANK + NH), s_ref[...].reshape(n, DH, DH), c,
    )
    s_ref[...] = s_new.reshape(s_ref.shape)
    y_c = _group_norm(o, _head_rows(ng_ref, 0, bb), rms=True) * _silu(_chains(z_ref, c, bb, DH))
    _store_chains(y_ref, y_c, c, bb, 0)

    tt = lax.broadcasted_iota(jnp.int32, (c, c), 0)
    ss = lax.broadcasted_iota(jnp.int32, (c, c), 1)
    ones_lt = jnp.where(tt >= ss, 1.0, 0.0).astype(_bf16)
    bcs = []
    for j in range(bb):
        la = log_alpha[j * c : (j + 1) * c, :]
        hi = la.astype(_bf16)
        r1 = la - hi.astype(_f32)
        mid = r1.astype(_bf16)
        lo = (r1 - mid.astype(_f32)).astype(_bf16)
        cum = functools.partial(jnp.dot, ones_lt, preferred_element_type=_f32)
        bcs.append(cum(hi) + (cum(mid) + cum(lo)))
    bc = jnp.stack([bcs[j][:, h * DK_D : (h + 1) * DK_D] for j in range(bb) for h in range(NH)])
    o, d_new = _gla_chunk(
        _chains(qkd_ref, c, bb, DK_D) * DK_D**-0.5, _chains(qkd_ref, c, bb, DK_D, off=NH * DK_D),
        _chains(vd_ref, c, bb, DH), bc, d_ref[...].reshape(n, DK_D, DH), c,
    )
    d_ref[...] = d_new.reshape(d_ref.shape)
    y_d = _group_norm(o, _head_rows(ng_ref, 1, bb), rms=False) * _silu(_chains(rd_ref, c, bb, DH))
    _store_chains(y_ref, y_d, c, bb, w4)


def _mixer1(proj, ybuf, conv_w, gdn_par, w_alpha, b_alpha, norm_g, init, *, row_off, nb, nchunk, c, bb, bcast_init):
    rows, row_map, st_map, ib = _group_maps(row_off, nb, nchunk, c, bb, bcast_init)
    w4 = NH * DH

    def const2(i, ci):
        return (0, 0)

    in_specs = [
        pl.BlockSpec((rows, QKV_C), row_map(0)),
        pl.BlockSpec((rows, w4), row_map(3)),
        pl.BlockSpec((rows, w4), row_map(4)),
        pl.BlockSpec((rows, w4), row_map(5)),
        pl.BlockSpec((rows, w4), row_map(6)),
        pl.BlockSpec((rows, LANES), row_map(COL_SMALL1)),
        pl.BlockSpec((CONV_W, QKV_C), const2),
        pl.BlockSpec((2, LANES), const2),
        pl.BlockSpec((GLA_RANK, NH * DK_D), const2),
        pl.BlockSpec((1, NH * DK_D), const2),
        pl.BlockSpec((2, w4), const2),
        pl.BlockSpec((ib, NH, DH, DH), st_map(0, 0, 0)),
        pl.BlockSpec((ib, 8, QKV_C), st_map(0, 0)),
        pl.BlockSpec((ib, NH, DK_D, DH), st_map(0, 0, 0)),
        pl.BlockSpec(memory_space=pl.ANY),
    ]
    out_specs = [
        pl.BlockSpec((rows, 2 * w4), row_map(0)),
        pl.BlockSpec((bb, NH, DH, DH), lambda i, ci: (i, 0, 0, 0)),
        pl.BlockSpec((bb, 8, QKV_C), lambda i, ci: (i, 0, 0)),
        pl.BlockSpec((bb, NH, DK_D, DH), lambda i, ci: (i, 0, 0, 0)),
    ]
    out_shape = [
        jax.ShapeDtypeStruct(ybuf.shape, ybuf.dtype),
        jax.ShapeDtypeStruct((nb, NH, DH, DH), _f32),
        jax.ShapeDtypeStruct((nb, 8, QKV_C), _f32),
        jax.ShapeDtypeStruct((nb, NH, DK_D, DH), _f32),
    ]
    args = [proj] * 6 + [conv_w, gdn_par, w_alpha, b_alpha, norm_g, *init, ybuf]
    return pl.pallas_call(
        functools.partial(_mixer1_kernel, c=c, bb=bb),
        grid=(nb // bb, nchunk),
        in_specs=in_specs,
        out_specs=out_specs,
        out_shape=out_shape,
        input_output_aliases={len(args) - 1: 0},
        compiler_params=pltpu.CompilerParams(dimension_semantics=("parallel", "arbitrary"), vmem_limit_bytes=VMEM_LIMIT),
        name=f"mixer1_c{c}",
    )(*args)


_GROUP_META = dict(row_off=ROW_META, nb=1, nchunk=1, c=N_META, bb=1)
_GROUP_PROMPT = dict(row_off=0, nb=BATCH, nchunk=SEQ // CHUNK, c=CHUNK, bb=2)
_GROUP_SAMPLE = dict(row_off=ROW_SAMPLE, nb=DEC_BATCH, nchunk=1, c=DEC_SEQ, bb=8)


def _rotary_tables(pos):
    half = DH // 2
    inv = ROPE_BASE ** (-jnp.arange(half, dtype=_f32) / half)
    ang = pos.astype(_f32)[:, None] * inv[None, :]
    cos, sin = jnp.cos(ang), jnp.sin(ang)
    return jnp.concatenate([cos, cos], -1), jnp.concatenate([-sin, sin], -1)


def _lanes(m):
    return jnp.broadcast_to(m.astype(_f32)[..., None], m.shape + (LANES,))


def _conv_rows(s):
    return jnp.pad(s.astype(_f32), ((0, 0), (8 - (CONV_W - 1), 0), (0, 0)))


def kernel(x_prompt, x_sample, state_ret, state_mlstm_c, state_mlstm_n, state_mlstm_m, state_gdn, state_gdn_conv, state_gla, meta_tokens, w_in0, ret_norm_g, mlstm_gate_bias, mlstm_norm_g, w_out0, ln0_mix_g, ln0_mix_b, ffn0_w_gate, ffn0_w_up, ffn0_w_down, ln0_ffn_g, ln0_ffn_b, w_in1, gdn_conv_w, gdn_a_log, gdn_dt_bias, gdn_norm_g, gla_w_alpha, gla_b_alpha, gla_norm_g, w_out1, ln1_mix_g, ln1_mix_b, moe_w_router, moe_b_router, moe_w_gate, moe_w_up, moe_w_down, ln1_ffn_g, ln1_ffn_b):
    w4 = NH * DH
    nchunk = SEQ // CHUNK
    xp = x_prompt.reshape(BATCH, nchunk, CHUNK, D_MODEL).transpose(1, 0, 2, 3).reshape(N_PROMPT, D_MODEL)
    x = jnp.concatenate(
        [
            xp,
            x_sample.reshape(N_SAMPLE, D_MODEL),
            meta_tokens.astype(x_prompt.dtype),
            jnp.zeros((R_ROWS - ROW_META - N_META, D_MODEL), x_prompt.dtype),
        ],
        0,
    )

    w_in0_p = jnp.pad(w_in0, ((0, 0), (0, P0_PAD - w_in0.shape[1]))).astype(_bf16)
    c_z_end = QKV_C + w4
    c_qd = c_z_end + 2 * NH
    c_lr = c_qd + 2 * NH * DK_D + 2 * w4
    w_in1_p = jnp.concatenate([w_in1[:, :c_z_end], w_in1[:, c_qd:c_lr], w_in1[:, c_lr:], w_in1[:, c_z_end:c_qd]], 1)
    w_in1_p = jnp.pad(w_in1_p, ((0, 0), (0, P1_PAD - w_in1_p.shape[1]))).astype(_bf16)
    gb_row = jnp.pad(mlstm_gate_bias.astype(_f32), (0, LANES - 2 * NH))[None]
    norm0 = jnp.stack([ret_norm_g, mlstm_norm_g]).astype(_f32)
    norm1 = jnp.stack([gdn_norm_g, gla_norm_g]).astype(_f32)
    lo = GLA_RANK + NH
    gdn_par = jnp.stack(
        [
            jnp.pad(gdn_a_log.astype(_f32), (lo, LANES - lo - NH)),
            jnp.pad(gdn_dt_bias.astype(_f32), (lo, LANES - lo - NH)),
        ]
    )
    w_router = jnp.pad(moe_w_router.astype(_f32), ((0, 0), (0, LANES - N_EXPERTS)))
    b_router = jnp.pad(moe_b_router.astype(_f32), (0, LANES - N_EXPERTS), constant_values=-jnp.inf)[None]

    def row(v):
        return v.astype(_f32)[None]

    proj0 = _proj(x, w_in0_p, tn=P0_PAD // 3)
    ybuf = jnp.zeros((R_ROWS, 2 * w4), _bf16)
    zeros_even = (
        jnp.zeros((1, NH, DH, DH), _f32), jnp.zeros((1, NH, DH, DH), _f32),
        jnp.zeros((1, NH, DH), _f32), jnp.zeros((1, NH, LANES), _f32),
    )
    cos_m, sin_m = _rotary_tables(jnp.arange(N_META))
    cos_p, sin_p = _rotary_tables(N_META + jnp.arange(SEQ))
    cos_s, sin_s = _rotary_tables(PAST_LEN + jnp.arange(DEC_SEQ))
    ybuf, *meta_even = _mixer0(proj0, ybuf, cos_m, sin_m, gb_row, norm0, zeros_even, bcast_init=True, **_GROUP_META)
    ybuf, p_ret, p_mc, p_mn, p_mm = _mixer0(proj0, ybuf, cos_p, sin_p, gb_row, norm0, meta_even, bcast_init=True, **_GROUP_PROMPT)
    init_s = (state_ret.astype(_f32), state_mlstm_c.astype(_f32), state_mlstm_n.astype(_f32), _lanes(state_mlstm_m))
    ybuf, s_ret, s_mc, s_mn, s_mm = _mixer0(proj0, ybuf, cos_s, sin_s, gb_row, norm0, init_s, bcast_init=False, **_GROUP_SAMPLE)
    x = _out_ln(ybuf, w_out0.astype(_bf16), x, row(ln0_mix_g), row(ln0_mix_b))
    x = _ffn_ln(x, ffn0_w_gate.astype(_bf16), ffn0_w_up.astype(_bf16), ffn0_w_down.astype(_bf16), row(ln0_ffn_g), row(ln0_ffn_b))

    proj1 = _proj(x, w_in1_p, tn=P1_PAD // 3)
    ybuf = jnp.zeros((R_ROWS, 2 * w4), _bf16)
    zeros_odd = (jnp.zeros((1, NH, DH, DH), _f32), jnp.zeros((1, 8, QKV_C), _f32), jnp.zeros((1, NH, DK_D, DH), _f32))
    m1_par = (gdn_conv_w.astype(_f32), gdn_par, gla_w_alpha.astype(_f32), row(gla_b_alpha), norm1)
    ybuf, *meta_odd = _mixer1(proj1, ybuf, *m1_par, zeros_odd, bcast_init=True, **_GROUP_META)
    ybuf, p_gdn, p_conv, p_gla = _mixer1(proj1, ybuf, *m1_par, meta_odd, bcast_init=True, **_GROUP_PROMPT)
    init_s = (state_gdn.astype(_f32), _conv_rows(state_gdn_conv), state_gla.astype(_f32))
    ybuf, s_gdn, s_conv, s_gla = _mixer1(proj1, ybuf, *m1_par, init_s, bcast_init=False, **_GROUP_SAMPLE)
    x = _out_ln(ybuf, w_out1.astype(_bf16), x, row(ln1_mix_g), row(ln1_mix_b))
    x = _moe_ln(x, w_router, b_router, moe_w_gate.astype(_bf16), moe_w_up.astype(_bf16), moe_w_down.astype(_bf16), row(ln1_ffn_g), row(ln1_ffn_b))

    y_prompt = x[:N_PROMPT].reshape(nchunk, BATCH, CHUNK, D_MODEL).transpose(1, 0, 2, 3).reshape(BATCH, SEQ, D_MODEL)
    y_sample = x[ROW_SAMPLE:ROW_META].reshape(DEC_BATCH, DEC_SEQ, D_MODEL)
    tail = 8 - (CONV_W - 1)
    return (
        y_prompt, y_sample,
        p_ret, p_mc, p_mn, p_mm[..., 0], p_gdn, p_conv[:, tail:], p_gla,
        s_ret, s_mc, s_mn, s_mm[..., 0], s_gdn, s_conv[:, tail:], s_gla,
    )
```

```python
import functools
import math

import jax
import jax.numpy as jnp
import numpy as np
from jax import lax
from jax.experimental import pallas as pl
from jax.experimental.pallas import tpu as pltpu

D_MODEL = 1024
BATCH = 8
SEQ = 2048
DEC_BATCH = 128
DEC_SEQ = 4
PAST_LEN = 16384
N_META = 16
CHUNK = 64
NH = 4
DH = 128
DK_D = 64
CONV_W = 4
GLA_RANK = 16
GLA_TAU = 16.0
D_FF = 2816
N_EXPERTS = 8
MOE_FF = 1408
ROPE_BASE = 10000.0
LN_EPS = 1e-5
NORM_EPS = 1e-6
DEPTH = 2
ALPHA = (2 * DEPTH) ** 0.25
QKV_C = 3 * NH * DH

LANES = 128
N_PROMPT = BATCH * SEQ
N_SAMPLE = DEC_BATCH * DEC_SEQ
ROW_SAMPLE = N_PROMPT
ROW_META = N_PROMPT + N_SAMPLE
TM_MOE = 448
MOE_BLOCK = 128
MOE_GROUP = 2
TM = TM_MOE * MOE_GROUP
R_ROWS = ((ROW_META + N_META + TM - 1) // TM) * TM

P0_PAD = 33 * LANES
COL_GATE0 = 32
P1_PAD = 30 * LANES
COL_SMALL1 = 28
SOLVE_BLOCK = 16

VMEM_LIMIT = 56 * 1024 * 1024

_bf16 = jnp.bfloat16
_f32 = jnp.float32
_HI = lax.Precision.HIGHEST


def _dot_hi(a, b):
    return jnp.dot(a, b, preferred_element_type=_f32, precision=_HI)


def _bmm(a, b):
    return jnp.einsum("nmk,nkp->nmp", a.astype(_bf16), b.astype(_bf16), preferred_element_type=_f32)


def _bmm_nt(a, b):
    return jnp.einsum("nmk,npk->nmp", a.astype(_bf16), b.astype(_bf16), preferred_element_type=_f32)


def _bmm_tn(a, b):
    return jnp.einsum("nkm,nkp->nmp", a.astype(_bf16), b.astype(_bf16), preferred_element_type=_f32)


def _split2(a):
    hi = a.astype(_bf16)
    lo = (a - hi.astype(_f32)).astype(_bf16)
    return hi, lo


def _bmm_x3(a, b):
    ah, al = _split2(a)
    bh, bl = _split2(b)
    mm = functools.partial(jnp.einsum, "nmk,nkp->nmp", preferred_element_type=_f32)
    return mm(ah, bh) + (mm(ah, bl) + mm(al, bh))


def _sigmoid(x):
    return 1.0 / (1.0 + jnp.exp(-x))


def _silu(x):
    return x * _sigmoid(x)


def _softplus(x):
    return jnp.maximum(x, 0.0) + jnp.log1p(jnp.exp(-jnp.abs(x)))


def _group_norm(o, g_row, rms):
    if not rms:
        o = o - jnp.mean(o, axis=-1, keepdims=True)
    return o * lax.rsqrt(jnp.mean(o * o, axis=-1, keepdims=True) + NORM_EPS) * g_row


def _layer_norm(x, g_row, b_row):
    mu = jnp.mean(x, axis=-1, keepdims=True)
    xc = x - mu
    var = jnp.mean(xc * xc, axis=-1, keepdims=True)
    return xc * lax.rsqrt(var + LN_EPS) * g_row + b_row


def _tri3(c):
    t = lax.broadcasted_iota(jnp.int32, (1, c, c), 1)
    s = lax.broadcasted_iota(jnp.int32, (1, c, c), 2)
    return t, s


def _cumsum_col_row(x_col, x_row, c):
    t, s = _tri3(c)
    col = jnp.sum(jnp.where(t >= s, x_row, 0.0), axis=2, keepdims=True)
    row = jnp.sum(jnp.where(t <= s, x_col, 0.0), axis=1, keepdims=True)
    return col, row


def _proj_kernel(x_ref, w_ref, o_ref, xb_ref):
    @pl.when(pl.program_id(1) == 0)
    def _():
        xb_ref[...] = x_ref[...].astype(_bf16)

    o_ref[...] = jnp.dot(xb_ref[...], w_ref[...], preferred_element_type=_f32)


def _proj(x, w, tn):
    rows, k = x.shape
    n = w.shape[1]
    return pl.pallas_call(
        _proj_kernel,
        grid=(rows // TM, n // tn),
        in_specs=[pl.BlockSpec((TM, k), lambda i, j: (i, 0)), pl.BlockSpec((k, tn), lambda i, j: (0, j))],
        out_specs=pl.BlockSpec((TM, tn), lambda i, j: (i, j)),
        out_shape=jax.ShapeDtypeStruct((rows, n), _f32),
        scratch_shapes=[pltpu.VMEM((TM, k), _bf16)],
        compiler_params=pltpu.CompilerParams(dimension_semantics=("parallel", "arbitrary"), vmem_limit_bytes=VMEM_LIMIT),
        name="in_proj",
    )(x, w)


def _out_ln_kernel(y_ref, w_ref, x_ref, g_ref, b_ref, o_ref):
    h = jnp.dot(y_ref[...], w_ref[...], preferred_element_type=_f32)
    o_ref[...] = _layer_norm(ALPHA * x_ref[...] + h, g_ref[...], b_ref[...])


def _out_ln(y, w, x, g, b):
    rows, k = y.shape
    d = w.shape[1]
    return pl.pallas_call(
        _out_ln_kernel,
        grid=(rows // TM,),
        in_specs=[
            pl.BlockSpec((TM, k), lambda i: (i, 0)),
            pl.BlockSpec((k, d), lambda i: (0, 0)),
            pl.BlockSpec((TM, d), lambda i: (i, 0)),
            pl.BlockSpec((1, d), lambda i: (0, 0)),
            pl.BlockSpec((1, d), lambda i: (0, 0)),
        ],
        out_specs=pl.BlockSpec((TM, d), lambda i: (i, 0)),
        out_shape=jax.ShapeDtypeStruct((rows, d), _f32),
        compiler_params=pltpu.CompilerParams(dimension_semantics=("parallel",), vmem_limit_bytes=VMEM_LIMIT),
        name="out_proj_ln",
    )(y, w, x, g, b)


FF_SUB = (0, 512, 1024, 1408)


def _swiglu_partial(xb, wg_ref, wu_ref, wd_ref):
    out = None
    for lo, hi in zip(FF_SUB[:-1], FF_SUB[1:]):
        hg = jnp.dot(xb, wg_ref[:, lo:hi], preferred_element_type=_f32)
        hu = jnp.dot(xb, wu_ref[:, lo:hi], preferred_element_type=_f32)
        a = (_silu(hg) * hu).astype(_bf16)
        part = jnp.dot(a, wd_ref[lo:hi, :], preferred_element_type=_f32)
        out = part if out is None else out + part
    return out


def _ffn_ln_kernel(x_ref, wg_ref, wu_ref, wd_ref, g_ref, b_ref, o_ref, xb_ref, acc_ref):
    f = pl.program_id(1)

    @pl.when(f == 0)
    def _():
        xb_ref[...] = x_ref[...].astype(_bf16)
        acc_ref[...] = jnp.zeros_like(acc_ref)

    acc_ref[...] += _swiglu_partial(xb_ref[...], wg_ref, wu_ref, wd_ref)

    @pl.when(f == pl.num_programs(1) - 1)
    def _():
        o_ref[...] = _layer_norm(ALPHA * x_ref[...] + acc_ref[...], g_ref[...], b_ref[...])


def _ffn_ln(x, wg, wu, wd, g, b):
    rows, d = x.shape
    ff = wg.shape[1]
    tf = MOE_FF
    return pl.pallas_call(
        _ffn_ln_kernel,
        grid=(rows // TM, ff // tf),
        in_specs=[
            pl.BlockSpec((TM, d), lambda i, f: (i, 0)),
            pl.BlockSpec((d, tf), lambda i, f: (0, f)),
            pl.BlockSpec((d, tf), lambda i, f: (0, f)),
            pl.BlockSpec((tf, d), lambda i, f: (f, 0)),
            pl.BlockSpec((1, d), lambda i, f: (0, 0)),
            pl.BlockSpec((1, d), lambda i, f: (0, 0)),
        ],
        out_specs=pl.BlockSpec((TM, d), lambda i, f: (i, 0)),
        out_shape=jax.ShapeDtypeStruct((rows, d), _f32),
        scratch_shapes=[pltpu.VMEM((TM, d), _bf16), pltpu.VMEM((TM, d), _f32)],
        compiler_params=pltpu.CompilerParams(dimension_semantics=("parallel", "arbitrary"), vmem_limit_bytes=VMEM_LIMIT),
        name="ffn_ln",
    )(x, wg, wu, wd, g, b)


def _router_kernel(x_ref, wr_ref, br_ref, gate_ref, code_ref, code_t_ref, cnt_ref):
    x = x_ref[...]
    tm = x.shape[0]
    logits = _dot_hi(x, wr_ref[...]) + br_ref[...]
    lane = lax.broadcasted_iota(jnp.int32, logits.shape, 1)
    ex = jnp.exp(logits - jnp.max(logits, axis=-1, keepdims=True))
    probs = ex / jnp.sum(ex, axis=-1, keepdims=True)
    p1 = jnp.max(probs, axis=-1, keepdims=True)
    i1 = jnp.min(jnp.where(probs == p1, lane, LANES), axis=-1, keepdims=True)
    rest = jnp.where(lane == i1, -1.0, probs)
    p2 = jnp.max(rest, axis=-1, keepdims=True)
    i2 = jnp.min(jnp.where(rest == p2, lane, LANES), axis=-1, keepdims=True)
    tot = p1 + p2
    gate_ref[...] = jnp.where(lane == i1, p1 / tot, 0.0) + jnp.where(lane == i2, p2 / tot, 0.0)
    sel = jnp.where(lane == i1, 1.0, jnp.where(lane == i2, 1.0, 0.0))
    selb = sel.astype(_bf16)
    tt = lax.broadcasted_iota(jnp.int32, (tm, tm), 0)
    ss = lax.broadcasted_iota(jnp.int32, (tm, tm), 1)
    rank = jnp.dot(jnp.where(tt > ss, 1.0, 0.0).astype(_bf16), selb, preferred_element_type=_f32)
    code_ref[...] = jnp.where(sel > 0.0, rank, -1.0)
    tn = (((0,), (0,)), ((), ()))
    rank_t = lax.dot_general(selb, jnp.where(tt < ss, 1.0, 0.0).astype(_bf16), tn, preferred_element_type=_f32)
    sel_t = lax.dot_general(selb, jnp.where(tt == ss, 1.0, 0.0).astype(_bf16), tn, preferred_element_type=_f32)
    code_t_ref[0] = jnp.where(sel_t > 0.0, rank_t, -1.0)[0:N_EXPERTS, :]
    cnt_ref[0] = jnp.sum(sel, axis=0, keepdims=True).astype(jnp.int32)


def _router(x, wr, br):
    rows, d = x.shape
    nt = rows // TM_MOE
    return pl.pallas_call(
        _router_kernel,
        grid=(nt,),
        in_specs=[
            pl.BlockSpec((TM_MOE, d), lambda i: (i, 0)),
            pl.BlockSpec((d, LANES), lambda i: (0, 0)),
            pl.BlockSpec((1, LANES), lambda i: (0, 0)),
        ],
        out_specs=[
            pl.BlockSpec((TM_MOE, LANES), lambda i: (i, 0)),
            pl.BlockSpec((TM_MOE, LANES), lambda i: (i, 0)),
            pl.BlockSpec((1, N_EXPERTS, TM_MOE), lambda i: (i, 0, 0)),
            pl.BlockSpec((1, 1, LANES), lambda i: (i, 0, 0)),
        ],
        out_shape=[
            jax.ShapeDtypeStruct((rows, LANES), _f32),
            jax.ShapeDtypeStruct((rows, LANES), _f32),
            jax.ShapeDtypeStruct((nt, N_EXPERTS, TM_MOE), _f32),
            jax.ShapeDtypeStruct((nt, 1, LANES), jnp.int32),
        ],
        compiler_params=pltpu.CompilerParams(dimension_semantics=("parallel",), vmem_limit_bytes=VMEM_LIMIT),
        name="router",
    )(x, wr, br)


def _moe_ln_kernel(cnt_ref, x_ref, gate_ref, code_ref, code_t_ref, wg_ref, wu_ref, wd_ref, g_ref, b_ref, o_ref, xb_ref, acc_ref):
    grp = pl.program_id(0)
    e = pl.program_id(1)

    @pl.when(e == 0)
    def _():
        xb_ref[...] = x_ref[...].astype(_bf16)
        acc_ref[...] = jnp.zeros_like(acc_ref)

    lane = lax.broadcasted_iota(jnp.int32, (TM_MOE, LANES), 1)
    slot_col = lax.broadcasted_iota(jnp.int32, (MOE_BLOCK, 1), 0)
    slot_row = lax.broadcasted_iota(jnp.int32, (1, MOE_BLOCK), 1)
    for i in range(MOE_GROUP):
        r0, r1 = i * TM_MOE, (i + 1) * TM_MOE
        count = cnt_ref[(grp * MOE_GROUP + i) * N_EXPERTS + e]
        on_e = lane == e
        code_col = jnp.sum(jnp.where(on_e, code_ref[r0:r1, :], 0.0), axis=1, keepdims=True)
        gate_col = jnp.sum(jnp.where(on_e, gate_ref[r0:r1, :], 0.0), axis=1, keepdims=True)
        code_row = code_t_ref[i, pl.ds(e, 1), :]
        for blk in range(-(-TM_MOE // MOE_BLOCK)):

            @pl.when(count > blk * MOE_BLOCK)
            def _():
                base = blk * MOE_BLOCK
                gather = jnp.where(code_row == (slot_col + base).astype(_f32), 1.0, 0.0).astype(_bf16)
                xs = jnp.dot(gather, xb_ref[r0:r1, :], preferred_element_type=_f32).astype(_bf16)
                out = _swiglu_partial(xs, wg_ref, wu_ref, wd_ref).astype(_bf16)
                scatter = jnp.where(code_col == (slot_row + base).astype(_f32), 1.0, 0.0).astype(_bf16)
                acc_ref[r0:r1, :] += gate_col * jnp.dot(scatter, out, preferred_element_type=_f32)

    @pl.when(e == pl.num_programs(1) - 1)
    def _():
        o_ref[...] = _layer_norm(ALPHA * x_ref[...] + acc_ref[...], g_ref[...], b_ref[...])


def _moe_ln(x, gates, code, code_t, counts, wg, wu, wd, g, b):
    rows, d = x.shape
    ne, _, ff = wg.shape
    grid_spec = pltpu.PrefetchScalarGridSpec(
        num_scalar_prefetch=1,
        grid=(rows // TM, ne),
        in_specs=[
            pl.BlockSpec((TM, d), lambda i, e, cnt: (i, 0)),
            pl.BlockSpec((TM, LANES), lambda i, e, cnt: (i, 0)),
            pl.BlockSpec((TM, LANES), lambda i, e, cnt: (i, 0)),
            pl.BlockSpec((MOE_GROUP, N_EXPERTS, TM_MOE), lambda i, e, cnt: (i, 0, 0)),
            pl.BlockSpec((None, d, ff), lambda i, e, cnt: (e, 0, 0)),
            pl.BlockSpec((None, d, ff), lambda i, e, cnt: (e, 0, 0)),
            pl.BlockSpec((None, ff, d), lambda i, e, cnt: (e, 0, 0)),
            pl.BlockSpec((1, d), lambda i, e, cnt: (0, 0)),
            pl.BlockSpec((1, d), lambda i, e, cnt: (0, 0)),
        ],
        out_specs=pl.BlockSpec((TM, d), lambda i, e, cnt: (i, 0)),
        scratch_shapes=[pltpu.VMEM((TM, d), _bf16), pltpu.VMEM((TM, d), _f32)],
    )
    return pl.pallas_call(
        _moe_ln_kernel,
        grid_spec=grid_spec,
        out_shape=jax.ShapeDtypeStruct((rows, d), _f32),
        compiler_params=pltpu.CompilerParams(dimension_semantics=("parallel", "arbitrary"), vmem_limit_bytes=VMEM_LIMIT),
        name="moe_ln",
    )(counts, x, gates, code, code_t, wg, wu, wd, g, b)


def _chains(ref, c, bb, width, off=0):
    return jnp.stack([ref[j * c : (j + 1) * c, off + h * width : off + (h + 1) * width] for j in range(bb) for h in range(NH)])


def _chain_cols(a, c, bb, lane0):
    return jnp.stack([a[j * c : (j + 1) * c, lane0 + h : lane0 + h + 1] for j in range(bb) for h in range(NH)])


def _chain_rows(a_t, c, bb, lane0):
    return jnp.stack([a_t[lane0 + h : lane0 + h + 1, j * c : (j + 1) * c] for j in range(bb) for h in range(NH)])


def _head_rows(ref, row, bb):
    return jnp.stack([ref[row : row + 1, h * DH : (h + 1) * DH] for _ in range(bb) for h in range(NH)])


def _store_chains(y_ref, y, c, bb, off):
    for j in range(bb):
        for h in range(NH):
            y_ref[j * c : (j + 1) * c, off + h * DH : off + (h + 1) * DH] = y[j * NH + h].astype(y_ref.dtype)


def _retention_chunk(q, k, v, s_prev, dec, vec):
    att = _bmm_nt(q, k) * dec
    o = _bmm(att, v) + vec[:, :, 0:1] * _bmm(q, s_prev)
    s_new = vec[:, 0:1, 2:3] * s_prev + _bmm_tn(k * vec[:, :, 1:2], v)
    return o, s_new


def _mlstm_chunk(q, k, v, it_col, it_row, lf_col, lf_row, c_prev, n_prev, m_prev, c):
    t, s = _tri3(c)
    b_col, b_row = _cumsum_col_row(lf_col, lf_row, c)
    logw = jnp.where(t >= s, b_col - b_row + it_row, -jnp.inf)
    m_t = jnp.maximum(b_col + m_prev, jnp.max(logw, axis=2, keepdims=True))
    w = jnp.exp(logw - m_t)
    carry = jnp.exp(b_col + m_prev - m_t)
    qk = _bmm_nt(q, k) * w
    num = _bmm(qk, v) + carry * _bmm(q, c_prev)
    den = jnp.sum(qk, axis=2, keepdims=True) + carry * jnp.sum(q * n_prev, axis=2, keepdims=True)
    h = num / jnp.maximum(jnp.abs(den), jnp.exp(-m_t))
    m_new = m_t[:, c - 1 : c, :]
    b_last = b_col[:, c - 1 : c, :]
    w_last = jnp.exp(b_last - b_col + it_col - m_new)
    decay = jnp.exp(b_last + m_prev - m_new)
    kw = k * w_last
    c_new = decay * c_prev + _bmm_tn(kw, v)
    n_new = decay * n_prev + jnp.sum(kw, axis=1, keepdims=True)
    return h, c_new, n_new, m_new


def _mixer0_kernel(
    qa_ref, ka_ref, va_ref, ga_ref, qb_ref, kb_ref, vb_ref, ob_ref, gate_ref, cos_ref, sin_ref,
    dec_ref, vec_ref, gb_ref, ng_ref, s0_ref, c0_ref, n0_ref, m0_ref, ybuf_ref,
    y_ref, s_ref, c_ref, n_ref, m_ref, *, c, bb,
):
    del ybuf_ref
    n = bb * NH

    @pl.when(pl.program_id(1) == 0)
    def _():
        s_ref[...] = jnp.broadcast_to(s0_ref[...], s_ref.shape)
        c_ref[...] = jnp.broadcast_to(c0_ref[...], c_ref.shape)
        n_ref[...] = jnp.broadcast_to(n0_ref[...], n_ref.shape)
        m_ref[...] = jnp.broadcast_to(m0_ref[...], m_ref.shape)

    cosf = cos_ref[...][None]
    sinf = sin_ref[...][None]
    gates = gate_ref[...] + gb_ref[...]
    lane = lax.broadcasted_iota(jnp.int32, gates.shape, 1)
    gates = jnp.where(lane < NH, gates, -_softplus(-gates))
    gates_t = jnp.transpose(gates)

    q = _chains(qa_ref, c, bb, DH)
    k = _chains(ka_ref, c, bb, DH)
    q = q * cosf + pltpu.roll(q, DH // 2, axis=2) * sinf
    k = (k * cosf + pltpu.roll(k, DH // 2, axis=2) * sinf) * DH**-0.5
    dec = jnp.concatenate([dec_ref[...]] * bb, axis=0)
    vec = jnp.concatenate([vec_ref[...]] * bb, axis=0)
    o, s_new = _retention_chunk(q, k, _chains(va_ref, c, bb, DH), s_ref[...].reshape(n, DH, DH), dec, vec)
    s_ref[...] = s_new.reshape(s_ref.shape)
    y_a = _group_norm(o, _head_rows(ng_ref, 0, bb), rms=False) * _silu(_chains(ga_ref, c, bb, DH))
    _store_chains(y_ref, y_a, c, bb, 0)

    n_prev = jnp.stack([n_ref[j, h : h + 1, :] for j in range(bb) for h in range(NH)])
    m_prev = jnp.stack([m_ref[j, h : h + 1, 0:1] for j in range(bb) for h in range(NH)])
    h_b, c_new, n_new, m_new = _mlstm_chunk(
        _chains(qb_ref, c, bb, DH), _chains(kb_ref, c, bb, DH) * DH**-0.5, _chains(vb_ref, c, bb, DH),
        _chain_cols(gates, c, bb, 0), _chain_rows(gates_t, c, bb, 0),
        _chain_cols(gates, c, bb, NH), _chain_rows(gates_t, c, bb, NH),
        c_ref[...].reshape(n, DH, DH), n_prev, m_prev, c,
    )
    c_ref[...] = c_new.reshape(c_ref.shape)
    for j in range(bb):
        for h in range(NH):
            n_ref[j, h : h + 1, :] = n_new[j * NH + h]
            m_ref[j, h : h + 1, :] = jnp.broadcast_to(m_new[j * NH + h], (1, LANES))
    h_b = _sigmoid(_chains(ob_ref, c, bb, DH)) * h_b
    y_b = _group_norm(h_b, _head_rows(ng_ref, 1, bb), rms=False)
    _store_chains(y_ref, y_b, c, bb, NH * DH)


def _retention_tables(c):
    lg = np.log(1.0 - 2.0 ** (-5.0 - np.arange(NH, dtype=np.float64)))[:, None, None]
    t = np.arange(c, dtype=np.float64)
    diff = t[None, :, None] - t[None, None, :]
    dec = np.where(diff >= 0, np.exp(np.maximum(diff, 0.0) * lg), 0.0)
    vec = np.zeros((NH, c, LANES))
    vec[:, :, 0] = np.exp((t[None, :] + 1.0) * lg[:, 0])
    vec[:, :, 1] = np.exp((c - 1.0 - t[None, :]) * lg[:, 0])
    vec[:, :, 2] = np.exp(c * lg[:, 0])
    return jnp.asarray(dec, _f32), jnp.asarray(vec, _f32)


def _group_maps(row_off, nb, nchunk, c, bb, bcast_init):
    rows = bb * c
    blk0 = row_off // rows
    nbb = nb // bb

    def row_map(col):
        return lambda i, ci: (blk0 + ci * nbb + i, col)

    def st_map(*zeros):
        return (lambda i, ci: (0,) + zeros) if bcast_init else (lambda i, ci: (i,) + zeros)

    return rows, row_map, st_map, (1 if bcast_init else bb)


def _mixer0(proj, ybuf, cosf, sinf, gb_row, norm_g, init, *, row_off, nb, nchunk, c, bb, bcast_init):
    rows, row_map, st_map, ib = _group_maps(row_off, nb, nchunk, c, bb, bcast_init)
    w4 = NH * DH
    dec, vec = _retention_tables(c)

    def const(nd):
        return lambda i, ci: (0,) * nd

    in_specs = [pl.BlockSpec((rows, w4), row_map(col)) for col in range(8)]
    in_specs += [
        pl.BlockSpec((rows, LANES), row_map(COL_GATE0)),
        pl.BlockSpec((c, DH), lambda i, ci: (ci, 0)),
        pl.BlockSpec((c, DH), lambda i, ci: (ci, 0)),
        pl.BlockSpec((NH, c, c), const(3)),
        pl.BlockSpec((NH, c, LANES), const(3)),
        pl.BlockSpec((1, LANES), const(2)),
        pl.BlockSpec((2, w4), const(2)),
        pl.BlockSpec((ib, NH, DH, DH), st_map(0, 0, 0)),
        pl.BlockSpec((ib, NH, DH, DH), st_map(0, 0, 0)),
        pl.BlockSpec((ib, NH, DH), st_map(0, 0)),
        pl.BlockSpec((ib, NH, LANES), st_map(0, 0)),
        pl.BlockSpec(memory_space=pl.ANY),
    ]
    out_specs = [
        pl.BlockSpec((rows, 2 * w4), row_map(0)),
        pl.BlockSpec((bb, NH, DH, DH), lambda i, ci: (i, 0, 0, 0)),
        pl.BlockSpec((bb, NH, DH, DH), lambda i, ci: (i, 0, 0, 0)),
        pl.BlockSpec((bb, NH, DH), lambda i, ci: (i, 0, 0)),
        pl.BlockSpec((bb, NH, LANES), lambda i, ci: (i, 0, 0)),
    ]
    out_shape = [
        jax.ShapeDtypeStruct(ybuf.shape, ybuf.dtype),
        jax.ShapeDtypeStruct((nb, NH, DH, DH), _f32),
        jax.ShapeDtypeStruct((nb, NH, DH, DH), _f32),
        jax.ShapeDtypeStruct((nb, NH, DH), _f32),
        jax.ShapeDtypeStruct((nb, NH, LANES), _f32),
    ]
    args = [proj] * 9 + [cosf, sinf, dec, vec, gb_row, norm_g, *init, ybuf]
    return pl.pallas_call(
        functools.partial(_mixer0_kernel, c=c, bb=bb),
        grid=(nb // bb, nchunk),
        in_specs=in_specs,
        out_specs=out_specs,
        out_shape=out_shape,
        input_output_aliases={len(args) - 1: 0},
        compiler_params=pltpu.CompilerParams(dimension_semantics=("parallel", "arbitrary"), vmem_limit_bytes=VMEM_LIMIT),
        name=f"mixer0_c{c}",
    )(*args)


def _unit_lower_solve(a, rhs, c):
    bs = min(SOLVE_BLOCK, c)
    t, s = _tri3(c)
    if c > bs:
        shift = bs.bit_length() - 1
        same = jnp.right_shift(t, shift) == jnp.right_shift(s, shift)
        d = jnp.where(same, a, 0.0)
    else:
        d = a
    inv = jnp.where(t == s, 1.0, 0.0) - d
    p = d
    span = 2
    while span < bs:
        p = _bmm_x3(p, p)
        inv = inv + _bmm_x3(inv, p)
        span *= 2
    y = _bmm(inv, rhs)
    if c == bs:
        return y
    b = _bmm(inv, jnp.where(same, 0.0, a))
    y = y - _bmm(b, y)
    p = b
    span = 2
    while span < c // bs:
        p = _bmm(p, p)
        y = y + _bmm(p, y)
        span *= 2
    return y


def _gdn_chunk(q, k, v, beta_col, g_col, g_row, s_prev, c):
    t, s = _tri3(c)
    gc_col, gc_row = _cumsum_col_row(g_col, g_row, c)
    dec_incl = jnp.exp(jnp.where(t >= s, gc_col - gc_row, -jnp.inf))
    dec_strict = jnp.where(t > s, dec_incl, 0.0)
    e_col = jnp.exp(gc_col)
    a = beta_col * _bmm_nt(k, k) * dec_strict
    rhs = jnp.concatenate([beta_col * v, (beta_col * e_col) * k], axis=-1)
    sol = _unit_lower_solve(a, rhs, c)
    u = sol[:, :, :DH] - _bmm(sol[:, :, DH:], s_prev)
    qk = _bmm_nt(q, k) * dec_incl
    o = e_col * _bmm(q, s_prev) + _bmm(qk, u)
    gl = gc_col[:, c - 1 : c, :]
    s_new = jnp.exp(gl) * s_prev + _bmm_tn(k * jnp.exp(gl - gc_col), u)
    return o, s_new


def _gla_chunk(q, k, v, bc, s_prev, c):
    t, s = _tri3(c)
    qe = q * jnp.exp(bc)
    att = jnp.where(t >= s, _bmm_nt(qe, k * jnp.exp(-bc)), 0.0)
    o = _bmm(att, v) + _bmm(qe, s_prev)
    bl = bc[:, c - 1 : c, :]
    ti, si = _tri3(DK_D)
    el_col = jnp.sum(jnp.where(ti == si, jnp.exp(bl), 0.0), axis=2, keepdims=True)
    s_new = el_col * s_prev + _bmm_tn(k * jnp.exp(bl - bc), v)
    return o, s_new


def _mixer1_kernel(
    qkv_ref, z_ref, qkd_ref, vd_ref, rd_ref, small_ref, cw_ref, gp_ref, wa_ref, ba_ref, ng_ref,
    s0_ref, cv0_ref, d0_ref, ybuf_ref,
    y_ref, s_ref, cv_ref, d_ref, *, c, bb,
):
    del ybuf_ref
    n = bb * NH
    w4 = NH * DH

    @pl.when(pl.program_id(1) == 0)
    def _():
        s_ref[...] = jnp.broadcast_to(s0_ref[...], s_ref.shape)
        cv_ref[...] = jnp.broadcast_to(cv0_ref[...], cv_ref.shape)
        d_ref[...] = jnp.broadcast_to(d0_ref[...], d_ref.shape)

    small = small_ref[...]
    beta_all = _sigmoid(small)
    g_all = -jnp.exp(gp_ref[0:1, :]) * _softplus(small + gp_ref[1:2, :])
    g_all_t = jnp.transpose(g_all)
    log_alpha = -_softplus(-(_dot_hi(small[:, 0:GLA_RANK], wa_ref[...]) + ba_ref[...])) * (1.0 / GLA_TAU)
    cw = cw_ref[...]

    acts = []
    for j in range(bb):
        ext = jnp.concatenate([cv_ref[j], qkv_ref[j * c : (j + 1) * c, :]], axis=0)
        conv = cw[3:4] * ext[8 : 8 + c] + cw[2:3] * ext[7 : 7 + c] + cw[1:2] * ext[6 : 6 + c] + cw[0:1] * ext[5 : 5 + c]
        cv_ref[j] = ext[c : c + 8]
        acts.append(_silu(conv))

    def act_chains(off):
        return jnp.stack([acts[j][:, off + h * DH : off + (h + 1) * DH] for j in range(bb) for h in range(NH)])

    qc = act_chains(0)
    kc = act_chains(w4)
    qc = qc * lax.rsqrt(jnp.sum(qc * qc, axis=-1, keepdims=True) + NORM_EPS) * DH**-0.5
    kc = kc * lax.rsqrt(jnp.sum(kc * kc, axis=-1, keepdims=True) + NORM_EPS)
    o, s_new = _gdn_chunk(
        qc, kc, act_chains(2 * w4),
        _chain_cols(beta_all, c, bb, GLA_RANK), _chain_cols(g_all, c, bb, GLA_RANK + NH),
        _chain_rows(g_all_t, c, bb, GLA_RANK + NH), s_ref[...].reshape(n, DH, DH), c,
    )
    s_ref[...] = s_new.reshape(s_ref.shape)
    y_c = _group_norm(o, _head_rows(ng_ref, 0, bb), rms=True) * _silu(_chains(z_ref, c, bb, DH))
    _store_chains(y_ref, y_c, c, bb, 0)

    tt = lax.broadcasted_iota(jnp.int32, (c, c), 0)
    ss = lax.broadcasted_iota(jnp.int32, (c, c), 1)
    ones_lt = jnp.where(tt >= ss, 1.0, 0.0).astype(_bf16)
    bcs = []
    for j in range(bb):
        la = log_alpha[j * c : (j + 1) * c, :]
        hi = la.astype(_bf16)
        r1 = la - hi.astype(_f32)
        mid = r1.astype(_bf16)
        lo = (r1 - mid.astype(_f32)).astype(_bf16)
        cum = functools.partial(jnp.dot, ones_lt, preferred_element_type=_f32)
        bcs.append(cum(hi) + (cum(mid) + cum(lo)))
    bc = jnp.stack([bcs[j][:, h * DK_D : (h + 1) * DK_D] for j in range(bb) for h in range(NH)])
    o, d_new = _gla_chunk(
        _chains(qkd_ref, c, bb, DK_D) * DK_D**-0.5, _chains(qkd_ref, c, bb, DK_D, off=NH * DK_D),
        _chains(vd_ref, c, bb, DH), bc, d_ref[...].reshape(n, DK_D, DH), c,
    )
    d_ref[...] = d_new.reshape(d_ref.shape)
    y_d = _group_norm(o, _head_rows(ng_ref, 1, bb), rms=False) * _silu(_chains(rd_ref, c, bb, DH))
    _store_chains(y_ref, y_d, c, bb, w4)


def _mixer1(proj, ybuf, conv_w, gdn_par, w_alpha, b_alpha, norm_g, init, *, row_off, nb, nchunk, c, bb, bcast_init):
    rows, row_map, st_map, ib = _group_maps(row_off, nb, nchunk, c, bb, bcast_init)
    w4 = NH * DH

    def const2(i, ci):
        return (0, 0)

    in_specs = [
        pl.BlockSpec((rows, QKV_C), row_map(0)),
        pl.BlockSpec((rows, w4), row_map(3)),
        pl.BlockSpec((rows, w4), row_map(4)),
        pl.BlockSpec((rows, w4), row_map(5)),
        pl.BlockSpec((rows, w4), row_map(6)),
        pl.BlockSpec((rows, LANES), row_map(COL_SMALL1)),
        pl.BlockSpec((CONV_W, QKV_C), const2),
        pl.BlockSpec((2, LANES), const2),
        pl.BlockSpec((GLA_RANK, NH * DK_D), const2),
        pl.BlockSpec((1, NH * DK_D), const2),
        pl.BlockSpec((2, w4), const2),
        pl.BlockSpec((ib, NH, DH, DH), st_map(0, 0, 0)),
        pl.BlockSpec((ib, 8, QKV_C), st_map(0, 0)),
        pl.BlockSpec((ib, NH, DK_D, DH), st_map(0, 0, 0)),
        pl.BlockSpec(memory_space=pl.ANY),
    ]
    out_specs = [
        pl.BlockSpec((rows, 2 * w4), row_map(0)),
        pl.BlockSpec((bb, NH, DH, DH), lambda i, ci: (i, 0, 0, 0)),
        pl.BlockSpec((bb, 8, QKV_C), lambda i, ci: (i, 0, 0)),
        pl.BlockSpec((bb, NH, DK_D, DH), lambda i, ci: (i, 0, 0, 0)),
    ]
    out_shape = [
        jax.ShapeDtypeStruct(ybuf.shape, ybuf.dtype),
        jax.ShapeDtypeStruct((nb, NH, DH, DH), _f32),
        jax.ShapeDtypeStruct((nb, 8, QKV_C), _f32),
        jax.ShapeDtypeStruct((nb, NH, DK_D, DH), _f32),
    ]
    args = [proj] * 6 + [conv_w, gdn_par, w_alpha, b_alpha, norm_g, *init, ybuf]
    return pl.pallas_call(
        functools.partial(_mixer1_kernel, c=c, bb=bb),
        grid=(nb // bb, nchunk),
        in_specs=in_specs,
        out_specs=out_specs,
        out_shape=out_shape,
        input_output_aliases={len(args) - 1: 0},
        compiler_params=pltpu.CompilerParams(dimension_semantics=("parallel", "arbitrary"), vmem_limit_bytes=VMEM_LIMIT),
        name=f"mixer1_c{c}",
    )(*args)


_GROUP_META = dict(row_off=ROW_META, nb=1, nchunk=1, c=N_META, bb=1)
_GROUP_PROMPT = dict(row_off=0, nb=BATCH, nchunk=SEQ // CHUNK, c=CHUNK, bb=2)
_GROUP_SAMPLE = dict(row_off=ROW_SAMPLE, nb=DEC_BATCH, nchunk=1, c=DEC_SEQ, bb=8)


def _rotary_tables(pos):
    half = DH // 2
    inv = ROPE_BASE ** (-jnp.arange(half, dtype=_f32) / half)
    ang = pos.astype(_f32)[:, None] * inv[None, :]
    cos, sin = jnp.cos(ang), jnp.sin(ang)
    return jnp.concatenate([cos, cos], -1), jnp.concatenate([-sin, sin], -1)


def _lanes(m):
    return jnp.broadcast_to(m.astype(_f32)[..., None], m.shape + (LANES,))


def _conv_rows(s):
    return jnp.pad(s.astype(_f32), ((0, 0), (8 - (CONV_W - 1), 0), (0, 0)))


def kernel(x_prompt, x_sample, state_ret, state_mlstm_c, state_mlstm_n, state_mlstm_m, state_gdn, state_gdn_conv, state_gla, meta_tokens, w_in0, ret_norm_g, mlstm_gate_bias, mlstm_norm_g, w_out0, ln0_mix_g, ln0_mix_b, ffn0_w_gate, ffn0_w_up, ffn0_w_down, ln0_ffn_g, ln0_ffn_b, w_in1, gdn_conv_w, gdn_a_log, gdn_dt_bias, gdn_norm_g, gla_w_alpha, gla_b_alpha, gla_norm_g, w_out1, ln1_mix_g, ln1_mix_b, moe_w_router, moe_b_router, moe_w_gate, moe_w_up, moe_w_down, ln1_ffn_g, ln1_ffn_b):
    w4 = NH * DH
    nchunk = SEQ // CHUNK
    xp = x_prompt.reshape(BATCH, nchunk, CHUNK, D_MODEL).transpose(1, 0, 2, 3).reshape(N_PROMPT, D_MODEL)
    x = jnp.concatenate(
        [
            xp,
            x_sample.reshape(N_SAMPLE, D_MODEL),
            meta_tokens.astype(x_prompt.dtype),
            jnp.zeros((R_ROWS - ROW_META - N_META, D_MODEL), x_prompt.dtype),
        ],
        0,
    )

    w_in0_p = jnp.pad(w_in0, ((0, 0), (0, P0_PAD - w_in0.shape[1]))).astype(_bf16)
    c_z_end = QKV_C + w4
    c_qd = c_z_end + 2 * NH
    c_lr = c_qd + 2 * NH * DK_D + 2 * w4
    w_in1_p = jnp.concatenate([w_in1[:, :c_z_end], w_in1[:, c_qd:c_lr], w_in1[:, c_lr:], w_in1[:, c_z_end:c_qd]], 1)
    w_in1_p = jnp.pad(w_in1_p, ((0, 0), (0, P1_PAD - w_in1_p.shape[1]))).astype(_bf16)
    gb_row = jnp.pad(mlstm_gate_bias.astype(_f32), (0, LANES - 2 * NH))[None]
    norm0 = jnp.stack([ret_norm_g, mlstm_norm_g]).astype(_f32)
    norm1 = jnp.stack([gdn_norm_g, gla_norm_g]).astype(_f32)
    lo = GLA_RANK + NH
    gdn_par = jnp.stack(
        [
            jnp.pad(gdn_a_log.astype(_f32), (lo, LANES - lo - NH)),
            jnp.pad(gdn_dt_bias.astype(_f32), (lo, LANES - lo - NH)),
        ]
    )
    w_router = jnp.pad(moe_w_router.astype(_f32), ((0, 0), (0, LANES - N_EXPERTS)))
    b_router = jnp.pad(moe_b_router.astype(_f32), (0, LANES - N_EXPERTS), constant_values=-jnp.inf)[None]

    def row(v):
        return v.astype(_f32)[None]

    proj0 = _proj(x, w_in0_p, tn=P0_PAD // 3)
    ybuf = jnp.zeros((R_ROWS, 2 * w4), _bf16)
    zeros_even = (
        jnp.zeros((1, NH, DH, DH), _f32), jnp.zeros((1, NH, DH, DH), _f32),
        jnp.zeros((1, NH, DH), _f32), jnp.zeros((1, NH, LANES), _f32),
    )
    cos_m, sin_m = _rotary_tables(jnp.arange(N_META))
    cos_p, sin_p = _rotary_tables(N_META + jnp.arange(SEQ))
    cos_s, sin_s = _rotary_tables(PAST_LEN + jnp.arange(DEC_SEQ))
    ybuf, *meta_even = _mixer0(proj0, ybuf, cos_m, sin_m, gb_row, norm0, zeros_even, bcast_init=True, **_GROUP_META)
    ybuf, p_ret, p_mc, p_mn, p_mm = _mixer0(proj0, ybuf, cos_p, sin_p, gb_row, norm0, meta_even, bcast_init=True, **_GROUP_PROMPT)
    init_s = (state_ret.astype(_f32), state_mlstm_c.astype(_f32), state_mlstm_n.astype(_f32), _lanes(state_mlstm_m))
    ybuf, s_ret, s_mc, s_mn, s_mm = _mixer0(proj0, ybuf, cos_s, sin_s, gb_row, norm0, init_s, bcast_init=False, **_GROUP_SAMPLE)
    x = _out_ln(ybuf, w_out0.astype(_bf16), x, row(ln0_mix_g), row(ln0_mix_b))
    x = _ffn_ln(x, ffn0_w_gate.astype(_bf16), ffn0_w_up.astype(_bf16), ffn0_w_down.astype(_bf16), row(ln0_ffn_g), row(ln0_ffn_b))

    proj1 = _proj(x, w_in1_p, tn=P1_PAD // 3)
    ybuf = jnp.zeros((R_ROWS, 2 * w4), _bf16)
    zeros_odd = (jnp.zeros((1, NH, DH, DH), _f32), jnp.zeros((1, 8, QKV_C), _f32), jnp.zeros((1, NH, DK_D, DH), _f32))
    m1_par = (gdn_conv_w.astype(_f32), gdn_par, gla_w_alpha.astype(_f32), row(gla_b_alpha), norm1)
    ybuf, *meta_odd = _mixer1(proj1, ybuf, *m1_par, zeros_odd, bcast_init=True, **_GROUP_META)
    ybuf, p_gdn, p_conv, p_gla = _mixer1(proj1, ybuf, *m1_par, meta_odd, bcast_init=True, **_GROUP_PROMPT)
    init_s = (state_gdn.astype(_f32), _conv_rows(state_gdn_conv), state_gla.astype(_f32))
    ybuf, s_gdn, s_conv, s_gla = _mixer1(proj1, ybuf, *m1_par, init_s, bcast_init=False, **_GROUP_SAMPLE)
    x = _out_ln(ybuf, w_out1.astype(_bf16), x, row(ln1_mix_g), row(ln1_mix_b))
    gates, code, code_t, counts = _router(x, w_router, b_router)
    counts = counts[:, 0, :N_EXPERTS].reshape(-1)
    x = _moe_ln(x, gates, code, code_t, counts, moe_w_gate.astype(_bf16), moe_w_up.astype(_bf16), moe_w_down.astype(_bf16), row(ln1_ffn_g), row(ln1_ffn_b))

    y_prompt = x[:N_PROMPT].reshape(nchunk, BATCH, CHUNK, D_MODEL).transpose(1, 0, 2, 3).reshape(BATCH, SEQ, D_MODEL)
    y_sample = x[ROW_SAMPLE:ROW_META].reshape(DEC_BATCH, DEC_SEQ, D_MODEL)
    tail = 8 - (CONV_W - 1)
    return (
        y_prompt, y_sample,
        p_ret, p_mc, p_mn, p_mm[..., 0], p_gdn, p_conv[:, tail:], p_gla,
        s_ret, s_mc, s_mn, s_mm[..., 0], s_gdn, s_conv[:, tail:], s_gla,
    )
```

```python
import functools
import math

import jax
import jax.numpy as jnp
import numpy as np
from jax import lax
from jax.experimental import pallas as pl
from jax.experimental.pallas import tpu as pltpu

D_MODEL = 1024
BATCH = 8
SEQ = 2048
DEC_BATCH = 128
DEC_SEQ = 4
PAST_LEN = 16384
N_META = 16
CHUNK = 64
NH = 4
DH = 128
DK_D = 64
CONV_W = 4
GLA_RANK = 16
GLA_TAU = 16.0
D_FF = 2816
N_EXPERTS = 8
MOE_FF = 1408
ROPE_BASE = 10000.0
LN_EPS = 1e-5
NORM_EPS = 1e-6
DEPTH = 2
ALPHA = (2 * DEPTH) ** 0.25
QKV_C = 3 * NH * DH

LANES = 128
N_PROMPT = BATCH * SEQ
N_SAMPLE = DEC_BATCH * DEC_SEQ
ROW_SAMPLE = N_PROMPT
ROW_META = N_PROMPT + N_SAMPLE
TM_MOE = 448
MOE_BLOCK = 128
MOE_GROUP = 2
TM = TM_MOE * MOE_GROUP
R_ROWS = ((ROW_META + N_META + TM - 1) // TM) * TM

P0_PAD = 33 * LANES
COL_GATE0 = 32
P1_PAD = 30 * LANES
COL_SMALL1 = 28
SOLVE_BLOCK = 16

VMEM_LIMIT = 56 * 1024 * 1024

_bf16 = jnp.bfloat16
_f32 = jnp.float32
_HI = lax.Precision.HIGHEST


def _dot_hi(a, b):
    return jnp.dot(a, b, preferred_element_type=_f32, precision=_HI)


def _bmm(a, b):
    return jnp.einsum("nmk,nkp->nmp", a.astype(_bf16), b.astype(_bf16), preferred_element_type=_f32)


def _bmm_nt(a, b):
    return jnp.einsum("nmk,npk->nmp", a.astype(_bf16), b.astype(_bf16), preferred_element_type=_f32)


def _bmm_tn(a, b):
    return jnp.einsum("nkm,nkp->nmp", a.astype(_bf16), b.astype(_bf16), preferred_element_type=_f32)


def _split2(a):
    hi = a.astype(_bf16)
    lo = (a - hi.astype(_f32)).astype(_bf16)
    return hi, lo


def _bmm_x3(a, b):
    ah, al = _split2(a)
    bh, bl = _split2(b)
    mm = functools.partial(jnp.einsum, "nmk,nkp->nmp", preferred_element_type=_f32)
    return mm(ah, bh) + (mm(ah, bl) + mm(al, bh))


def _sigmoid(x):
    return 1.0 / (1.0 + jnp.exp(-x))


def _silu(x):
    return x * _sigmoid(x)


def _softplus(x):
    return jnp.maximum(x, 0.0) + jnp.log1p(jnp.exp(-jnp.abs(x)))


def _group_norm(o, g_row, rms):
    if not rms:
        o = o - jnp.mean(o, axis=-1, keepdims=True)
    return o * lax.rsqrt(jnp.mean(o * o, axis=-1, keepdims=True) + NORM_EPS) * g_row


def _layer_norm(x, g_row, b_row):
    mu = jnp.mean(x, axis=-1, keepdims=True)
    xc = x - mu
    var = jnp.mean(xc * xc, axis=-1, keepdims=True)
    return xc * lax.rsqrt(var + LN_EPS) * g_row + b_row


def _tri3(c):
    t = lax.broadcasted_iota(jnp.int32, (1, c, c), 1)
    s = lax.broadcasted_iota(jnp.int32, (1, c, c), 2)
    return t, s


def _cumsum_col_row(x_col, x_row, c):
    t, s = _tri3(c)
    col = jnp.sum(jnp.where(t >= s, x_row, 0.0), axis=2, keepdims=True)
    row = jnp.sum(jnp.where(t <= s, x_col, 0.0), axis=1, keepdims=True)
    return col, row


def _proj_kernel(x_ref, w_ref, o_ref, xb_ref):
    @pl.when(pl.program_id(1) == 0)
    def _():
        xb_ref[...] = x_ref[...].astype(_bf16)

    o_ref[...] = jnp.dot(xb_ref[...], w_ref[...], preferred_element_type=_f32)


def _proj(x, w, tn):
    rows, k = x.shape
    n = w.shape[1]
    return pl.pallas_call(
        _proj_kernel,
        grid=(rows // TM, n // tn),
        in_specs=[pl.BlockSpec((TM, k), lambda i, j: (i, 0)), pl.BlockSpec((k, tn), lambda i, j: (0, j))],
        out_specs=pl.BlockSpec((TM, tn), lambda i, j: (i, j)),
        out_shape=jax.ShapeDtypeStruct((rows, n), _f32),
        scratch_shapes=[pltpu.VMEM((TM, k), _bf16)],
        compiler_params=pltpu.CompilerParams(dimension_semantics=("parallel", "arbitrary"), vmem_limit_bytes=VMEM_LIMIT),
        name="in_proj",
    )(x, w)


def _out_ln_kernel(y_ref, w_ref, x_ref, g_ref, b_ref, o_ref):
    h = jnp.dot(y_ref[...], w_ref[...], preferred_element_type=_f32)
    o_ref[...] = _layer_norm(ALPHA * x_ref[...] + h, g_ref[...], b_ref[...])


def _out_ln(y, w, x, g, b):
    rows, k = y.shape
    d = w.shape[1]
    return pl.pallas_call(
        _out_ln_kernel,
        grid=(rows // TM,),
        in_specs=[
            pl.BlockSpec((TM, k), lambda i: (i, 0)),
            pl.BlockSpec((k, d), lambda i: (0, 0)),
            pl.BlockSpec((TM, d), lambda i: (i, 0)),
            pl.BlockSpec((1, d), lambda i: (0, 0)),
            pl.BlockSpec((1, d), lambda i: (0, 0)),
        ],
        out_specs=pl.BlockSpec((TM, d), lambda i: (i, 0)),
        out_shape=jax.ShapeDtypeStruct((rows, d), _f32),
        compiler_params=pltpu.CompilerParams(dimension_semantics=("parallel",), vmem_limit_bytes=VMEM_LIMIT),
        name="out_proj_ln",
    )(y, w, x, g, b)


FF_SUB = (0, 512, 1024, 1408)


def _swiglu_partial(xb, wg_ref, wu_ref, wd_ref):
    out = None
    for lo, hi in zip(FF_SUB[:-1], FF_SUB[1:]):
        hg = jnp.dot(xb, wg_ref[:, lo:hi], preferred_element_type=_f32)
        hu = jnp.dot(xb, wu_ref[:, lo:hi], preferred_element_type=_f32)
        a = (_silu(hg) * hu).astype(_bf16)
        part = jnp.dot(a, wd_ref[lo:hi, :], preferred_element_type=_f32)
        out = part if out is None else out + part
    return out


def _ffn_ln_kernel(x_ref, wg_ref, wu_ref, wd_ref, g_ref, b_ref, o_ref, xb_ref, acc_ref):
    f = pl.program_id(1)

    @pl.when(f == 0)
    def _():
        xb_ref[...] = x_ref[...].astype(_bf16)
        acc_ref[...] = jnp.zeros_like(acc_ref)

    acc_ref[...] += _swiglu_partial(xb_ref[...], wg_ref, wu_ref, wd_ref)

    @pl.when(f == pl.num_programs(1) - 1)
    def _():
        o_ref[...] = _layer_norm(ALPHA * x_ref[...] + acc_ref[...], g_ref[...], b_ref[...])


def _ffn_ln(x, wg, wu, wd, g, b):
    rows, d = x.shape
    ff = wg.shape[1]
    tf = MOE_FF
    return pl.pallas_call(
        _ffn_ln_kernel,
        grid=(rows // TM, ff // tf),
        in_specs=[
            pl.BlockSpec((TM, d), lambda i, f: (i, 0)),
            pl.BlockSpec((d, tf), lambda i, f: (0, f)),
            pl.BlockSpec((d, tf), lambda i, f: (0, f)),
            pl.BlockSpec((tf, d), lambda i, f: (f, 0)),
            pl.BlockSpec((1, d), lambda i, f: (0, 0)),
            pl.BlockSpec((1, d), lambda i, f: (0, 0)),
        ],
        out_specs=pl.BlockSpec((TM, d), lambda i, f: (i, 0)),
        out_shape=jax.ShapeDtypeStruct((rows, d), _f32),
        scratch_shapes=[pltpu.VMEM((TM, d), _bf16), pltpu.VMEM((TM, d), _f32)],
        compiler_params=pltpu.CompilerParams(dimension_semantics=("parallel", "arbitrary"), vmem_limit_bytes=VMEM_LIMIT),
        name="ffn_ln",
    )(x, wg, wu, wd, g, b)


def _router_kernel(x_ref, wr_ref, br_ref, before_ref, ident_ref, gate_t_ref, code_t_ref, cnt_ref):
    x = x_ref[...]
    xh, xl = _split2(x)
    wh, wl = _split2(wr_ref[...])
    mm = functools.partial(jnp.dot, preferred_element_type=_f32)
    logits = mm(xh, wh) + (mm(xh, wl) + mm(xl, wh)) + br_ref[...]
    lane = lax.broadcasted_iota(jnp.int32, logits.shape, 1)
    ex = jnp.exp(logits - jnp.max(logits, axis=-1, keepdims=True))
    probs = ex / jnp.sum(ex, axis=-1, keepdims=True)
    p1 = jnp.max(probs, axis=-1, keepdims=True)
    i1 = jnp.min(jnp.where(probs == p1, lane, LANES), axis=-1, keepdims=True)
    rest = jnp.where(lane == i1, -1.0, probs)
    p2 = jnp.max(rest, axis=-1, keepdims=True)
    i2 = jnp.min(jnp.where(rest == p2, lane, LANES), axis=-1, keepdims=True)
    tot = p1 + p2
    gates =jnp.where(lane == i1, p1 / tot, 0.0) + jnp.where(lane == i2, p2 / tot, 0.0)
    sel = jnp.where(lane == i1, 1.0, jnp.where(lane == i2, 1.0, 0.0))
    selb = sel.astype(_bf16)
    g_hi = gates.astype(_bf16)
    r1 = gates - g_hi.astype(_f32)
    g_mid = r1.astype(_bf16)
    g_lo = (r1 - g_mid.astype(_f32)).astype(_bf16)
    stacked = jnp.concatenate([selb, g_hi, g_mid, g_lo], axis=1)
    stacked_t = lax.dot_general(stacked, ident_ref[...], (((0,), (0,)), ((), ())), preferred_element_type=_f32)
    sel_t = stacked_t[0:N_EXPERTS, :]
    rank_t = jnp.dot(sel_t.astype(_bf16), before_ref[...], preferred_element_type=_f32)
    code_t_ref[0] = jnp.where(sel_t > 0.0, rank_t, -1.0)
    parts = [stacked_t[k * LANES : k * LANES + N_EXPERTS, :] for k in (1, 2, 3)]
    gate_t_ref[0] = parts[0] + (parts[1] + parts[2])
    cnt_ref[0] = jnp.sum(sel, axis=0, keepdims=True).astype(jnp.int32)


def _router(x, wr, br):
    rows, d = x.shape
    nt = rows // TM_MOE
    t = np.arange(TM_MOE)
    before = jnp.asarray(t[:, None] < t[None, :], _bf16)
    ident = jnp.asarray(t[:, None] == t[None, :], _bf16)
    return pl.pallas_call(
        _router_kernel,
        grid=(nt,),
        in_specs=[
            pl.BlockSpec((TM_MOE, d), lambda i: (i, 0)),
            pl.BlockSpec((d, LANES), lambda i: (0, 0)),
            pl.BlockSpec((1, LANES), lambda i: (0, 0)),
            pl.BlockSpec((TM_MOE, TM_MOE), lambda i: (0, 0)),
            pl.BlockSpec((TM_MOE, TM_MOE), lambda i: (0, 0)),
        ],
        out_specs=[
            pl.BlockSpec((1, N_EXPERTS, TM_MOE), lambda i: (i, 0, 0)),
            pl.BlockSpec((1, N_EXPERTS, TM_MOE), lambda i: (i, 0, 0)),
            pl.BlockSpec((1, 1, LANES), lambda i: (i, 0, 0)),
        ],
        out_shape=[
            jax.ShapeDtypeStruct((nt, N_EXPERTS, TM_MOE), _f32),
            jax.ShapeDtypeStruct((nt, N_EXPERTS, TM_MOE), _f32),
            jax.ShapeDtypeStruct((nt, 1, LANES), jnp.int32),
        ],
        compiler_params=pltpu.CompilerParams(dimension_semantics=("parallel",), vmem_limit_bytes=VMEM_LIMIT),
        name="router",
    )(x, wr, br, before, ident)


def _moe_ln_kernel(cnt_ref, x_ref, gate_t_ref, code_t_ref, wg_ref, wu_ref, wd_ref, g_ref, b_ref, o_ref, xb_ref, acc_ref):
    grp = pl.program_id(0)
    e = pl.program_id(1)

    @pl.when(e == 0)
    def _():
        xb_ref[...] = x_ref[...].astype(_bf16)
        acc_ref[...] = jnp.zeros_like(acc_ref)

    slot_col = lax.broadcasted_iota(jnp.int32, (MOE_BLOCK, 1), 0)
    counts = [cnt_ref[(grp * MOE_GROUP + i) * N_EXPERTS + e] for i in range(MOE_GROUP)]
    code_rows = [code_t_ref[i, pl.ds(e, 1), :] for i in range(MOE_GROUP)]
    gate_rows = [gate_t_ref[i, pl.ds(e, 1), :] for i in range(MOE_GROUP)]

    def gather_rows(i, blk):
        hit = code_rows[i] == (slot_col + blk * MOE_BLOCK).astype(_f32)
        onehot = jnp.where(hit, 1.0, 0.0).astype(_bf16)
        xs = jnp.dot(onehot, xb_ref[i * TM_MOE : (i + 1) * TM_MOE, :], preferred_element_type=_f32).astype(_bf16)
        gate = jnp.sum(jnp.where(hit, gate_rows[i], 0.0), axis=1, keepdims=True)
        return onehot, xs, gate

    def scatter_rows(i, onehot, out):
        back = lax.dot_general(onehot, out.astype(_bf16), (((0,), (0,)), ((), ())), preferred_element_type=_f32)
        acc_ref[i * TM_MOE : (i + 1) * TM_MOE, :] += back

    def single_pass(i, blk):
        onehot, xs, gate = gather_rows(i, blk)
        scatter_rows(i, onehot, _swiglu_partial(xs, wg_ref, wu_ref, wd_ref) * gate)

    assert MOE_GROUP == 2
    has0 = counts[0] > 0
    has1 = counts[1] > 0

    @pl.when(jnp.logical_and(has0, has1))
    def _():
        oh0, xs0, gate0 = gather_rows(0, 0)
        oh1, xs1, gate1 = gather_rows(1, 0)
        out = _swiglu_partial(jnp.concatenate([xs0, xs1], axis=0), wg_ref, wu_ref, wd_ref)
        scatter_rows(0, oh0, out[:MOE_BLOCK] * gate0)
        scatter_rows(1, oh1, out[MOE_BLOCK:] * gate1)

    @pl.when(jnp.logical_and(has0, jnp.logical_not(has1)))
    def _():
        single_pass(0, 0)

    @pl.when(jnp.logical_and(has1, jnp.logical_not(has0)))
    def _():
        single_pass(1, 0)

    for blk in range(1, -(-TM_MOE // MOE_BLOCK)):
        for i in range(MOE_GROUP):

            @pl.when(counts[i] > blk * MOE_BLOCK)
            def _():
                single_pass(i, blk)

    @pl.when(e == pl.num_programs(1) - 1)
    def _():
        o_ref[...] = _layer_norm(ALPHA * x_ref[...] + acc_ref[...], g_ref[...], b_ref[...])


def _moe_ln(x, gate_t, code_t, counts, wg, wu, wd, g, b):
    rows, d = x.shape
    ne, _, ff = wg.shape
    grid_spec = pltpu.PrefetchScalarGridSpec(
        num_scalar_prefetch=1,
        grid=(rows // TM, ne),
        in_specs=[
            pl.BlockSpec((TM, d), lambda i, e, cnt: (i, 0)),
            pl.BlockSpec((MOE_GROUP, N_EXPERTS, TM_MOE), lambda i, e, cnt: (i, 0, 0)),
            pl.BlockSpec((MOE_GROUP, N_EXPERTS, TM_MOE), lambda i, e, cnt: (i, 0, 0)),
            pl.BlockSpec((None, d, ff), lambda i, e, cnt: (e, 0, 0)),
            pl.BlockSpec((None, d, ff), lambda i, e, cnt: (e, 0, 0)),
            pl.BlockSpec((None, ff, d), lambda i, e, cnt: (e, 0, 0)),
            pl.BlockSpec((1, d), lambda i, e, cnt: (0, 0)),
            pl.BlockSpec((1, d), lambda i, e, cnt: (0, 0)),
        ],
        out_specs=pl.BlockSpec((TM, d), lambda i, e, cnt: (i, 0)),
        scratch_shapes=[pltpu.VMEM((TM, d), _bf16), pltpu.VMEM((TM, d), _f32)],
    )
    return pl.pallas_call(
        _moe_ln_kernel,
        grid_spec=grid_spec,
        out_shape=jax.ShapeDtypeStruct((rows, d), _f32),
        compiler_params=pltpu.CompilerParams(dimension_semantics=("parallel", "arbitrary"), vmem_limit_bytes=VMEM_LIMIT),
        name="moe_ln",
    )(counts, x, gate_t, code_t, wg, wu, wd, g, b)


def _chains(ref, c, bb, width, off=0):
    return jnp.stack([ref[j * c : (j + 1) * c, off + h * width : off + (h + 1) * width] for j in range(bb) for h in range(NH)])


def _chain_cols(a, c, bb, lane0):
    return jnp.stack([a[j * c : (j + 1) * c, lane0 + h : lane0 + h + 1] for j in range(bb) for h in range(NH)])


def _chain_rows(a_t, c, bb, lane0):
    return jnp.stack([a_t[lane0 + h : lane0 + h + 1, j * c : (j + 1) * c] for j in range(bb) for h in range(NH)])


def _head_rows(ref, row, bb):
    return jnp.stack([ref[row : row + 1, h * DH : (h + 1) * DH] for _ in range(bb) for h in range(NH)])


def _store_chains(y_ref, y, c, bb, off):
    for j in range(bb):
        for h in range(NH):
            y_ref[j * c : (j + 1) * c, off + h * DH : off + (h + 1) * DH] = y[j * NH + h].astype(y_ref.dtype)
    if y_ref.shape[0] > bb * c:
        y_ref[bb * c :, off : off + NH * DH] = jnp.zeros((y_ref.shape[0] - bb * c, NH * DH), y_ref.dtype)


def _drop_alias_ref(kernel, n_in):
    def body(*refs):
        return kernel(*refs[:n_in], None, *refs[n_in:])

    return body


def _y_buffer_specs(ybuf, total_rows, rows, y_rows, row_off, row_map, width):
    if y_rows == rows:
        spec = pl.BlockSpec((rows, width), row_map(0))
    else:
        spec = pl.BlockSpec((y_rows, width), lambda i, ci: (row_off // y_rows, 0))
    return spec, jax.ShapeDtypeStruct((total_rows, width), _bf16), ([] if ybuf is None else [pl.BlockSpec(memory_space=pl.ANY)])


def _retention_chunk(q, k, v, s_prev, dec, vec):
    att = _bmm_nt(q, k) * dec
    o = _bmm(att, v) + vec[:, :, 0:1] * _bmm(q, s_prev)
    s_new = vec[:, 0:1, 2:3] * s_prev + _bmm_tn(k * vec[:, :, 1:2], v)
    return o, s_new


def _mlstm_chunk(q, k, v, it_col, it_row, lf_col, lf_row, c_prev, n_prev, m_prev, c):
    t, s = _tri3(c)
    b_col, b_row = _cumsum_col_row(lf_col, lf_row, c)
    logw = jnp.where(t >= s, b_col - b_row + it_row, -jnp.inf)
    m_t = jnp.maximum(b_col + m_prev, jnp.max(logw, axis=2, keepdims=True))
    w = jnp.exp(logw - m_t)
    carry = jnp.exp(b_col + m_prev - m_t)
    qk = _bmm_nt(q, k) * w
    num = _bmm(qk, v) + carry * _bmm(q, c_prev)
    den = jnp.sum(qk, axis=2, keepdims=True) + carry * jnp.sum(q * n_prev, axis=2, keepdims=True)
    h = num / jnp.maximum(jnp.abs(den), jnp.exp(-m_t))
    m_new = m_t[:, c - 1 : c, :]
    b_last = b_col[:, c - 1 : c, :]
    w_last = jnp.exp(b_last - b_col + it_col - m_new)
    decay = jnp.exp(b_last + m_prev - m_new)
    kw = k * w_last
    c_new = decay * c_prev + _bmm_tn(kw, v)
    n_new = decay * n_prev + jnp.sum(kw, axis=1, keepdims=True)
    return h, c_new, n_new, m_new


def _mixer0_kernel(
    qa_ref, ka_ref, va_ref, ga_ref, qb_ref, kb_ref, vb_ref, ob_ref, gate_ref, cos_ref, sin_ref,
    dec_ref, vec_ref, gb_ref, ng_ref, s0_ref, c0_ref, n0_ref, m0_ref, ybuf_ref,
    y_ref, s_ref, c_ref, n_ref, m_ref, *, c, bb,
):
    del ybuf_ref
    n = bb * NH

    @pl.when(pl.program_id(1) == 0)
    def _():
        s_ref[...] = jnp.broadcast_to(s0_ref[...], s_ref.shape)
        c_ref[...] = jnp.broadcast_to(c0_ref[...], c_ref.shape)
        n_ref[...] = jnp.broadcast_to(n0_ref[...], n_ref.shape)
        m_ref[...] = jnp.broadcast_to(m0_ref[...], m_ref.shape)

    cosf = cos_ref[...][None]
    sinf = sin_ref[...][None]
    gates = gate_ref[...] + gb_ref[...]
    lane = lax.broadcasted_iota(jnp.int32, gates.shape, 1)
    gates = jnp.where(lane < NH, gates, -_softplus(-gates))
    gates_t = jnp.transpose(gates)

    q = _chains(qa_ref, c, bb, DH)
    k = _chains(ka_ref, c, bb, DH)
    q = q * cosf + pltpu.roll(q, DH // 2, axis=2) * sinf
    k = (k * cosf + pltpu.roll(k, DH // 2, axis=2) * sinf) * DH**-0.5
    dec = jnp.concatenate([dec_ref[...]] * bb, axis=0)
    vec = jnp.concatenate([vec_ref[...]] * bb, axis=0)
    o, s_new = _retention_chunk(q, k, _chains(va_ref, c, bb, DH), s_ref[...].reshape(n, DH, DH), dec, vec)
    s_ref[...] = s_new.reshape(s_ref.shape)
    y_a = _group_norm(o, _head_rows(ng_ref, 0, bb), rms=False) * _silu(_chains(ga_ref, c, bb, DH))
    _store_chains(y_ref, y_a, c, bb, 0)

    n_prev = jnp.stack([n_ref[j, h : h + 1, :] for j in range(bb) for h in range(NH)])
    m_prev = jnp.stack([m_ref[j, h : h + 1, 0:1] for j in range(bb) for h in range(NH)])
    h_b, c_new, n_new, m_new = _mlstm_chunk(
        _chains(qb_ref, c, bb, DH), _chains(kb_ref, c, bb, DH) * DH**-0.5, _chains(vb_ref, c, bb, DH),
        _chain_cols(gates, c, bb, 0), _chain_rows(gates_t, c, bb, 0),
        _chain_cols(gates, c, bb, NH), _chain_rows(gates_t, c, bb, NH),
        c_ref[...].reshape(n, DH, DH), n_prev, m_prev, c,
    )
    c_ref[...] = c_new.reshape(c_ref.shape)
    for j in range(bb):
        for h in range(NH):
            n_ref[j, h : h + 1, :] = n_new[j * NH + h]
            m_ref[j, h : h + 1, :] = jnp.broadcast_to(m_new[j * NH + h], (1, LANES))
    h_b = _sigmoid(_chains(ob_ref, c, bb, DH)) * h_b
    y_b = _group_norm(h_b, _head_rows(ng_ref, 1, bb), rms=False)
    _store_chains(y_ref, y_b, c, bb, NH * DH)


def _retention_tables(c):
    lg = np.log(1.0 - 2.0 ** (-5.0 - np.arange(NH, dtype=np.float64)))[:, None, None]
    t = np.arange(c, dtype=np.float64)
    diff = t[None, :, None] - t[None, None, :]
    dec = np.where(diff >= 0, np.exp(np.maximum(diff, 0.0) * lg), 0.0)
    vec = np.zeros((NH, c, LANES))
    vec[:, :, 0] = np.exp((t[None, :] + 1.0) * lg[:, 0])
    vec[:, :, 1] = np.exp((c - 1.0 - t[None, :]) * lg[:, 0])
    vec[:, :, 2] = np.exp(c * lg[:, 0])
    return jnp.asarray(dec, _f32), jnp.asarray(vec, _f32)


def _group_maps(row_off, nb, nchunk, c, bb, bcast_init):
    rows = bb * c
    blk0 = row_off // rows
    nbb = nb // bb

    def row_map(col):
        return lambda i, ci: (blk0 + ci * nbb + i, col)

    def st_map(*zeros):
        return (lambda i, ci: (0,) + zeros) if bcast_init else (lambda i, ci: (i,) + zeros)

    return rows, row_map, st_map, (1 if bcast_init else bb)


def _mixer0(proj, ybuf, cosf, sinf, gb_row, norm_g, init, *, row_off, nb, nchunk, c, bb, bcast_init, y_rows=None):
    rows, row_map, st_map, ib = _group_maps(row_off, nb, nchunk, c, bb, bcast_init)
    w4 = NH * DH
    dec, vec = _retention_tables(c)
    y_spec, y_shape, alias_spec = _y_buffer_specs(ybuf, proj.shape[0], rows, y_rows or rows, row_off, row_map, 2 * w4)

    def const(nd):
        return lambda i, ci: (0,) * nd

    in_specs = [pl.BlockSpec((rows, w4), row_map(col)) for col in range(8)]
    in_specs += [
        pl.BlockSpec((rows, LANES), row_map(COL_GATE0)),
        pl.BlockSpec((c, DH), lambda i, ci: (ci, 0)),
        pl.BlockSpec((c, DH), lambda i, ci: (ci, 0)),
        pl.BlockSpec((NH, c, c), const(3)),
        pl.BlockSpec((NH, c, LANES), const(3)),
        pl.BlockSpec((1, LANES), const(2)),
        pl.BlockSpec((2, w4), const(2)),
        pl.BlockSpec((ib, NH, DH, DH), st_map(0, 0, 0)),
        pl.BlockSpec((ib, NH, DH, DH), st_map(0, 0, 0)),
        pl.BlockSpec((ib, NH, DH), st_map(0, 0)),
        pl.BlockSpec((ib, NH, LANES), st_map(0, 0)),
    ]
    out_specs = [
        y_spec,
        pl.BlockSpec((bb, NH, DH, DH), lambda i, ci: (i, 0, 0, 0)),
        pl.BlockSpec((bb, NH, DH, DH), lambda i, ci: (i, 0, 0, 0)),
        pl.BlockSpec((bb, NH, DH), lambda i, ci: (i, 0, 0)),
        pl.BlockSpec((bb, NH, LANES), lambda i, ci: (i, 0, 0)),
    ]
    out_shape = [
        y_shape,
        jax.ShapeDtypeStruct((nb, NH, DH, DH), _f32),
        jax.ShapeDtypeStruct((nb, NH, DH, DH), _f32),
        jax.ShapeDtypeStruct((nb, NH, DH), _f32),
        jax.ShapeDtypeStruct((nb, NH, LANES), _f32),
    ]
    args = [proj] * 9 + [cosf, sinf, dec, vec, gb_row, norm_g, *init]
    body = functools.partial(_mixer0_kernel, c=c, bb=bb)
    return pl.pallas_call(
        body if ybuf is not None else _drop_alias_ref(body, len(args)),
        grid=(nb // bb, nchunk),
        in_specs=in_specs + alias_spec,
        out_specs=out_specs,
        out_shape=out_shape,
        input_output_aliases={len(args): 0} if ybuf is not None else {},
        compiler_params=pltpu.CompilerParams(dimension_semantics=("parallel", "arbitrary"), vmem_limit_bytes=VMEM_LIMIT),
        name=f"mixer0_c{c}",
    )(*args, *([] if ybuf is None else [ybuf]))


def _unit_lower_solve(a, rhs, c):
    bs = min(SOLVE_BLOCK, c)
    t, s = _tri3(c)
    if c > bs:
        shift = bs.bit_length() - 1
        same = jnp.right_shift(t, shift) == jnp.right_shift(s, shift)
        d = jnp.where(same, a, 0.0)
    else:
        d = a
    inv = jnp.where(t == s, 1.0, 0.0) - d
    p = d
    span = 2
    while span < bs:
        p = _bmm_x3(p, p)
        inv = inv + _bmm_x3(inv, p)
        span *= 2
    y = _bmm(inv, rhs)
    if c == bs:
        return y
    b = _bmm(inv, jnp.where(same, 0.0, a))
    y = y - _bmm(b, y)
    p = b
    span = 2
    while span < c // bs:
        p = _bmm(p, p)
        y = y + _bmm(p, y)
        span *= 2
    return y


def _gdn_chunk(q, k, v, beta_col, g_col, g_row, s_prev, c):
    t, s = _tri3(c)
    gc_col, gc_row = _cumsum_col_row(g_col, g_row, c)
    dec_incl = jnp.exp(jnp.where(t >= s, gc_col - gc_row, -jnp.inf))
    dec_strict = jnp.where(t > s, dec_incl, 0.0)
    e_col = jnp.exp(gc_col)
    a = beta_col * _bmm_nt(k, k) * dec_strict
    rhs = jnp.concatenate([beta_col * v, (beta_col * e_col) * k], axis=-1)
    sol = _unit_lower_solve(a, rhs, c)
    u = sol[:, :, :DH] - _bmm(sol[:, :, DH:], s_prev)
    qk = _bmm_nt(q, k) * dec_incl
    o = e_col * _bmm(q, s_prev) + _bmm(qk, u)
    gl = gc_col[:, c - 1 : c, :]
    s_new = jnp.exp(gl) * s_prev + _bmm_tn(k * jnp.exp(gl - gc_col), u)
    return o, s_new


def _gla_chunk(q, k, v, bc, s_prev, c):
    t, s = _tri3(c)
    qe = q * jnp.exp(bc)
    att = jnp.where(t >= s, _bmm_nt(qe, k * jnp.exp(-bc)), 0.0)
    o = _bmm(att, v) + _bmm(qe, s_prev)
    bl = bc[:, c - 1 : c, :]
    ti, si = _tri3(DK_D)
    el_col = jnp.sum(jnp.where(ti == si, jnp.exp(bl), 0.0), axis=2, keepdims=True)
    s_new = el_col * s_prev + _bmm_tn(k * jnp.exp(bl - bc), v)
    return o, s_new


def _mixer1_kernel(
    qkv_ref, z_ref, qkd_ref, vd_ref, rd_ref, small_ref, cw_ref, gp_ref, wa_ref, ba_ref, ng_ref,
    s0_ref, cv0_ref, d0_ref, ybuf_ref,
    y_ref, s_ref, cv_ref, d_ref, *, c, bb,
):
    del ybuf_ref
    n = bb * NH
    w4 = NH * DH

    @pl.when(pl.program_id(1) == 0)
    def _():
        s_ref[...] = jnp.broadcast_to(s0_ref[...], s_ref.shape)
        cv_ref[...] = jnp.broadcast_to(cv0_ref[...], cv_ref.shape)
        d_ref[...] = jnp.broadcast_to(d0_ref[...], d_ref.shape)

    small = small_ref[...]
    beta_all = _sigmoid(small)
    g_all = -jnp.exp(gp_ref[0:1, :]) * _softplus(small + gp_ref[1:2, :])
    g_all_t = jnp.transpose(g_all)
    log_alpha = -_softplus(-(_dot_hi(small[:, 0:GLA_RANK], wa_ref[...]) + ba_ref[...])) * (1.0 / GLA_TAU)
    cw = cw_ref[...]

    acts = []
    for j in range(bb):
        ext = jnp.concatenate([cv_ref[j], qkv_ref[j * c : (j + 1) * c, :]], axis=0)
        conv = cw[3:4] * ext[8 : 8 + c] + cw[2:3] * ext[7 : 7 + c] + cw[1:2] * ext[6 : 6 + c] + cw[0:1] * ext[5 : 5 + c]
        cv_ref[j] = ext[c : c + 8]
        acts.append(_silu(conv))

    def act_chains(off):
        return jnp.stack([acts[j][:, off + h * DH : off + (h + 1) * DH] for j in range(bb) for h in range(NH)])

    qc = act_chains(0)
    kc = act_chains(w4)
    qc = qc * lax.rsqrt(jnp.sum(qc * qc, axis=-1, keepdims=True) + NORM_EPS) * DH**-0.5
    kc = kc * lax.rsqrt(jnp.sum(kc * kc, axis=-1, keepdims=True) + NORM_EPS)
    o, s_new = _gdn_chunk(
        qc, kc, act_chains(2 * w4),
        _chain_cols(beta_all, c, bb, GLA_RANK), _chain_cols(g_all, c, bb, GLA_RANK + NH),
        _chain_rows(g_all_t, c, bb, GLA_RANK + NH), s_ref[...].reshape(n, DH, DH), c,
    )
    s_ref[...] = s_new.reshape(s_ref.shape)
    y_c = _group_norm(o, _head_rows(ng_ref, 0, bb), rms=True) * _silu(_chains(z_ref, c, bb, DH))
    _store_chains(y_ref, y_c, c, bb, 0)

    tt = lax.broadcasted_iota(jnp.int32, (c, c), 0)
    ss = lax.broadcasted_iota(jnp.int32, (c, c), 1)
    ones_lt = jnp.where(tt >= ss, 1.0, 0.0).astype(_bf16)
    bcs = []
    for j in range(bb):
        la = log_alpha[j * c : (j + 1) * c, :]
        hi = la.astype(_bf16)
        r1 = la - hi.astype(_f32)
        mid = r1.astype(_bf16)
        lo = (r1 - mid.astype(_f32)).astype(_bf16)
        cum = functools.partial(jnp.dot, ones_lt, preferred_element_type=_f32)
        bcs.append(cum(hi) + (cum(mid) + cum(lo)))
    bc = jnp.stack([bcs[j][:, h * DK_D : (h + 1) * DK_D] for j in range(bb) for h in range(NH)])
    o, d_new = _gla_chunk(
        _chains(qkd_ref, c, bb, DK_D) * DK_D**-0.5, _chains(qkd_ref, c, bb, DK_D, off=NH * DK_D),
        _chains(vd_ref, c, bb, DH), bc, d_ref[...].reshape(n, DK_D, DH), c,
    )
    d_ref[...] = d_new.reshape(d_ref.shape)
    y_d = _group_norm(o, _head_rows(ng_ref, 1, bb), rms=False) * _silu(_chains(rd_ref, c, bb, DH))
    _store_chains(y_ref, y_d, c, bb, w4)


def _mixer1(proj, ybuf, conv_w, gdn_par, w_alpha, b_alpha, norm_g, init, *, row_off, nb, nchunk, c, bb, bcast_init, y_rows=None):
    rows, row_map, st_map, ib = _group_maps(row_off, nb, nchunk, c, bb, bcast_init)
    w4 = NH * DH
    y_spec, y_shape, alias_spec = _y_buffer_specs(ybuf, proj.shape[0], rows, y_rows or rows, row_off, row_map, 2 * w4)

    def const2(i, ci):
        return (0, 0)

    in_specs = [
        pl.BlockSpec((rows, QKV_C), row_map(0)),
        pl.BlockSpec((rows, w4), row_map(3)),
        pl.BlockSpec((rows, w4), row_map(4)),
        pl.BlockSpec((rows, w4), row_map(5)),
        pl.BlockSpec((rows, w4), row_map(6)),
        pl.BlockSpec((rows, LANES), row_map(COL_SMALL1)),
        pl.BlockSpec((CONV_W, QKV_C), const2),
        pl.BlockSpec((2, LANES), const2),
        pl.BlockSpec((GLA_RANK, NH * DK_D), const2),
        pl.BlockSpec((1, NH * DK_D), const2),
        pl.BlockSpec((2, w4), const2),
        pl.BlockSpec((ib, NH, DH, DH), st_map(0, 0, 0)),
        pl.BlockSpec((ib, 8, QKV_C), st_map(0, 0)),
        pl.BlockSpec((ib, NH, DK_D, DH), st_map(0, 0, 0)),
    ]
    out_specs = [
        y_spec,
        pl.BlockSpec((bb, NH, DH, DH), lambda i, ci: (i, 0, 0, 0)),
        pl.BlockSpec((bb, 8, QKV_C), lambda i, ci: (i, 0, 0)),
        pl.BlockSpec((bb, NH, DK_D, DH), lambda i, ci: (i, 0, 0, 0)),
    ]
    out_shape = [
        y_shape,
        jax.ShapeDtypeStruct((nb, NH, DH, DH), _f32),
        jax.ShapeDtypeStruct((nb, 8, QKV_C), _f32),
        jax.ShapeDtypeStruct((nb, NH, DK_D, DH), _f32),
    ]
    args = [proj] * 6 + [conv_w, gdn_par, w_alpha, b_alpha, norm_g, *init]
    body = functools.partial(_mixer1_kernel, c=c, bb=bb)
    return pl.pallas_call(
        body if ybuf is not None else _drop_alias_ref(body, len(args)),
        grid=(nb // bb, nchunk),
        in_specs=in_specs + alias_spec,
        out_specs=out_specs,
        out_shape=out_shape,
        input_output_aliases={len(args): 0} if ybuf is not None else {},
        compiler_params=pltpu.CompilerParams(dimension_semantics=("parallel", "arbitrary"), vmem_limit_bytes=VMEM_LIMIT),
        name=f"mixer1_c{c}",
    )(*args, *([] if ybuf is None else [ybuf]))


_GROUP_META = dict(row_off=ROW_META, nb=1, nchunk=1, c=N_META, bb=1, y_rows=R_ROWS - ROW_META)
_GROUP_PROMPT = dict(row_off=0, nb=BATCH, nchunk=SEQ // CHUNK, c=CHUNK, bb=4)
_GROUP_SAMPLE = dict(row_off=ROW_SAMPLE, nb=DEC_BATCH, nchunk=1, c=DEC_SEQ, bb=8)


def _rows_to_batch_major_kernel(x_ref, o_ref):
    o_ref[...] = x_ref[...].reshape(o_ref.shape)


def _prompt_rows_to_batch_major(x):
    nchunk = SEQ // CHUNK
    out = pl.pallas_call(
        _rows_to_batch_major_kernel,
        grid=(nchunk,),
        in_specs=[pl.BlockSpec((BATCH * CHUNK, D_MODEL), lambda ci: (ci, 0))],
        out_specs=pl.BlockSpec((BATCH, None, CHUNK, D_MODEL), lambda ci: (0, ci, 0, 0)),
        out_shape=jax.ShapeDtypeStruct((BATCH, nchunk, CHUNK, D_MODEL), x.dtype),
        compiler_params=pltpu.CompilerParams(dimension_semantics=("parallel",)),
        name="rows_to_batch_major",
    )(x)
    return out.reshape(BATCH, SEQ, D_MODEL)


def _rotary_tables(pos):
    half = DH // 2
    inv = ROPE_BASE ** (-jnp.arange(half, dtype=_f32) / half)
    ang = pos.astype(_f32)[:, None] * inv[None, :]
    cos, sin = jnp.cos(ang), jnp.sin(ang)
    return jnp.concatenate([cos, cos], -1), jnp.concatenate([-sin, sin], -1)


def _lanes(m):
    return jnp.broadcast_to(m.astype(_f32)[..., None], m.shape + (LANES,))


def _conv_rows(s):
    return jnp.pad(s.astype(_f32), ((0, 0), (8 - (CONV_W - 1), 0), (0, 0)))


def kernel(x_prompt, x_sample, state_ret, state_mlstm_c, state_mlstm_n, state_mlstm_m, state_gdn, state_gdn_conv, state_gla, meta_tokens, w_in0, ret_norm_g, mlstm_gate_bias, mlstm_norm_g, w_out0, ln0_mix_g, ln0_mix_b, ffn0_w_gate, ffn0_w_up, ffn0_w_down, ln0_ffn_g, ln0_ffn_b, w_in1, gdn_conv_w, gdn_a_log, gdn_dt_bias, gdn_norm_g, gla_w_alpha, gla_b_alpha, gla_norm_g, w_out1, ln1_mix_g, ln1_mix_b, moe_w_router, moe_b_router, moe_w_gate, moe_w_up, moe_w_down, ln1_ffn_g, ln1_ffn_b):
    w4 = NH * DH
    nchunk = SEQ // CHUNK
    xp = x_prompt.reshape(BATCH, nchunk, CHUNK, D_MODEL).transpose(1, 0, 2, 3).reshape(N_PROMPT, D_MODEL)
    x = jnp.concatenate(
        [
            xp,
            x_sample.reshape(N_SAMPLE, D_MODEL),
            meta_tokens.astype(x_prompt.dtype),
            jnp.zeros((R_ROWS - ROW_META - N_META, D_MODEL), x_prompt.dtype),
        ],
        0,
    )

    w_in0_p = jnp.pad(w_in0, ((0, 0), (0, P0_PAD - w_in0.shape[1]))).astype(_bf16)
    c_z_end = QKV_C + w4
    c_qd = c_z_end + 2 * NH
    c_lr = c_qd + 2 * NH * DK_D + 2 * w4
    w_in1_p = jnp.concatenate([w_in1[:, :c_z_end], w_in1[:, c_qd:c_lr], w_in1[:, c_lr:], w_in1[:, c_z_end:c_qd]], 1)
    w_in1_p = jnp.pad(w_in1_p, ((0, 0), (0, P1_PAD - w_in1_p.shape[1]))).astype(_bf16)
    gb_row = jnp.pad(mlstm_gate_bias.astype(_f32), (0, LANES - 2 * NH))[None]
    norm0 = jnp.stack([ret_norm_g, mlstm_norm_g]).astype(_f32)
    norm1 = jnp.stack([gdn_norm_g, gla_norm_g]).astype(_f32)
    lo = GLA_RANK + NH
    gdn_par = jnp.stack(
        [
            jnp.pad(gdn_a_log.astype(_f32), (lo, LANES - lo - NH)),
            jnp.pad(gdn_dt_bias.astype(_f32), (lo, LANES - lo - NH)),
        ]
    )
    w_router = jnp.pad(moe_w_router.astype(_f32), ((0, 0), (0, LANES - N_EXPERTS)))
    b_router = jnp.pad(moe_b_router.astype(_f32), (0, LANES - N_EXPERTS), constant_values=-jnp.inf)[None]

    def row(v):
        return v.astype(_f32)[None]

    proj0 = _proj(x, w_in0_p, tn=P0_PAD // 3)
    zeros_even = (
        jnp.zeros((1, NH, DH, DH), _f32), jnp.zeros((1, NH, DH, DH), _f32),
        jnp.zeros((1, NH, DH), _f32), jnp.zeros((1, NH, LANES), _f32),
    )
    cos_m, sin_m = _rotary_tables(jnp.arange(N_META))
    cos_p, sin_p = _rotary_tables(N_META + jnp.arange(SEQ))
    cos_s, sin_s = _rotary_tables(PAST_LEN + jnp.arange(DEC_SEQ))
    ybuf, *meta_even = _mixer0(proj0, None, cos_m, sin_m, gb_row, norm0, zeros_even, bcast_init=True, **_GROUP_META)
    ybuf, p_ret, p_mc, p_mn, p_mm = _mixer0(proj0, ybuf, cos_p, sin_p, gb_row, norm0, meta_even, bcast_init=True, **_GROUP_PROMPT)
    init_s = (state_ret.astype(_f32), state_mlstm_c.astype(_f32), state_mlstm_n.astype(_f32), _lanes(state_mlstm_m))
    ybuf, s_ret, s_mc, s_mn, s_mm = _mixer0(proj0, ybuf, cos_s, sin_s, gb_row, norm0, init_s, bcast_init=False, **_GROUP_SAMPLE)
    x = _out_ln(ybuf, w_out0.astype(_bf16), x, row(ln0_mix_g), row(ln0_mix_b))
    x = _ffn_ln(x, ffn0_w_gate.astype(_bf16), ffn0_w_up.astype(_bf16), ffn0_w_down.astype(_bf16), row(ln0_ffn_g), row(ln0_ffn_b))

    proj1 = _proj(x, w_in1_p, tn=P1_PAD // 3)
    zeros_odd = (jnp.zeros((1, NH, DH, DH), _f32), jnp.zeros((1, 8, QKV_C), _f32), jnp.zeros((1, NH, DK_D, DH), _f32))
    m1_par = (gdn_conv_w.astype(_f32), gdn_par, gla_w_alpha.astype(_f32), row(gla_b_alpha), norm1)
    ybuf, *meta_odd = _mixer1(proj1, None, *m1_par, zeros_odd, bcast_init=True, **_GROUP_META)
    ybuf, p_gdn, p_conv, p_gla = _mixer1(proj1, ybuf, *m1_par, meta_odd, bcast_init=True, **_GROUP_PROMPT)
    init_s = (state_gdn.astype(_f32), _conv_rows(state_gdn_conv), state_gla.astype(_f32))
    ybuf, s_gdn, s_conv, s_gla = _mixer1(proj1, ybuf, *m1_par, init_s, bcast_init=False, **_GROUP_SAMPLE)
    x = _out_ln(ybuf, w_out1.astype(_bf16), x, row(ln1_mix_g), row(ln1_mix_b))
    gate_t, code_t, counts = _router(x, w_router, b_router)
    counts = counts[:, 0, :N_EXPERTS].reshape(-1)
    x = _moe_ln(x, gate_t, code_t, counts, moe_w_gate.astype(_bf16), moe_w_up.astype(_bf16), moe_w_down.astype(_bf16), row(ln1_ffn_g), row(ln1_ffn_b))

    y_prompt = _prompt_rows_to_batch_major(x)
    y_sample = x[ROW_SAMPLE:ROW_META].reshape(DEC_BATCH, DEC_SEQ, D_MODEL)
    tail = 8 - (CONV_W - 1)
    return (
        y_prompt, y_sample,
        p_ret, p_mc, p_mn, p_mm[..., 0], p_gdn, p_conv[:, tail:], p_gla,
        s_ret, s_mc, s_mn, s_mm[..., 0], s_gdn, s_conv[:, tail:], s_gla,
    )
```

```python
import functools
import math

import jax
import jax.numpy as jnp
import numpy as np
from jax import lax
from jax.experimental import pallas as pl
from jax.experimental.pallas import tpu as pltpu

D_MODEL = 1024
BATCH = 8
SEQ = 2048
DEC_BATCH = 128
DEC_SEQ = 4
PAST_LEN = 16384
N_META = 16
CHUNK = 64
NH = 4
DH = 128
DK_D = 64
CONV_W = 4
GLA_RANK = 16
GLA_TAU = 16.0
D_FF = 2816
N_EXPERTS = 8
MOE_FF = 1408
ROPE_BASE = 10000.0
LN_EPS = 1e-5
NORM_EPS = 1e-6
DEPTH = 2
ALPHA = (2 * DEPTH) ** 0.25
QKV_C = 3 * NH * DH

LANES = 128
N_PROMPT = BATCH * SEQ
N_SAMPLE = DEC_BATCH * DEC_SEQ
ROW_SAMPLE = N_PROMPT
ROW_META = N_PROMPT + N_SAMPLE
TM_MOE = 448
MOE_BLOCK = 128
MOE_GROUP = 2
TM = TM_MOE * MOE_GROUP
R_ROWS = ((ROW_META + N_META + TM - 1) // TM) * TM

P0_PAD = 33 * LANES
P1_MAIN = 28 * LANES
P1_PAD = 30 * LANES
SOLVE_BLOCK = 16

VMEM_LIMIT = 56 * 1024 * 1024

_bf16 = jnp.bfloat16
_f32 = jnp.float32
_HI = lax.Precision.HIGHEST


def _dot_hi(a, b):
    return jnp.dot(a, b, preferred_element_type=_f32, precision=_HI)


def _bmm(a, b):
    return jnp.einsum("nmk,nkp->nmp", a.astype(_bf16), b.astype(_bf16), preferred_element_type=_f32)


def _bmm_nt(a, b):
    return jnp.einsum("nmk,npk->nmp", a.astype(_bf16), b.astype(_bf16), preferred_element_type=_f32)


def _bmm_tn(a, b):
    return jnp.einsum("nkm,nkp->nmp", a.astype(_bf16), b.astype(_bf16), preferred_element_type=_f32)


def _split2(a):
    hi = a.astype(_bf16)
    lo = (a - hi.astype(_f32)).astype(_bf16)
    return hi, lo


def _bmm_x3(a, b):
    ah, al = _split2(a)
    bh, bl = _split2(b)
    mm = functools.partial(jnp.einsum, "nmk,nkp->nmp", preferred_element_type=_f32)
    return mm(ah, bh) + (mm(ah, bl) + mm(al, bh))


def _sigmoid(x):
    return 1.0 / (1.0 + jnp.exp(-x))


def _silu(x):
    return x * _sigmoid(x)


def _softplus(x):
    return jnp.maximum(x, 0.0) + jnp.log1p(jnp.exp(-jnp.abs(x)))


def _group_norm(o, g_row, rms):
    if not rms:
        o = o - jnp.mean(o, axis=-1, keepdims=True)
    return o * lax.rsqrt(jnp.mean(o * o, axis=-1, keepdims=True) + NORM_EPS) * g_row


def _layer_norm(x, g_row, b_row):
    mu = jnp.mean(x, axis=-1, keepdims=True)
    xc = x - mu
    var = jnp.mean(xc * xc, axis=-1, keepdims=True)
    return xc * lax.rsqrt(var + LN_EPS) * g_row + b_row


def _tri3(c):
    t = lax.broadcasted_iota(jnp.int32, (1, c, c), 1)
    s = lax.broadcasted_iota(jnp.int32, (1, c, c), 2)
    return t, s


def _cumsum_col_row(x_col, x_row, c):
    t, s = _tri3(c)
    col = jnp.sum(jnp.where(t >= s, x_row, 0.0), axis=2, keepdims=True)
    row = jnp.sum(jnp.where(t <= s, x_col, 0.0), axis=1, keepdims=True)
    return col, row


def _proj_kernel(x_ref, w_ref, o_ref, tail_ref, xb_ref):
    j = pl.program_id(1)

    @pl.when(j == 0)
    def _():
        xb_ref[...] = x_ref[...].astype(_bf16)

    acc = jnp.dot(xb_ref[...], w_ref[...], preferred_element_type=_f32)
    o_ref[...] = acc.astype(o_ref.dtype)

    @pl.when(j == pl.num_programs(1) - 1)
    def _():
        tail_ref[...] = acc[:, acc.shape[1] - LANES :]


def _proj(x, w, tn):
    rows, k = x.shape
    n = w.shape[1]
    return pl.pallas_call(
        _proj_kernel,
        grid=(rows // TM, n // tn),
        in_specs=[pl.BlockSpec((TM, k), lambda i, j: (i, 0)), pl.BlockSpec((k, tn), lambda i, j: (0, j))],
        out_specs=[pl.BlockSpec((TM, tn), lambda i, j: (i, j)), pl.BlockSpec((TM, LANES), lambda i, j: (i, 0))],
        out_shape=[jax.ShapeDtypeStruct((rows, n), _bf16), jax.ShapeDtypeStruct((rows, LANES), _f32)],
        scratch_shapes=[pltpu.VMEM((TM, k), _bf16)],
        compiler_params=pltpu.CompilerParams(dimension_semantics=("parallel", "arbitrary"), vmem_limit_bytes=VMEM_LIMIT),
        name="in_proj",
    )(x, w)


def _out_ln_kernel(y_ref, w_ref, x_ref, g_ref, b_ref, o_ref):
    h = jnp.dot(y_ref[...], w_ref[...], preferred_element_type=_f32)
    o_ref[...] = _layer_norm(ALPHA * x_ref[...] + h, g_ref[...], b_ref[...])


def _out_ln(y, w, x, g, b):
    rows, k = y.shape
    d = w.shape[1]
    return pl.pallas_call(
        _out_ln_kernel,
        grid=(rows // TM,),
        in_specs=[
            pl.BlockSpec((TM, k), lambda i: (i, 0)),
            pl.BlockSpec((k, d), lambda i: (0, 0)),
            pl.BlockSpec((TM, d), lambda i: (i, 0)),
            pl.BlockSpec((1, d), lambda i: (0, 0)),
            pl.BlockSpec((1, d), lambda i: (0, 0)),
        ],
        out_specs=pl.BlockSpec((TM, d), lambda i: (i, 0)),
        out_shape=jax.ShapeDtypeStruct((rows, d), _f32),
        compiler_params=pltpu.CompilerParams(dimension_semantics=("parallel",), vmem_limit_bytes=VMEM_LIMIT),
        name="out_proj_ln",
    )(y, w, x, g, b)


FF_SUB = (0, 512, 1024, 1408)


def _swiglu_partial(xb, wg_ref, wu_ref, wd_ref):
    out = None
    for lo, hi in zip(FF_SUB[:-1], FF_SUB[1:]):
        hg = jnp.dot(xb, wg_ref[:, lo:hi], preferred_element_type=_f32)
        hu = jnp.dot(xb, wu_ref[:, lo:hi], preferred_element_type=_f32)
        a = (_silu(hg) * hu).astype(_bf16)
        part = jnp.dot(a, wd_ref[lo:hi, :], preferred_element_type=_f32)
        out = part if out is None else out + part
    return out


def _ffn_ln_kernel(x_ref, wg_ref, wu_ref, wd_ref, g_ref, b_ref, o_ref, xb_ref, acc_ref):
    f = pl.program_id(1)

    @pl.when(f == 0)
    def _():
        xb_ref[...] = x_ref[...].astype(_bf16)
        acc_ref[...] = jnp.zeros_like(acc_ref)

    acc_ref[...] += _swiglu_partial(xb_ref[...], wg_ref, wu_ref, wd_ref)

    @pl.when(f == pl.num_programs(1) - 1)
    def _():
        o_ref[...] = _layer_norm(ALPHA * x_ref[...] + acc_ref[...], g_ref[...], b_ref[...])


def _ffn_ln(x, wg, wu, wd, g, b):
    rows, d = x.shape
    ff = wg.shape[1]
    tf = MOE_FF
    return pl.pallas_call(
        _ffn_ln_kernel,
        grid=(rows // TM, ff // tf),
        in_specs=[
            pl.BlockSpec((TM, d), lambda i, f: (i, 0)),
            pl.BlockSpec((d, tf), lambda i, f: (0, f)),
            pl.BlockSpec((d, tf), lambda i, f: (0, f)),
            pl.BlockSpec((tf, d), lambda i, f: (f, 0)),
            pl.BlockSpec((1, d), lambda i, f: (0, 0)),
            pl.BlockSpec((1, d), lambda i, f: (0, 0)),
        ],
        out_specs=pl.BlockSpec((TM, d), lambda i, f: (i, 0)),
        out_shape=jax.ShapeDtypeStruct((rows, d), _f32),
        scratch_shapes=[pltpu.VMEM((TM, d), _bf16), pltpu.VMEM((TM, d), _f32)],
        compiler_params=pltpu.CompilerParams(dimension_semantics=("parallel", "arbitrary"), vmem_limit_bytes=VMEM_LIMIT),
        name="ffn_ln",
    )(x, wg, wu, wd, g, b)


def _router_kernel(x_ref, wr_ref, br_ref, before_ref, ident_ref, gate_t_ref, code_t_ref, cnt_ref):
    x = x_ref[...]
    xh, xl = _split2(x)
    wh, wl = _split2(wr_ref[...])
    mm = functools.partial(jnp.dot, preferred_element_type=_f32)
    logits = mm(xh, wh) + (mm(xh, wl) + mm(xl, wh)) + br_ref[...]
    lane = lax.broadcasted_iota(jnp.int32, logits.shape, 1)
    ex = jnp.exp(logits - jnp.max(logits, axis=-1, keepdims=True))
    probs = ex / jnp.sum(ex, axis=-1, keepdims=True)
    p1 = jnp.max(probs, axis=-1, keepdims=True)
    i1 = jnp.min(jnp.where(probs == p1, lane, LANES), axis=-1, keepdims=True)
    rest = jnp.where(lane == i1, -1.0, probs)
    p2 = jnp.max(rest, axis=-1, keepdims=True)
    i2 = jnp.min(jnp.where(rest == p2, lane, LANES), axis=-1, keepdims=True)
    tot = p1 + p2
    gates =jnp.where(lane == i1, p1 / tot, 0.0) + jnp.where(lane == i2, p2 / tot, 0.0)
    sel = jnp.where(lane == i1, 1.0, jnp.where(lane == i2, 1.0, 0.0))
    selb = sel.astype(_bf16)
    g_hi = gates.astype(_bf16)
    r1 = gates - g_hi.astype(_f32)
    g_mid = r1.astype(_bf16)
    g_lo = (r1 - g_mid.astype(_f32)).astype(_bf16)
    stacked = jnp.concatenate([selb, g_hi, g_mid, g_lo], axis=1)
    stacked_t = lax.dot_general(stacked, ident_ref[...], (((0,), (0,)), ((), ())), preferred_element_type=_f32)
    sel_t = stacked_t[0:N_EXPERTS, :]
    rank_t = jnp.dot(sel_t.astype(_bf16), before_ref[...], preferred_element_type=_f32)
    code_t_ref[0] = jnp.where(sel_t > 0.0, rank_t, -1.0)
    parts = [stacked_t[k * LANES : k * LANES + N_EXPERTS, :] for k in (1, 2, 3)]
    gate_t_ref[0] = parts[0] + (parts[1] + parts[2])
    cnt_ref[0] = jnp.sum(sel, axis=0, keepdims=True).astype(jnp.int32)


def _router(x, wr, br):
    rows, d = x.shape
    nt = rows // TM_MOE
    t = np.arange(TM_MOE)
    before = jnp.asarray(t[:, None] < t[None, :], _bf16)
    ident = jnp.asarray(t[:, None] == t[None, :], _bf16)
    return pl.pallas_call(
        _router_kernel,
        grid=(nt,),
        in_specs=[
            pl.BlockSpec((TM_MOE, d), lambda i: (i, 0)),
            pl.BlockSpec((d, LANES), lambda i: (0, 0)),
            pl.BlockSpec((1, LANES), lambda i: (0, 0)),
            pl.BlockSpec((TM_MOE, TM_MOE), lambda i: (0, 0)),
            pl.BlockSpec((TM_MOE, TM_MOE), lambda i: (0, 0)),
        ],
        out_specs=[
            pl.BlockSpec((1, N_EXPERTS, TM_MOE), lambda i: (i, 0, 0)),
            pl.BlockSpec((1, N_EXPERTS, TM_MOE), lambda i: (i, 0, 0)),
            pl.BlockSpec((1, 1, LANES), lambda i: (i, 0, 0)),
        ],
        out_shape=[
            jax.ShapeDtypeStruct((nt, N_EXPERTS, TM_MOE), _f32),
            jax.ShapeDtypeStruct((nt, N_EXPERTS, TM_MOE), _f32),
            jax.ShapeDtypeStruct((nt, 1, LANES), jnp.int32),
        ],
        compiler_params=pltpu.CompilerParams(dimension_semantics=("parallel",), vmem_limit_bytes=VMEM_LIMIT),
        name="router",
    )(x, wr, br, before, ident)


def _moe_ln_kernel(cnt_ref, x_ref, gate_t_ref, code_t_ref, wg_ref, wu_ref, wd_ref, g_ref, b_ref, o_ref, xb_ref, acc_ref):
    grp = pl.program_id(0)
    e = pl.program_id(1)

    @pl.when(e == 0)
    def _():
        xb_ref[...] = x_ref[...].astype(_bf16)
        acc_ref[...] = jnp.zeros_like(acc_ref)

    slot_col = lax.broadcasted_iota(jnp.int32, (MOE_BLOCK, 1), 0)
    counts = [cnt_ref[(grp * MOE_GROUP + i) * N_EXPERTS + e] for i in range(MOE_GROUP)]
    code_rows = [code_t_ref[i, pl.ds(e, 1), :] for i in range(MOE_GROUP)]
    gate_rows = [gate_t_ref[i, pl.ds(e, 1), :] for i in range(MOE_GROUP)]

    def gather_rows(i, blk):
        hit = code_rows[i] == (slot_col + blk * MOE_BLOCK).astype(_f32)
        onehot = jnp.where(hit, 1.0, 0.0).astype(_bf16)
        xs = jnp.dot(onehot, xb_ref[i * TM_MOE : (i + 1) * TM_MOE, :], preferred_element_type=_f32).astype(_bf16)
        gate = jnp.sum(jnp.where(hit, gate_rows[i], 0.0), axis=1, keepdims=True)
        return onehot, xs, gate

    def scatter_rows(i, onehot, out):
        back = lax.dot_general(onehot, out.astype(_bf16), (((0,), (0,)), ((), ())), preferred_element_type=_f32)
        acc_ref[i * TM_MOE : (i + 1) * TM_MOE, :] += back

    def single_pass(i, blk):
        onehot, xs, gate = gather_rows(i, blk)
        scatter_rows(i, onehot, _swiglu_partial(xs, wg_ref, wu_ref, wd_ref) * gate)

    assert MOE_GROUP == 2
    has0 = counts[0] > 0
    has1 = counts[1] > 0

    @pl.when(jnp.logical_and(has0, has1))
    def _():
        oh0, xs0, gate0 = gather_rows(0, 0)
        oh1, xs1, gate1 = gather_rows(1, 0)
        out = _swiglu_partial(jnp.concatenate([xs0, xs1], axis=0), wg_ref, wu_ref, wd_ref)
        scatter_rows(0, oh0, out[:MOE_BLOCK] * gate0)
        scatter_rows(1, oh1, out[MOE_BLOCK:] * gate1)

    @pl.when(jnp.logical_and(has0, jnp.logical_not(has1)))
    def _():
        single_pass(0, 0)

    @pl.when(jnp.logical_and(has1, jnp.logical_not(has0)))
    def _():
        single_pass(1, 0)

    for blk in range(1, -(-TM_MOE // MOE_BLOCK)):
        for i in range(MOE_GROUP):

            @pl.when(counts[i] > blk * MOE_BLOCK)
            def _():
                single_pass(i, blk)

    @pl.when(e == pl.num_programs(1) - 1)
    def _():
        o_ref[...] = _layer_norm(ALPHA * x_ref[...] + acc_ref[...], g_ref[...], b_ref[...])


def _moe_ln(x, gate_t, code_t, counts, wg, wu, wd, g, b):
    rows, d = x.shape
    ne, _, ff = wg.shape
    grid_spec = pltpu.PrefetchScalarGridSpec(
        num_scalar_prefetch=1,
        grid=(rows // TM, ne),
        in_specs=[
            pl.BlockSpec((TM, d), lambda i, e, cnt: (i, 0)),
            pl.BlockSpec((MOE_GROUP, N_EXPERTS, TM_MOE), lambda i, e, cnt: (i, 0, 0)),
            pl.BlockSpec((MOE_GROUP, N_EXPERTS, TM_MOE), lambda i, e, cnt: (i, 0, 0)),
            pl.BlockSpec((None, d, ff), lambda i, e, cnt: (e, 0, 0)),
            pl.BlockSpec((None, d, ff), lambda i, e, cnt: (e, 0, 0)),
            pl.BlockSpec((None, ff, d), lambda i, e, cnt: (e, 0, 0)),
            pl.BlockSpec((1, d), lambda i, e, cnt: (0, 0)),
            pl.BlockSpec((1, d), lambda i, e, cnt: (0, 0)),
        ],
        out_specs=pl.BlockSpec((TM, d), lambda i, e, cnt: (i, 0)),
        scratch_shapes=[pltpu.VMEM((TM, d), _bf16), pltpu.VMEM((TM, d), _f32)],
    )
    return pl.pallas_call(
        _moe_ln_kernel,
        grid_spec=grid_spec,
        out_shape=jax.ShapeDtypeStruct((rows, d), _f32),
        compiler_params=pltpu.CompilerParams(dimension_semantics=("parallel", "arbitrary"), vmem_limit_bytes=VMEM_LIMIT),
        name="moe_ln",
    )(counts, x, gate_t, code_t, wg, wu, wd, g, b)


def _chains(ref, c, bb, width, off=0):
    a = ref[...].astype(_f32)
    return jnp.stack([a[j * c : (j + 1) * c, off + h * width : off + (h + 1) * width] for j in range(bb) for h in range(NH)])


def _chain_cols(a, c, bb, lane0):
    return jnp.stack([a[j * c : (j + 1) * c, lane0 + h : lane0 + h + 1] for j in range(bb) for h in range(NH)])


def _chain_rows(a_t, c, bb, lane0):
    return jnp.stack([a_t[lane0 + h : lane0 + h + 1, j * c : (j + 1) * c] for j in range(bb) for h in range(NH)])


def _head_rows(ref, row, bb):
    return jnp.stack([ref[row : row + 1, h * DH : (h + 1) * DH] for _ in range(bb) for h in range(NH)])


def _store_chains(y_ref, y, c, bb, off):
    for j in range(bb):
        for h in range(NH):
            y_ref[j * c : (j + 1) * c, off + h * DH : off + (h + 1) * DH] = y[j * NH + h].astype(y_ref.dtype)
    if y_ref.shape[0] > bb * c:
        y_ref[bb * c :, off : off + NH * DH] = jnp.zeros((y_ref.shape[0] - bb * c, NH * DH), y_ref.dtype)


def _drop_alias_ref(kernel, n_in):
    def body(*refs):
        return kernel(*refs[:n_in], None, *refs[n_in:])

    return body


def _y_buffer_specs(ybuf, total_rows, rows, y_rows, row_off, row_map, width):
    if y_rows == rows:
        spec = pl.BlockSpec((rows, width), row_map(0))
    else:
        spec = pl.BlockSpec((y_rows, width), lambda i, ci: (row_off // y_rows, 0))
    return spec, jax.ShapeDtypeStruct((total_rows, width), _bf16), ([] if ybuf is None else [pl.BlockSpec(memory_space=pl.ANY)])


def _retention_chunk(q, k, v, s_prev, dec, vec):
    att = _bmm_nt(q, k) * dec
    o = _bmm(att, v) + vec[:, :, 0:1] * _bmm(q, s_prev)
    s_new = vec[:, 0:1, 2:3] * s_prev + _bmm_tn(k * vec[:, :, 1:2], v)
    return o, s_new


def _mlstm_chunk(q, k, v, it_col, it_row, lf_col, lf_row, c_prev, n_prev, m_prev, c):
    t, s = _tri3(c)
    b_col, b_row = _cumsum_col_row(lf_col, lf_row, c)
    logw = jnp.where(t >= s, b_col - b_row + it_row, -jnp.inf)
    m_t = jnp.maximum(b_col + m_prev, jnp.max(logw, axis=2, keepdims=True))
    w = jnp.exp(logw - m_t)
    carry = jnp.exp(b_col + m_prev - m_t)
    qk = _bmm_nt(q, k) * w
    num = _bmm(qk, v) + carry * _bmm(q, c_prev)
    den = jnp.sum(qk, axis=2, keepdims=True) + carry * jnp.sum(q * n_prev, axis=2, keepdims=True)
    h = num / jnp.maximum(jnp.abs(den), jnp.exp(-m_t))
    m_new = m_t[:, c - 1 : c, :]
    b_last = b_col[:, c - 1 : c, :]
    w_last = jnp.exp(b_last - b_col + it_col - m_new)
    decay = jnp.exp(b_last + m_prev - m_new)
    kw = k * w_last
    c_new = decay * c_prev + _bmm_tn(kw, v)
    n_new = decay * n_prev + jnp.sum(kw, axis=1, keepdims=True)
    return h, c_new, n_new, m_new


def _mixer0_kernel(
    qa_ref, ka_ref, va_ref, ga_ref, qb_ref, kb_ref, vb_ref, ob_ref, gate_ref, cos_ref, sin_ref,
    dec_ref, vec_ref, gb_ref, ng_ref, s0_ref, c0_ref, n0_ref, m0_ref, ybuf_ref,
    y_ref, s_ref, c_ref, n_ref, m_ref, *, c, bb,
):
    del ybuf_ref
    n = bb * NH

    @pl.when(pl.program_id(1) == 0)
    def _():
        s_ref[...] = jnp.broadcast_to(s0_ref[...], s_ref.shape)
        c_ref[...] = jnp.broadcast_to(c0_ref[...], c_ref.shape)
        n_ref[...] = jnp.broadcast_to(n0_ref[...], n_ref.shape)
        m_ref[...] = jnp.broadcast_to(m0_ref[...], m_ref.shape)

    cosf = cos_ref[...][None]
    sinf = sin_ref[...][None]
    gates = gate_ref[...] + gb_ref[...]
    lane = lax.broadcasted_iota(jnp.int32, gates.shape, 1)
    gates = jnp.where(lane < NH, gates, -_softplus(-gates))
    gates_t = jnp.transpose(gates)

    q = _chains(qa_ref, c, bb, DH)
    k = _chains(ka_ref, c, bb, DH)
    q = q * cosf + pltpu.roll(q, DH // 2, axis=2) * sinf
    k = (k * cosf + pltpu.roll(k, DH // 2, axis=2) * sinf) * DH**-0.5
    dec = jnp.concatenate([dec_ref[...]] * bb, axis=0)
    vec = jnp.concatenate([vec_ref[...]] * bb, axis=0)
    o, s_new = _retention_chunk(q, k, _chains(va_ref, c, bb, DH), s_ref[...].reshape(n, DH, DH), dec, vec)
    s_ref[...] = s_new.reshape(s_ref.shape)
    y_a = _group_norm(o, _head_rows(ng_ref, 0, bb), rms=False) * _silu(_chains(ga_ref, c, bb, DH))
    _store_chains(y_ref, y_a, c, bb, 0)

    n_prev = jnp.stack([n_ref[j, h : h + 1, :] for j in range(bb) for h in range(NH)])
    m_prev = jnp.stack([m_ref[j, h : h + 1, 0:1] for j in range(bb) for h in range(NH)])
    h_b, c_new, n_new, m_new = _mlstm_chunk(
        _chains(qb_ref, c, bb, DH), _chains(kb_ref, c, bb, DH) * DH**-0.5, _chains(vb_ref, c, bb, DH),
        _chain_cols(gates, c, bb, 0), _chain_rows(gates_t, c, bb, 0),
        _chain_cols(gates, c, bb, NH), _chain_rows(gates_t, c, bb, NH),
        c_ref[...].reshape(n, DH, DH), n_prev, m_prev, c,
    )
    c_ref[...] = c_new.reshape(c_ref.shape)
    for j in range(bb):
        for h in range(NH):
            n_ref[j, h : h + 1, :] = n_new[j * NH + h]
            m_ref[j, h : h + 1, :] = jnp.broadcast_to(m_new[j * NH + h], (1, LANES))
    h_b = _sigmoid(_chains(ob_ref, c, bb, DH)) * h_b
    y_b = _group_norm(h_b, _head_rows(ng_ref, 1, bb), rms=False)
    _store_chains(y_ref, y_b, c, bb, NH * DH)


def _retention_tables(c):
    lg = np.log(1.0 - 2.0 ** (-5.0 - np.arange(NH, dtype=np.float64)))[:, None, None]
    t = np.arange(c, dtype=np.float64)
    diff = t[None, :, None] - t[None, None, :]
    dec = np.where(diff >= 0, np.exp(np.maximum(diff, 0.0) * lg), 0.0)
    vec = np.zeros((NH, c, LANES))
    vec[:, :, 0] = np.exp((t[None, :] + 1.0) * lg[:, 0])
    vec[:, :, 1] = np.exp((c - 1.0 - t[None, :]) * lg[:, 0])
    vec[:, :, 2] = np.exp(c * lg[:, 0])
    return jnp.asarray(dec, _f32), jnp.asarray(vec, _f32)


def _group_maps(row_off, nb, nchunk, c, bb, bcast_init):
    rows = bb * c
    blk0 = row_off // rows
    nbb = nb // bb

    def row_map(col):
        return lambda i, ci: (blk0 + ci * nbb + i, col)

    def st_map(*zeros):
        return (lambda i, ci: (0,) + zeros) if bcast_init else (lambda i, ci: (i,) + zeros)

    return rows, row_map, st_map, (1 if bcast_init else bb)


def _mixer0(proj, ybuf, cosf, sinf, gb_row, norm_g, init, *, row_off, nb, nchunk, c, bb, bcast_init, y_rows=None):
    proj, tail = proj
    rows, row_map, st_map, ib = _group_maps(row_off, nb, nchunk, c, bb, bcast_init)
    w4 = NH * DH
    dec, vec = _retention_tables(c)
    y_spec, y_shape, alias_spec = _y_buffer_specs(ybuf, proj.shape[0], rows, y_rows or rows, row_off, row_map, 2 * w4)

    def const(nd):
        return lambda i, ci: (0,) * nd

    in_specs = [pl.BlockSpec((rows, w4), row_map(col)) for col in range(8)]
    in_specs += [
        pl.BlockSpec((rows, LANES), row_map(0)),
        pl.BlockSpec((c, DH), lambda i, ci: (ci, 0)),
        pl.BlockSpec((c, DH), lambda i, ci: (ci, 0)),
        pl.BlockSpec((NH, c, c), const(3)),
        pl.BlockSpec((NH, c, LANES), const(3)),
        pl.BlockSpec((1, LANES), const(2)),
        pl.BlockSpec((2, w4), const(2)),
        pl.BlockSpec((ib, NH, DH, DH), st_map(0, 0, 0)),
        pl.BlockSpec((ib, NH, DH, DH), st_map(0, 0, 0)),
        pl.BlockSpec((ib, NH, DH), st_map(0, 0)),
        pl.BlockSpec((ib, NH, LANES), st_map(0, 0)),
    ]
    out_specs = [
        y_spec,
        pl.BlockSpec((bb, NH, DH, DH), lambda i, ci: (i, 0, 0, 0)),
        pl.BlockSpec((bb, NH, DH, DH), lambda i, ci: (i, 0, 0, 0)),
        pl.BlockSpec((bb, NH, DH), lambda i, ci: (i, 0, 0)),
        pl.BlockSpec((bb, NH, LANES), lambda i, ci: (i, 0, 0)),
    ]
    out_shape = [
        y_shape,
        jax.ShapeDtypeStruct((nb, NH, DH, DH), _f32),
        jax.ShapeDtypeStruct((nb, NH, DH, DH), _f32),
        jax.ShapeDtypeStruct((nb, NH, DH), _f32),
        jax.ShapeDtypeStruct((nb, NH, LANES), _f32),
    ]
    args = [proj] * 8 + [tail, cosf, sinf, dec, vec, gb_row, norm_g, *init]
    body = functools.partial(_mixer0_kernel, c=c, bb=bb)
    return pl.pallas_call(
        body if ybuf is not None else _drop_alias_ref(body, len(args)),
        grid=(nb // bb, nchunk),
        in_specs=in_specs + alias_spec,
        out_specs=out_specs,
        out_shape=out_shape,
        input_output_aliases={len(args): 0} if ybuf is not None else {},
        compiler_params=pltpu.CompilerParams(dimension_semantics=("parallel", "arbitrary"), vmem_limit_bytes=VMEM_LIMIT),
        name=f"mixer0_c{c}",
    )(*args, *([] if ybuf is None else [ybuf]))


def _unit_lower_solve(a, rhs, c):
    bs = min(SOLVE_BLOCK, c)
    t, s = _tri3(c)
    if c > bs:
        shift = bs.bit_length() - 1
        same = jnp.right_shift(t, shift) == jnp.right_shift(s, shift)
        d = jnp.where(same, a, 0.0)
    else:
        d = a
    inv = jnp.where(t == s, 1.0, 0.0) - d
    p = d
    span = 2
    while span < bs:
        p = _bmm_x3(p, p)
        inv = inv + _bmm_x3(inv, p)
        span *= 2
    y = _bmm(inv, rhs)
    if c == bs:
        return y
    b = _bmm(inv, jnp.where(same, 0.0, a))
    y = y - _bmm(b, y)
    p = b
    span = 2
    while span < c // bs:
        p = _bmm(p, p)
        y = y + _bmm(p, y)
        span *= 2
    return y


def _gdn_chunk(q, k, v, beta_col, g_col, g_row, s_prev, c):
    t, s = _tri3(c)
    gc_col, gc_row = _cumsum_col_row(g_col, g_row, c)
    dec_incl = jnp.exp(jnp.where(t >= s, gc_col - gc_row, -jnp.inf))
    dec_strict = jnp.where(t > s, dec_incl, 0.0)
    e_col = jnp.exp(gc_col)
    a = beta_col * _bmm_nt(k, k) * dec_strict
    rhs = jnp.concatenate([beta_col * v, (beta_col * e_col) * k], axis=-1)
    sol = _unit_lower_solve(a, rhs, c)
    u = sol[:, :, :DH] - _bmm(sol[:, :, DH:], s_prev)
    qk = _bmm_nt(q, k) * dec_incl
    o = e_col * _bmm(q, s_prev) + _bmm(qk, u)
    gl = gc_col[:, c - 1 : c, :]
    s_new = jnp.exp(gl) * s_prev + _bmm_tn(k * jnp.exp(gl - gc_col), u)
    return o, s_new


def _gla_chunk(q, k, v, bc, s_prev, c):
    t, s = _tri3(c)
    qe = q * jnp.exp(bc)
    att = jnp.where(t >= s, _bmm_nt(qe, k * jnp.exp(-bc)), 0.0)
    o = _bmm(att, v) + _bmm(qe, s_prev)
    bl = bc[:, c - 1 : c, :]
    ti, si = _tri3(DK_D)
    el_col = jnp.sum(jnp.where(ti == si, jnp.exp(bl), 0.0), axis=2, keepdims=True)
    s_new = el_col * s_prev + _bmm_tn(k * jnp.exp(bl - bc), v)
    return o, s_new


def _mixer1_kernel(
    qkv_ref, z_ref, qkd_ref, vd_ref, rd_ref, small_ref, cw_ref, gp_ref, wa_ref, ba_ref, ng_ref,
    s0_ref, cv0_ref, d0_ref, ybuf_ref,
    y_ref, s_ref, cv_ref, d_ref, *, c, bb,
):
    del ybuf_ref
    n = bb * NH
    w4 = NH * DH

    @pl.when(pl.program_id(1) == 0)
    def _():
        s_ref[...] = jnp.broadcast_to(s0_ref[...], s_ref.shape)
        cv_ref[...] = jnp.broadcast_to(cv0_ref[...], cv_ref.shape)
        d_ref[...] = jnp.broadcast_to(d0_ref[...], d_ref.shape)

    small = small_ref[...]
    beta_all = _sigmoid(small)
    g_all = -jnp.exp(gp_ref[0:1, :]) * _softplus(small + gp_ref[1:2, :])
    g_all_t = jnp.transpose(g_all)
    log_alpha = -_softplus(-(_dot_hi(small[:, 0:GLA_RANK], wa_ref[...]) + ba_ref[...])) * (1.0 / GLA_TAU)
    cw = cw_ref[...]

    acts = []
    qkv = qkv_ref[...].astype(_f32)
    for j in range(bb):
        ext = jnp.concatenate([cv_ref[j], qkv[j * c : (j + 1) * c, :]], axis=0)
        conv = cw[3:4] * ext[8 : 8 + c] + cw[2:3] * ext[7 : 7 + c] + cw[1:2] * ext[6 : 6 + c] + cw[0:1] * ext[5 : 5 + c]
        cv_ref[j] = ext[c : c + 8]
        acts.append(_silu(conv))

    def act_chains(off):
        return jnp.stack([acts[j][:, off + h * DH : off + (h + 1) * DH] for j in range(bb) for h in range(NH)])

    qc = act_chains(0)
    kc = act_chains(w4)
    qc = qc * lax.rsqrt(jnp.sum(qc * qc, axis=-1, keepdims=True) + NORM_EPS) * DH**-0.5
    kc = kc * lax.rsqrt(jnp.sum(kc * kc, axis=-1, keepdims=True) + NORM_EPS)
    o, s_new = _gdn_chunk(
        qc, kc, act_chains(2 * w4),
        _chain_cols(beta_all, c, bb, GLA_RANK), _chain_cols(g_all, c, bb, GLA_RANK + NH),
        _chain_rows(g_all_t, c, bb, GLA_RANK + NH), s_ref[...].reshape(n, DH, DH), c,
    )
    s_ref[...] = s_new.reshape(s_ref.shape)
    y_c = _group_norm(o, _head_rows(ng_ref, 0, bb), rms=True) * _silu(_chains(z_ref, c, bb, DH))
    _store_chains(y_ref, y_c, c, bb, 0)

    tt = lax.broadcasted_iota(jnp.int32, (c, c), 0)
    ss = lax.broadcasted_iota(jnp.int32, (c, c), 1)
    ones_lt = jnp.where(tt >= ss, 1.0, 0.0).astype(_bf16)
    bcs = []
    for j in range(bb):
        la = log_alpha[j * c : (j + 1) * c, :]
        hi = la.astype(_bf16)
        r1 = la - hi.astype(_f32)
        mid = r1.astype(_bf16)
        lo = (r1 - mid.astype(_f32)).astype(_bf16)
        cum = functools.partial(jnp.dot, ones_lt, preferred_element_type=_f32)
        bcs.append(cum(hi) + (cum(mid) + cum(lo)))
    bc = jnp.stack([bcs[j][:, h * DK_D : (h + 1) * DK_D] for j in range(bb) for h in range(NH)])
    o, d_new = _gla_chunk(
        _chains(qkd_ref, c, bb, DK_D) * DK_D**-0.5, _chains(qkd_ref, c, bb, DK_D, off=NH * DK_D),
        _chains(vd_ref, c, bb, DH), bc, d_ref[...].reshape(n, DK_D, DH), c,
    )
    d_ref[...] = d_new.reshape(d_ref.shape)
    y_d = _group_norm(o, _head_rows(ng_ref, 1, bb), rms=False) * _silu(_chains(rd_ref, c, bb, DH))
    _store_chains(y_ref, y_d, c, bb, w4)


def _mixer1(proj, ybuf, conv_w, gdn_par, w_alpha, b_alpha, norm_g, init, *, row_off, nb, nchunk, c, bb, bcast_init, y_rows=None):
    proj, tail = proj
    rows, row_map, st_map, ib = _group_maps(row_off, nb, nchunk, c, bb, bcast_init)
    w4 = NH * DH
    y_spec, y_shape, alias_spec = _y_buffer_specs(ybuf, proj.shape[0], rows, y_rows or rows, row_off, row_map, 2 * w4)

    def const2(i, ci):
        return (0, 0)

    in_specs = [
        pl.BlockSpec((rows, QKV_C), row_map(0)),
        pl.BlockSpec((rows, w4), row_map(3)),
        pl.BlockSpec((rows, w4), row_map(4)),
        pl.BlockSpec((rows, w4), row_map(5)),
        pl.BlockSpec((rows, w4), row_map(6)),
        pl.BlockSpec((rows, LANES), row_map(0)),
        pl.BlockSpec((CONV_W, QKV_C), const2),
        pl.BlockSpec((2, LANES), const2),
        pl.BlockSpec((GLA_RANK, NH * DK_D), const2),
        pl.BlockSpec((1, NH * DK_D), const2),
        pl.BlockSpec((2, w4), const2),
        pl.BlockSpec((ib, NH, DH, DH), st_map(0, 0, 0)),
        pl.BlockSpec((ib, 8, QKV_C), st_map(0, 0)),
        pl.BlockSpec((ib, NH, DK_D, DH), st_map(0, 0, 0)),
    ]
    out_specs = [
        y_spec,
        pl.BlockSpec((bb, NH, DH, DH), lambda i, ci: (i, 0, 0, 0)),
        pl.BlockSpec((bb, 8, QKV_C), lambda i, ci: (i, 0, 0)),
        pl.BlockSpec((bb, NH, DK_D, DH), lambda i, ci: (i, 0, 0, 0)),
    ]
    out_shape = [
        y_shape,
        jax.ShapeDtypeStruct((nb, NH, DH, DH), _f32),
        jax.ShapeDtypeStruct((nb, 8, QKV_C), _f32),
        jax.ShapeDtypeStruct((nb, NH, DK_D, DH), _f32),
    ]
    args = [proj] * 5 + [tail, conv_w, gdn_par, w_alpha, b_alpha, norm_g, *init]
    body = functools.partial(_mixer1_kernel, c=c, bb=bb)
    return pl.pallas_call(
        body if ybuf is not None else _drop_alias_ref(body, len(args)),
        grid=(nb // bb, nchunk),
        in_specs=in_specs + alias_spec,
        out_specs=out_specs,
        out_shape=out_shape,
        input_output_aliases={len(args): 0} if ybuf is not None else {},
        compiler_params=pltpu.CompilerParams(dimension_semantics=("parallel", "arbitrary"), vmem_limit_bytes=VMEM_LIMIT),
        name=f"mixer1_c{c}",
    )(*args, *([] if ybuf is None else [ybuf]))


_GROUP_META = dict(row_off=ROW_META, nb=1, nchunk=1, c=N_META, bb=1, y_rows=R_ROWS - ROW_META)
_GROUP_PROMPT = dict(row_off=0, nb=BATCH, nchunk=SEQ // CHUNK, c=CHUNK, bb=4)
_GROUP_SAMPLE = dict(row_off=ROW_SAMPLE, nb=DEC_BATCH, nchunk=1, c=DEC_SEQ, bb=16)


def _rows_to_batch_major_kernel(x_ref, o_ref):
    o_ref[...] = x_ref[...].reshape(o_ref.shape)


def _prompt_rows_to_batch_major(x):
    nchunk = SEQ // CHUNK
    out = pl.pallas_call(
        _rows_to_batch_major_kernel,
        grid=(nchunk,),
        in_specs=[pl.BlockSpec((BATCH * CHUNK, D_MODEL), lambda ci: (ci, 0))],
        out_specs=pl.BlockSpec((BATCH, None, CHUNK, D_MODEL), lambda ci: (0, ci, 0, 0)),
        out_shape=jax.ShapeDtypeStruct((BATCH, nchunk, CHUNK, D_MODEL), x.dtype),
        compiler_params=pltpu.CompilerParams(dimension_semantics=("parallel",)),
        name="rows_to_batch_major",
    )(x)
    return out.reshape(BATCH, SEQ, D_MODEL)


def _rotary_tables(pos):
    half = DH // 2
    inv = ROPE_BASE ** (-jnp.arange(half, dtype=_f32) / half)
    ang = pos.astype(_f32)[:, None] * inv[None, :]
    cos, sin = jnp.cos(ang), jnp.sin(ang)
    return jnp.concatenate([cos, cos], -1), jnp.concatenate([-sin, sin], -1)


def _lanes(m):
    return jnp.broadcast_to(m.astype(_f32)[..., None], m.shape + (LANES,))


def _conv_rows(s):
    return jnp.pad(s.astype(_f32), ((0, 0), (8 - (CONV_W - 1), 0), (0, 0)))


def kernel(x_prompt, x_sample, state_ret, state_mlstm_c, state_mlstm_n, state_mlstm_m, state_gdn, state_gdn_conv, state_gla, meta_tokens, w_in0, ret_norm_g, mlstm_gate_bias, mlstm_norm_g, w_out0, ln0_mix_g, ln0_mix_b, ffn0_w_gate, ffn0_w_up, ffn0_w_down, ln0_ffn_g, ln0_ffn_b, w_in1, gdn_conv_w, gdn_a_log, gdn_dt_bias, gdn_norm_g, gla_w_alpha, gla_b_alpha, gla_norm_g, w_out1, ln1_mix_g, ln1_mix_b, moe_w_router, moe_b_router, moe_w_gate, moe_w_up, moe_w_down, ln1_ffn_g, ln1_ffn_b):
    w4 = NH * DH
    nchunk = SEQ // CHUNK
    xp = x_prompt.reshape(BATCH, nchunk, CHUNK, D_MODEL).transpose(1, 0, 2, 3).reshape(N_PROMPT, D_MODEL)
    x = jnp.concatenate(
        [
            xp,
            x_sample.reshape(N_SAMPLE, D_MODEL),
            meta_tokens.astype(x_prompt.dtype),
            jnp.zeros((R_ROWS - ROW_META - N_META, D_MODEL), x_prompt.dtype),
        ],
        0,
    )

    w_in0_p = jnp.pad(w_in0, ((0, 0), (0, P0_PAD - w_in0.shape[1]))).astype(_bf16)
    c_z_end = QKV_C + w4
    c_qd = c_z_end + 2 * NH
    c_lr = c_qd + 2 * NH * DK_D + 2 * w4
    gap = jnp.zeros((D_MODEL, P1_PAD - LANES - P1_MAIN), w_in1.dtype)
    w_in1_p = jnp.concatenate([w_in1[:, :c_z_end], w_in1[:, c_qd:c_lr], gap, w_in1[:, c_lr:], w_in1[:, c_z_end:c_qd]], 1)
    w_in1_p = jnp.pad(w_in1_p, ((0, 0), (0, P1_PAD - w_in1_p.shape[1]))).astype(_bf16)
    gb_row = jnp.pad(mlstm_gate_bias.astype(_f32), (0, LANES - 2 * NH))[None]
    norm0 = jnp.stack([ret_norm_g, mlstm_norm_g]).astype(_f32)
    norm1 = jnp.stack([gdn_norm_g, gla_norm_g]).astype(_f32)
    lo = GLA_RANK + NH
    gdn_par = jnp.stack(
        [
            jnp.pad(gdn_a_log.astype(_f32), (lo, LANES - lo - NH)),
            jnp.pad(gdn_dt_bias.astype(_f32), (lo, LANES - lo - NH)),
        ]
    )
    w_router = jnp.pad(moe_w_router.astype(_f32), ((0, 0), (0, LANES - N_EXPERTS)))
    b_router = jnp.pad(moe_b_router.astype(_f32), (0, LANES - N_EXPERTS), constant_values=-jnp.inf)[None]

    def row(v):
        return v.astype(_f32)[None]

    proj0 = _proj(x, w_in0_p, tn=P0_PAD // 3)
    zeros_even = (
        jnp.zeros((1, NH, DH, DH), _f32), jnp.zeros((1, NH, DH, DH), _f32),
        jnp.zeros((1, NH, DH), _f32), jnp.zeros((1, NH, LANES), _f32),
    )
    cos_m, sin_m = _rotary_tables(jnp.arange(N_META))
    cos_p, sin_p = _rotary_tables(N_META + jnp.arange(SEQ))
    cos_s, sin_s = _rotary_tables(PAST_LEN + jnp.arange(DEC_SEQ))
    ybuf, *meta_even = _mixer0(proj0, None, cos_m, sin_m, gb_row, norm0, zeros_even, bcast_init=True, **_GROUP_META)
    ybuf, p_ret, p_mc, p_mn, p_mm = _mixer0(proj0, ybuf, cos_p, sin_p, gb_row, norm0, meta_even, bcast_init=True, **_GROUP_PROMPT)
    init_s = (state_ret.astype(_f32), state_mlstm_c.astype(_f32), state_mlstm_n.astype(_f32), _lanes(state_mlstm_m))
    ybuf, s_ret, s_mc, s_mn, s_mm = _mixer0(proj0, ybuf, cos_s, sin_s, gb_row, norm0, init_s, bcast_init=False, **_GROUP_SAMPLE)
    x = _out_ln(ybuf, w_out0.astype(_bf16), x, row(ln0_mix_g), row(ln0_mix_b))
    x = _ffn_ln(x, ffn0_w_gate.astype(_bf16), ffn0_w_up.astype(_bf16), ffn0_w_down.astype(_bf16), row(ln0_ffn_g), row(ln0_ffn_b))

    proj1 = _proj(x, w_in1_p, tn=P1_PAD // 3)
    zeros_odd = (jnp.zeros((1, NH, DH, DH), _f32), jnp.zeros((1, 8, QKV_C), _f32), jnp.zeros((1, NH, DK_D, DH), _f32))
    m1_par = (gdn_conv_w.astype(_f32), gdn_par, gla_w_alpha.astype(_f32), row(gla_b_alpha), norm1)
    ybuf, *meta_odd = _mixer1(proj1, None, *m1_par, zeros_odd, bcast_init=True, **_GROUP_META)
    ybuf, p_gdn, p_conv, p_gla = _mixer1(proj1, ybuf, *m1_par, meta_odd, bcast_init=True, **_GROUP_PROMPT)
    init_s = (state_gdn.astype(_f32), _conv_rows(state_gdn_conv), state_gla.astype(_f32))
    ybuf, s_gdn, s_conv, s_gla = _mixer1(proj1, ybuf, *m1_par, init_s, bcast_init=False, **_GROUP_SAMPLE)
    x = _out_ln(ybuf, w_out1.astype(_bf16), x, row(ln1_mix_g), row(ln1_mix_b))
    gate_t, code_t, counts = _router(x, w_router, b_router)
    counts = counts[:, 0, :N_EXPERTS].reshape(-1)
    x = _moe_ln(x, gate_t, code_t, counts, moe_w_gate.astype(_bf16), moe_w_up.astype(_bf16), moe_w_down.astype(_bf16), row(ln1_ffn_g), row(ln1_ffn_b))

    y_prompt = _prompt_rows_to_batch_major(x)
    y_sample = x[ROW_SAMPLE:ROW_META].reshape(DEC_BATCH, DEC_SEQ, D_MODEL)
    tail = 8 - (CONV_W - 1)
    return (
        y_prompt, y_sample,
        p_ret, p_mc, p_mn, p_mm[..., 0], p_gdn, p_conv[:, tail:], p_gla,
        s_ret, s_mc, s_mn, s_mm[..., 0], s_gdn, s_conv[:, tail:], s_gla,
    )
```

```python
import functools
import math

import jax
import jax.numpy as jnp
import numpy as np
from jax import lax
from jax.experimental import pallas as pl
from jax.experimental.pallas import tpu as pltpu

D_MODEL = 1024
BATCH = 8
SEQ = 2048
DEC_BATCH = 128
DEC_SEQ = 4
PAST_LEN = 16384
N_META = 16
CHUNK = 64
NH = 4
DH = 128
DK_D = 64
CONV_W = 4
GLA_RANK = 16
GLA_TAU = 16.0
D_FF = 2816
N_EXPERTS = 8
MOE_FF = 1408
ROPE_BASE = 10000.0
LN_EPS = 1e-5
NORM_EPS = 1e-6
DEPTH = 2
ALPHA = (2 * DEPTH) ** 0.25
QKV_C = 3 * NH * DH

LANES = 128
N_PROMPT = BATCH * SEQ
N_SAMPLE = DEC_BATCH * DEC_SEQ
ROW_SAMPLE = N_PROMPT
ROW_META = N_PROMPT + N_SAMPLE
TM_MOE = 448
MOE_BLOCK = 128
MOE_GROUP = 2
TM = TM_MOE * MOE_GROUP
R_ROWS = ((ROW_META + N_META + TM - 1) // TM) * TM

P0_PAD = 33 * LANES
P1_MAIN = 28 * LANES
P1_PAD = 30 * LANES
SOLVE_BLOCK = 16

VMEM_LIMIT = 56 * 1024 * 1024

_bf16 = jnp.bfloat16
_f32 = jnp.float32
_HI = lax.Precision.HIGHEST


def _dot_hi(a, b):
    return jnp.dot(a, b, preferred_element_type=_f32, precision=_HI)


def _bmm(a, b):
    return jnp.einsum("nmk,nkp->nmp", a.astype(_bf16), b.astype(_bf16), preferred_element_type=_f32)


def _bmm_nt(a, b):
    return jnp.einsum("nmk,npk->nmp", a.astype(_bf16), b.astype(_bf16), preferred_element_type=_f32)


def _bmm_tn(a, b):
    return jnp.einsum("nkm,nkp->nmp", a.astype(_bf16), b.astype(_bf16), preferred_element_type=_f32)


def _split2(a):
    hi = a.astype(_bf16)
    lo = (a - hi.astype(_f32)).astype(_bf16)
    return hi, lo


def _bmm_x3(a, b):
    ah, al = _split2(a)
    bh, bl = _split2(b)
    mm = functools.partial(jnp.einsum, "nmk,nkp->nmp", preferred_element_type=_f32)
    return mm(ah, bh) + (mm(ah, bl) + mm(al, bh))


def _sigmoid(x):
    return 1.0 / (1.0 + jnp.exp(-x))


def _silu(x):
    return x * _sigmoid(x)


def _softplus(x):
    return jnp.maximum(x, 0.0) + jnp.log1p(jnp.exp(-jnp.abs(x)))


def _group_norm(o, g_row, rms):
    if not rms:
        o = o - jnp.mean(o, axis=-1, keepdims=True)
    return o * lax.rsqrt(jnp.mean(o * o, axis=-1, keepdims=True) + NORM_EPS) * g_row


def _layer_norm(x, g_row, b_row):
    mu = jnp.mean(x, axis=-1, keepdims=True)
    xc = x - mu
    var = jnp.mean(xc * xc, axis=-1, keepdims=True)
    return xc * lax.rsqrt(var + LN_EPS) * g_row + b_row


def _tri3(c):
    t = lax.broadcasted_iota(jnp.int32, (1, c, c), 1)
    s = lax.broadcasted_iota(jnp.int32, (1, c, c), 2)
    return t, s


def _cumsum_col_row(x_col, x_row, c):
    t, s = _tri3(c)
    col = jnp.sum(jnp.where(t >= s, x_row, 0.0), axis=2, keepdims=True)
    row = jnp.sum(jnp.where(t <= s, x_col, 0.0), axis=1, keepdims=True)
    return col, row


def _proj_kernel(x_ref, w_ref, o_ref, tail_ref, xb_ref):
    j = pl.program_id(1)

    @pl.when(j == 0)
    def _():
        xb_ref[...] = x_ref[...].astype(_bf16)

    acc = jnp.dot(xb_ref[...], w_ref[...], preferred_element_type=_f32)
    o_ref[...] = acc.astype(o_ref.dtype)

    @pl.when(j == pl.num_programs(1) - 1)
    def _():
        tail_ref[...] = acc[:, acc.shape[1] - LANES :]


def _proj(x, w, tn):
    rows, k = x.shape
    n = w.shape[1]
    tm = min(TM, rows)
    return pl.pallas_call(
        _proj_kernel,
        grid=(rows // tm, n // tn),
        in_specs=[pl.BlockSpec((tm, k), lambda i, j: (i, 0)), pl.BlockSpec((k, tn), lambda i, j: (0, j))],
        out_specs=[pl.BlockSpec((tm, tn), lambda i, j: (i, j)), pl.BlockSpec((tm, LANES), lambda i, j: (i, 0))],
        out_shape=[jax.ShapeDtypeStruct((rows, n), _bf16), jax.ShapeDtypeStruct((rows, LANES), _f32)],
        scratch_shapes=[pltpu.VMEM((tm, k), _bf16)],
        compiler_params=pltpu.CompilerParams(dimension_semantics=("parallel", "arbitrary"), vmem_limit_bytes=VMEM_LIMIT),
        name="in_proj",
    )(x, w)


def _out_ln_kernel(y_ref, w_ref, x_ref, g_ref, b_ref, o_ref):
    h = jnp.dot(y_ref[...], w_ref[...], preferred_element_type=_f32)
    o_ref[...] = _layer_norm(ALPHA * x_ref[...] + h, g_ref[...], b_ref[...])


def _out_ln(y, w, x, g, b):
    rows, k = y.shape
    d = w.shape[1]
    return pl.pallas_call(
        _out_ln_kernel,
        grid=(rows // TM,),
        in_specs=[
            pl.BlockSpec((TM, k), lambda i: (i, 0)),
            pl.BlockSpec((k, d), lambda i: (0, 0)),
            pl.BlockSpec((TM, d), lambda i: (i, 0)),
            pl.BlockSpec((1, d), lambda i: (0, 0)),
            pl.BlockSpec((1, d), lambda i: (0, 0)),
        ],
        out_specs=pl.BlockSpec((TM, d), lambda i: (i, 0)),
        out_shape=jax.ShapeDtypeStruct((rows, d), _f32),
        compiler_params=pltpu.CompilerParams(dimension_semantics=("parallel",), vmem_limit_bytes=VMEM_LIMIT),
        name="out_proj_ln",
    )(y, w, x, g, b)


FF_SUB = (0, 512, 1024, 1408)


def _swiglu_partial(xb, wg_ref, wu_ref, wd_ref):
    out = None
    for lo, hi in zip(FF_SUB[:-1], FF_SUB[1:]):
        hg = jnp.dot(xb, wg_ref[:, lo:hi], preferred_element_type=_f32)
        hu = jnp.dot(xb, wu_ref[:, lo:hi], preferred_element_type=_f32)
        a = (_silu(hg) * hu).astype(_bf16)
        part = jnp.dot(a, wd_ref[lo:hi, :], preferred_element_type=_f32)
        out = part if out is None else out + part
    return out


def _ffn_ln_kernel(x_ref, wg_ref, wu_ref, wd_ref, g_ref, b_ref, o_ref, xb_ref, acc_ref):
    f = pl.program_id(1)

    @pl.when(f == 0)
    def _():
        xb_ref[...] = x_ref[...].astype(_bf16)
        acc_ref[...] = jnp.zeros_like(acc_ref)

    acc_ref[...] += _swiglu_partial(xb_ref[...], wg_ref, wu_ref, wd_ref)

    @pl.when(f == pl.num_programs(1) - 1)
    def _():
        o_ref[...] = _layer_norm(ALPHA * x_ref[...] + acc_ref[...], g_ref[...], b_ref[...])


def _ffn_ln(x, wg, wu, wd, g, b):
    rows, d = x.shape
    ff = wg.shape[1]
    tf = MOE_FF
    return pl.pallas_call(
        _ffn_ln_kernel,
        grid=(rows // TM, ff // tf),
        in_specs=[
            pl.BlockSpec((TM, d), lambda i, f: (i, 0)),
            pl.BlockSpec((d, tf), lambda i, f: (0, f)),
            pl.BlockSpec((d, tf), lambda i, f: (0, f)),
            pl.BlockSpec((tf, d), lambda i, f: (f, 0)),
            pl.BlockSpec((1, d), lambda i, f: (0, 0)),
            pl.BlockSpec((1, d), lambda i, f: (0, 0)),
        ],
        out_specs=pl.BlockSpec((TM, d), lambda i, f: (i, 0)),
        out_shape=jax.ShapeDtypeStruct((rows, d), _f32),
        scratch_shapes=[pltpu.VMEM((TM, d), _bf16), pltpu.VMEM((TM, d), _f32)],
        compiler_params=pltpu.CompilerParams(dimension_semantics=("parallel", "arbitrary"), vmem_limit_bytes=VMEM_LIMIT),
        name="ffn_ln",
    )(x, wg, wu, wd, g, b)


def _router_kernel(x_ref, wr_ref, br_ref, before_ref, ident_ref, gate_t_ref, code_t_ref, cnt_ref):
    x = x_ref[...]
    xh, xl = _split2(x)
    wh, wl = _split2(wr_ref[...])
    mm = functools.partial(jnp.dot, preferred_element_type=_f32)
    logits = mm(xh, wh) + (mm(xh, wl) + mm(xl, wh)) + br_ref[...]
    lane = lax.broadcasted_iota(jnp.int32, logits.shape, 1)
    ex = jnp.exp(logits - jnp.max(logits, axis=-1, keepdims=True))
    probs = ex / jnp.sum(ex, axis=-1, keepdims=True)
    p1 = jnp.max(probs, axis=-1, keepdims=True)
    i1 = jnp.min(jnp.where(probs == p1, lane, LANES), axis=-1, keepdims=True)
    rest = jnp.where(lane == i1, -1.0, probs)
    p2 = jnp.max(rest, axis=-1, keepdims=True)
    i2 = jnp.min(jnp.where(rest == p2, lane, LANES), axis=-1, keepdims=True)
    tot = p1 + p2
    gates =jnp.where(lane == i1, p1 / tot, 0.0) + jnp.where(lane == i2, p2 / tot, 0.0)
    sel = jnp.where(lane == i1, 1.0, jnp.where(lane == i2, 1.0, 0.0))
    selb = sel.astype(_bf16)
    g_hi = gates.astype(_bf16)
    r1 = gates - g_hi.astype(_f32)
    g_mid = r1.astype(_bf16)
    g_lo = (r1 - g_mid.astype(_f32)).astype(_bf16)
    stacked = jnp.concatenate([selb, g_hi, g_mid, g_lo], axis=1)
    stacked_t = lax.dot_general(stacked, ident_ref[...], (((0,), (0,)), ((), ())), preferred_element_type=_f32)
    sel_t = stacked_t[0:N_EXPERTS, :]
    rank_t = jnp.dot(sel_t.astype(_bf16), before_ref[...], preferred_element_type=_f32)
    code_t_ref[0] = jnp.where(sel_t > 0.0, rank_t, -1.0)
    parts = [stacked_t[k * LANES : k * LANES + N_EXPERTS, :] for k in (1, 2, 3)]
    gate_t_ref[0] = parts[0] + (parts[1] + parts[2])
    cnt_ref[0] = jnp.sum(sel, axis=0, keepdims=True).astype(jnp.int32)


def _router(x, wr, br):
    rows, d = x.shape
    nt = rows // TM_MOE
    t = np.arange(TM_MOE)
    before = jnp.asarray(t[:, None] < t[None, :], _bf16)
    ident = jnp.asarray(t[:, None] == t[None, :], _bf16)
    return pl.pallas_call(
        _router_kernel,
        grid=(nt,),
        in_specs=[
            pl.BlockSpec((TM_MOE, d), lambda i: (i, 0)),
            pl.BlockSpec((d, LANES), lambda i: (0, 0)),
            pl.BlockSpec((1, LANES), lambda i: (0, 0)),
            pl.BlockSpec((TM_MOE, TM_MOE), lambda i: (0, 0)),
            pl.BlockSpec((TM_MOE, TM_MOE), lambda i: (0, 0)),
        ],
        out_specs=[
            pl.BlockSpec((1, N_EXPERTS, TM_MOE), lambda i: (i, 0, 0)),
            pl.BlockSpec((1, N_EXPERTS, TM_MOE), lambda i: (i, 0, 0)),
            pl.BlockSpec((1, 1, LANES), lambda i: (i, 0, 0)),
        ],
        out_shape=[
            jax.ShapeDtypeStruct((nt, N_EXPERTS, TM_MOE), _f32),
            jax.ShapeDtypeStruct((nt, N_EXPERTS, TM_MOE), _f32),
            jax.ShapeDtypeStruct((nt, 1, LANES), jnp.int32),
        ],
        compiler_params=pltpu.CompilerParams(dimension_semantics=("parallel",), vmem_limit_bytes=VMEM_LIMIT),
        name="router",
    )(x, wr, br, before, ident)


def _moe_ln_kernel(cnt_ref, x_ref, gate_t_ref, code_t_ref, wg_ref, wu_ref, wd_ref, g_ref, b_ref, o_ref, xb_ref, acc_ref):
    grp = pl.program_id(0)
    e = pl.program_id(1)

    @pl.when(e == 0)
    def _():
        xb_ref[...] = x_ref[...].astype(_bf16)
        acc_ref[...] = jnp.zeros_like(acc_ref)

    slot_col = lax.broadcasted_iota(jnp.int32, (MOE_BLOCK, 1), 0)
    counts = [cnt_ref[(grp * MOE_GROUP + i) * N_EXPERTS + e] for i in range(MOE_GROUP)]
    code_rows = [code_t_ref[i, pl.ds(e, 1), :] for i in range(MOE_GROUP)]
    gate_rows = [gate_t_ref[i, pl.ds(e, 1), :] for i in range(MOE_GROUP)]

    def gather_rows(i, blk):
        hit = code_rows[i] == (slot_col + blk * MOE_BLOCK).astype(_f32)
        onehot = jnp.where(hit, 1.0, 0.0).astype(_bf16)
        xs = jnp.dot(onehot, xb_ref[i * TM_MOE : (i + 1) * TM_MOE, :], preferred_element_type=_f32).astype(_bf16)
        gate = jnp.sum(jnp.where(hit, gate_rows[i], 0.0), axis=1, keepdims=True)
        return onehot, xs, gate

    def scatter_rows(i, onehot, out):
        back = lax.dot_general(onehot, out.astype(_bf16), (((0,), (0,)), ((), ())), preferred_element_type=_f32)
        acc_ref[i * TM_MOE : (i + 1) * TM_MOE, :] += back

    def single_pass(i, blk):
        onehot, xs, gate = gather_rows(i, blk)
        scatter_rows(i, onehot, _swiglu_partial(xs, wg_ref, wu_ref, wd_ref) * gate)

    assert MOE_GROUP == 2
    has0 = counts[0] > 0
    has1 = counts[1] > 0

    @pl.when(jnp.logical_and(has0, has1))
    def _():
        oh0, xs0, gate0 = gather_rows(0, 0)
        oh1, xs1, gate1 = gather_rows(1, 0)
        out = _swiglu_partial(jnp.concatenate([xs0, xs1], axis=0), wg_ref, wu_ref, wd_ref)
        scatter_rows(0, oh0, out[:MOE_BLOCK] * gate0)
        scatter_rows(1, oh1, out[MOE_BLOCK:] * gate1)

    @pl.when(jnp.logical_and(has0, jnp.logical_not(has1)))
    def _():
        single_pass(0, 0)

    @pl.when(jnp.logical_and(has1, jnp.logical_not(has0)))
    def _():
        single_pass(1, 0)

    for blk in range(1, -(-TM_MOE // MOE_BLOCK)):
        for i in range(MOE_GROUP):

            @pl.when(counts[i] > blk * MOE_BLOCK)
            def _():
                single_pass(i, blk)

    @pl.when(e == pl.num_programs(1) - 1)
    def _():
        o_ref[...] = _layer_norm(ALPHA * x_ref[...] + acc_ref[...], g_ref[...], b_ref[...])


def _moe_ln(x, gate_t, code_t, counts, wg, wu, wd, g, b):
    rows, d = x.shape
    ne, _, ff = wg.shape
    grid_spec = pltpu.PrefetchScalarGridSpec(
        num_scalar_prefetch=1,
        grid=(rows // TM, ne),
        in_specs=[
            pl.BlockSpec((TM, d), lambda i, e, cnt: (i, 0)),
            pl.BlockSpec((MOE_GROUP, N_EXPERTS, TM_MOE), lambda i, e, cnt: (i, 0, 0)),
            pl.BlockSpec((MOE_GROUP, N_EXPERTS, TM_MOE), lambda i, e, cnt: (i, 0, 0)),
            pl.BlockSpec((None, d, ff), lambda i, e, cnt: (e, 0, 0)),
            pl.BlockSpec((None, d, ff), lambda i, e, cnt: (e, 0, 0)),
            pl.BlockSpec((None, ff, d), lambda i, e, cnt: (e, 0, 0)),
            pl.BlockSpec((1, d), lambda i, e, cnt: (0, 0)),
            pl.BlockSpec((1, d), lambda i, e, cnt: (0, 0)),
        ],
        out_specs=pl.BlockSpec((TM, d), lambda i, e, cnt: (i, 0)),
        scratch_shapes=[pltpu.VMEM((TM, d), _bf16), pltpu.VMEM((TM, d), _f32)],
    )
    return pl.pallas_call(
        _moe_ln_kernel,
        grid_spec=grid_spec,
        out_shape=jax.ShapeDtypeStruct((rows, d), _f32),
        compiler_params=pltpu.CompilerParams(dimension_semantics=("parallel", "arbitrary"), vmem_limit_bytes=VMEM_LIMIT),
        name="moe_ln",
    )(counts, x, gate_t, code_t, wg, wu, wd, g, b)


def _chains(a, c, bb, width, off=0):
    return jnp.stack([a[j * c : (j + 1) * c, off + h * width : off + (h + 1) * width] for j in range(bb) for h in range(NH)])


def _chain_cols(a, c, bb, lane0):
    return jnp.stack([a[j * c : (j + 1) * c, lane0 + h : lane0 + h + 1] for j in range(bb) for h in range(NH)])


def _chain_rows(a_t, c, bb, lane0):
    return jnp.stack([a_t[lane0 + h : lane0 + h + 1, j * c : (j + 1) * c] for j in range(bb) for h in range(NH)])


def _head_rows(ref, row, bb):
    return jnp.stack([ref[row : row + 1, h * DH : (h + 1) * DH] for _ in range(bb) for h in range(NH)])


def _store_chains(y_ref, y, c, bb, off):
    for j in range(bb):
        for h in range(NH):
            y_ref[j * c : (j + 1) * c, off + h * DH : off + (h + 1) * DH] = y[j * NH + h].astype(y_ref.dtype)
    if y_ref.shape[0] > bb * c:
        y_ref[bb * c :, off : off + NH * DH] = jnp.zeros((y_ref.shape[0] - bb * c, NH * DH), y_ref.dtype)


def _drop_alias_ref(kernel, n_in):
    def body(*refs):
        return kernel(*refs[:n_in], None, *refs[n_in:])

    return body


def _y_buffer_specs(ybuf, rows, y_rows, y_row_off, nbb, width):
    if y_rows == rows:
        spec = pl.BlockSpec((rows, width), lambda i, ci: (y_row_off // rows + ci * nbb + i, 0))
    else:
        spec = pl.BlockSpec((y_rows, width), lambda i, ci: (y_row_off // y_rows, 0))
    return spec, jax.ShapeDtypeStruct((R_ROWS, width), _bf16), ([] if ybuf is None else [pl.BlockSpec(memory_space=pl.ANY)])


PIECE = NH * DH


def _stream_projection(xn_ref, xc_ref, w_ref, main_sc, tail_sc, xb_sc, n_piece):
    rows = xn_ref.shape[0]
    ci = pl.program_id(1)
    tail_col = w_ref.shape[1] - LANES

    def project(xb, row0, p):
        if p < n_piece:
            cols = slice(p * PIECE, (p + 1) * PIECE)
            main_sc[pl.ds(row0, rows), cols] = jnp.dot(xb, w_ref[:, cols], preferred_element_type=_f32).astype(_bf16)
        else:
            tail_sc[pl.ds(row0, rows), :] = jnp.dot(xb, w_ref[:, tail_col:], preferred_element_type=_f32)

    @pl.when(ci == 0)
    def _():
        xb0 = xc_ref[...].astype(_bf16)
        for p in range(n_piece + 1):
            project(xb0, 0, p)

    cur = pl.multiple_of((ci % 2) * rows, rows)
    nxt = pl.multiple_of(((ci + 1) % 2) * rows, rows)
    xb_sc[...] = xn_ref[...].astype(_bf16)
    pending = iter(range(n_piece + 1))

    def between(count=1):
        for _ in range(count):
            p = next(pending, None)
            if p is not None:
                project(xb_sc[...], nxt, p)

    def load(p):
        return main_sc[pl.ds(cur, rows), p * PIECE : (p + 1) * PIECE].astype(_f32)

    return load, tail_sc[pl.ds(cur, rows), :], between


def _retention_chunk(q, k, v, s_prev, dec, vec):
    att = _bmm_nt(q, k) * dec
    o = _bmm(att, v) + vec[:, :, 0:1] * _bmm(q, s_prev)
    s_new = vec[:, 0:1, 2:3] * s_prev + _bmm_tn(k * vec[:, :, 1:2], v)
    return o, s_new


def _mlstm_chunk(q, k, v, it_col, it_row, lf_col, lf_row, c_prev, n_prev, m_prev, c):
    t, s = _tri3(c)
    b_col, b_row = _cumsum_col_row(lf_col, lf_row, c)
    logw = jnp.where(t >= s, b_col - b_row + it_row, -jnp.inf)
    m_t = jnp.maximum(b_col + m_prev, jnp.max(logw, axis=2, keepdims=True))
    w = jnp.exp(logw - m_t)
    carry = jnp.exp(b_col + m_prev - m_t)
    qk = _bmm_nt(q, k) * w
    num = _bmm(qk, v) + carry * _bmm(q, c_prev)
    den = jnp.sum(qk, axis=2, keepdims=True) + carry * jnp.sum(q * n_prev, axis=2, keepdims=True)
    h = num / jnp.maximum(jnp.abs(den), jnp.exp(-m_t))
    m_new = m_t[:, c - 1 : c, :]
    b_last = b_col[:, c - 1 : c, :]
    w_last = jnp.exp(b_last - b_col + it_col - m_new)
    decay = jnp.exp(b_last + m_prev - m_new)
    kw = k * w_last
    c_new = decay * c_prev + _bmm_tn(kw, v)
    n_new = decay * n_prev + jnp.sum(kw, axis=1, keepdims=True)
    return h, c_new, n_new, m_new


N_PIECE0 = 8


def _mixer0_body(
    load, gate_tail, between,
    cos_ref, sin_ref, dec_ref, vec_ref, gb_ref, ng_ref, s0_ref, c0_ref, n0_ref, m0_ref,
    y_ref, s_ref, c_ref, n_ref, m_ref, *, c, bb,
):
    n = bb * NH

    @pl.when(pl.program_id(1) == 0)
    def _():
        s_ref[...] = jnp.broadcast_to(s0_ref[...], s_ref.shape)
        c_ref[...] = jnp.broadcast_to(c0_ref[...], c_ref.shape)
        n_ref[...] = jnp.broadcast_to(n0_ref[...], n_ref.shape)
        m_ref[...] = jnp.broadcast_to(m0_ref[...], m_ref.shape)

    def chains(p):
        return _chains(load(p), c, bb, DH)

    cosf = cos_ref[...][None]
    sinf = sin_ref[...][None]
    gates = gate_tail + gb_ref[...]
    lane = lax.broadcasted_iota(jnp.int32, gates.shape, 1)
    gates = jnp.where(lane < NH, gates, -_softplus(-gates))
    gates_t = jnp.transpose(gates)

    q = chains(0)
    k = chains(1)
    q = q * cosf + pltpu.roll(q, DH // 2, axis=2) * sinf
    k = (k * cosf + pltpu.roll(k, DH // 2, axis=2) * sinf) * DH**-0.5
    between(2)
    dec = jnp.concatenate([dec_ref[...]] * bb, axis=0)
    vec = jnp.concatenate([vec_ref[...]] * bb, axis=0)
    o, s_new = _retention_chunk(q, k, chains(2), s_ref[...].reshape(n, DH, DH), dec, vec)
    between(2)
    s_ref[...] = s_new.reshape(s_ref.shape)
    y_a = _group_norm(o, _head_rows(ng_ref, 0, bb), rms=False) * _silu(chains(3))
    _store_chains(y_ref, y_a, c, bb, 0)
    between(1)

    n_prev = jnp.stack([n_ref[j, h : h + 1, :] for j in range(bb) for h in range(NH)])
    m_prev = jnp.stack([m_ref[j, h : h + 1, 0:1] for j in range(bb) for h in range(NH)])
    h_b, c_new, n_new, m_new = _mlstm_chunk(
        chains(4), chains(5) * DH**-0.5, chains(6),
        _chain_cols(gates, c, bb, 0), _chain_rows(gates_t, c, bb, 0),
        _chain_cols(gates, c, bb, NH), _chain_rows(gates_t, c, bb, NH),
        c_ref[...].reshape(n, DH, DH), n_prev, m_prev, c,
    )
    between(2)
    c_ref[...] = c_new.reshape(c_ref.shape)
    for j in range(bb):
        for h in range(NH):
            n_ref[j, h : h + 1, :] = n_new[j * NH + h]
            m_ref[j, h : h + 1, :] = jnp.broadcast_to(m_new[j * NH + h], (1, LANES))
    h_b = _sigmoid(chains(7)) * h_b
    between(1)
    y_b = _group_norm(h_b, _head_rows(ng_ref, 1, bb), rms=False)
    _store_chains(y_ref, y_b, c, bb, NH * DH)
    between(N_PIECE0 + 1)


def _no_op(count=1):
    del count


def _mixer0_kernel(*refs, c, bb):
    pieces, gate_ref, rest = refs[:N_PIECE0], refs[N_PIECE0], refs[N_PIECE0 + 1 :]
    consts, outs = rest[:10], rest[11:]
    _mixer0_body(lambda p: pieces[p][...].astype(_f32), gate_ref[...], _no_op, *consts, *outs, c=c, bb=bb)


def _mixer0_stream_kernel(xn_ref, xc_ref, w_ref, *rest, c, bb):
    consts, outs, scratch = rest[:10], rest[11:16], rest[16:]
    load, gate_tail, between = _stream_projection(xn_ref, xc_ref, w_ref, *scratch, N_PIECE0)
    _mixer0_body(load, gate_tail, between, *consts, *outs, c=c, bb=bb)


def _retention_tables(c):
    lg = np.log(1.0 - 2.0 ** (-5.0 - np.arange(NH, dtype=np.float64)))[:, None, None]
    t = np.arange(c, dtype=np.float64)
    diff = t[None, :, None] - t[None, None, :]
    dec = np.where(diff >= 0, np.exp(np.maximum(diff, 0.0) * lg), 0.0)
    vec = np.zeros((NH, c, LANES))
    vec[:, :, 0] = np.exp((t[None, :] + 1.0) * lg[:, 0])
    vec[:, :, 1] = np.exp((c - 1.0 - t[None, :]) * lg[:, 0])
    vec[:, :, 2] = np.exp(c * lg[:, 0])
    return jnp.asarray(dec, _f32), jnp.asarray(vec, _f32)


def _whole(a):
    return a, pl.BlockSpec(a.shape, lambda i, ci: (0,) * a.ndim)


def _mixer_call(name, kernels, source, n_piece, consts, init, ybuf, *, row_off, y_row_off, nb, nchunk, c, bb, bcast_init, y_rows=None, stream=False):
    rows = bb * c
    nbb = nb // bb
    blk0 = row_off // rows
    ib = 1 if bcast_init else bb
    y_spec, y_shape, alias_spec = _y_buffer_specs(ybuf, rows, y_rows or rows, y_row_off, nbb, 2 * NH * DH)

    def row_block(col):
        return lambda i, ci: (blk0 + ci * nbb + i, col)

    def batch_block(nd, lead):
        return lambda i, ci: ((i if lead else 0),) + (0,) * nd

    if stream:
        x, w = source
        src_args = [x, x, w]
        src_specs = [
            pl.BlockSpec((rows, D_MODEL), lambda i, ci: (blk0 + jnp.minimum(ci + 1, nchunk - 1) * nbb + i, 0)),
            pl.BlockSpec((rows, D_MODEL), lambda i, ci: (blk0 + i, 0)),
            pl.BlockSpec(w.shape, lambda i, ci: (0, 0)),
        ]
        scratch = [pltpu.VMEM((2 * rows, n_piece * PIECE), _bf16), pltpu.VMEM((2 * rows, LANES), _f32), pltpu.VMEM((rows, D_MODEL), _bf16)]
    else:
        proj, tail = source
        src_args = [proj] * n_piece + [tail]
        src_specs = [pl.BlockSpec((rows, PIECE), row_block(p)) for p in range(n_piece)] + [pl.BlockSpec((rows, LANES), row_block(0))]
        scratch = []
    in_specs = src_specs + [spec for _, spec in consts]
    in_specs += [pl.BlockSpec((ib,) + a.shape[1:], batch_block(a.ndim - 1, not bcast_init)) for a in init]
    out_specs = [y_spec] + [pl.BlockSpec((bb,) + a.shape[1:], batch_block(a.ndim - 1, True)) for a in init]
    out_shape = [y_shape] + [jax.ShapeDtypeStruct((nb,) + a.shape[1:], _f32) for a in init]
    args = src_args + [a for a, _ in consts] + list(init)
    body = functools.partial(kernels[int(stream)], c=c, bb=bb)
    return pl.pallas_call(
        body if ybuf is not None else _drop_alias_ref(body, len(args)),
        grid=(nbb, nchunk),
        in_specs=in_specs + alias_spec,
        out_specs=out_specs,
        out_shape=out_shape,
        scratch_shapes=scratch,
        input_output_aliases={len(args): 0} if ybuf is not None else {},
        compiler_params=pltpu.CompilerParams(dimension_semantics=("parallel", "arbitrary"), vmem_limit_bytes=VMEM_LIMIT),
        name=f"{name}_c{c}",
    )(*args, *([] if ybuf is None else [ybuf]))


def _mixer0(source, ybuf, cosf, sinf, gb_row, norm_g, init, **group):
    c = group["c"]
    dec, vec = _retention_tables(c)
    by_chunk = pl.BlockSpec((c, DH), lambda i, ci: (ci, 0))
    consts = [(cosf, by_chunk), (sinf, by_chunk), _whole(dec), _whole(vec), _whole(gb_row), _whole(norm_g)]
    return _mixer_call("mixer0", (_mixer0_kernel, _mixer0_stream_kernel), source, N_PIECE0, consts, init, ybuf, **group)


def _unit_lower_solve(a, rhs, c):
    bs = min(SOLVE_BLOCK, c)
    t, s = _tri3(c)
    if c > bs:
        shift = bs.bit_length() - 1
        same = jnp.right_shift(t, shift) == jnp.right_shift(s, shift)
        d = jnp.where(same, a, 0.0)
    else:
        d = a
    inv = jnp.where(t == s, 1.0, 0.0) - d
    p = d
    span = 2
    while span < bs:
        p = _bmm_x3(p, p)
        inv = inv + _bmm_x3(inv, p)
        span *= 2
    y = _bmm(inv, rhs)
    if c == bs:
        return y
    b = _bmm(inv, jnp.where(same, 0.0, a))
    y = y - _bmm(b, y)
    p = b
    span = 2
    while span < c // bs:
        p = _bmm(p, p)
        y = y + _bmm(p, y)
        span *= 2
    return y


def _gdn_chunk(q, k, v, beta_col, g_col, g_row, s_prev, c, between=_no_op):
    t, s = _tri3(c)
    gc_col, gc_row = _cumsum_col_row(g_col, g_row, c)
    dec_incl = jnp.exp(jnp.where(t >= s, gc_col - gc_row, -jnp.inf))
    dec_strict = jnp.where(t > s, dec_incl, 0.0)
    e_col = jnp.exp(gc_col)
    a = beta_col * _bmm_nt(k, k) * dec_strict
    rhs = jnp.concatenate([beta_col * v, (beta_col * e_col) * k], axis=-1)
    between(1)
    sol = _unit_lower_solve(a, rhs, c)
    between(1)
    u = sol[:, :, :DH] - _bmm(sol[:, :, DH:], s_prev)
    qk = _bmm_nt(q, k) * dec_incl
    between(1)
    o = e_col * _bmm(q, s_prev) + _bmm(qk, u)
    gl = gc_col[:, c - 1 : c, :]
    s_new = jnp.exp(gl) * s_prev + _bmm_tn(k * jnp.exp(gl - gc_col), u)
    return o, s_new


def _gla_chunk(q, k, v, bc, s_prev, c):
    t, s = _tri3(c)
    qe = q * jnp.exp(bc)
    att = jnp.where(t >= s, _bmm_nt(qe, k * jnp.exp(-bc)), 0.0)
    o = _bmm(att, v) + _bmm(qe, s_prev)
    bl = bc[:, c - 1 : c, :]
    ti, si = _tri3(DK_D)
    el_col = jnp.sum(jnp.where(ti == si, jnp.exp(bl), 0.0), axis=2, keepdims=True)
    s_new = el_col * s_prev + _bmm_tn(k * jnp.exp(bl - bc), v)
    return o, s_new


N_PIECE1 = 7


def _mixer1_body(
    load, small, between,
    cw_ref, gp_ref, wa_ref, ba_ref, ng_ref, s0_ref, cv0_ref, d0_ref,
    y_ref, s_ref, cv_ref, d_ref, *, c, bb,
):
    n = bb * NH
    w4 = NH * DH

    @pl.when(pl.program_id(1) == 0)
    def _():
        s_ref[...] = jnp.broadcast_to(s0_ref[...], s_ref.shape)
        cv_ref[...] = jnp.broadcast_to(cv0_ref[...], cv_ref.shape)
        d_ref[...] = jnp.broadcast_to(d0_ref[...], d_ref.shape)

    beta_all = _sigmoid(small)
    g_all = -jnp.exp(gp_ref[0:1, :]) * _softplus(small + gp_ref[1:2, :])
    g_all_t = jnp.transpose(g_all)
    log_alpha = -_softplus(-(_dot_hi(small[:, 0:GLA_RANK], wa_ref[...]) + ba_ref[...])) * (1.0 / GLA_TAU)
    cw = cw_ref[...]

    acts = []
    qkv = jnp.concatenate([load(0), load(1), load(2)], axis=1)
    between(1)
    for j in range(bb):
        ext = jnp.concatenate([cv_ref[j], qkv[j * c : (j + 1) * c, :]], axis=0)
        conv = cw[3:4] * ext[8 : 8 + c] + cw[2:3] * ext[7 : 7 + c] + cw[1:2] * ext[6 : 6 + c] + cw[0:1] * ext[5 : 5 + c]
        cv_ref[j] = ext[c : c + 8]
        acts.append(_silu(conv))

    def act_chains(off):
        return jnp.stack([acts[j][:, off + h * DH : off + (h + 1) * DH] for j in range(bb) for h in range(NH)])

    qc = act_chains(0)
    kc = act_chains(w4)
    qc = qc * lax.rsqrt(jnp.sum(qc * qc, axis=-1, keepdims=True) + NORM_EPS) * DH**-0.5
    kc = kc * lax.rsqrt(jnp.sum(kc * kc, axis=-1, keepdims=True) + NORM_EPS)
    between(1)
    o, s_new = _gdn_chunk(
        qc, kc, act_chains(2 * w4),
        _chain_cols(beta_all, c, bb, GLA_RANK), _chain_cols(g_all, c, bb, GLA_RANK + NH),
        _chain_rows(g_all_t, c, bb, GLA_RANK + NH), s_ref[...].reshape(n, DH, DH), c, between,
    )
    s_ref[...] = s_new.reshape(s_ref.shape)
    y_c = _group_norm(o, _head_rows(ng_ref, 0, bb), rms=True) * _silu(_chains(load(3), c, bb, DH))
    _store_chains(y_ref, y_c, c, bb, 0)
    between(1)

    tt = lax.broadcasted_iota(jnp.int32, (c, c), 0)
    ss = lax.broadcasted_iota(jnp.int32, (c, c), 1)
    ones_lt = jnp.where(tt >= ss, 1.0, 0.0).astype(_bf16)
    bcs = []
    for j in range(bb):
        la = log_alpha[j * c : (j + 1) * c, :]
        hi = la.astype(_bf16)
        r1 = la - hi.astype(_f32)
        mid = r1.astype(_bf16)
        lo = (r1 - mid.astype(_f32)).astype(_bf16)
        cum = functools.partial(jnp.dot, ones_lt, preferred_element_type=_f32)
        bcs.append(cum(hi) + (cum(mid) + cum(lo)))
    bc = jnp.stack([bcs[j][:, h * DK_D : (h + 1) * DK_D] for j in range(bb) for h in range(NH)])
    qkd = load(4)
    between(1)
    o, d_new = _gla_chunk(
        _chains(qkd, c, bb, DK_D) * DK_D**-0.5, _chains(qkd, c, bb, DK_D, off=NH * DK_D),
        _chains(load(5), c, bb, DH), bc, d_ref[...].reshape(n, DK_D, DH), c,
    )
    d_ref[...] = d_new.reshape(d_ref.shape)
    between(1)
    y_d = _group_norm(o, _head_rows(ng_ref, 1, bb), rms=False) * _silu(_chains(load(6), c, bb, DH))
    _store_chains(y_ref, y_d, c, bb, w4)
    between(N_PIECE1 + 1)


def _mixer1_kernel(*refs, c, bb):
    pieces, small_ref, rest = refs[:N_PIECE1], refs[N_PIECE1], refs[N_PIECE1 + 1 :]
    consts, outs = rest[:8], rest[9:]
    _mixer1_body(lambda p: pieces[p][...].astype(_f32), small_ref[...], _no_op, *consts, *outs, c=c, bb=bb)


def _mixer1_stream_kernel(xn_ref, xc_ref, w_ref, *rest, c, bb):
    consts, outs, scratch = rest[:8], rest[9:13], rest[13:]
    load, small, between = _stream_projection(xn_ref, xc_ref, w_ref, *scratch, N_PIECE1)
    _mixer1_body(load, small, between, *consts, *outs, c=c, bb=bb)


def _mixer1(source, ybuf, conv_w, gdn_par, w_alpha, b_alpha, norm_g, init, **group):
    consts = [_whole(a) for a in (conv_w, gdn_par, w_alpha, b_alpha, norm_g)]
    return _mixer_call("mixer1", (_mixer1_kernel, _mixer1_stream_kernel), source, N_PIECE1, consts, init, ybuf, **group)


_GROUP_META = dict(
    row_off=ROW_META - ROW_SAMPLE, y_row_off=ROW_META, nb=1, nchunk=1, c=N_META, bb=1, bcast_init=True,
    y_rows=R_ROWS - ROW_META,
)
_GROUP_PROMPT = dict(row_off=0, y_row_off=0, nb=BATCH, nchunk=SEQ // CHUNK, c=CHUNK, bb=4, bcast_init=True, stream=True)
_GROUP_SAMPLE = dict(row_off=0, y_row_off=ROW_SAMPLE, nb=DEC_BATCH, nchunk=1, c=DEC_SEQ, bb=16, bcast_init=False)


def _rows_to_batch_major_kernel(x_ref, o_ref):
    o_ref[...] = x_ref[...].reshape(o_ref.shape)


def _prompt_rows_to_batch_major(x):
    nchunk = SEQ // CHUNK
    out = pl.pallas_call(
        _rows_to_batch_major_kernel,
        grid=(nchunk,),
        in_specs=[pl.BlockSpec((BATCH * CHUNK, D_MODEL), lambda ci: (ci, 0))],
        out_specs=pl.BlockSpec((BATCH, None, CHUNK, D_MODEL), lambda ci: (0, ci, 0, 0)),
        out_shape=jax.ShapeDtypeStruct((BATCH, nchunk, CHUNK, D_MODEL), x.dtype),
        compiler_params=pltpu.CompilerParams(dimension_semantics=("parallel",)),
        name="rows_to_batch_major",
    )(x)
    return out.reshape(BATCH, SEQ, D_MODEL)


def _rotary_tables(pos):
    half = DH // 2
    inv = ROPE_BASE ** (-jnp.arange(half, dtype=_f32) / half)
    ang = pos.astype(_f32)[:, None] * inv[None, :]
    cos, sin = jnp.cos(ang), jnp.sin(ang)
    return jnp.concatenate([cos, cos], -1), jnp.concatenate([-sin, sin], -1)


def _lanes(m):
    return jnp.broadcast_to(m.astype(_f32)[..., None], m.shape + (LANES,))


def _conv_rows(s):
    return jnp.pad(s.astype(_f32), ((0, 0), (8 - (CONV_W - 1), 0), (0, 0)))


def kernel(x_prompt, x_sample, state_ret, state_mlstm_c, state_mlstm_n, state_mlstm_m, state_gdn, state_gdn_conv, state_gla, meta_tokens, w_in0, ret_norm_g, mlstm_gate_bias, mlstm_norm_g, w_out0, ln0_mix_g, ln0_mix_b, ffn0_w_gate, ffn0_w_up, ffn0_w_down, ln0_ffn_g, ln0_ffn_b, w_in1, gdn_conv_w, gdn_a_log, gdn_dt_bias, gdn_norm_g, gla_w_alpha, gla_b_alpha, gla_norm_g, w_out1, ln1_mix_g, ln1_mix_b, moe_w_router, moe_b_router, moe_w_gate, moe_w_up, moe_w_down, ln1_ffn_g, ln1_ffn_b):
    w4 = NH * DH
    nchunk = SEQ // CHUNK
    xp = x_prompt.reshape(BATCH, nchunk, CHUNK, D_MODEL).transpose(1, 0, 2, 3).reshape(N_PROMPT, D_MODEL)
    x = jnp.concatenate(
        [
            xp,
            x_sample.reshape(N_SAMPLE, D_MODEL),
            meta_tokens.astype(x_prompt.dtype),
            jnp.zeros((R_ROWS - ROW_META - N_META, D_MODEL), x_prompt.dtype),
        ],
        0,
    )

    w_in0_p = jnp.pad(w_in0, ((0, 0), (0, P0_PAD - w_in0.shape[1]))).astype(_bf16)
    c_z_end = QKV_C + w4
    c_qd = c_z_end + 2 * NH
    c_lr = c_qd + 2 * NH * DK_D + 2 * w4
    gap = jnp.zeros((D_MODEL, P1_PAD - LANES - P1_MAIN), w_in1.dtype)
    w_in1_p = jnp.concatenate([w_in1[:, :c_z_end], w_in1[:, c_qd:c_lr], gap, w_in1[:, c_lr:], w_in1[:, c_z_end:c_qd]], 1)
    w_in1_p = jnp.pad(w_in1_p, ((0, 0), (0, P1_PAD - w_in1_p.shape[1]))).astype(_bf16)
    gb_row = jnp.pad(mlstm_gate_bias.astype(_f32), (0, LANES - 2 * NH))[None]
    norm0 = jnp.stack([ret_norm_g, mlstm_norm_g]).astype(_f32)
    norm1 = jnp.stack([gdn_norm_g, gla_norm_g]).astype(_f32)
    lo = GLA_RANK + NH
    gdn_par = jnp.stack(
        [
            jnp.pad(gdn_a_log.astype(_f32), (lo, LANES - lo - NH)),
            jnp.pad(gdn_dt_bias.astype(_f32), (lo, LANES - lo - NH)),
        ]
    )
    w_router = jnp.pad(moe_w_router.astype(_f32), ((0, 0), (0, LANES - N_EXPERTS)))
    b_router = jnp.pad(moe_b_router.astype(_f32), (0, LANES - N_EXPERTS), constant_values=-jnp.inf)[None]

    def row(v):
        return v.astype(_f32)[None]

    proj0 = _proj(x[ROW_SAMPLE:], w_in0_p, tn=P0_PAD // 3)
    zeros_even = (
        jnp.zeros((1, NH, DH, DH), _f32), jnp.zeros((1, NH, DH, DH), _f32),
        jnp.zeros((1, NH, DH), _f32), jnp.zeros((1, NH, LANES), _f32),
    )
    cos_m, sin_m = _rotary_tables(jnp.arange(N_META))
    cos_p, sin_p = _rotary_tables(N_META + jnp.arange(SEQ))
    cos_s, sin_s = _rotary_tables(PAST_LEN + jnp.arange(DEC_SEQ))
    ybuf, *meta_even = _mixer0(proj0, None, cos_m, sin_m, gb_row, norm0, zeros_even, **_GROUP_META)
    ybuf, p_ret, p_mc, p_mn, p_mm = _mixer0((x, w_in0_p), ybuf, cos_p, sin_p, gb_row, norm0, meta_even, **_GROUP_PROMPT)
    init_s = (state_ret.astype(_f32), state_mlstm_c.astype(_f32), state_mlstm_n.astype(_f32), _lanes(state_mlstm_m))
    ybuf, s_ret, s_mc, s_mn, s_mm = _mixer0(proj0, ybuf, cos_s, sin_s, gb_row, norm0, init_s, **_GROUP_SAMPLE)
    x = _out_ln(ybuf, w_out0.astype(_bf16), x, row(ln0_mix_g), row(ln0_mix_b))
    x = _ffn_ln(x, ffn0_w_gate.astype(_bf16), ffn0_w_up.astype(_bf16), ffn0_w_down.astype(_bf16), row(ln0_ffn_g), row(ln0_ffn_b))

    proj1 = _proj(x[ROW_SAMPLE:], w_in1_p, tn=P1_PAD // 3)
    zeros_odd = (jnp.zeros((1, NH, DH, DH), _f32), jnp.zeros((1, 8, QKV_C), _f32), jnp.zeros((1, NH, DK_D, DH), _f32))
    m1_par = (gdn_conv_w.astype(_f32), gdn_par, gla_w_alpha.astype(_f32), row(gla_b_alpha), norm1)
    ybuf, *meta_odd = _mixer1(proj1, None, *m1_par, zeros_odd, **_GROUP_META)
    ybuf, p_gdn, p_conv, p_gla = _mixer1((x, w_in1_p), ybuf, *m1_par, meta_odd, **_GROUP_PROMPT)
    init_s = (state_gdn.astype(_f32), _conv_rows(state_gdn_conv), state_gla.astype(_f32))
    ybuf, s_gdn, s_conv, s_gla = _mixer1(proj1, ybuf, *m1_par, init_s, **_GROUP_SAMPLE)
    x = _out_ln(ybuf, w_out1.astype(_bf16), x, row(ln1_mix_g), row(ln1_mix_b))
    gate_t, code_t, counts = _router(x, w_router, b_router)
    counts = counts[:, 0, :N_EXPERTS].reshape(-1)
    x = _moe_ln(x, gate_t, code_t, counts, moe_w_gate.astype(_bf16), moe_w_up.astype(_bf16), moe_w_down.astype(_bf16), row(ln1_ffn_g), row(ln1_ffn_b))

    y_prompt = _prompt_rows_to_batch_major(x)
    y_sample = x[ROW_SAMPLE:ROW_META].reshape(DEC_BATCH, DEC_SEQ, D_MODEL)
    tail = 8 - (CONV_W - 1)
    return (
        y_prompt, y_sample,
        p_ret, p_mc, p_mn, p_mm[..., 0], p_gdn, p_conv[:, tail:], p_gla,
        s_ret, s_mc, s_mn, s_mm[..., 0], s_gdn, s_conv[:, tail:], s_gla,
    )
```

```python
import functools
import math

import jax
import jax.numpy as jnp
import numpy as np
from jax import lax
from jax.experimental import pallas as pl
from jax.experimental.pallas import tpu as pltpu

D_MODEL = 1024
BATCH = 8
SEQ = 2048
DEC_BATCH = 128
DEC_SEQ = 4
PAST_LEN = 16384
N_META = 16
CHUNK = 64
NH = 4
DH = 128
DK_D = 64
CONV_W = 4
GLA_RANK = 16
GLA_TAU = 16.0
D_FF = 2816
N_EXPERTS = 8
MOE_FF = 1408
ROPE_BASE = 10000.0
LN_EPS = 1e-5
NORM_EPS = 1e-6
DEPTH = 2
ALPHA = (2 * DEPTH) ** 0.25
QKV_C = 3 * NH * DH

LANES = 128
N_PROMPT = BATCH * SEQ
N_SAMPLE = DEC_BATCH * DEC_SEQ
ROW_SAMPLE = N_PROMPT
ROW_META = N_PROMPT + N_SAMPLE
TM_MOE = 448
MOE_BLOCK = 128
MOE_GROUP = 2
TM = TM_MOE * MOE_GROUP
R_ROWS = ((ROW_META + N_META + TM - 1) // TM) * TM

P0_PAD = 33 * LANES
P1_MAIN = 28 * LANES
P1_PAD = 30 * LANES
SOLVE_BLOCK = 16

VMEM_LIMIT = 56 * 1024 * 1024

_bf16 = jnp.bfloat16
_f32 = jnp.float32
_HI = lax.Precision.HIGHEST


def _dot_hi(a, b):
    return jnp.dot(a, b, preferred_element_type=_f32, precision=_HI)


def _bmm(a, b):
    return jnp.einsum("nmk,nkp->nmp", a.astype(_bf16), b.astype(_bf16), preferred_element_type=_f32)


def _bmm_nt(a, b):
    return jnp.einsum("nmk,npk->nmp", a.astype(_bf16), b.astype(_bf16), preferred_element_type=_f32)


def _bmm_tn(a, b):
    return jnp.einsum("nkm,nkp->nmp", a.astype(_bf16), b.astype(_bf16), preferred_element_type=_f32)


def _split2(a):
    hi = a.astype(_bf16)
    lo = (a - hi.astype(_f32)).astype(_bf16)
    return hi, lo


def _bmm_x3(a, b):
    ah, al = _split2(a)
    bh, bl = _split2(b)
    mm = functools.partial(jnp.einsum, "nmk,nkp->nmp", preferred_element_type=_f32)
    return mm(ah, bh) + (mm(ah, bl) + mm(al, bh))


def _sigmoid(x):
    return 1.0 / (1.0 + jnp.exp(-x))


def _silu(x):
    return x * _sigmoid(x)


def _softplus(x):
    return jnp.maximum(x, 0.0) + jnp.log1p(jnp.exp(-jnp.abs(x)))


def _group_norm(o, g_row, rms):
    if not rms:
        o = o - jnp.mean(o, axis=-1, keepdims=True)
    return o * lax.rsqrt(jnp.mean(o * o, axis=-1, keepdims=True) + NORM_EPS) * g_row


def _layer_norm(x, g_row, b_row):
    mu = jnp.mean(x, axis=-1, keepdims=True)
    xc = x - mu
    var = jnp.mean(xc * xc, axis=-1, keepdims=True)
    return xc * lax.rsqrt(var + LN_EPS) * g_row + b_row


def _tri3(c):
    t = lax.broadcasted_iota(jnp.int32, (1, c, c), 1)
    s = lax.broadcasted_iota(jnp.int32, (1, c, c), 2)
    return t, s


def _cumsum_col_row(x_col, x_row, c):
    t, s = _tri3(c)
    col = jnp.sum(jnp.where(t >= s, x_row, 0.0), axis=2, keepdims=True)
    row = jnp.sum(jnp.where(t <= s, x_col, 0.0), axis=1, keepdims=True)
    return col, row


def _proj_kernel(x_ref, w_ref, o_ref, tail_ref, xb_ref):
    j = pl.program_id(1)

    @pl.when(j == 0)
    def _():
        xb_ref[...] = x_ref[...].astype(_bf16)

    acc = jnp.dot(xb_ref[...], w_ref[...], preferred_element_type=_f32)
    o_ref[...] = acc.astype(o_ref.dtype)

    @pl.when(j == pl.num_programs(1) - 1)
    def _():
        tail_ref[...] = acc[:, acc.shape[1] - LANES :]


def _proj(x, w, tn):
    rows, k = x.shape
    n = w.shape[1]
    tm = min(TM, rows)
    return pl.pallas_call(
        _proj_kernel,
        grid=(rows // tm, n // tn),
        in_specs=[pl.BlockSpec((tm, k), lambda i, j: (i, 0)), pl.BlockSpec((k, tn), lambda i, j: (0, j))],
        out_specs=[pl.BlockSpec((tm, tn), lambda i, j: (i, j)), pl.BlockSpec((tm, LANES), lambda i, j: (i, 0))],
        out_shape=[jax.ShapeDtypeStruct((rows, n), _bf16), jax.ShapeDtypeStruct((rows, LANES), _f32)],
        scratch_shapes=[pltpu.VMEM((tm, k), _bf16)],
        compiler_params=pltpu.CompilerParams(dimension_semantics=("parallel", "arbitrary"), vmem_limit_bytes=VMEM_LIMIT),
        name="in_proj",
    )(x, w)


def _out_ln_kernel(y_ref, w_ref, x_ref, g_ref, b_ref, o_ref):
    h = jnp.dot(y_ref[...], w_ref[...], preferred_element_type=_f32)
    o_ref[...] = _layer_norm(ALPHA * x_ref[...] + h, g_ref[...], b_ref[...])


def _out_ln(y, w, x, g, b):
    rows, k = y.shape
    d = w.shape[1]
    return pl.pallas_call(
        _out_ln_kernel,
        grid=(rows // TM,),
        in_specs=[
            pl.BlockSpec((TM, k), lambda i: (i, 0)),
            pl.BlockSpec((k, d), lambda i: (0, 0)),
            pl.BlockSpec((TM, d), lambda i: (i, 0)),
            pl.BlockSpec((1, d), lambda i: (0, 0)),
            pl.BlockSpec((1, d), lambda i: (0, 0)),
        ],
        out_specs=pl.BlockSpec((TM, d), lambda i: (i, 0)),
        out_shape=jax.ShapeDtypeStruct((rows, d), _f32),
        compiler_params=pltpu.CompilerParams(dimension_semantics=("parallel",), vmem_limit_bytes=VMEM_LIMIT),
        name="out_proj_ln",
    )(y, w, x, g, b)


FF_SUB = (0, 512, 1024, 1408)


def _swiglu_partial(xb, wg_ref, wu_ref, wd_ref):
    out = None
    for lo, hi in zip(FF_SUB[:-1], FF_SUB[1:]):
        hg = jnp.dot(xb, wg_ref[:, lo:hi], preferred_element_type=_f32)
        hu = jnp.dot(xb, wu_ref[:, lo:hi], preferred_element_type=_f32)
        a = (_silu(hg) * hu).astype(_bf16)
        part = jnp.dot(a, wd_ref[lo:hi, :], preferred_element_type=_f32)
        out = part if out is None else out + part
    return out


def _ffn_ln_kernel(x_ref, wg_ref, wu_ref, wd_ref, g_ref, b_ref, o_ref, xb_ref, acc_ref):
    f = pl.program_id(1)

    @pl.when(f == 0)
    def _():
        xb_ref[...] = x_ref[...].astype(_bf16)
        acc_ref[...] = jnp.zeros_like(acc_ref)

    acc_ref[...] += _swiglu_partial(xb_ref[...], wg_ref, wu_ref, wd_ref)

    @pl.when(f == pl.num_programs(1) - 1)
    def _():
        o_ref[...] = _layer_norm(ALPHA * x_ref[...] + acc_ref[...], g_ref[...], b_ref[...])


def _ffn_ln(x, wg, wu, wd, g, b):
    rows, d = x.shape
    ff = wg.shape[1]
    tf = MOE_FF
    return pl.pallas_call(
        _ffn_ln_kernel,
        grid=(rows // TM, ff // tf),
        in_specs=[
            pl.BlockSpec((TM, d), lambda i, f: (i, 0)),
            pl.BlockSpec((d, tf), lambda i, f: (0, f)),
            pl.BlockSpec((d, tf), lambda i, f: (0, f)),
            pl.BlockSpec((tf, d), lambda i, f: (f, 0)),
            pl.BlockSpec((1, d), lambda i, f: (0, 0)),
            pl.BlockSpec((1, d), lambda i, f: (0, 0)),
        ],
        out_specs=pl.BlockSpec((TM, d), lambda i, f: (i, 0)),
        out_shape=jax.ShapeDtypeStruct((rows, d), _f32),
        scratch_shapes=[pltpu.VMEM((TM, d), _bf16), pltpu.VMEM((TM, d), _f32)],
        compiler_params=pltpu.CompilerParams(dimension_semantics=("parallel", "arbitrary"), vmem_limit_bytes=VMEM_LIMIT),
        name="ffn_ln",
    )(x, wg, wu, wd, g, b)


WCAST_BLOCKS = 32


def _router_kernel(x_ref, wr_ref, br_ref, before_ref, ident_ref, wg32_ref, wu32_ref, wd32_ref,
                   gate_t_ref, code_t_ref, cnt_ref, wg_ref, wu_ref, wd_ref):
    wg_ref[...] = wg32_ref[...].astype(wg_ref.dtype)
    wu_ref[...] = wu32_ref[...].astype(wu_ref.dtype)
    wd_ref[...] = wd32_ref[...].astype(wd_ref.dtype)
    x = x_ref[...]
    xh, xl = _split2(x)
    wh, wl = _split2(wr_ref[...])
    mm = functools.partial(jnp.dot, preferred_element_type=_f32)
    logits = mm(xh, wh) + (mm(xh, wl) + mm(xl, wh)) + br_ref[...]
    lane = lax.broadcasted_iota(jnp.int32, logits.shape, 1)
    ex = jnp.exp(logits - jnp.max(logits, axis=-1, keepdims=True))
    probs = ex / jnp.sum(ex, axis=-1, keepdims=True)
    p1 = jnp.max(probs, axis=-1, keepdims=True)
    i1 = jnp.min(jnp.where(probs == p1, lane, LANES), axis=-1, keepdims=True)
    rest = jnp.where(lane == i1, -1.0, probs)
    p2 = jnp.max(rest, axis=-1, keepdims=True)
    i2 = jnp.min(jnp.where(rest == p2, lane, LANES), axis=-1, keepdims=True)
    tot = p1 + p2
    gates =jnp.where(lane == i1, p1 / tot, 0.0) + jnp.where(lane == i2, p2 / tot, 0.0)
    sel = jnp.where(lane == i1, 1.0, jnp.where(lane == i2, 1.0, 0.0))
    selb = sel.astype(_bf16)
    g_hi = gates.astype(_bf16)
    r1 = gates - g_hi.astype(_f32)
    g_mid = r1.astype(_bf16)
    g_lo = (r1 - g_mid.astype(_f32)).astype(_bf16)
    stacked = jnp.concatenate([selb, g_hi, g_mid, g_lo], axis=1)
    stacked_t = lax.dot_general(stacked, ident_ref[...], (((0,), (0,)), ((), ())), preferred_element_type=_f32)
    sel_t = stacked_t[0:N_EXPERTS, :]
    rank_t = jnp.dot(sel_t.astype(_bf16), before_ref[...], preferred_element_type=_f32)
    code_t_ref[0] = jnp.where(sel_t > 0.0, rank_t, -1.0)
    parts = [stacked_t[k * LANES : k * LANES + N_EXPERTS, :] for k in (1, 2, 3)]
    gate_t_ref[0] = parts[0] + (parts[1] + parts[2])
    cnt_ref[0] = jnp.sum(sel, axis=0, keepdims=True).astype(jnp.int32)


def _router(x, wr, br, wg, wu, wd):
    rows, d = x.shape
    nt = rows // TM_MOE
    assert nt >= WCAST_BLOCKS
    t = np.arange(TM_MOE)
    before = jnp.asarray(t[:, None] < t[None, :], _bf16)
    ident = jnp.asarray(t[:, None] == t[None, :], _bf16)
    flat = [w.reshape(-1, w.shape[-1]) for w in (wg, wu, wd)]

    def w_spec(w):
        return pl.BlockSpec((w.shape[0] // WCAST_BLOCKS, w.shape[1]), lambda i: (jnp.minimum(i, WCAST_BLOCKS - 1), 0))

    gate_t, code_t, cnt, *cast = pl.pallas_call(
        _router_kernel,
        grid=(nt,),
        in_specs=[
            pl.BlockSpec((TM_MOE, d), lambda i: (i, 0)),
            pl.BlockSpec((d, LANES), lambda i: (0, 0)),
            pl.BlockSpec((1, LANES), lambda i: (0, 0)),
            pl.BlockSpec((TM_MOE, TM_MOE), lambda i: (0, 0)),
            pl.BlockSpec((TM_MOE, TM_MOE), lambda i: (0, 0)),
        ] + [w_spec(w) for w in flat],
        out_specs=[
            pl.BlockSpec((1, N_EXPERTS, TM_MOE), lambda i: (i, 0, 0)),
            pl.BlockSpec((1, N_EXPERTS, TM_MOE), lambda i: (i, 0, 0)),
            pl.BlockSpec((1, 1, LANES), lambda i: (i, 0, 0)),
        ] + [w_spec(w) for w in flat],
        out_shape=[
            jax.ShapeDtypeStruct((nt, N_EXPERTS, TM_MOE), _f32),
            jax.ShapeDtypeStruct((nt, N_EXPERTS, TM_MOE), _f32),
            jax.ShapeDtypeStruct((nt, 1, LANES), jnp.int32),
        ] + [jax.ShapeDtypeStruct(w.shape, _bf16) for w in flat],
        compiler_params=pltpu.CompilerParams(dimension_semantics=("arbitrary",), vmem_limit_bytes=VMEM_LIMIT),
        name="router",
    )(x, wr, br, before, ident, *flat)
    return gate_t, code_t, cnt, [c.reshape(w.shape) for c, w in zip(cast, (wg, wu, wd))]


def _moe_ln_kernel(cnt_ref, x_ref, gate_t_ref, code_t_ref, wg_ref, wu_ref, wd_ref, g_ref, b_ref, o_ref, xb_ref, acc_ref, first_ref):
    grp = pl.program_id(0)
    e = pl.program_id(1)

    @pl.when(e == 0)
    def _():
        xb_ref[...] = x_ref[...].astype(_bf16)
        acc_ref[...] = jnp.zeros_like(acc_ref)

    slot_col = lax.broadcasted_iota(jnp.int32, (MOE_BLOCK, 1), 0)
    counts = [cnt_ref[(grp * MOE_GROUP + i) * N_EXPERTS + e] for i in range(MOE_GROUP)]
    code_rows = [code_t_ref[i, pl.ds(e, 1), :] for i in range(MOE_GROUP)]
    gate_rows = [gate_t_ref[i, pl.ds(e, 1), :] for i in range(MOE_GROUP)]
    first_rows = pl.ds(pl.multiple_of(e * MOE_BLOCK, MOE_BLOCK), MOE_BLOCK)
    scatter_dims = (((0,), (0,)), ((), ()))

    def gather_rows(i, blk):
        hit = code_rows[i] == (slot_col + blk * MOE_BLOCK).astype(_f32)
        onehot = jnp.where(hit, 1.0, 0.0).astype(_bf16)
        xs = jnp.dot(onehot, xb_ref[i * TM_MOE : (i + 1) * TM_MOE, :], preferred_element_type=_f32).astype(_bf16)
        gate = jnp.sum(jnp.where(hit, gate_rows[i], 0.0), axis=1, keepdims=True)
        return onehot, xs, gate

    def first_pass(i):
        _, xs, gate = gather_rows(i, 0)
        first_ref[i, first_rows, :] = (_swiglu_partial(xs, wg_ref, wu_ref, wd_ref) * gate).astype(_bf16)

    def no_pass(i):
        first_ref[i, first_rows, :] = jnp.zeros((MOE_BLOCK, first_ref.shape[2]), _bf16)

    assert MOE_GROUP == 2
    has0 = counts[0] > 0
    has1 = counts[1] > 0

    @pl.when(jnp.logical_and(has0, has1))
    def _():
        _, xs0, gate0 = gather_rows(0, 0)
        _, xs1, gate1 = gather_rows(1, 0)
        out = _swiglu_partial(jnp.concatenate([xs0, xs1], axis=0), wg_ref, wu_ref, wd_ref)
        first_ref[0, first_rows, :] = (out[:MOE_BLOCK] * gate0).astype(_bf16)
        first_ref[1, first_rows, :] = (out[MOE_BLOCK:] * gate1).astype(_bf16)

    @pl.when(jnp.logical_and(has0, jnp.logical_not(has1)))
    def _():
        first_pass(0)
        no_pass(1)

    @pl.when(jnp.logical_and(has1, jnp.logical_not(has0)))
    def _():
        no_pass(0)
        first_pass(1)

    @pl.when(jnp.logical_and(jnp.logical_not(has0), jnp.logical_not(has1)))
    def _():
        no_pass(0)
        no_pass(1)

    for blk in range(1, -(-TM_MOE // MOE_BLOCK)):
        for i in range(MOE_GROUP):

            @pl.when(counts[i] > blk * MOE_BLOCK)
            def _():
                onehot, xs, gate = gather_rows(i, blk)
                out = (_swiglu_partial(xs, wg_ref, wu_ref, wd_ref) * gate).astype(_bf16)
                acc_ref[i * TM_MOE : (i + 1) * TM_MOE, :] += lax.dot_general(onehot, out, scatter_dims, preferred_element_type=_f32)

    @pl.when(e == pl.num_programs(1) - 1)
    def _():
        slot = lax.broadcasted_iota(jnp.int32, (1, MOE_BLOCK, 1), 1).astype(_f32)
        for i in range(MOE_GROUP):
            hit = code_t_ref[i][:, None, :] == slot
            onehot = jnp.where(hit, 1.0, 0.0).astype(_bf16).reshape(N_EXPERTS * MOE_BLOCK, TM_MOE)
            rows = slice(i * TM_MOE, (i + 1) * TM_MOE)
            moe = acc_ref[rows, :] + lax.dot_general(onehot, first_ref[i], scatter_dims, preferred_element_type=_f32)
            o_ref[rows, :] = _layer_norm(ALPHA * x_ref[rows, :] + moe, g_ref[...], b_ref[...])


def _moe_ln(x, gate_t, code_t, counts, wg, wu, wd, g, b):
    rows, d = x.shape
    ne, _, ff = wg.shape
    grid_spec = pltpu.PrefetchScalarGridSpec(
        num_scalar_prefetch=1,
        grid=(rows // TM, ne),
        in_specs=[
            pl.BlockSpec((TM, d), lambda i, e, cnt: (i, 0)),
            pl.BlockSpec((MOE_GROUP, N_EXPERTS, TM_MOE), lambda i, e, cnt: (i, 0, 0)),
            pl.BlockSpec((MOE_GROUP, N_EXPERTS, TM_MOE), lambda i, e, cnt: (i, 0, 0)),
            pl.BlockSpec((None, d, ff), lambda i, e, cnt: (e, 0, 0)),
            pl.BlockSpec((None, d, ff), lambda i, e, cnt: (e, 0, 0)),
            pl.BlockSpec((None, ff, d), lambda i, e, cnt: (e, 0, 0)),
            pl.BlockSpec((1, d), lambda i, e, cnt: (0, 0)),
            pl.BlockSpec((1, d), lambda i, e, cnt: (0, 0)),
        ],
        out_specs=pl.BlockSpec((TM, d), lambda i, e, cnt: (i, 0)),
        scratch_shapes=[
            pltpu.VMEM((TM, d), _bf16),
            pltpu.VMEM((TM, d), _f32),
            pltpu.VMEM((MOE_GROUP, ne * MOE_BLOCK, d), _bf16),
        ],
    )
    return pl.pallas_call(
        _moe_ln_kernel,
        grid_spec=grid_spec,
        out_shape=jax.ShapeDtypeStruct((rows, d), _f32),
        compiler_params=pltpu.CompilerParams(dimension_semantics=("parallel", "arbitrary"), vmem_limit_bytes=VMEM_LIMIT),
        name="moe_ln",
    )(counts, x, gate_t, code_t, wg, wu, wd, g, b)


def _chains(a, c, bb, width, off=0):
    return jnp.stack([a[j * c : (j + 1) * c, off + h * width : off + (h + 1) * width] for j in range(bb) for h in range(NH)])


def _chain_cols(a, c, bb, lane0):
    return jnp.stack([a[j * c : (j + 1) * c, lane0 + h : lane0 + h + 1] for j in range(bb) for h in range(NH)])


def _chain_rows(a_t, c, bb, lane0):
    return jnp.stack([a_t[lane0 + h : lane0 + h + 1, j * c : (j + 1) * c] for j in range(bb) for h in range(NH)])


def _head_rows(ref, row, bb):
    return jnp.stack([ref[row : row + 1, h * DH : (h + 1) * DH] for _ in range(bb) for h in range(NH)])


def _store_chains(y_ref, y, c, bb, off):
    for j in range(bb):
        for h in range(NH):
            y_ref[j * c : (j + 1) * c, off + h * DH : off + (h + 1) * DH] = y[j * NH + h].astype(y_ref.dtype)
    if y_ref.shape[0] > bb * c:
        y_ref[bb * c :, off : off + NH * DH] = jnp.zeros((y_ref.shape[0] - bb * c, NH * DH), y_ref.dtype)


def _drop_alias_ref(kernel, n_in):
    def body(*refs):
        return kernel(*refs[:n_in], None, *refs[n_in:])

    return body


def _y_buffer_specs(ybuf, rows, y_rows, y_row_off, nbb, width):
    if y_rows == rows:
        spec = pl.BlockSpec((rows, width), lambda i, ci: (y_row_off // rows + ci * nbb + i, 0))
    else:
        spec = pl.BlockSpec((y_rows, width), lambda i, ci: (y_row_off // y_rows, 0))
    return spec, jax.ShapeDtypeStruct((R_ROWS, width), _bf16), ([] if ybuf is None else [pl.BlockSpec(memory_space=pl.ANY)])


PIECE = NH * DH


def _stream_projection(xn_ref, xc_ref, w_ref, main_sc, tail_sc, xb_sc, n_piece):
    rows = xn_ref.shape[0]
    ci = pl.program_id(1)
    tail_col = w_ref.shape[1] - LANES

    def project(xb, row0, p):
        if p < n_piece:
            cols = slice(p * PIECE, (p + 1) * PIECE)
            main_sc[pl.ds(row0, rows), cols] = jnp.dot(xb, w_ref[:, cols], preferred_element_type=_f32).astype(_bf16)
        else:
            tail_sc[pl.ds(row0, rows), :] = jnp.dot(xb, w_ref[:, tail_col:], preferred_element_type=_f32)

    @pl.when(ci == 0)
    def _():
        xb0 = xc_ref[...].astype(_bf16)
        for p in range(n_piece + 1):
            project(xb0, 0, p)

    cur = pl.multiple_of((ci % 2) * rows, rows)
    nxt = pl.multiple_of(((ci + 1) % 2) * rows, rows)
    xb_sc[...] = xn_ref[...].astype(_bf16)
    pending = iter(range(n_piece + 1))

    def between(count=1):
        for _ in range(count):
            p = next(pending, None)
            if p is not None:
                project(xb_sc[...], nxt, p)

    def load(p):
        return main_sc[pl.ds(cur, rows), p * PIECE : (p + 1) * PIECE].astype(_f32)

    return load, tail_sc[pl.ds(cur, rows), :], between


def _retention_chunk(q, k, v, s_prev, dec, vec):
    att = _bmm_nt(q, k) * dec
    o = _bmm(att, v) + vec[:, :, 0:1] * _bmm(q, s_prev)
    s_new = vec[:, 0:1, 2:3] * s_prev + _bmm_tn(k * vec[:, :, 1:2], v)
    return o, s_new


def _mlstm_chunk(q, k, v, it_col, it_row, lf_col, lf_row, c_prev, n_prev, m_prev, c):
    t, s = _tri3(c)
    b_col, b_row = _cumsum_col_row(lf_col, lf_row, c)
    logw = jnp.where(t >= s, b_col - b_row + it_row, -jnp.inf)
    m_t = jnp.maximum(b_col + m_prev, jnp.max(logw, axis=2, keepdims=True))
    w = jnp.exp(logw - m_t)
    carry = jnp.exp(b_col + m_prev - m_t)
    qk = _bmm_nt(q, k) * w
    num = _bmm(qk, v) + carry * _bmm(q, c_prev)
    den = jnp.sum(qk, axis=2, keepdims=True) + carry * jnp.sum(q * n_prev, axis=2, keepdims=True)
    h = num / jnp.maximum(jnp.abs(den), jnp.exp(-m_t))
    m_new = m_t[:, c - 1 : c, :]
    b_last = b_col[:, c - 1 : c, :]
    w_last = jnp.exp(b_last - b_col + it_col - m_new)
    decay = jnp.exp(b_last + m_prev - m_new)
    kw = k * w_last
    c_new = decay * c_prev + _bmm_tn(kw, v)
    n_new = decay * n_prev + jnp.sum(kw, axis=1, keepdims=True)
    return h, c_new, n_new, m_new


N_PIECE0 = 8


def _mixer0_body(
    load, gate_tail, between,
    cos_ref, sin_ref, dec_ref, vec_ref, gb_ref, ng_ref, s0_ref, c0_ref, n0_ref, m0_ref,
    y_ref, s_ref, c_ref, n_ref, m_ref, *, c, bb,
):
    n = bb * NH

    @pl.when(pl.program_id(1) == 0)
    def _():
        s_ref[...] = jnp.broadcast_to(s0_ref[...], s_ref.shape)
        c_ref[...] = jnp.broadcast_to(c0_ref[...], c_ref.shape)
        n_ref[...] = jnp.broadcast_to(n0_ref[...], n_ref.shape)
        m_ref[...] = jnp.broadcast_to(m0_ref[...], m_ref.shape)

    def chains(p):
        return _chains(load(p), c, bb, DH)

    cosf = cos_ref[...][None]
    sinf = sin_ref[...][None]
    gates = gate_tail + gb_ref[...]
    lane = lax.broadcasted_iota(jnp.int32, gates.shape, 1)
    gates = jnp.where(lane < NH, gates, -_softplus(-gates))
    gates_t = jnp.transpose(gates)

    q = chains(0)
    k = chains(1)
    q = q * cosf + pltpu.roll(q, DH // 2, axis=2) * sinf
    k = (k * cosf + pltpu.roll(k, DH // 2, axis=2) * sinf) * DH**-0.5
    between(2)
    dec = jnp.concatenate([dec_ref[...]] * bb, axis=0)
    vec = jnp.concatenate([vec_ref[...]] * bb, axis=0)
    o, s_new = _retention_chunk(q, k, chains(2), s_ref[...].reshape(n, DH, DH), dec, vec)
    between(2)
    s_ref[...] = s_new.reshape(s_ref.shape)
    y_a = _group_norm(o, _head_rows(ng_ref, 0, bb), rms=False) * _silu(chains(3))
    _store_chains(y_ref, y_a, c, bb, 0)
    between(1)

    n_prev = jnp.stack([n_ref[j, h : h + 1, :] for j in range(bb) for h in range(NH)])
    m_prev = jnp.stack([m_ref[j, h : h + 1, 0:1] for j in range(bb) for h in range(NH)])
    h_b, c_new, n_new, m_new = _mlstm_chunk(
        chains(4), chains(5) * DH**-0.5, chains(6),
        _chain_cols(gates, c, bb, 0), _chain_rows(gates_t, c, bb, 0),
        _chain_cols(gates, c, bb, NH), _chain_rows(gates_t, c, bb, NH),
        c_ref[...].reshape(n, DH, DH), n_prev, m_prev, c,
    )
    between(2)
    c_ref[...] = c_new.reshape(c_ref.shape)
    for j in range(bb):
        for h in range(NH):
            n_ref[j, h : h + 1, :] = n_new[j * NH + h]
            m_ref[j, h : h + 1, :] = jnp.broadcast_to(m_new[j * NH + h], (1, LANES))
    h_b = _sigmoid(chains(7)) * h_b
    between(1)
    y_b = _group_norm(h_b, _head_rows(ng_ref, 1, bb), rms=False)
    _store_chains(y_ref, y_b, c, bb, NH * DH)
    between(N_PIECE0 + 1)


def _no_op(count=1):
    del count


def _mixer0_kernel(*refs, c, bb):
    pieces, gate_ref, rest = refs[:N_PIECE0], refs[N_PIECE0], refs[N_PIECE0 + 1 :]
    consts, outs = rest[:10], rest[11:]
    _mixer0_body(lambda p: pieces[p][...].astype(_f32), gate_ref[...], _no_op, *consts, *outs, c=c, bb=bb)


def _mixer0_stream_kernel(xn_ref, xc_ref, w_ref, *rest, c, bb):
    consts, outs, scratch = rest[:10], rest[11:16], rest[16:]
    load, gate_tail, between = _stream_projection(xn_ref, xc_ref, w_ref, *scratch, N_PIECE0)
    _mixer0_body(load, gate_tail, between, *consts, *outs, c=c, bb=bb)


def _retention_tables(c):
    lg = np.log(1.0 - 2.0 ** (-5.0 - np.arange(NH, dtype=np.float64)))[:, None, None]
    t = np.arange(c, dtype=np.float64)
    diff = t[None, :, None] - t[None, None, :]
    dec = np.where(diff >= 0, np.exp(np.maximum(diff, 0.0) * lg), 0.0)
    vec = np.zeros((NH, c, LANES))
    vec[:, :, 0] = np.exp((t[None, :] + 1.0) * lg[:, 0])
    vec[:, :, 1] = np.exp((c - 1.0 - t[None, :]) * lg[:, 0])
    vec[:, :, 2] = np.exp(c * lg[:, 0])
    return jnp.asarray(dec, _f32), jnp.asarray(vec, _f32)


def _whole(a):
    return a, pl.BlockSpec(a.shape, lambda i, ci: (0,) * a.ndim)


def _mixer_call(name, kernels, source, n_piece, consts, init, ybuf, *, row_off, y_row_off, nb, nchunk, c, bb, bcast_init, y_rows=None, stream=False):
    rows = bb * c
    nbb = nb // bb
    blk0 = row_off // rows
    ib = 1 if bcast_init else bb
    y_spec, y_shape, alias_spec = _y_buffer_specs(ybuf, rows, y_rows or rows, y_row_off, nbb, 2 * NH * DH)

    def row_block(col):
        return lambda i, ci: (blk0 + ci * nbb + i, col)

    def batch_block(nd, lead):
        return lambda i, ci: ((i if lead else 0),) + (0,) * nd

    if stream:
        x, w = source
        src_args = [x, x, w]
        src_specs = [
            pl.BlockSpec((rows, D_MODEL), lambda i, ci: (blk0 + jnp.minimum(ci + 1, nchunk - 1) * nbb + i, 0)),
            pl.BlockSpec((rows, D_MODEL), lambda i, ci: (blk0 + i, 0)),
            pl.BlockSpec(w.shape, lambda i, ci: (0, 0)),
        ]
        scratch = [pltpu.VMEM((2 * rows, n_piece * PIECE), _bf16), pltpu.VMEM((2 * rows, LANES), _f32), pltpu.VMEM((rows, D_MODEL), _bf16)]
    else:
        proj, tail = source
        src_args = [proj] * n_piece + [tail]
        src_specs = [pl.BlockSpec((rows, PIECE), row_block(p)) for p in range(n_piece)] + [pl.BlockSpec((rows, LANES), row_block(0))]
        scratch = []
    in_specs = src_specs + [spec for _, spec in consts]
    in_specs += [pl.BlockSpec((ib,) + a.shape[1:], batch_block(a.ndim - 1, not bcast_init)) for a in init]
    out_specs = [y_spec] + [pl.BlockSpec((bb,) + a.shape[1:], batch_block(a.ndim - 1, True)) for a in init]
    out_shape = [y_shape] + [jax.ShapeDtypeStruct((nb,) + a.shape[1:], _f32) for a in init]
    args = src_args + [a for a, _ in consts] + list(init)
    body = functools.partial(kernels[int(stream)], c=c, bb=bb)
    return pl.pallas_call(
        body if ybuf is not None else _drop_alias_ref(body, len(args)),
        grid=(nbb, nchunk),
        in_specs=in_specs + alias_spec,
        out_specs=out_specs,
        out_shape=out_shape,
        scratch_shapes=scratch,
        input_output_aliases={len(args): 0} if ybuf is not None else {},
        compiler_params=pltpu.CompilerParams(dimension_semantics=("parallel", "arbitrary"), vmem_limit_bytes=VMEM_LIMIT),
        name=f"{name}_c{c}",
    )(*args, *([] if ybuf is None else [ybuf]))


def _mixer0(source, ybuf, cosf, sinf, gb_row, norm_g, init, **group):
    c = group["c"]
    dec, vec = _retention_tables(c)
    by_chunk = pl.BlockSpec((c, DH), lambda i, ci: (ci, 0))
    consts = [(cosf, by_chunk), (sinf, by_chunk), _whole(dec), _whole(vec), _whole(gb_row), _whole(norm_g)]
    return _mixer_call("mixer0", (_mixer0_kernel, _mixer0_stream_kernel), source, N_PIECE0, consts, init, ybuf, **group)


def _unit_lower_solve(a, rhs, c):
    bs = min(SOLVE_BLOCK, c)
    t, s = _tri3(c)
    if c > bs:
        shift = bs.bit_length() - 1
        same = jnp.right_shift(t, shift) == jnp.right_shift(s, shift)
        d = jnp.where(same, a, 0.0)
    else:
        d = a
    inv = jnp.where(t == s, 1.0, 0.0) - d
    p = d
    span = 2
    while span < bs:
        p = _bmm_x3(p, p)
        inv = inv + _bmm_x3(inv, p)
        span *= 2
    y = _bmm(inv, rhs)
    if c == bs:
        return y
    b = _bmm(inv, jnp.where(same, 0.0, a))
    y = y - _bmm(b, y)
    p = b
    span = 2
    while span < c // bs:
        p = _bmm(p, p)
        y = y + _bmm(p, y)
        span *= 2
    return y


def _gdn_chunk(q, k, v, beta_col, g_col, g_row, s_prev, c, between=_no_op):
    t, s = _tri3(c)
    gc_col, gc_row = _cumsum_col_row(g_col, g_row, c)
    dec_incl = jnp.exp(jnp.where(t >= s, gc_col - gc_row, -jnp.inf))
    dec_strict = jnp.where(t > s, dec_incl, 0.0)
    e_col = jnp.exp(gc_col)
    a = beta_col * _bmm_nt(k, k) * dec_strict
    rhs = jnp.concatenate([beta_col * v, (beta_col * e_col) * k], axis=-1)
    between(1)
    sol = _unit_lower_solve(a, rhs, c)
    between(1)
    u = sol[:, :, :DH] - _bmm(sol[:, :, DH:], s_prev)
    qk = _bmm_nt(q, k) * dec_incl
    between(1)
    o = e_col * _bmm(q, s_prev) + _bmm(qk, u)
    gl = gc_col[:, c - 1 : c, :]
    s_new = jnp.exp(gl) * s_prev + _bmm_tn(k * jnp.exp(gl - gc_col), u)
    return o, s_new


def _gla_chunk(q, k, v, bc, s_prev, c):
    t, s = _tri3(c)
    qe = q * jnp.exp(bc)
    att = jnp.where(t >= s, _bmm_nt(qe, k * jnp.exp(-bc)), 0.0)
    o = _bmm(att, v) + _bmm(qe, s_prev)
    bl = bc[:, c - 1 : c, :]
    ti, si = _tri3(DK_D)
    el_col = jnp.sum(jnp.where(ti == si, jnp.exp(bl), 0.0), axis=2, keepdims=True)
    s_new = el_col * s_prev + _bmm_tn(k * jnp.exp(bl - bc), v)
    return o, s_new


N_PIECE1 = 7


def _mixer1_body(
    load, small, between,
    cw_ref, gp_ref, wa_ref, ba_ref, ng_ref, s0_ref, cv0_ref, d0_ref,
    y_ref, s_ref, cv_ref, d_ref, *, c, bb,
):
    n = bb * NH
    w4 = NH * DH

    @pl.when(pl.program_id(1) == 0)
    def _():
        s_ref[...] = jnp.broadcast_to(s0_ref[...], s_ref.shape)
        cv_ref[...] = jnp.broadcast_to(cv0_ref[...], cv_ref.shape)
        d_ref[...] = jnp.broadcast_to(d0_ref[...], d_ref.shape)

    beta_all = _sigmoid(small)
    g_all = -jnp.exp(gp_ref[0:1, :]) * _softplus(small + gp_ref[1:2, :])
    g_all_t = jnp.transpose(g_all)
    log_alpha = -_softplus(-(_dot_hi(small[:, 0:GLA_RANK], wa_ref[...]) + ba_ref[...])) * (1.0 / GLA_TAU)
    cw = cw_ref[...]

    acts = []
    qkv = jnp.concatenate([load(0), load(1), load(2)], axis=1)
    between(1)
    for j in range(bb):
        ext = jnp.concatenate([cv_ref[j], qkv[j * c : (j + 1) * c, :]], axis=0)
        conv = cw[3:4] * ext[8 : 8 + c] + cw[2:3] * ext[7 : 7 + c] + cw[1:2] * ext[6 : 6 + c] + cw[0:1] * ext[5 : 5 + c]
        cv_ref[j] = ext[c : c + 8]
        acts.append(_silu(conv))

    def act_chains(off):
        return jnp.stack([acts[j][:, off + h * DH : off + (h + 1) * DH] for j in range(bb) for h in range(NH)])

    qc = act_chains(0)
    kc = act_chains(w4)
    qc = qc * lax.rsqrt(jnp.sum(qc * qc, axis=-1, keepdims=True) + NORM_EPS) * DH**-0.5
    kc = kc * lax.rsqrt(jnp.sum(kc * kc, axis=-1, keepdims=True) + NORM_EPS)
    between(1)
    o, s_new = _gdn_chunk(
        qc, kc, act_chains(2 * w4),
        _chain_cols(beta_all, c, bb, GLA_RANK), _chain_cols(g_all, c, bb, GLA_RANK + NH),
        _chain_rows(g_all_t, c, bb, GLA_RANK + NH), s_ref[...].reshape(n, DH, DH), c, between,
    )
    s_ref[...] = s_new.reshape(s_ref.shape)
    y_c = _group_norm(o, _head_rows(ng_ref, 0, bb), rms=True) * _silu(_chains(load(3), c, bb, DH))
    _store_chains(y_ref, y_c, c, bb, 0)
    between(1)

    tt = lax.broadcasted_iota(jnp.int32, (c, c), 0)
    ss = lax.broadcasted_iota(jnp.int32, (c, c), 1)
    ones_lt = jnp.where(tt >= ss, 1.0, 0.0).astype(_bf16)
    bcs = []
    for j in range(bb):
        la = log_alpha[j * c : (j + 1) * c, :]
        hi = la.astype(_bf16)
        r1 = la - hi.astype(_f32)
        mid = r1.astype(_bf16)
        lo = (r1 - mid.astype(_f32)).astype(_bf16)
        cum = functools.partial(jnp.dot, ones_lt, preferred_element_type=_f32)
        bcs.append(cum(hi) + (cum(mid) + cum(lo)))
    bc = jnp.stack([bcs[j][:, h * DK_D : (h + 1) * DK_D] for j in range(bb) for h in range(NH)])
    qkd = load(4)
    between(1)
    o, d_new = _gla_chunk(
        _chains(qkd, c, bb, DK_D) * DK_D**-0.5, _chains(qkd, c, bb, DK_D, off=NH * DK_D),
        _chains(load(5), c, bb, DH), bc, d_ref[...].reshape(n, DK_D, DH), c,
    )
    d_ref[...] = d_new.reshape(d_ref.shape)
    between(1)
    y_d = _group_norm(o, _head_rows(ng_ref, 1, bb), rms=False) * _silu(_chains(load(6), c, bb, DH))
    _store_chains(y_ref, y_d, c, bb, w4)
    between(N_PIECE1 + 1)


def _mixer1_kernel(*refs, c, bb):
    pieces, small_ref, rest = refs[:N_PIECE1], refs[N_PIECE1], refs[N_PIECE1 + 1 :]
    consts, outs = rest[:8], rest[9:]
    _mixer1_body(lambda p: pieces[p][...].astype(_f32), small_ref[...], _no_op, *consts, *outs, c=c, bb=bb)


def _mixer1_stream_kernel(xn_ref, xc_ref, w_ref, *rest, c, bb):
    consts, outs, scratch = rest[:8], rest[9:13], rest[13:]
    load, small, between = _stream_projection(xn_ref, xc_ref, w_ref, *scratch, N_PIECE1)
    _mixer1_body(load, small, between, *consts, *outs, c=c, bb=bb)


def _mixer1(source, ybuf, conv_w, gdn_par, w_alpha, b_alpha, norm_g, init, **group):
    consts = [_whole(a) for a in (conv_w, gdn_par, w_alpha, b_alpha, norm_g)]
    return _mixer_call("mixer1", (_mixer1_kernel, _mixer1_stream_kernel), source, N_PIECE1, consts, init, ybuf, **group)


_GROUP_META = dict(
    row_off=ROW_META - ROW_SAMPLE, y_row_off=ROW_META, nb=1, nchunk=1, c=N_META, bb=1, bcast_init=True,
    y_rows=R_ROWS - ROW_META,
)
_GROUP_PROMPT = dict(row_off=0, y_row_off=0, nb=BATCH, nchunk=SEQ // CHUNK, c=CHUNK, bb=4, bcast_init=True, stream=True)
_GROUP_SAMPLE = dict(row_off=0, y_row_off=ROW_SAMPLE, nb=DEC_BATCH, nchunk=1, c=DEC_SEQ, bb=16, bcast_init=False)


def _rows_to_batch_major_kernel(x_ref, o_ref):
    o_ref[...] = x_ref[...].reshape(o_ref.shape)


def _prompt_rows_to_batch_major(x):
    nchunk = SEQ // CHUNK
    out = pl.pallas_call(
        _rows_to_batch_major_kernel,
        grid=(nchunk,),
        in_specs=[pl.BlockSpec((BATCH * CHUNK, D_MODEL), lambda ci: (ci, 0))],
        out_specs=pl.BlockSpec((BATCH, None, CHUNK, D_MODEL), lambda ci: (0, ci, 0, 0)),
        out_shape=jax.ShapeDtypeStruct((BATCH, nchunk, CHUNK, D_MODEL), x.dtype),
        compiler_params=pltpu.CompilerParams(dimension_semantics=("parallel",)),
        name="rows_to_batch_major",
    )(x)
    return out.reshape(BATCH, SEQ, D_MODEL)


def _rotary_tables(pos):
    half = DH // 2
    inv = ROPE_BASE ** (-jnp.arange(half, dtype=_f32) / half)
    ang = pos.astype(_f32)[:, None] * inv[None, :]
    cos, sin = jnp.cos(ang), jnp.sin(ang)
    return jnp.concatenate([cos, cos], -1), jnp.concatenate([-sin, sin], -1)


def _lanes(m):
    return jnp.broadcast_to(m.astype(_f32)[..., None], m.shape + (LANES,))


def _conv_rows(s):
    return jnp.pad(s.astype(_f32), ((0, 0), (8 - (CONV_W - 1), 0), (0, 0)))


def kernel(x_prompt, x_sample, state_ret, state_mlstm_c, state_mlstm_n, state_mlstm_m, state_gdn, state_gdn_conv, state_gla, meta_tokens, w_in0, ret_norm_g, mlstm_gate_bias, mlstm_norm_g, w_out0, ln0_mix_g, ln0_mix_b, ffn0_w_gate, ffn0_w_up, ffn0_w_down, ln0_ffn_g, ln0_ffn_b, w_in1, gdn_conv_w, gdn_a_log, gdn_dt_bias, gdn_norm_g, gla_w_alpha, gla_b_alpha, gla_norm_g, w_out1, ln1_mix_g, ln1_mix_b, moe_w_router, moe_b_router, moe_w_gate, moe_w_up, moe_w_down, ln1_ffn_g, ln1_ffn_b):
    w4 = NH * DH
    nchunk = SEQ // CHUNK
    xp = x_prompt.reshape(BATCH, nchunk, CHUNK, D_MODEL).transpose(1, 0, 2, 3).reshape(N_PROMPT, D_MODEL)
    x = jnp.concatenate(
        [
            xp,
            x_sample.reshape(N_SAMPLE, D_MODEL),
            meta_tokens.astype(x_prompt.dtype),
            jnp.zeros((R_ROWS - ROW_META - N_META, D_MODEL), x_prompt.dtype),
        ],
        0,
    )

    w_in0_p = jnp.pad(w_in0, ((0, 0), (0, P0_PAD - w_in0.shape[1]))).astype(_bf16)
    c_z_end = QKV_C + w4
    c_qd = c_z_end + 2 * NH
    c_lr = c_qd + 2 * NH * DK_D + 2 * w4
    gap = jnp.zeros((D_MODEL, P1_PAD - LANES - P1_MAIN), w_in1.dtype)
    w_in1_p = jnp.concatenate([w_in1[:, :c_z_end], w_in1[:, c_qd:c_lr], gap, w_in1[:, c_lr:], w_in1[:, c_z_end:c_qd]], 1)
    w_in1_p = jnp.pad(w_in1_p, ((0, 0), (0, P1_PAD - w_in1_p.shape[1]))).astype(_bf16)
    gb_row = jnp.pad(mlstm_gate_bias.astype(_f32), (0, LANES - 2 * NH))[None]
    norm0 = jnp.stack([ret_norm_g, mlstm_norm_g]).astype(_f32)
    norm1 = jnp.stack([gdn_norm_g, gla_norm_g]).astype(_f32)
    lo = GLA_RANK + NH
    gdn_par = jnp.stack(
        [
            jnp.pad(gdn_a_log.astype(_f32), (lo, LANES - lo - NH)),
            jnp.pad(gdn_dt_bias.astype(_f32), (lo, LANES - lo - NH)),
        ]
    )
    w_router = jnp.pad(moe_w_router.astype(_f32), ((0, 0), (0, LANES - N_EXPERTS)))
    b_router = jnp.pad(moe_b_router.astype(_f32), (0, LANES - N_EXPERTS), constant_values=-jnp.inf)[None]

    def row(v):
        return v.astype(_f32)[None]

    proj0 = _proj(x[ROW_SAMPLE:], w_in0_p, tn=P0_PAD // 3)
    zeros_even = (
        jnp.zeros((1, NH, DH, DH), _f32), jnp.zeros((1, NH, DH, DH), _f32),
        jnp.zeros((1, NH, DH), _f32), jnp.zeros((1, NH, LANES), _f32),
    )
    cos_m, sin_m = _rotary_tables(jnp.arange(N_META))
    cos_p, sin_p = _rotary_tables(N_META + jnp.arange(SEQ))
    cos_s, sin_s = _rotary_tables(PAST_LEN + jnp.arange(DEC_SEQ))
    ybuf, *meta_even = _mixer0(proj0, None, cos_m, sin_m, gb_row, norm0, zeros_even, **_GROUP_META)
    ybuf, p_ret, p_mc, p_mn, p_mm = _mixer0((x, w_in0_p), ybuf, cos_p, sin_p, gb_row, norm0, meta_even, **_GROUP_PROMPT)
    init_s = (state_ret.astype(_f32), state_mlstm_c.astype(_f32), state_mlstm_n.astype(_f32), _lanes(state_mlstm_m))
    ybuf, s_ret, s_mc, s_mn, s_mm = _mixer0(proj0, ybuf, cos_s, sin_s, gb_row, norm0, init_s, **_GROUP_SAMPLE)
    x = _out_ln(ybuf, w_out0.astype(_bf16), x, row(ln0_mix_g), row(ln0_mix_b))
    x = _ffn_ln(x, ffn0_w_gate.astype(_bf16), ffn0_w_up.astype(_bf16), ffn0_w_down.astype(_bf16), row(ln0_ffn_g), row(ln0_ffn_b))

    proj1 = _proj(x[ROW_SAMPLE:], w_in1_p, tn=P1_PAD // 3)
    zeros_odd = (jnp.zeros((1, NH, DH, DH), _f32), jnp.zeros((1, 8, QKV_C), _f32), jnp.zeros((1, NH, DK_D, DH), _f32))
    m1_par = (gdn_conv_w.astype(_f32), gdn_par, gla_w_alpha.astype(_f32), row(gla_b_alpha), norm1)
    ybuf, *meta_odd = _mixer1(proj1, None, *m1_par, zeros_odd, **_GROUP_META)
    ybuf, p_gdn, p_conv, p_gla = _mixer1((x, w_in1_p), ybuf, *m1_par, meta_odd, **_GROUP_PROMPT)
    init_s = (state_gdn.astype(_f32), _conv_rows(state_gdn_conv), state_gla.astype(_f32))
    ybuf, s_gdn, s_conv, s_gla = _mixer1(proj1, ybuf, *m1_par, init_s, **_GROUP_SAMPLE)
    x = _out_ln(ybuf, w_out1.astype(_bf16), x, row(ln1_mix_g), row(ln1_mix_b))
    gate_t, code_t, counts, moe_w = _router(x, w_router, b_router, moe_w_gate, moe_w_up, moe_w_down)
    counts = counts[:, 0, :N_EXPERTS].reshape(-1)
    x = _moe_ln(x, gate_t, code_t, counts, *moe_w, row(ln1_ffn_g), row(ln1_ffn_b))

    y_prompt = _prompt_rows_to_batch_major(x)
    y_sample = x[ROW_SAMPLE:ROW_META].reshape(DEC_BATCH, DEC_SEQ, D_MODEL)
    tail = 8 - (CONV_W - 1)
    return (
        y_prompt, y_sample,
        p_ret, p_mc, p_mn, p_mm[..., 0], p_gdn, p_conv[:, tail:], p_gla,
        s_ret, s_mc, s_mn, s_mm[..., 0], s_gdn, s_conv[:, tail:], s_gla,
    )
```

```python
import functools
import math

import jax
import jax.numpy as jnp
import numpy as np
from jax import lax
from jax.experimental import pallas as pl
from jax.experimental.pallas import tpu as pltpu

D_MODEL = 1024
BATCH = 8
SEQ = 2048
DEC_BATCH = 128
DEC_SEQ = 4
PAST_LEN = 16384
N_META = 16
CHUNK = 64
NH = 4
DH = 128
DK_D = 64
CONV_W = 4
GLA_RANK = 16
GLA_TAU = 16.0
D_FF = 2816
N_EXPERTS = 8
MOE_FF = 1408
ROPE_BASE = 10000.0
LN_EPS = 1e-5
NORM_EPS = 1e-6
DEPTH = 2
ALPHA = (2 * DEPTH) ** 0.25
QKV_C = 3 * NH * DH

LANES = 128
N_PROMPT = BATCH * SEQ
N_SAMPLE = DEC_BATCH * DEC_SEQ
ROW_SAMPLE = N_PROMPT
ROW_META = N_PROMPT + N_SAMPLE
TM_MOE = 448
MOE_BLOCK = 128
MOE_GROUP = 2
TM = TM_MOE * MOE_GROUP
R_ROWS = ((ROW_META + N_META + TM - 1) // TM) * TM

P0_PAD = 33 * LANES
P1_MAIN = 28 * LANES
P1_PAD = 30 * LANES
SOLVE_BLOCK = 16

VMEM_LIMIT = 56 * 1024 * 1024

_bf16 = jnp.bfloat16
_f32 = jnp.float32
_HI = lax.Precision.HIGHEST


def _dot_hi(a, b):
    return jnp.dot(a, b, preferred_element_type=_f32, precision=_HI)


def _bmm(a, b):
    return jnp.einsum("nmk,nkp->nmp", a.astype(_bf16), b.astype(_bf16), preferred_element_type=_f32)


def _bmm_nt(a, b):
    return jnp.einsum("nmk,npk->nmp", a.astype(_bf16), b.astype(_bf16), preferred_element_type=_f32)


def _bmm_tn(a, b):
    return jnp.einsum("nkm,nkp->nmp", a.astype(_bf16), b.astype(_bf16), preferred_element_type=_f32)


def _split2(a):
    hi = a.astype(_bf16)
    lo = (a - hi.astype(_f32)).astype(_bf16)
    return hi, lo


def _bmm_x3(a, b):
    ah, al = _split2(a)
    bh, bl = _split2(b)
    mm = functools.partial(jnp.einsum, "nmk,nkp->nmp", preferred_element_type=_f32)
    return mm(ah, bh) + (mm(ah, bl) + mm(al, bh))


def _sigmoid(x):
    return 1.0 / (1.0 + jnp.exp(-x))


def _silu(x):
    return x * _sigmoid(x)


def _softplus(x):
    return jnp.maximum(x, 0.0) + jnp.log1p(jnp.exp(-jnp.abs(x)))


def _group_norm(o, g_row, rms):
    if not rms:
        o = o - jnp.mean(o, axis=-1, keepdims=True)
    return o * lax.rsqrt(jnp.mean(o * o, axis=-1, keepdims=True) + NORM_EPS) * g_row


def _layer_norm(x, g_row, b_row):
    mu = jnp.mean(x, axis=-1, keepdims=True)
    xc = x - mu
    var = jnp.mean(xc * xc, axis=-1, keepdims=True)
    return xc * lax.rsqrt(var + LN_EPS) * g_row + b_row


def _tri3(c):
    t = lax.broadcasted_iota(jnp.int32, (1, c, c), 1)
    s = lax.broadcasted_iota(jnp.int32, (1, c, c), 2)
    return t, s


def _cumsum_col_row(x_col, x_row, c):
    t, s = _tri3(c)
    col = jnp.sum(jnp.where(t >= s, x_row, 0.0), axis=2, keepdims=True)
    row = jnp.sum(jnp.where(t <= s, x_col, 0.0), axis=1, keepdims=True)
    return col, row


def _proj_kernel(x_ref, w_ref, o_ref, tail_ref, xb_ref):
    j = pl.program_id(1)

    @pl.when(j == 0)
    def _():
        xb_ref[...] = x_ref[...].astype(_bf16)

    acc = jnp.dot(xb_ref[...], w_ref[...], preferred_element_type=_f32)
    o_ref[...] = acc.astype(o_ref.dtype)

    @pl.when(j == pl.num_programs(1) - 1)
    def _():
        tail_ref[...] = acc[:, acc.shape[1] - LANES :]


def _proj(x, w, tn):
    rows, k = x.shape
    n = w.shape[1]
    tm = min(TM, rows)
    return pl.pallas_call(
        _proj_kernel,
        grid=(rows // tm, n // tn),
        in_specs=[pl.BlockSpec((tm, k), lambda i, j: (i, 0)), pl.BlockSpec((k, tn), lambda i, j: (0, j))],
        out_specs=[pl.BlockSpec((tm, tn), lambda i, j: (i, j)), pl.BlockSpec((tm, LANES), lambda i, j: (i, 0))],
        out_shape=[jax.ShapeDtypeStruct((rows, n), _bf16), jax.ShapeDtypeStruct((rows, LANES), _f32)],
        scratch_shapes=[pltpu.VMEM((tm, k), _bf16)],
        compiler_params=pltpu.CompilerParams(dimension_semantics=("parallel", "arbitrary"), vmem_limit_bytes=VMEM_LIMIT),
        name="in_proj",
    )(x, w)


def _out_ln_kernel(y_ref, w_ref, x_ref, g_ref, b_ref, o_ref):
    h = jnp.dot(y_ref[...], w_ref[...], preferred_element_type=_f32)
    o_ref[...] = _layer_norm(ALPHA * x_ref[...] + h, g_ref[...], b_ref[...])


def _out_ln(y, w, x, g, b):
    rows, k = y.shape
    d = w.shape[1]
    return pl.pallas_call(
        _out_ln_kernel,
        grid=(rows // TM,),
        in_specs=[
            pl.BlockSpec((TM, k), lambda i: (i, 0)),
            pl.BlockSpec((k, d), lambda i: (0, 0)),
            pl.BlockSpec((TM, d), lambda i: (i, 0)),
            pl.BlockSpec((1, d), lambda i: (0, 0)),
            pl.BlockSpec((1, d), lambda i: (0, 0)),
        ],
        out_specs=pl.BlockSpec((TM, d), lambda i: (i, 0)),
        out_shape=jax.ShapeDtypeStruct((rows, d), _f32),
        compiler_params=pltpu.CompilerParams(dimension_semantics=("parallel",), vmem_limit_bytes=VMEM_LIMIT),
        name="out_proj_ln",
    )(y, w, x, g, b)


FF_SUB = (0, 512, 1024, 1408)


def _swiglu_partial(xb, wg_ref, wu_ref, wd_ref):
    out = None
    for lo, hi in zip(FF_SUB[:-1], FF_SUB[1:]):
        hg = jnp.dot(xb, wg_ref[:, lo:hi], preferred_element_type=_f32)
        hu = jnp.dot(xb, wu_ref[:, lo:hi], preferred_element_type=_f32)
        a = (_silu(hg) * hu).astype(_bf16)
        part = jnp.dot(a, wd_ref[lo:hi, :], preferred_element_type=_f32)
        out = part if out is None else out + part
    return out


def _ffn_ln_kernel(x_ref, wg_ref, wu_ref, wd_ref, g_ref, b_ref, o_ref, xb_ref, acc_ref):
    f = pl.program_id(1)

    @pl.when(f == 0)
    def _():
        xb_ref[...] = x_ref[...].astype(_bf16)
        acc_ref[...] = jnp.zeros_like(acc_ref)

    acc_ref[...] += _swiglu_partial(xb_ref[...], wg_ref, wu_ref, wd_ref)

    @pl.when(f == pl.num_programs(1) - 1)
    def _():
        o_ref[...] = _layer_norm(ALPHA * x_ref[...] + acc_ref[...], g_ref[...], b_ref[...])


def _ffn_ln(x, wg, wu, wd, g, b):
    rows, d = x.shape
    ff = wg.shape[1]
    tf = MOE_FF
    return pl.pallas_call(
        _ffn_ln_kernel,
        grid=(rows // TM, ff // tf),
        in_specs=[
            pl.BlockSpec((TM, d), lambda i, f: (i, 0)),
            pl.BlockSpec((d, tf), lambda i, f: (0, f)),
            pl.BlockSpec((d, tf), lambda i, f: (0, f)),
            pl.BlockSpec((tf, d), lambda i, f: (f, 0)),
            pl.BlockSpec((1, d), lambda i, f: (0, 0)),
            pl.BlockSpec((1, d), lambda i, f: (0, 0)),
        ],
        out_specs=pl.BlockSpec((TM, d), lambda i, f: (i, 0)),
        out_shape=jax.ShapeDtypeStruct((rows, d), _f32),
        scratch_shapes=[pltpu.VMEM((TM, d), _bf16), pltpu.VMEM((TM, d), _f32)],
        compiler_params=pltpu.CompilerParams(dimension_semantics=("parallel", "arbitrary"), vmem_limit_bytes=VMEM_LIMIT),
        name="ffn_ln",
    )(x, wg, wu, wd, g, b)


ROUTE_ROWS = 16


def _route_tile(x, wr_t, bias_col, before):
    xh, xl = _split2(x)
    wh, wl = _split2(wr_t)
    nt = functools.partial(lax.dot_general, dimension_numbers=(((1,), (1,)), ((), ())), preferred_element_type=_f32)
    logits = nt(wh, xh) + (nt(wh, xl) + nt(wl, xh)) + bias_col
    row = lax.broadcasted_iota(jnp.int32, logits.shape, 0)
    ex = jnp.exp(logits - jnp.max(logits, axis=0, keepdims=True))
    probs = ex / jnp.sum(ex, axis=0, keepdims=True)
    p1 = jnp.max(probs, axis=0, keepdims=True)
    i1 = jnp.min(jnp.where(probs == p1, row, ROUTE_ROWS), axis=0, keepdims=True)
    rest = jnp.where(row == i1, -1.0, probs)
    p2 = jnp.max(rest, axis=0, keepdims=True)
    i2 = jnp.min(jnp.where(rest == p2, row, ROUTE_ROWS), axis=0, keepdims=True)
    tot = p1 + p2
    gate_t = jnp.where(row == i1, p1 / tot, 0.0) + jnp.where(row == i2, p2 / tot, 0.0)
    sel_t = jnp.where(row == i1, 1.0, jnp.where(row == i2, 1.0, 0.0))
    rank_t = jnp.dot(sel_t.astype(_bf16), before, preferred_element_type=_f32)
    code_t = jnp.where(sel_t > 0.0, rank_t, -1.0)
    return gate_t[:N_EXPERTS], code_t[:N_EXPERTS], jnp.sum(sel_t, axis=1, keepdims=True)


def _out_ln_route_kernel(y_ref, w_ref, x_ref, g_ref, b_ref, wr_ref, br_ref, before_ref, o_ref, gate_t_ref, code_t_ref, cnt_ref):
    h = jnp.dot(y_ref[...], w_ref[...], preferred_element_type=_f32)
    x_new = _layer_norm(ALPHA * x_ref[...] + h, g_ref[...], b_ref[...])
    o_ref[...] = x_new
    for i in range(MOE_GROUP):
        gate_t, code_t, cnt = _route_tile(x_new[i * TM_MOE : (i + 1) * TM_MOE], wr_ref[...], br_ref[:, 0:1], before_ref[...])
        gate_t_ref[i] = gate_t
        code_t_ref[i] = code_t
        cnt_ref[i] = jnp.broadcast_to(cnt, (ROUTE_ROWS, LANES)).astype(jnp.int32)


def _out_ln_route(y, w, x, g, b, wr_t, br_col):
    rows, k = y.shape
    d = w.shape[1]
    nt = rows // TM_MOE
    t = np.arange(TM_MOE)
    before = jnp.asarray(t[:, None] < t[None, :], _bf16)

    def tile_spec(shape):
        return pl.BlockSpec((MOE_GROUP,) + shape, lambda i: (i, 0, 0))

    x_new, gate_t, code_t, cnt = pl.pallas_call(
        _out_ln_route_kernel,
        grid=(rows // TM,),
        in_specs=[
            pl.BlockSpec((TM, k), lambda i: (i, 0)),
            pl.BlockSpec((k, d), lambda i: (0, 0)),
            pl.BlockSpec((TM, d), lambda i: (i, 0)),
            pl.BlockSpec((1, d), lambda i: (0, 0)),
            pl.BlockSpec((1, d), lambda i: (0, 0)),
            pl.BlockSpec((ROUTE_ROWS, d), lambda i: (0, 0)),
            pl.BlockSpec((ROUTE_ROWS, LANES), lambda i: (0, 0)),
            pl.BlockSpec((TM_MOE, TM_MOE), lambda i: (0, 0)),
        ],
        out_specs=[
            pl.BlockSpec((TM, d), lambda i: (i, 0)),
            tile_spec((N_EXPERTS, TM_MOE)),
            tile_spec((N_EXPERTS, TM_MOE)),
            tile_spec((ROUTE_ROWS, LANES)),
        ],
        out_shape=[
            jax.ShapeDtypeStruct((rows, d), _f32),
            jax.ShapeDtypeStruct((nt, N_EXPERTS, TM_MOE), _f32),
            jax.ShapeDtypeStruct((nt, N_EXPERTS, TM_MOE), _f32),
            jax.ShapeDtypeStruct((nt, ROUTE_ROWS, LANES), jnp.int32),
        ],
        compiler_params=pltpu.CompilerParams(dimension_semantics=("parallel",), vmem_limit_bytes=VMEM_LIMIT),
        name="out_proj_ln_route",
    )(y, w, x, g, b, wr_t, br_col, before)
    return x_new, gate_t, code_t, cnt[:, :N_EXPERTS, 0].reshape(-1)


def _moe_ln_kernel(cnt_ref, x_ref, gate_t_ref, code_t_ref, wg_ref, wu_ref, wd_ref, g_ref, b_ref, o_ref, xb_ref, acc_ref, first_ref):
    grp = pl.program_id(0)
    e = pl.program_id(1)

    @pl.when(e == 0)
    def _():
        xb_ref[...] = x_ref[...].astype(_bf16)
        acc_ref[...] = jnp.zeros_like(acc_ref)

    slot_col = lax.broadcasted_iota(jnp.int32, (MOE_BLOCK, 1), 0)
    counts = [cnt_ref[(grp * MOE_GROUP + i) * N_EXPERTS + e] for i in range(MOE_GROUP)]
    code_rows = [code_t_ref[i, pl.ds(e, 1), :] for i in range(MOE_GROUP)]
    gate_rows = [gate_t_ref[i, pl.ds(e, 1), :] for i in range(MOE_GROUP)]
    first_rows = pl.ds(pl.multiple_of(e * MOE_BLOCK, MOE_BLOCK), MOE_BLOCK)
    scatter_dims = (((0,), (0,)), ((), ()))

    def gather_rows(i, blk):
        hit = code_rows[i] == (slot_col + blk * MOE_BLOCK).astype(_f32)
        onehot = jnp.where(hit, 1.0, 0.0).astype(_bf16)
        xs = jnp.dot(onehot, xb_ref[i * TM_MOE : (i + 1) * TM_MOE, :], preferred_element_type=_f32).astype(_bf16)
        gate = jnp.sum(jnp.where(hit, gate_rows[i], 0.0), axis=1, keepdims=True)
        return onehot, xs, gate

    def first_pass(i):
        _, xs, gate = gather_rows(i, 0)
        first_ref[i, first_rows, :] = (_swiglu_partial(xs, wg_ref, wu_ref, wd_ref) * gate).astype(_bf16)

    def no_pass(i):
        first_ref[i, first_rows, :] = jnp.zeros((MOE_BLOCK, first_ref.shape[2]), _bf16)

    assert MOE_GROUP == 2
    has0 = counts[0] > 0
    has1 = counts[1] > 0

    @pl.when(jnp.logical_and(has0, has1))
    def _():
        _, xs0, gate0 = gather_rows(0, 0)
        _, xs1, gate1 = gather_rows(1, 0)
        out = _swiglu_partial(jnp.concatenate([xs0, xs1], axis=0), wg_ref, wu_ref, wd_ref)
        first_ref[0, first_rows, :] = (out[:MOE_BLOCK] * gate0).astype(_bf16)
        first_ref[1, first_rows, :] = (out[MOE_BLOCK:] * gate1).astype(_bf16)

    @pl.when(jnp.logical_and(has0, jnp.logical_not(has1)))
    def _():
        first_pass(0)
        no_pass(1)

    @pl.when(jnp.logical_and(has1, jnp.logical_not(has0)))
    def _():
        no_pass(0)
        first_pass(1)

    @pl.when(jnp.logical_and(jnp.logical_not(has0), jnp.logical_not(has1)))
    def _():
        no_pass(0)
        no_pass(1)

    for blk in range(1, -(-TM_MOE // MOE_BLOCK)):
        for i in range(MOE_GROUP):

            @pl.when(counts[i] > blk * MOE_BLOCK)
            def _():
                onehot, xs, gate = gather_rows(i, blk)
                out = (_swiglu_partial(xs, wg_ref, wu_ref, wd_ref) * gate).astype(_bf16)
                acc_ref[i * TM_MOE : (i + 1) * TM_MOE, :] += lax.dot_general(onehot, out, scatter_dims, preferred_element_type=_f32)

    @pl.when(e == pl.num_programs(1) - 1)
    def _():
        slot = lax.broadcasted_iota(jnp.int32, (1, MOE_BLOCK, 1), 1).astype(_f32)
        for i in range(MOE_GROUP):
            hit = code_t_ref[i][:, None, :] == slot
            onehot = jnp.where(hit, 1.0, 0.0).astype(_bf16).reshape(N_EXPERTS * MOE_BLOCK, TM_MOE)
            rows = slice(i * TM_MOE, (i + 1) * TM_MOE)
            moe = acc_ref[rows, :] + lax.dot_general(onehot, first_ref[i], scatter_dims, preferred_element_type=_f32)
            o_ref[rows, :] = _layer_norm(ALPHA * x_ref[rows, :] + moe, g_ref[...], b_ref[...])


def _moe_ln(x, gate_t, code_t, counts, wg, wu, wd, g, b):
    rows, d = x.shape
    ne, _, ff = wg.shape
    grid_spec = pltpu.PrefetchScalarGridSpec(
        num_scalar_prefetch=1,
        grid=(rows // TM, ne),
        in_specs=[
            pl.BlockSpec((TM, d), lambda i, e, cnt: (i, 0)),
            pl.BlockSpec((MOE_GROUP, N_EXPERTS, TM_MOE), lambda i, e, cnt: (i, 0, 0)),
            pl.BlockSpec((MOE_GROUP, N_EXPERTS, TM_MOE), lambda i, e, cnt: (i, 0, 0)),
            pl.BlockSpec((None, d, ff), lambda i, e, cnt: (e, 0, 0)),
            pl.BlockSpec((None, d, ff), lambda i, e, cnt: (e, 0, 0)),
            pl.BlockSpec((None, ff, d), lambda i, e, cnt: (e, 0, 0)),
            pl.BlockSpec((1, d), lambda i, e, cnt: (0, 0)),
            pl.BlockSpec((1, d), lambda i, e, cnt: (0, 0)),
        ],
        out_specs=pl.BlockSpec((TM, d), lambda i, e, cnt: (i, 0)),
        scratch_shapes=[
            pltpu.VMEM((TM, d), _bf16),
            pltpu.VMEM((TM, d), _f32),
            pltpu.VMEM((MOE_GROUP, ne * MOE_BLOCK, d), _bf16),
        ],
    )
    return pl.pallas_call(
        _moe_ln_kernel,
        grid_spec=grid_spec,
        out_shape=jax.ShapeDtypeStruct((rows, d), _f32),
        compiler_params=pltpu.CompilerParams(dimension_semantics=("parallel", "arbitrary"), vmem_limit_bytes=VMEM_LIMIT),
        name="moe_ln",
    )(counts, x, gate_t, code_t, wg, wu, wd, g, b)


def _chains(a, c, bb, width, off=0):
    return jnp.stack([a[j * c : (j + 1) * c, off + h * width : off + (h + 1) * width] for j in range(bb) for h in range(NH)])


def _chain_cols(a, c, bb, lane0):
    return jnp.stack([a[j * c : (j + 1) * c, lane0 + h : lane0 + h + 1] for j in range(bb) for h in range(NH)])


def _chain_rows(a_t, c, bb, lane0):
    return jnp.stack([a_t[lane0 + h : lane0 + h + 1, j * c : (j + 1) * c] for j in range(bb) for h in range(NH)])


def _head_rows(ref, row, bb):
    return jnp.stack([ref[row : row + 1, h * DH : (h + 1) * DH] for _ in range(bb) for h in range(NH)])


def _store_chains(y_ref, y, c, bb, off):
    for j in range(bb):
        for h in range(NH):
            y_ref[j * c : (j + 1) * c, off + h * DH : off + (h + 1) * DH] = y[j * NH + h].astype(y_ref.dtype)
    if y_ref.shape[0] > bb * c:
        y_ref[bb * c :, off : off + NH * DH] = jnp.zeros((y_ref.shape[0] - bb * c, NH * DH), y_ref.dtype)


def _drop_alias_ref(kernel, n_in):
    def body(*refs):
        return kernel(*refs[:n_in], None, *refs[n_in:])

    return body


def _y_buffer_specs(ybuf, rows, y_rows, y_row_off, nbb, width):
    if y_rows == rows:
        spec = pl.BlockSpec((rows, width), lambda i, ci: (y_row_off // rows + ci * nbb + i, 0))
    else:
        spec = pl.BlockSpec((y_rows, width), lambda i, ci: (y_row_off // y_rows, 0))
    return spec, jax.ShapeDtypeStruct((R_ROWS, width), _bf16), ([] if ybuf is None else [pl.BlockSpec(memory_space=pl.ANY)])


PIECE = NH * DH


def _stream_projection(xn_ref, xc_ref, w_ref, main_sc, tail_sc, xb_sc, n_piece):
    rows = xn_ref.shape[0]
    ci = pl.program_id(1)
    tail_col = w_ref.shape[1] - LANES

    def project(xb, row0, p):
        if p < n_piece:
            cols = slice(p * PIECE, (p + 1) * PIECE)
            main_sc[pl.ds(row0, rows), cols] = jnp.dot(xb, w_ref[:, cols], preferred_element_type=_f32).astype(_bf16)
        else:
            tail_sc[pl.ds(row0, rows), :] = jnp.dot(xb, w_ref[:, tail_col:], preferred_element_type=_f32)

    @pl.when(ci == 0)
    def _():
        xb0 = xc_ref[...].astype(_bf16)
        for p in range(n_piece + 1):
            project(xb0, 0, p)

    cur = pl.multiple_of((ci % 2) * rows, rows)
    nxt = pl.multiple_of(((ci + 1) % 2) * rows, rows)
    xb_sc[...] = xn_ref[...].astype(_bf16)
    pending = iter(range(n_piece + 1))

    def between(count=1):
        for _ in range(count):
            p = next(pending, None)
            if p is not None:
                project(xb_sc[...], nxt, p)

    def load(p):
        return main_sc[pl.ds(cur, rows), p * PIECE : (p + 1) * PIECE].astype(_f32)

    return load, tail_sc[pl.ds(cur, rows), :], between


def _retention_chunk(q, k, v, s_prev, dec, vec):
    att = _bmm_nt(q, k) * dec
    o = _bmm(att, v) + vec[:, :, 0:1] * _bmm(q, s_prev)
    s_new = vec[:, 0:1, 2:3] * s_prev + _bmm_tn(k * vec[:, :, 1:2], v)
    return o, s_new


def _mlstm_chunk(q, k, v, it_col, it_row, lf_col, lf_row, c_prev, n_prev, m_prev, c):
    t, s = _tri3(c)
    b_col, b_row = _cumsum_col_row(lf_col, lf_row, c)
    logw = jnp.where(t >= s, b_col - b_row + it_row, -jnp.inf)
    m_t = jnp.maximum(b_col + m_prev, jnp.max(logw, axis=2, keepdims=True))
    w = jnp.exp(logw - m_t)
    carry = jnp.exp(b_col + m_prev - m_t)
    qk = _bmm_nt(q, k) * w
    num = _bmm(qk, v) + carry * _bmm(q, c_prev)
    den = jnp.sum(qk, axis=2, keepdims=True) + carry * jnp.sum(q * n_prev, axis=2, keepdims=True)
    h = num / jnp.maximum(jnp.abs(den), jnp.exp(-m_t))
    m_new = m_t[:, c - 1 : c, :]
    b_last = b_col[:, c - 1 : c, :]
    w_last = jnp.exp(b_last - b_col + it_col - m_new)
    decay = jnp.exp(b_last + m_prev - m_new)
    kw = k * w_last
    c_new = decay * c_prev + _bmm_tn(kw, v)
    n_new = decay * n_prev + jnp.sum(kw, axis=1, keepdims=True)
    return h, c_new, n_new, m_new


N_PIECE0 = 8


def _mixer0_body(
    load, gate_tail, between,
    cos_ref, sin_ref, dec_ref, vec_ref, gb_ref, ng_ref, s0_ref, c0_ref, n0_ref, m0_ref,
    y_ref, s_ref, c_ref, n_ref, m_ref, *, c, bb,
):
    n = bb * NH

    @pl.when(pl.program_id(1) == 0)
    def _():
        s_ref[...] = jnp.broadcast_to(s0_ref[...], s_ref.shape)
        c_ref[...] = jnp.broadcast_to(c0_ref[...], c_ref.shape)
        n_ref[...] = jnp.broadcast_to(n0_ref[...], n_ref.shape)
        m_ref[...] = jnp.broadcast_to(m0_ref[...], m_ref.shape)

    def chains(p):
        return _chains(load(p), c, bb, DH)

    cosf = cos_ref[...][None]
    sinf = sin_ref[...][None]
    gates = gate_tail + gb_ref[...]
    lane = lax.broadcasted_iota(jnp.int32, gates.shape, 1)
    gates = jnp.where(lane < NH, gates, -_softplus(-gates))
    gates_t = jnp.transpose(gates)

    q = chains(0)
    k = chains(1)
    q = q * cosf + pltpu.roll(q, DH // 2, axis=2) * sinf
    k = (k * cosf + pltpu.roll(k, DH // 2, axis=2) * sinf) * DH**-0.5
    between(2)
    dec = jnp.concatenate([dec_ref[...]] * bb, axis=0)
    vec = jnp.concatenate([vec_ref[...]] * bb, axis=0)
    o, s_new = _retention_chunk(q, k, chains(2), s_ref[...].reshape(n, DH, DH), dec, vec)
    between(2)
    s_ref[...] = s_new.reshape(s_ref.shape)
    y_a = _group_norm(o, _head_rows(ng_ref, 0, bb), rms=False) * _silu(chains(3))
    _store_chains(y_ref, y_a, c, bb, 0)
    between(1)

    n_prev = jnp.stack([n_ref[j, h : h + 1, :] for j in range(bb) for h in range(NH)])
    m_prev = jnp.stack([m_ref[j, h : h + 1, 0:1] for j in range(bb) for h in range(NH)])
    h_b, c_new, n_new, m_new = _mlstm_chunk(
        chains(4), chains(5) * DH**-0.5, chains(6),
        _chain_cols(gates, c, bb, 0), _chain_rows(gates_t, c, bb, 0),
        _chain_cols(gates, c, bb, NH), _chain_rows(gates_t, c, bb, NH),
        c_ref[...].reshape(n, DH, DH), n_prev, m_prev, c,
    )
    between(2)
    c_ref[...] = c_new.reshape(c_ref.shape)
    for j in range(bb):
        for h in range(NH):
            n_ref[j, h : h + 1, :] = n_new[j * NH + h]
            m_ref[j, h : h + 1, :] = jnp.broadcast_to(m_new[j * NH + h], (1, LANES))
    h_b = _sigmoid(chains(7)) * h_b
    between(1)
    y_b = _group_norm(h_b, _head_rows(ng_ref, 1, bb), rms=False)
    _store_chains(y_ref, y_b, c, bb, NH * DH)
    between(N_PIECE0 + 1)


def _no_op(count=1):
    del count


def _mixer0_kernel(*refs, c, bb):
    pieces, gate_ref, rest = refs[:N_PIECE0], refs[N_PIECE0], refs[N_PIECE0 + 1 :]
    consts, outs = rest[:10], rest[11:]
    _mixer0_body(lambda p: pieces[p][...].astype(_f32), gate_ref[...], _no_op, *consts, *outs, c=c, bb=bb)


def _mixer0_stream_kernel(xn_ref, xc_ref, w_ref, *rest, c, bb):
    consts, outs, scratch = rest[:10], rest[11:16], rest[16:]
    load, gate_tail, between = _stream_projection(xn_ref, xc_ref, w_ref, *scratch, N_PIECE0)
    _mixer0_body(load, gate_tail, between, *consts, *outs, c=c, bb=bb)


def _retention_tables(c):
    lg = np.log(1.0 - 2.0 ** (-5.0 - np.arange(NH, dtype=np.float64)))[:, None, None]
    t = np.arange(c, dtype=np.float64)
    diff = t[None, :, None] - t[None, None, :]
    dec = np.where(diff >= 0, np.exp(np.maximum(diff, 0.0) * lg), 0.0)
    vec = np.zeros((NH, c, LANES))
    vec[:, :, 0] = np.exp((t[None, :] + 1.0) * lg[:, 0])
    vec[:, :, 1] = np.exp((c - 1.0 - t[None, :]) * lg[:, 0])
    vec[:, :, 2] = np.exp(c * lg[:, 0])
    return jnp.asarray(dec, _f32), jnp.asarray(vec, _f32)


def _whole(a):
    return a, pl.BlockSpec(a.shape, lambda i, ci: (0,) * a.ndim)


def _mixer_call(name, kernels, source, n_piece, consts, init, ybuf, *, row_off, y_row_off, nb, nchunk, c, bb, bcast_init, y_rows=None, stream=False, casts=()):
    rows = bb * c
    nbb = nb // bb
    blk0 = row_off // rows
    ib = 1 if bcast_init else bb
    y_spec, y_shape, alias_spec = _y_buffer_specs(ybuf, rows, y_rows or rows, y_row_off, nbb, 2 * NH * DH)

    def row_block(col):
        return lambda i, ci: (blk0 + ci * nbb + i, col)

    def batch_block(nd, lead):
        return lambda i, ci: ((i if lead else 0),) + (0,) * nd

    if stream:
        x, w = source
        src_args = [x, x, w]
        src_specs = [
            pl.BlockSpec((rows, D_MODEL), lambda i, ci: (blk0 + jnp.minimum(ci + 1, nchunk - 1) * nbb + i, 0)),
            pl.BlockSpec((rows, D_MODEL), lambda i, ci: (blk0 + i, 0)),
            pl.BlockSpec(w.shape, lambda i, ci: (0, 0)),
        ]
        scratch = [pltpu.VMEM((2 * rows, n_piece * PIECE), _bf16), pltpu.VMEM((2 * rows, LANES), _f32), pltpu.VMEM((rows, D_MODEL), _bf16)]
    else:
        proj, tail = source
        src_args = [proj] * n_piece + [tail]
        src_specs = [pl.BlockSpec((rows, PIECE), row_block(p)) for p in range(n_piece)] + [pl.BlockSpec((rows, LANES), row_block(0))]
        scratch = []
    in_specs = src_specs + [spec for _, spec in consts]
    in_specs += [pl.BlockSpec((ib,) + a.shape[1:], batch_block(a.ndim - 1, not bcast_init)) for a in init]
    out_specs = [y_spec] + [pl.BlockSpec((bb,) + a.shape[1:], batch_block(a.ndim - 1, True)) for a in init]
    out_shape = [y_shape] + [jax.ShapeDtypeStruct((nb,) + a.shape[1:], _f32) for a in init]
    args = src_args + [a for a, _ in consts] + list(init)
    cast_specs = [pl.BlockSpec((a.shape[0] // (nbb * nchunk), a.shape[1]), lambda i, ci: (i * nchunk + ci, 0)) for a in casts]
    assert all(a.shape[0] % (16 * nbb * nchunk) == 0 for a in casts)
    body = functools.partial(kernels[int(stream)], c=c, bb=bb, **({"n_cast": len(casts)} if casts else {}))
    return pl.pallas_call(
        body if ybuf is not None else _drop_alias_ref(body, len(args)),
        grid=(nbb, nchunk),
        in_specs=in_specs + alias_spec + cast_specs,
        out_specs=out_specs + cast_specs,
        out_shape=out_shape + [jax.ShapeDtypeStruct(a.shape, _bf16) for a in casts],
        scratch_shapes=scratch,
        input_output_aliases={len(args): 0} if ybuf is not None else {},
        compiler_params=pltpu.CompilerParams(dimension_semantics=("parallel", "arbitrary"), vmem_limit_bytes=VMEM_LIMIT),
        name=f"{name}_c{c}",
    )(*args, *([] if ybuf is None else [ybuf]), *casts)


def _mixer0(source, ybuf, cosf, sinf, gb_row, norm_g, init, **group):
    c = group["c"]
    dec, vec = _retention_tables(c)
    by_chunk = pl.BlockSpec((c, DH), lambda i, ci: (ci, 0))
    consts = [(cosf, by_chunk), (sinf, by_chunk), _whole(dec), _whole(vec), _whole(gb_row), _whole(norm_g)]
    return _mixer_call("mixer0", (_mixer0_kernel, _mixer0_stream_kernel), source, N_PIECE0, consts, init, ybuf, **group)


def _unit_lower_solve(a, rhs, c):
    bs = min(SOLVE_BLOCK, c)
    t, s = _tri3(c)
    if c > bs:
        shift = bs.bit_length() - 1
        same = jnp.right_shift(t, shift) == jnp.right_shift(s, shift)
        d = jnp.where(same, a, 0.0)
    else:
        d = a
    inv = jnp.where(t == s, 1.0, 0.0) - d
    p = d
    span = 2
    while span < bs:
        p = _bmm_x3(p, p)
        inv = inv + _bmm_x3(inv, p)
        span *= 2
    y = _bmm(inv, rhs)
    if c == bs:
        return y
    b = _bmm(inv, jnp.where(same, 0.0, a))
    y = y - _bmm(b, y)
    p = b
    span = 2
    while span < c // bs:
        p = _bmm(p, p)
        y = y + _bmm(p, y)
        span *= 2
    return y


def _gdn_chunk(q, k, v, beta_col, g_col, g_row, s_prev, c, between=_no_op):
    t, s = _tri3(c)
    gc_col, gc_row = _cumsum_col_row(g_col, g_row, c)
    dec_incl = jnp.exp(jnp.where(t >= s, gc_col - gc_row, -jnp.inf))
    dec_strict = jnp.where(t > s, dec_incl, 0.0)
    e_col = jnp.exp(gc_col)
    a = beta_col * _bmm_nt(k, k) * dec_strict
    rhs = jnp.concatenate([beta_col * v, (beta_col * e_col) * k], axis=-1)
    between(1)
    sol = _unit_lower_solve(a, rhs, c)
    between(1)
    u = sol[:, :, :DH] - _bmm(sol[:, :, DH:], s_prev)
    qk = _bmm_nt(q, k) * dec_incl
    between(1)
    o = e_col * _bmm(q, s_prev) + _bmm(qk, u)
    gl = gc_col[:, c - 1 : c, :]
    s_new = jnp.exp(gl) * s_prev + _bmm_tn(k * jnp.exp(gl - gc_col), u)
    return o, s_new


def _gla_chunk(q, k, v, bc, s_prev, c):
    t, s = _tri3(c)
    qe = q * jnp.exp(bc)
    att = jnp.where(t >= s, _bmm_nt(qe, k * jnp.exp(-bc)), 0.0)
    o = _bmm(att, v) + _bmm(qe, s_prev)
    bl = bc[:, c - 1 : c, :]
    ti, si = _tri3(DK_D)
    el_col = jnp.sum(jnp.where(ti == si, jnp.exp(bl), 0.0), axis=2, keepdims=True)
    s_new = el_col * s_prev + _bmm_tn(k * jnp.exp(bl - bc), v)
    return o, s_new


N_PIECE1 = 7


def _mixer1_body(
    load, small, between,
    cw_ref, gp_ref, wa_ref, ba_ref, ng_ref, s0_ref, cv0_ref, d0_ref,
    y_ref, s_ref, cv_ref, d_ref, *, c, bb,
):
    n = bb * NH
    w4 = NH * DH

    @pl.when(pl.program_id(1) == 0)
    def _():
        s_ref[...] = jnp.broadcast_to(s0_ref[...], s_ref.shape)
        cv_ref[...] = jnp.broadcast_to(cv0_ref[...], cv_ref.shape)
        d_ref[...] = jnp.broadcast_to(d0_ref[...], d_ref.shape)

    beta_all = _sigmoid(small)
    g_all = -jnp.exp(gp_ref[0:1, :]) * _softplus(small + gp_ref[1:2, :])
    g_all_t = jnp.transpose(g_all)
    log_alpha = -_softplus(-(_dot_hi(small[:, 0:GLA_RANK], wa_ref[...]) + ba_ref[...])) * (1.0 / GLA_TAU)
    cw = cw_ref[...]

    acts = []
    qkv = jnp.concatenate([load(0), load(1), load(2)], axis=1)
    between(1)
    for j in range(bb):
        ext = jnp.concatenate([cv_ref[j], qkv[j * c : (j + 1) * c, :]], axis=0)
        conv = cw[3:4] * ext[8 : 8 + c] + cw[2:3] * ext[7 : 7 + c] + cw[1:2] * ext[6 : 6 + c] + cw[0:1] * ext[5 : 5 + c]
        cv_ref[j] = ext[c : c + 8]
        acts.append(_silu(conv))

    def act_chains(off):
        return jnp.stack([acts[j][:, off + h * DH : off + (h + 1) * DH] for j in range(bb) for h in range(NH)])

    qc = act_chains(0)
    kc = act_chains(w4)
    qc = qc * lax.rsqrt(jnp.sum(qc * qc, axis=-1, keepdims=True) + NORM_EPS) * DH**-0.5
    kc = kc * lax.rsqrt(jnp.sum(kc * kc, axis=-1, keepdims=True) + NORM_EPS)
    between(1)
    o, s_new = _gdn_chunk(
        qc, kc, act_chains(2 * w4),
        _chain_cols(beta_all, c, bb, GLA_RANK), _chain_cols(g_all, c, bb, GLA_RANK + NH),
        _chain_rows(g_all_t, c, bb, GLA_RANK + NH), s_ref[...].reshape(n, DH, DH), c, between,
    )
    s_ref[...] = s_new.reshape(s_ref.shape)
    y_c = _group_norm(o, _head_rows(ng_ref, 0, bb), rms=True) * _silu(_chains(load(3), c, bb, DH))
    _store_chains(y_ref, y_c, c, bb, 0)
    between(1)

    tt = lax.broadcasted_iota(jnp.int32, (c, c), 0)
    ss = lax.broadcasted_iota(jnp.int32, (c, c), 1)
    ones_lt = jnp.where(tt >= ss, 1.0, 0.0).astype(_bf16)
    bcs = []
    for j in range(bb):
        la = log_alpha[j * c : (j + 1) * c, :]
        hi = la.astype(_bf16)
        r1 = la - hi.astype(_f32)
        mid = r1.astype(_bf16)
        lo = (r1 - mid.astype(_f32)).astype(_bf16)
        cum = functools.partial(jnp.dot, ones_lt, preferred_element_type=_f32)
        bcs.append(cum(hi) + (cum(mid) + cum(lo)))
    bc = jnp.stack([bcs[j][:, h * DK_D : (h + 1) * DK_D] for j in range(bb) for h in range(NH)])
    qkd = load(4)
    between(1)
    o, d_new = _gla_chunk(
        _chains(qkd, c, bb, DK_D) * DK_D**-0.5, _chains(qkd, c, bb, DK_D, off=NH * DK_D),
        _chains(load(5), c, bb, DH), bc, d_ref[...].reshape(n, DK_D, DH), c,
    )
    d_ref[...] = d_new.reshape(d_ref.shape)
    between(1)
    y_d = _group_norm(o, _head_rows(ng_ref, 1, bb), rms=False) * _silu(_chains(load(6), c, bb, DH))
    _store_chains(y_ref, y_d, c, bb, w4)
    between(N_PIECE1 + 1)


def _mixer1_kernel(*refs, c, bb):
    pieces, small_ref, rest = refs[:N_PIECE1], refs[N_PIECE1], refs[N_PIECE1 + 1 :]
    consts, outs = rest[:8], rest[9:]
    _mixer1_body(lambda p: pieces[p][...].astype(_f32), small_ref[...], _no_op, *consts, *outs, c=c, bb=bb)


def _mixer1_stream_kernel(xn_ref, xc_ref, w_ref, *rest, c, bb, n_cast=0):
    consts, rest = rest[:8], rest[9:]
    cast_in, outs, cast_out, scratch = rest[:n_cast], rest[n_cast : n_cast + 4], rest[n_cast + 4 : 2 * n_cast + 4], rest[2 * n_cast + 4 :]
    for src, dst in zip(cast_in, cast_out):
        dst[...] = src[...].astype(dst.dtype)
    load, small, between = _stream_projection(xn_ref, xc_ref, w_ref, *scratch, N_PIECE1)
    _mixer1_body(load, small, between, *consts, *outs, c=c, bb=bb)


def _mixer1(source, ybuf, conv_w, gdn_par, w_alpha, b_alpha, norm_g, init, **group):
    consts = [_whole(a) for a in (conv_w, gdn_par, w_alpha, b_alpha, norm_g)]
    return _mixer_call("mixer1", (_mixer1_kernel, _mixer1_stream_kernel), source, N_PIECE1, consts, init, ybuf, **group)


_GROUP_META = dict(
    row_off=ROW_META - ROW_SAMPLE, y_row_off=ROW_META, nb=1, nchunk=1, c=N_META, bb=1, bcast_init=True,
    y_rows=R_ROWS - ROW_META,
)
_GROUP_PROMPT = dict(row_off=0, y_row_off=0, nb=BATCH, nchunk=SEQ // CHUNK, c=CHUNK, bb=4, bcast_init=True, stream=True)
_GROUP_SAMPLE = dict(row_off=0, y_row_off=ROW_SAMPLE, nb=DEC_BATCH, nchunk=1, c=DEC_SEQ, bb=16, bcast_init=False)


def _rows_to_batch_major_kernel(x_ref, o_ref):
    o_ref[...] = x_ref[...].reshape(o_ref.shape)


def _prompt_rows_to_batch_major(x):
    nchunk = SEQ // CHUNK
    out = pl.pallas_call(
        _rows_to_batch_major_kernel,
        grid=(nchunk,),
        in_specs=[pl.BlockSpec((BATCH * CHUNK, D_MODEL), lambda ci: (ci, 0))],
        out_specs=pl.BlockSpec((BATCH, None, CHUNK, D_MODEL), lambda ci: (0, ci, 0, 0)),
        out_shape=jax.ShapeDtypeStruct((BATCH, nchunk, CHUNK, D_MODEL), x.dtype),
        compiler_params=pltpu.CompilerParams(dimension_semantics=("parallel",)),
        name="rows_to_batch_major",
    )(x)
    return out.reshape(BATCH, SEQ, D_MODEL)


def _rotary_tables(pos):
    half = DH // 2
    inv = ROPE_BASE ** (-jnp.arange(half, dtype=_f32) / half)
    ang = pos.astype(_f32)[:, None] * inv[None, :]
    cos, sin = jnp.cos(ang), jnp.sin(ang)
    return jnp.concatenate([cos, cos], -1), jnp.concatenate([-sin, sin], -1)


def _lanes(m):
    return jnp.broadcast_to(m.astype(_f32)[..., None], m.shape + (LANES,))


def _conv_rows(s):
    return jnp.pad(s.astype(_f32), ((0, 0), (8 - (CONV_W - 1), 0), (0, 0)))


def kernel(x_prompt, x_sample, state_ret, state_mlstm_c, state_mlstm_n, state_mlstm_m, state_gdn, state_gdn_conv, state_gla, meta_tokens, w_in0, ret_norm_g, mlstm_gate_bias, mlstm_norm_g, w_out0, ln0_mix_g, ln0_mix_b, ffn0_w_gate, ffn0_w_up, ffn0_w_down, ln0_ffn_g, ln0_ffn_b, w_in1, gdn_conv_w, gdn_a_log, gdn_dt_bias, gdn_norm_g, gla_w_alpha, gla_b_alpha, gla_norm_g, w_out1, ln1_mix_g, ln1_mix_b, moe_w_router, moe_b_router, moe_w_gate, moe_w_up, moe_w_down, ln1_ffn_g, ln1_ffn_b):
    w4 = NH * DH
    nchunk = SEQ // CHUNK
    xp = x_prompt.reshape(BATCH, nchunk, CHUNK, D_MODEL).transpose(1, 0, 2, 3).reshape(N_PROMPT, D_MODEL)
    x = jnp.concatenate(
        [
            xp,
            x_sample.reshape(N_SAMPLE, D_MODEL),
            meta_tokens.astype(x_prompt.dtype),
            jnp.zeros((R_ROWS - ROW_META - N_META, D_MODEL), x_prompt.dtype),
        ],
        0,
    )

    w_in0_p = jnp.pad(w_in0, ((0, 0), (0, P0_PAD - w_in0.shape[1]))).astype(_bf16)
    c_z_end = QKV_C + w4
    c_qd = c_z_end + 2 * NH
    c_lr = c_qd + 2 * NH * DK_D + 2 * w4
    gap = jnp.zeros((D_MODEL, P1_PAD - LANES - P1_MAIN), w_in1.dtype)
    w_in1_p = jnp.concatenate([w_in1[:, :c_z_end], w_in1[:, c_qd:c_lr], gap, w_in1[:, c_lr:], w_in1[:, c_z_end:c_qd]], 1)
    w_in1_p = jnp.pad(w_in1_p, ((0, 0), (0, P1_PAD - w_in1_p.shape[1]))).astype(_bf16)
    gb_row = jnp.pad(mlstm_gate_bias.astype(_f32), (0, LANES - 2 * NH))[None]
    norm0 = jnp.stack([ret_norm_g, mlstm_norm_g]).astype(_f32)
    norm1 = jnp.stack([gdn_norm_g, gla_norm_g]).astype(_f32)
    lo = GLA_RANK + NH
    gdn_par = jnp.stack(
        [
            jnp.pad(gdn_a_log.astype(_f32), (lo, LANES - lo - NH)),
            jnp.pad(gdn_dt_bias.astype(_f32), (lo, LANES - lo - NH)),
        ]
    )
    w_router_t = jnp.pad(moe_w_router.astype(_f32).T, ((0, ROUTE_ROWS - N_EXPERTS), (0, 0)))
    b_router = jnp.pad(moe_b_router.astype(_f32), (0, ROUTE_ROWS - N_EXPERTS), constant_values=-jnp.inf)
    b_router_col = jnp.broadcast_to(b_router[:, None], (ROUTE_ROWS, LANES))
    moe_w_flat = [w.reshape(-1, w.shape[-1]) for w in (moe_w_gate, moe_w_up, moe_w_down)]

    def row(v):
        return v.astype(_f32)[None]

    proj0 = _proj(x[ROW_SAMPLE:], w_in0_p, tn=P0_PAD // 3)
    zeros_even = (
        jnp.zeros((1, NH, DH, DH), _f32), jnp.zeros((1, NH, DH, DH), _f32),
        jnp.zeros((1, NH, DH), _f32), jnp.zeros((1, NH, LANES), _f32),
    )
    cos_m, sin_m = _rotary_tables(jnp.arange(N_META))
    cos_p, sin_p = _rotary_tables(N_META + jnp.arange(SEQ))
    cos_s, sin_s = _rotary_tables(PAST_LEN + jnp.arange(DEC_SEQ))
    ybuf = jnp.zeros((R_ROWS, 2 * w4), _bf16)
    ybuf, *meta_even = _mixer0(proj0, ybuf, cos_m, sin_m, gb_row, norm0, zeros_even, **_GROUP_META)
    ybuf, p_ret, p_mc, p_mn, p_mm = _mixer0((x, w_in0_p), ybuf, cos_p, sin_p, gb_row, norm0, meta_even, **_GROUP_PROMPT)
    init_s = (state_ret.astype(_f32), state_mlstm_c.astype(_f32), state_mlstm_n.astype(_f32), _lanes(state_mlstm_m))
    ybuf, s_ret, s_mc, s_mn, s_mm = _mixer0(proj0, ybuf, cos_s, sin_s, gb_row, norm0, init_s, **_GROUP_SAMPLE)
    x = _out_ln(ybuf, w_out0.astype(_bf16), x, row(ln0_mix_g), row(ln0_mix_b))
    x = _ffn_ln(x, ffn0_w_gate.astype(_bf16), ffn0_w_up.astype(_bf16), ffn0_w_down.astype(_bf16), row(ln0_ffn_g), row(ln0_ffn_b))

    proj1 = _proj(x[ROW_SAMPLE:], w_in1_p, tn=P1_PAD // 3)
    zeros_odd = (jnp.zeros((1, NH, DH, DH), _f32), jnp.zeros((1, 8, QKV_C), _f32), jnp.zeros((1, NH, DK_D, DH), _f32))
    m1_par = (gdn_conv_w.astype(_f32), gdn_par, gla_w_alpha.astype(_f32), row(gla_b_alpha), norm1)
    ybuf, *meta_odd = _mixer1(proj1, ybuf, *m1_par, zeros_odd, **_GROUP_META)
    ybuf, p_gdn, p_conv, p_gla, *moe_w = _mixer1((x, w_in1_p), ybuf, *m1_par, meta_odd, casts=moe_w_flat, **_GROUP_PROMPT)
    moe_w = [w.reshape(w32.shape) for w, w32 in zip(moe_w, (moe_w_gate, moe_w_up, moe_w_down))]
    init_s = (state_gdn.astype(_f32), _conv_rows(state_gdn_conv), state_gla.astype(_f32))
    ybuf, s_gdn, s_conv, s_gla = _mixer1(proj1, ybuf, *m1_par, init_s, **_GROUP_SAMPLE)
    x, gate_t, code_t, counts = _out_ln_route(ybuf, w_out1.astype(_bf16), x, row(ln1_mix_g), row(ln1_mix_b), w_router_t, b_router_col)
    x = _moe_ln(x, gate_t, code_t, counts, *moe_w, row(ln1_ffn_g), row(ln1_ffn_b))

    y_prompt = _prompt_rows_to_batch_major(x)
    y_sample = x[ROW_SAMPLE:ROW_META].reshape(DEC_BATCH, DEC_SEQ, D_MODEL)
    tail = 8 - (CONV_W - 1)
    return (
        y_prompt, y_sample,
        p_ret, p_mc, p_mn, p_mm[..., 0], p_gdn, p_conv[:, tail:], p_gla,
        s_ret, s_mc, s_mn, s_mm[..., 0], s_gdn, s_conv[:, tail:], s_gla,
    )
```

```python
import functools
import math

import jax
import jax.numpy as jnp
import numpy as np
from jax import lax
from jax.experimental import pallas as pl
from jax.experimental.pallas import tpu as pltpu

D_MODEL = 1024
BATCH = 8
SEQ = 2048
DEC_BATCH = 128
DEC_SEQ = 4
PAST_LEN = 16384
N_META = 16
CHUNK = 64
NH = 4
DH = 128
DK_D = 64
CONV_W = 4
GLA_RANK = 16
GLA_TAU = 16.0
D_FF = 2816
N_EXPERTS = 8
MOE_FF = 1408
ROPE_BASE = 10000.0
LN_EPS = 1e-5
NORM_EPS = 1e-6
DEPTH = 2
ALPHA = (2 * DEPTH) ** 0.25
QKV_C = 3 * NH * DH

LANES = 128
N_PROMPT = BATCH * SEQ
N_SAMPLE = DEC_BATCH * DEC_SEQ
ROW_SAMPLE = N_PROMPT
ROW_META = N_PROMPT + N_SAMPLE
TM_MOE = 448
MOE_BLOCK = 128
MOE_GROUP = 2
TM = TM_MOE * MOE_GROUP
R_ROWS = ((ROW_META + N_META + TM - 1) // TM) * TM

P0_PAD = 33 * LANES
P1_MAIN = 28 * LANES
P1_PAD = 30 * LANES
SOLVE_BLOCK = 16

VMEM_LIMIT = 56 * 1024 * 1024

_bf16 = jnp.bfloat16
_f32 = jnp.float32
_HI = lax.Precision.HIGHEST


def _dot_hi(a, b):
    return jnp.dot(a, b, preferred_element_type=_f32, precision=_HI)


def _bmm(a, b):
    return jnp.einsum("nmk,nkp->nmp", a.astype(_bf16), b.astype(_bf16), preferred_element_type=_f32)


def _bmm_nt(a, b):
    return jnp.einsum("nmk,npk->nmp", a.astype(_bf16), b.astype(_bf16), preferred_element_type=_f32)


def _bmm_tn(a, b):
    return jnp.einsum("nkm,nkp->nmp", a.astype(_bf16), b.astype(_bf16), preferred_element_type=_f32)


def _split2(a):
    hi = a.astype(_bf16)
    lo = (a - hi.astype(_f32)).astype(_bf16)
    return hi, lo


def _bmm_x3(a, b):
    ah, al = _split2(a)
    bh, bl = _split2(b)
    mm = functools.partial(jnp.einsum, "nmk,nkp->nmp", preferred_element_type=_f32)
    return mm(ah, bh) + (mm(ah, bl) + mm(al, bh))


def _sigmoid(x):
    return 1.0 / (1.0 + jnp.exp(-x))


def _silu(x):
    return x * _sigmoid(x)


def _softplus(x):
    return jnp.maximum(x, 0.0) + jnp.log1p(jnp.exp(-jnp.abs(x)))


def _group_norm(o, g_row, rms):
    if not rms:
        o = o - jnp.mean(o, axis=-1, keepdims=True)
    return o * lax.rsqrt(jnp.mean(o * o, axis=-1, keepdims=True) + NORM_EPS) * g_row


def _layer_norm(x, g_row, b_row):
    mu = jnp.mean(x, axis=-1, keepdims=True)
    xc = x - mu
    var = jnp.mean(xc * xc, axis=-1, keepdims=True)
    return xc * lax.rsqrt(var + LN_EPS) * g_row + b_row


def _tri3(c):
    t = lax.broadcasted_iota(jnp.int32, (1, c, c), 1)
    s = lax.broadcasted_iota(jnp.int32, (1, c, c), 2)
    return t, s


def _cumsum_col_row(x_col, x_row, c):
    t, s = _tri3(c)
    col = jnp.sum(jnp.where(t >= s, x_row, 0.0), axis=2, keepdims=True)
    row = jnp.sum(jnp.where(t <= s, x_col, 0.0), axis=1, keepdims=True)
    return col, row


def _proj_kernel(x_ref, w_ref, o_ref, tail_ref, xb_ref):
    j = pl.program_id(1)

    @pl.when(j == 0)
    def _():
        xb_ref[...] = x_ref[...].astype(_bf16)

    acc = jnp.dot(xb_ref[...], w_ref[...], preferred_element_type=_f32)
    o_ref[...] = acc.astype(o_ref.dtype)

    @pl.when(j == pl.num_programs(1) - 1)
    def _():
        tail_ref[...] = acc[:, acc.shape[1] - LANES :]


def _proj(x, w, tn):
    rows, k = x.shape
    n = w.shape[1]
    tm = min(TM, rows)
    return pl.pallas_call(
        _proj_kernel,
        grid=(rows // tm, n // tn),
        in_specs=[pl.BlockSpec((tm, k), lambda i, j: (i, 0)), pl.BlockSpec((k, tn), lambda i, j: (0, j))],
        out_specs=[pl.BlockSpec((tm, tn), lambda i, j: (i, j)), pl.BlockSpec((tm, LANES), lambda i, j: (i, 0))],
        out_shape=[jax.ShapeDtypeStruct((rows, n), _bf16), jax.ShapeDtypeStruct((rows, LANES), _f32)],
        scratch_shapes=[pltpu.VMEM((tm, k), _bf16)],
        compiler_params=pltpu.CompilerParams(dimension_semantics=("parallel", "arbitrary"), vmem_limit_bytes=VMEM_LIMIT),
        name="in_proj",
    )(x, w)


def _out_ln_kernel(y_ref, w_ref, x_ref, g_ref, b_ref, o_ref):
    h = jnp.dot(y_ref[...], w_ref[...], preferred_element_type=_f32)
    o_ref[...] = _layer_norm(ALPHA * x_ref[...] + h, g_ref[...], b_ref[...])


def _out_ln(y, w, x, g, b):
    rows, k = y.shape
    d = w.shape[1]
    return pl.pallas_call(
        _out_ln_kernel,
        grid=(rows // TM,),
        in_specs=[
            pl.BlockSpec((TM, k), lambda i: (i, 0)),
            pl.BlockSpec((k, d), lambda i: (0, 0)),
            pl.BlockSpec((TM, d), lambda i: (i, 0)),
            pl.BlockSpec((1, d), lambda i: (0, 0)),
            pl.BlockSpec((1, d), lambda i: (0, 0)),
        ],
        out_specs=pl.BlockSpec((TM, d), lambda i: (i, 0)),
        out_shape=jax.ShapeDtypeStruct((rows, d), _f32),
        compiler_params=pltpu.CompilerParams(dimension_semantics=("parallel",), vmem_limit_bytes=VMEM_LIMIT),
        name="out_proj_ln",
    )(y, w, x, g, b)


FF_SUB = (0, 512, 1024, 1408)


def _swiglu_partial(xb, wg_ref, wu_ref, wd_ref):
    out = None
    for lo, hi in zip(FF_SUB[:-1], FF_SUB[1:]):
        hg = jnp.dot(xb, wg_ref[:, lo:hi], preferred_element_type=_f32)
        hu = jnp.dot(xb, wu_ref[:, lo:hi], preferred_element_type=_f32)
        a = (_silu(hg) * hu).astype(_bf16)
        part = jnp.dot(a, wd_ref[lo:hi, :], preferred_element_type=_f32)
        out = part if out is None else out + part
    return out


def _ffn_ln_kernel(x_ref, wg_ref, wu_ref, wd_ref, g_ref, b_ref, o_ref, xb_ref, acc_ref):
    f = pl.program_id(1)

    @pl.when(f == 0)
    def _():
        xb_ref[...] = x_ref[...].astype(_bf16)
        acc_ref[...] = jnp.zeros_like(acc_ref)

    acc_ref[...] += _swiglu_partial(xb_ref[...], wg_ref, wu_ref, wd_ref)

    @pl.when(f == pl.num_programs(1) - 1)
    def _():
        o_ref[...] = _layer_norm(ALPHA * x_ref[...] + acc_ref[...], g_ref[...], b_ref[...])


def _ffn_ln(x, wg, wu, wd, g, b):
    rows, d = x.shape
    ff = wg.shape[1]
    tf = MOE_FF
    return pl.pallas_call(
        _ffn_ln_kernel,
        grid=(rows // TM, ff // tf),
        in_specs=[
            pl.BlockSpec((TM, d), lambda i, f: (i, 0)),
            pl.BlockSpec((d, tf), lambda i, f: (0, f)),
            pl.BlockSpec((d, tf), lambda i, f: (0, f)),
            pl.BlockSpec((tf, d), lambda i, f: (f, 0)),
            pl.BlockSpec((1, d), lambda i, f: (0, 0)),
            pl.BlockSpec((1, d), lambda i, f: (0, 0)),
        ],
        out_specs=pl.BlockSpec((TM, d), lambda i, f: (i, 0)),
        out_shape=jax.ShapeDtypeStruct((rows, d), _f32),
        scratch_shapes=[pltpu.VMEM((TM, d), _bf16), pltpu.VMEM((TM, d), _f32)],
        compiler_params=pltpu.CompilerParams(dimension_semantics=("parallel", "arbitrary"), vmem_limit_bytes=VMEM_LIMIT),
        name="ffn_ln",
    )(x, wg, wu, wd, g, b)


ROUTE_ROWS = 16


def _route_tile(x, wr_t, bias_col, before):
    xh, xl = _split2(x)
    wh, wl = _split2(wr_t)
    nt = functools.partial(lax.dot_general, dimension_numbers=(((1,), (1,)), ((), ())), preferred_element_type=_f32)
    logits = nt(wh, xh) + (nt(wh, xl) + nt(wl, xh)) + bias_col
    row = lax.broadcasted_iota(jnp.int32, logits.shape, 0)
    ex = jnp.exp(logits - jnp.max(logits, axis=0, keepdims=True))
    probs = ex / jnp.sum(ex, axis=0, keepdims=True)
    p1 = jnp.max(probs, axis=0, keepdims=True)
    i1 = jnp.min(jnp.where(probs == p1, row, ROUTE_ROWS), axis=0, keepdims=True)
    rest = jnp.where(row == i1, -1.0, probs)
    p2 = jnp.max(rest, axis=0, keepdims=True)
    i2 = jnp.min(jnp.where(rest == p2, row, ROUTE_ROWS), axis=0, keepdims=True)
    tot = p1 + p2
    gate_t = jnp.where(row == i1, p1 / tot, 0.0) + jnp.where(row == i2, p2 / tot, 0.0)
    sel_t = jnp.where(row == i1, 1.0, jnp.where(row == i2, 1.0, 0.0))
    rank_t = jnp.dot(sel_t.astype(_bf16), before, preferred_element_type=_f32)
    code_t = jnp.where(sel_t > 0.0, rank_t, -1.0)
    return gate_t[:N_EXPERTS], code_t[:N_EXPERTS], jnp.sum(sel_t, axis=1, keepdims=True)


def _out_ln_route_kernel(y_ref, w_ref, x_ref, g_ref, b_ref, wr_ref, br_ref, before_ref, o_ref, gate_t_ref, code_t_ref, cnt_ref):
    h = jnp.dot(y_ref[...], w_ref[...], preferred_element_type=_f32)
    x_new = _layer_norm(ALPHA * x_ref[...] + h, g_ref[...], b_ref[...])
    o_ref[...] = x_new
    for i in range(MOE_GROUP):
        gate_t, code_t, cnt = _route_tile(x_new[i * TM_MOE : (i + 1) * TM_MOE], wr_ref[...], br_ref[:, 0:1], before_ref[...])
        gate_t_ref[i] = gate_t
        code_t_ref[i] = code_t
        cnt_ref[i] = jnp.broadcast_to(cnt, (ROUTE_ROWS, LANES)).astype(jnp.int32)


def _out_ln_route(y, w, x, g, b, wr_t, br_col):
    rows, k = y.shape
    d = w.shape[1]
    nt = rows // TM_MOE
    t = np.arange(TM_MOE)
    before = jnp.asarray(t[:, None] < t[None, :], _bf16)

    def tile_spec(shape):
        return pl.BlockSpec((MOE_GROUP,) + shape, lambda i: (i, 0, 0))

    x_new, gate_t, code_t, cnt = pl.pallas_call(
        _out_ln_route_kernel,
        grid=(rows // TM,),
        in_specs=[
            pl.BlockSpec((TM, k), lambda i: (i, 0)),
            pl.BlockSpec((k, d), lambda i: (0, 0)),
            pl.BlockSpec((TM, d), lambda i: (i, 0)),
            pl.BlockSpec((1, d), lambda i: (0, 0)),
            pl.BlockSpec((1, d), lambda i: (0, 0)),
            pl.BlockSpec((ROUTE_ROWS, d), lambda i: (0, 0)),
            pl.BlockSpec((ROUTE_ROWS, LANES), lambda i: (0, 0)),
            pl.BlockSpec((TM_MOE, TM_MOE), lambda i: (0, 0)),
        ],
        out_specs=[
            pl.BlockSpec((TM, d), lambda i: (i, 0)),
            tile_spec((N_EXPERTS, TM_MOE)),
            tile_spec((N_EXPERTS, TM_MOE)),
            tile_spec((ROUTE_ROWS, LANES)),
        ],
        out_shape=[
            jax.ShapeDtypeStruct((rows, d), _f32),
            jax.ShapeDtypeStruct((nt, N_EXPERTS, TM_MOE), _f32),
            jax.ShapeDtypeStruct((nt, N_EXPERTS, TM_MOE), _f32),
            jax.ShapeDtypeStruct((nt, ROUTE_ROWS, LANES), jnp.int32),
        ],
        compiler_params=pltpu.CompilerParams(dimension_semantics=("parallel",), vmem_limit_bytes=VMEM_LIMIT),
        name="out_proj_ln_route",
    )(y, w, x, g, b, wr_t, br_col, before)
    return x_new, gate_t, code_t, cnt[:, :N_EXPERTS, 0].reshape(-1)


def _moe_ln_kernel(cnt_ref, x_ref, gate_t_ref, code_t_ref, wg_ref, wu_ref, wd_ref, g_ref, b_ref, o_ref, xb_ref, acc_ref, first_ref):
    grp = pl.program_id(0)
    e = pl.program_id(1)

    @pl.when(e == 0)
    def _():
        xb_ref[...] = x_ref[...].astype(_bf16)
        acc_ref[...] = jnp.zeros_like(acc_ref)

    slot_col = lax.broadcasted_iota(jnp.int32, (MOE_BLOCK, 1), 0)
    counts = [cnt_ref[(grp * MOE_GROUP + i) * N_EXPERTS + e] for i in range(MOE_GROUP)]
    code_rows = [code_t_ref[i, pl.ds(e, 1), :] for i in range(MOE_GROUP)]
    gate_rows = [gate_t_ref[i, pl.ds(e, 1), :] for i in range(MOE_GROUP)]
    first_rows = pl.ds(pl.multiple_of(e * MOE_BLOCK, MOE_BLOCK), MOE_BLOCK)
    scatter_dims = (((0,), (0,)), ((), ()))

    def gather_rows(i, blk):
        hit = code_rows[i] == (slot_col + blk * MOE_BLOCK).astype(_f32)
        onehot = jnp.where(hit, 1.0, 0.0).astype(_bf16)
        xs = jnp.dot(onehot, xb_ref[i * TM_MOE : (i + 1) * TM_MOE, :], preferred_element_type=_f32).astype(_bf16)
        gate = jnp.sum(jnp.where(hit, gate_rows[i], 0.0), axis=1, keepdims=True)
        return onehot, xs, gate

    def first_pass(i):
        _, xs, gate = gather_rows(i, 0)
        first_ref[i, first_rows, :] = (_swiglu_partial(xs, wg_ref, wu_ref, wd_ref) * gate).astype(_bf16)

    def no_pass(i):
        first_ref[i, first_rows, :] = jnp.zeros((MOE_BLOCK, first_ref.shape[2]), _bf16)

    assert MOE_GROUP == 2
    has0 = counts[0] > 0
    has1 = counts[1] > 0

    @pl.when(jnp.logical_and(has0, has1))
    def _():
        _, xs0, gate0 = gather_rows(0, 0)
        _, xs1, gate1 = gather_rows(1, 0)
        out = _swiglu_partial(jnp.concatenate([xs0, xs1], axis=0), wg_ref, wu_ref, wd_ref)
        first_ref[0, first_rows, :] = (out[:MOE_BLOCK] * gate0).astype(_bf16)
        first_ref[1, first_rows, :] = (out[MOE_BLOCK:] * gate1).astype(_bf16)

    @pl.when(jnp.logical_and(has0, jnp.logical_not(has1)))
    def _():
        first_pass(0)
        no_pass(1)

    @pl.when(jnp.logical_and(has1, jnp.logical_not(has0)))
    def _():
        no_pass(0)
        first_pass(1)

    @pl.when(jnp.logical_and(jnp.logical_not(has0), jnp.logical_not(has1)))
    def _():
        no_pass(0)
        no_pass(1)

    for blk in range(1, -(-TM_MOE // MOE_BLOCK)):
        for i in range(MOE_GROUP):

            @pl.when(counts[i] > blk * MOE_BLOCK)
            def _():
                onehot, xs, gate = gather_rows(i, blk)
                out = (_swiglu_partial(xs, wg_ref, wu_ref, wd_ref) * gate).astype(_bf16)
                acc_ref[i * TM_MOE : (i + 1) * TM_MOE, :] += lax.dot_general(onehot, out, scatter_dims, preferred_element_type=_f32)

    @pl.when(e == pl.num_programs(1) - 1)
    def _():
        slot = lax.broadcasted_iota(jnp.int32, (1, MOE_BLOCK, 1), 1).astype(_f32)
        for i in range(MOE_GROUP):
            hit = code_t_ref[i][:, None, :] == slot
            onehot = jnp.where(hit, 1.0, 0.0).astype(_bf16).reshape(N_EXPERTS * MOE_BLOCK, TM_MOE)
            rows = slice(i * TM_MOE, (i + 1) * TM_MOE)
            moe = acc_ref[rows, :] + lax.dot_general(onehot, first_ref[i], scatter_dims, preferred_element_type=_f32)
            o_ref[rows, :] = _layer_norm(ALPHA * x_ref[rows, :] + moe, g_ref[...], b_ref[...])


def _moe_ln(x, gate_t, code_t, counts, wg, wu, wd, g, b):
    rows, d = x.shape
    ne, _, ff = wg.shape
    grid_spec = pltpu.PrefetchScalarGridSpec(
        num_scalar_prefetch=1,
        grid=(rows // TM, ne),
        in_specs=[
            pl.BlockSpec((TM, d), lambda i, e, cnt: (i, 0)),
            pl.BlockSpec((MOE_GROUP, N_EXPERTS, TM_MOE), lambda i, e, cnt: (i, 0, 0)),
            pl.BlockSpec((MOE_GROUP, N_EXPERTS, TM_MOE), lambda i, e, cnt: (i, 0, 0)),
            pl.BlockSpec((None, d, ff), lambda i, e, cnt: (e, 0, 0)),
            pl.BlockSpec((None, d, ff), lambda i, e, cnt: (e, 0, 0)),
            pl.BlockSpec((None, ff, d), lambda i, e, cnt: (e, 0, 0)),
            pl.BlockSpec((1, d), lambda i, e, cnt: (0, 0)),
            pl.BlockSpec((1, d), lambda i, e, cnt: (0, 0)),
        ],
        out_specs=pl.BlockSpec((TM, d), lambda i, e, cnt: (i, 0)),
        scratch_shapes=[
            pltpu.VMEM((TM, d), _bf16),
            pltpu.VMEM((TM, d), _f32),
            pltpu.VMEM((MOE_GROUP, ne * MOE_BLOCK, d), _bf16),
        ],
    )
    return pl.pallas_call(
        _moe_ln_kernel,
        grid_spec=grid_spec,
        out_shape=jax.ShapeDtypeStruct((rows, d), _f32),
        compiler_params=pltpu.CompilerParams(dimension_semantics=("parallel", "arbitrary"), vmem_limit_bytes=VMEM_LIMIT),
        name="moe_ln",
    )(counts, x, gate_t, code_t, wg, wu, wd, g, b)


def _chains(a, c, bb, width, off=0):
    return jnp.stack([a[j * c : (j + 1) * c, off + h * width : off + (h + 1) * width] for j in range(bb) for h in range(NH)])


def _chain_cols(a, c, bb, lane0):
    return jnp.stack([a[j * c : (j + 1) * c, lane0 + h : lane0 + h + 1] for j in range(bb) for h in range(NH)])


def _chain_rows(a_t, c, bb, lane0):
    return jnp.stack([a_t[lane0 + h : lane0 + h + 1, j * c : (j + 1) * c] for j in range(bb) for h in range(NH)])


def _head_rows(ref, row, bb):
    return jnp.stack([ref[row : row + 1, h * DH : (h + 1) * DH] for _ in range(bb) for h in range(NH)])


def _store_chains(y_ref, y, c, bb, off):
    for j in range(bb):
        for h in range(NH):
            y_ref[j * c : (j + 1) * c, off + h * DH : off + (h + 1) * DH] = y[j * NH + h].astype(y_ref.dtype)
    if y_ref.shape[0] > bb * c:
        y_ref[bb * c :, off : off + NH * DH] = jnp.zeros((y_ref.shape[0] - bb * c, NH * DH), y_ref.dtype)


def _drop_alias_ref(kernel, n_in):
    def body(*refs):
        return kernel(*refs[:n_in], None, *refs[n_in:])

    return body


def _y_buffer_specs(ybuf, rows, y_rows, y_row_off, nbb, width):
    if y_rows == rows:
        spec = pl.BlockSpec((rows, width), lambda i, ci: (y_row_off // rows + ci * nbb + i, 0))
    else:
        spec = pl.BlockSpec((y_rows, width), lambda i, ci: (y_row_off // y_rows, 0))
    return spec, jax.ShapeDtypeStruct((R_ROWS, width), _bf16), ([] if ybuf is None else [pl.BlockSpec(memory_space=pl.ANY)])


PIECE = NH * DH


def _stream_projection(xn_ref, xc_ref, w_ref, main_sc, tail_sc, xb_sc, n_piece):
    rows = xn_ref.shape[0]
    ci = pl.program_id(1)
    tail_col = w_ref.shape[1] - LANES

    def project(xb, row0, p):
        if p < n_piece:
            cols = slice(p * PIECE, (p + 1) * PIECE)
            main_sc[pl.ds(row0, rows), cols] = jnp.dot(xb, w_ref[:, cols], preferred_element_type=_f32).astype(_bf16)
        else:
            tail_sc[pl.ds(row0, rows), :] = jnp.dot(xb, w_ref[:, tail_col:], preferred_element_type=_f32)

    @pl.when(ci == 0)
    def _():
        xb0 = xc_ref[...].astype(_bf16)
        for p in range(n_piece + 1):
            project(xb0, 0, p)

    cur = pl.multiple_of((ci % 2) * rows, rows)
    nxt = pl.multiple_of(((ci + 1) % 2) * rows, rows)
    xb_sc[...] = xn_ref[...].astype(_bf16)
    pending = iter(range(n_piece + 1))

    def between(count=1):
        for _ in range(count):
            p = next(pending, None)
            if p is not None:
                project(xb_sc[...], nxt, p)

    def load(p):
        return main_sc[pl.ds(cur, rows), p * PIECE : (p + 1) * PIECE].astype(_f32)

    return load, tail_sc[pl.ds(cur, rows), :], between


def _retention_chunk(q, k, v, s_prev, dec, vec):
    att = _bmm_nt(q, k) * dec
    o = _bmm(att, v) + vec[:, :, 0:1] * _bmm(q, s_prev)
    s_new = vec[:, 0:1, 2:3] * s_prev + _bmm_tn(k * vec[:, :, 1:2], v)
    return o, s_new


def _mlstm_chunk(q, k, v, it_col, it_row, lf_col, lf_row, c_prev, n_prev, m_prev, c):
    t, s = _tri3(c)
    b_col, b_row = _cumsum_col_row(lf_col, lf_row, c)
    logw = jnp.where(t >= s, b_col - b_row + it_row, -jnp.inf)
    m_t = jnp.maximum(b_col + m_prev, jnp.max(logw, axis=2, keepdims=True))
    w = jnp.exp(logw - m_t)
    carry = jnp.exp(b_col + m_prev - m_t)
    qk = _bmm_nt(q, k) * w
    num = _bmm(qk, v) + carry * _bmm(q, c_prev)
    den = jnp.sum(qk, axis=2, keepdims=True) + carry * jnp.sum(q * n_prev, axis=2, keepdims=True)
    h = num / jnp.maximum(jnp.abs(den), jnp.exp(-m_t))
    m_new = m_t[:, c - 1 : c, :]
    b_last = b_col[:, c - 1 : c, :]
    w_last = jnp.exp(b_last - b_col + it_col - m_new)
    decay = jnp.exp(b_last + m_prev - m_new)
    kw = k * w_last
    c_new = decay * c_prev + _bmm_tn(kw, v)
    n_new = decay * n_prev + jnp.sum(kw, axis=1, keepdims=True)
    return h, c_new, n_new, m_new


N_PIECE0 = 8


def _mixer0_body(
    load, gate_tail, between,
    cos_ref, sin_ref, dec_ref, vec_ref, gb_ref, ng_ref, s0_ref, c0_ref, n0_ref, m0_ref,
    y_ref, s_ref, c_ref, n_ref, m_ref, *, c, bb,
):
    n = bb * NH

    @pl.when(pl.program_id(1) == 0)
    def _():
        s_ref[...] = jnp.broadcast_to(s0_ref[...], s_ref.shape)
        c_ref[...] = jnp.broadcast_to(c0_ref[...], c_ref.shape)
        n_ref[...] = jnp.broadcast_to(n0_ref[...], n_ref.shape)
        m_ref[...] = jnp.broadcast_to(m0_ref[...], m_ref.shape)

    def chains(p):
        return _chains(load(p), c, bb, DH)

    cosf = cos_ref[...][None]
    sinf = sin_ref[...][None]
    gates = gate_tail + gb_ref[...]
    lane = lax.broadcasted_iota(jnp.int32, gates.shape, 1)
    gates = jnp.where(lane < NH, gates, -_softplus(-gates))
    gates_t = jnp.transpose(gates)

    q = chains(0)
    k = chains(1)
    q = q * cosf + pltpu.roll(q, DH // 2, axis=2) * sinf
    k = (k * cosf + pltpu.roll(k, DH // 2, axis=2) * sinf) * DH**-0.5
    between(2)
    dec = jnp.concatenate([dec_ref[...]] * bb, axis=0)
    vec = jnp.concatenate([vec_ref[...]] * bb, axis=0)
    o, s_new = _retention_chunk(q, k, chains(2), s_ref[...].reshape(n, DH, DH), dec, vec)
    between(2)
    s_ref[...] = s_new.reshape(s_ref.shape)
    y_a = _group_norm(o, _head_rows(ng_ref, 0, bb), rms=False) * _silu(chains(3))
    _store_chains(y_ref, y_a, c, bb, 0)
    between(1)

    n_prev = jnp.stack([n_ref[j, h : h + 1, :] for j in range(bb) for h in range(NH)])
    m_prev = jnp.stack([m_ref[j, h : h + 1, 0:1] for j in range(bb) for h in range(NH)])
    h_b, c_new, n_new, m_new = _mlstm_chunk(
        chains(4), chains(5) * DH**-0.5, chains(6),
        _chain_cols(gates, c, bb, 0), _chain_rows(gates_t, c, bb, 0),
        _chain_cols(gates, c, bb, NH), _chain_rows(gates_t, c, bb, NH),
        c_ref[...].reshape(n, DH, DH), n_prev, m_prev, c,
    )
    between(2)
    c_ref[...] = c_new.reshape(c_ref.shape)
    for j in range(bb):
        for h in range(NH):
            n_ref[j, h : h + 1, :] = n_new[j * NH + h]
            m_ref[j, h : h + 1, :] = jnp.broadcast_to(m_new[j * NH + h], (1, LANES))
    h_b = _sigmoid(chains(7)) * h_b
    between(1)
    y_b = _group_norm(h_b, _head_rows(ng_ref, 1, bb), rms=False)
    _store_chains(y_ref, y_b, c, bb, NH * DH)
    between(N_PIECE0 + 1)


def _no_op(count=1):
    del count


def _mixer0_kernel(*refs, c, bb):
    pieces, gate_ref, rest = refs[:N_PIECE0], refs[N_PIECE0], refs[N_PIECE0 + 1 :]
    consts, outs = rest[:10], rest[11:]
    _mixer0_body(lambda p: pieces[p][...].astype(_f32), gate_ref[...], _no_op, *consts, *outs, c=c, bb=bb)


def _cast_blocks(cast_in, cast_out):
    for src, dst in zip(cast_in, cast_out):
        dst[...] = src[...].astype(dst.dtype)


def _mixer0_stream_kernel(xn_ref, xc_ref, w_ref, *rest, c, bb, n_cast=0):
    consts, rest = rest[:10], rest[11:]
    cast_in, outs, cast_out, scratch = rest[:n_cast], rest[n_cast : n_cast + 5], rest[n_cast + 5 : 2 * n_cast + 5], rest[2 * n_cast + 5 :]
    _cast_blocks(cast_in, cast_out)
    load, gate_tail, between = _stream_projection(xn_ref, xc_ref, w_ref, *scratch, N_PIECE0)
    _mixer0_body(load, gate_tail, between, *consts, *outs, c=c, bb=bb)


def _retention_tables(c):
    lg = np.log(1.0 - 2.0 ** (-5.0 - np.arange(NH, dtype=np.float64)))[:, None, None]
    t = np.arange(c, dtype=np.float64)
    diff = t[None, :, None] - t[None, None, :]
    dec = np.where(diff >= 0, np.exp(np.maximum(diff, 0.0) * lg), 0.0)
    vec = np.zeros((NH, c, LANES))
    vec[:, :, 0] = np.exp((t[None, :] + 1.0) * lg[:, 0])
    vec[:, :, 1] = np.exp((c - 1.0 - t[None, :]) * lg[:, 0])
    vec[:, :, 2] = np.exp(c * lg[:, 0])
    return jnp.asarray(dec, _f32), jnp.asarray(vec, _f32)


def _whole(a):
    return a, pl.BlockSpec(a.shape, lambda i, ci: (0,) * a.ndim)


def _mixer_call(name, kernels, source, n_piece, consts, init, ybuf, *, row_off, y_row_off, nb, nchunk, c, bb, bcast_init, y_rows=None, stream=False, casts=()):
    rows = bb * c
    nbb = nb // bb
    blk0 = row_off // rows
    ib = 1 if bcast_init else bb
    y_spec, y_shape, alias_spec = _y_buffer_specs(ybuf, rows, y_rows or rows, y_row_off, nbb, 2 * NH * DH)

    def row_block(col):
        return lambda i, ci: (blk0 + ci * nbb + i, col)

    def batch_block(nd, lead):
        return lambda i, ci: ((i if lead else 0),) + (0,) * nd

    if stream:
        x, w = source
        src_args = [x, x, w]
        src_specs = [
            pl.BlockSpec((rows, D_MODEL), lambda i, ci: (blk0 + jnp.minimum(ci + 1, nchunk - 1) * nbb + i, 0)),
            pl.BlockSpec((rows, D_MODEL), lambda i, ci: (blk0 + i, 0)),
            pl.BlockSpec(w.shape, lambda i, ci: (0, 0)),
        ]
        scratch = [pltpu.VMEM((2 * rows, n_piece * PIECE), _bf16), pltpu.VMEM((2 * rows, LANES), _f32), pltpu.VMEM((rows, D_MODEL), _bf16)]
    else:
        proj, tail = source
        src_args = [proj] * n_piece + [tail]
        src_specs = [pl.BlockSpec((rows, PIECE), row_block(p)) for p in range(n_piece)] + [pl.BlockSpec((rows, LANES), row_block(0))]
        scratch = []
    in_specs = src_specs + [spec for _, spec in consts]
    in_specs += [pl.BlockSpec((ib,) + a.shape[1:], batch_block(a.ndim - 1, not bcast_init)) for a in init]
    out_specs = [y_spec] + [pl.BlockSpec((bb,) + a.shape[1:], batch_block(a.ndim - 1, True)) for a in init]
    out_shape = [y_shape] + [jax.ShapeDtypeStruct((nb,) + a.shape[1:], _f32) for a in init]
    args = src_args + [a for a, _ in consts] + list(init)
    cast_specs = [pl.BlockSpec((a.shape[0] // (nbb * nchunk), a.shape[1]), lambda i, ci: (i * nchunk + ci, 0)) for a in casts]
    assert all(a.shape[0] % (16 * nbb * nchunk) == 0 for a in casts)
    body = functools.partial(kernels[int(stream)], c=c, bb=bb, **({"n_cast": len(casts)} if casts else {}))
    return pl.pallas_call(
        body if ybuf is not None else _drop_alias_ref(body, len(args)),
        grid=(nbb, nchunk),
        in_specs=in_specs + alias_spec + cast_specs,
        out_specs=out_specs + cast_specs,
        out_shape=out_shape + [jax.ShapeDtypeStruct(a.shape, _bf16) for a in casts],
        scratch_shapes=scratch,
        input_output_aliases={len(args): 0} if ybuf is not None else {},
        compiler_params=pltpu.CompilerParams(dimension_semantics=("parallel", "arbitrary"), vmem_limit_bytes=VMEM_LIMIT),
        name=f"{name}_c{c}",
    )(*args, *([] if ybuf is None else [ybuf]), *casts)


def _mixer0(source, ybuf, cosf, sinf, gb_row, norm_g, init, **group):
    c = group["c"]
    dec, vec = _retention_tables(c)
    by_chunk = pl.BlockSpec((c, DH), lambda i, ci: (ci, 0))
    consts = [(cosf, by_chunk), (sinf, by_chunk), _whole(dec), _whole(vec), _whole(gb_row), _whole(norm_g)]
    return _mixer_call("mixer0", (_mixer0_kernel, _mixer0_stream_kernel), source, N_PIECE0, consts, init, ybuf, **group)


def _unit_lower_solve(a, rhs, c):
    bs = min(SOLVE_BLOCK, c)
    t, s = _tri3(c)
    if c > bs:
        shift = bs.bit_length() - 1
        same = jnp.right_shift(t, shift) == jnp.right_shift(s, shift)
        d = jnp.where(same, a, 0.0)
    else:
        d = a
    inv = jnp.where(t == s, 1.0, 0.0) - d
    p = d
    span = 2
    while span < bs:
        p = _bmm_x3(p, p)
        inv = inv + _bmm_x3(inv, p)
        span *= 2
    y = _bmm(inv, rhs)
    if c == bs:
        return y
    b = _bmm(inv, jnp.where(same, 0.0, a))
    y = y - _bmm(b, y)
    p = b
    span = 2
    while span < c // bs:
        p = _bmm(p, p)
        y = y + _bmm(p, y)
        span *= 2
    return y


def _gdn_chunk(q, k, v, beta_col, g_col, g_row, s_prev, c, between=_no_op):
    t, s = _tri3(c)
    gc_col, gc_row = _cumsum_col_row(g_col, g_row, c)
    dec_incl = jnp.exp(jnp.where(t >= s, gc_col - gc_row, -jnp.inf))
    dec_strict = jnp.where(t > s, dec_incl, 0.0)
    e_col = jnp.exp(gc_col)
    a = beta_col * _bmm_nt(k, k) * dec_strict
    rhs = jnp.concatenate([beta_col * v, (beta_col * e_col) * k], axis=-1)
    between(1)
    sol = _unit_lower_solve(a, rhs, c)
    between(1)
    u = sol[:, :, :DH] - _bmm(sol[:, :, DH:], s_prev)
    qk = _bmm_nt(q, k) * dec_incl
    between(1)
    o = e_col * _bmm(q, s_prev) + _bmm(qk, u)
    gl = gc_col[:, c - 1 : c, :]
    s_new = jnp.exp(gl) * s_prev + _bmm_tn(k * jnp.exp(gl - gc_col), u)
    return o, s_new


def _gla_chunk(q, k, v, bc, s_prev, c):
    t, s = _tri3(c)
    qe = q * jnp.exp(bc)
    att = jnp.where(t >= s, _bmm_nt(qe, k * jnp.exp(-bc)), 0.0)
    o = _bmm(att, v) + _bmm(qe, s_prev)
    bl = bc[:, c - 1 : c, :]
    ti, si = _tri3(DK_D)
    el_col = jnp.sum(jnp.where(ti == si, jnp.exp(bl), 0.0), axis=2, keepdims=True)
    s_new = el_col * s_prev + _bmm_tn(k * jnp.exp(bl - bc), v)
    return o, s_new


N_PIECE1 = 7


def _mixer1_body(
    load, small, between,
    cw_ref, gp_ref, wa_ref, ba_ref, ng_ref, s0_ref, cv0_ref, d0_ref,
    y_ref, s_ref, cv_ref, d_ref, *, c, bb,
):
    n = bb * NH
    w4 = NH * DH

    @pl.when(pl.program_id(1) == 0)
    def _():
        s_ref[...] = jnp.broadcast_to(s0_ref[...], s_ref.shape)
        cv_ref[...] = jnp.broadcast_to(cv0_ref[...], cv_ref.shape)
        d_ref[...] = jnp.broadcast_to(d0_ref[...], d_ref.shape)

    beta_all = _sigmoid(small)
    g_all = -jnp.exp(gp_ref[0:1, :]) * _softplus(small + gp_ref[1:2, :])
    g_all_t = jnp.transpose(g_all)
    log_alpha = -_softplus(-(_dot_hi(small[:, 0:GLA_RANK], wa_ref[...]) + ba_ref[...])) * (1.0 / GLA_TAU)
    cw = cw_ref[...]

    acts = []
    qkv = jnp.concatenate([load(0), load(1), load(2)], axis=1)
    between(1)
    for j in range(bb):
        ext = jnp.concatenate([cv_ref[j], qkv[j * c : (j + 1) * c, :]], axis=0)
        conv = cw[3:4] * ext[8 : 8 + c] + cw[2:3] * ext[7 : 7 + c] + cw[1:2] * ext[6 : 6 + c] + cw[0:1] * ext[5 : 5 + c]
        cv_ref[j] = ext[c : c + 8]
        acts.append(_silu(conv))

    def act_chains(off):
        return jnp.stack([acts[j][:, off + h * DH : off + (h + 1) * DH] for j in range(bb) for h in range(NH)])

    qc = act_chains(0)
    kc = act_chains(w4)
    qc = qc * lax.rsqrt(jnp.sum(qc * qc, axis=-1, keepdims=True) + NORM_EPS) * DH**-0.5
    kc = kc * lax.rsqrt(jnp.sum(kc * kc, axis=-1, keepdims=True) + NORM_EPS)
    between(1)
    o, s_new = _gdn_chunk(
        qc, kc, act_chains(2 * w4),
        _chain_cols(beta_all, c, bb, GLA_RANK), _chain_cols(g_all, c, bb, GLA_RANK + NH),
        _chain_rows(g_all_t, c, bb, GLA_RANK + NH), s_ref[...].reshape(n, DH, DH), c, between,
    )
    s_ref[...] = s_new.reshape(s_ref.shape)
    y_c = _group_norm(o, _head_rows(ng_ref, 0, bb), rms=True) * _silu(_chains(load(3), c, bb, DH))
    _store_chains(y_ref, y_c, c, bb, 0)
    between(1)

    tt = lax.broadcasted_iota(jnp.int32, (c, c), 0)
    ss = lax.broadcasted_iota(jnp.int32, (c, c), 1)
    ones_lt = jnp.where(tt >= ss, 1.0, 0.0).astype(_bf16)
    bcs = []
    for j in range(bb):
        la = log_alpha[j * c : (j + 1) * c, :]
        hi = la.astype(_bf16)
        r1 = la - hi.astype(_f32)
        mid = r1.astype(_bf16)
        lo = (r1 - mid.astype(_f32)).astype(_bf16)
        cum = functools.partial(jnp.dot, ones_lt, preferred_element_type=_f32)
        bcs.append(cum(hi) + (cum(mid) + cum(lo)))
    bc = jnp.stack([bcs[j][:, h * DK_D : (h + 1) * DK_D] for j in range(bb) for h in range(NH)])
    qkd = load(4)
    between(1)
    o, d_new = _gla_chunk(
        _chains(qkd, c, bb, DK_D) * DK_D**-0.5, _chains(qkd, c, bb, DK_D, off=NH * DK_D),
        _chains(load(5), c, bb, DH), bc, d_ref[...].reshape(n, DK_D, DH), c,
    )
    d_ref[...] = d_new.reshape(d_ref.shape)
    between(1)
    y_d = _group_norm(o, _head_rows(ng_ref, 1, bb), rms=False) * _silu(_chains(load(6), c, bb, DH))
    _store_chains(y_ref, y_d, c, bb, w4)
    between(N_PIECE1 + 1)


def _mixer1_kernel(*refs, c, bb):
    pieces, small_ref, rest = refs[:N_PIECE1], refs[N_PIECE1], refs[N_PIECE1 + 1 :]
    consts, outs = rest[:8], rest[9:]
    _mixer1_body(lambda p: pieces[p][...].astype(_f32), small_ref[...], _no_op, *consts, *outs, c=c, bb=bb)


def _mixer1_stream_kernel(xn_ref, xc_ref, w_ref, *rest, c, bb, n_cast=0):
    consts, rest = rest[:8], rest[9:]
    cast_in, outs, cast_out, scratch = rest[:n_cast], rest[n_cast : n_cast + 4], rest[n_cast + 4 : 2 * n_cast + 4], rest[2 * n_cast + 4 :]
    _cast_blocks(cast_in, cast_out)
    load, small, between = _stream_projection(xn_ref, xc_ref, w_ref, *scratch, N_PIECE1)
    _mixer1_body(load, small, between, *consts, *outs, c=c, bb=bb)


def _mixer1(source, ybuf, conv_w, gdn_par, w_alpha, b_alpha, norm_g, init, **group):
    consts = [_whole(a) for a in (conv_w, gdn_par, w_alpha, b_alpha, norm_g)]
    return _mixer_call("mixer1", (_mixer1_kernel, _mixer1_stream_kernel), source, N_PIECE1, consts, init, ybuf, **group)


_GROUP_META = dict(
    row_off=ROW_META - ROW_SAMPLE, y_row_off=ROW_META, nb=1, nchunk=1, c=N_META, bb=1, bcast_init=True,
    y_rows=R_ROWS - ROW_META,
)
_GROUP_PROMPT = dict(row_off=0, y_row_off=0, nb=BATCH, nchunk=SEQ // CHUNK, c=CHUNK, bb=4, bcast_init=True, stream=True)
_GROUP_SAMPLE = dict(row_off=0, y_row_off=ROW_SAMPLE, nb=DEC_BATCH, nchunk=1, c=DEC_SEQ, bb=16, bcast_init=False)


def _rows_to_batch_major_kernel(x_ref, o_ref):
    o_ref[...] = x_ref[...].reshape(o_ref.shape)


def _prompt_rows_to_batch_major(x):
    nchunk = SEQ // CHUNK
    out = pl.pallas_call(
        _rows_to_batch_major_kernel,
        grid=(nchunk,),
        in_specs=[pl.BlockSpec((BATCH * CHUNK, D_MODEL), lambda ci: (ci, 0))],
        out_specs=pl.BlockSpec((BATCH, None, CHUNK, D_MODEL), lambda ci: (0, ci, 0, 0)),
        out_shape=jax.ShapeDtypeStruct((BATCH, nchunk, CHUNK, D_MODEL), x.dtype),
        compiler_params=pltpu.CompilerParams(dimension_semantics=("parallel",)),
        name="rows_to_batch_major",
    )(x)
    return out.reshape(BATCH, SEQ, D_MODEL)


def _rotary_tables(pos):
    half = DH // 2
    inv = ROPE_BASE ** (-jnp.arange(half, dtype=_f32) / half)
    ang = pos.astype(_f32)[:, None] * inv[None, :]
    cos, sin = jnp.cos(ang), jnp.sin(ang)
    return jnp.concatenate([cos, cos], -1), jnp.concatenate([-sin, sin], -1)


def _lanes(m):
    return jnp.broadcast_to(m.astype(_f32)[..., None], m.shape + (LANES,))


def _conv_rows(s):
    return jnp.pad(s.astype(_f32), ((0, 0), (8 - (CONV_W - 1), 0), (0, 0)))


def kernel(x_prompt, x_sample, state_ret, state_mlstm_c, state_mlstm_n, state_mlstm_m, state_gdn, state_gdn_conv, state_gla, meta_tokens, w_in0, ret_norm_g, mlstm_gate_bias, mlstm_norm_g, w_out0, ln0_mix_g, ln0_mix_b, ffn0_w_gate, ffn0_w_up, ffn0_w_down, ln0_ffn_g, ln0_ffn_b, w_in1, gdn_conv_w, gdn_a_log, gdn_dt_bias, gdn_norm_g, gla_w_alpha, gla_b_alpha, gla_norm_g, w_out1, ln1_mix_g, ln1_mix_b, moe_w_router, moe_b_router, moe_w_gate, moe_w_up, moe_w_down, ln1_ffn_g, ln1_ffn_b):
    w4 = NH * DH
    nchunk = SEQ // CHUNK
    xp = x_prompt.reshape(BATCH, nchunk, CHUNK, D_MODEL).transpose(1, 0, 2, 3).reshape(N_PROMPT, D_MODEL)
    x = jnp.concatenate(
        [
            xp,
            x_sample.reshape(N_SAMPLE, D_MODEL),
            meta_tokens.astype(x_prompt.dtype),
            jnp.zeros((R_ROWS - ROW_META - N_META, D_MODEL), x_prompt.dtype),
        ],
        0,
    )

    w_in0_p = jnp.pad(w_in0.astype(_bf16), ((0, 0), (0, P0_PAD - w_in0.shape[1])))
    c_z_end = QKV_C + w4
    c_qd = c_z_end + 2 * NH
    c_lr = c_qd + 2 * NH * DK_D + 2 * w4
    w1b = w_in1.astype(_bf16)
    n_small = w_in1.shape[1] - c_lr + 2 * NH
    gaps = [jnp.zeros((D_MODEL, n), _bf16) for n in (P1_PAD - LANES - P1_MAIN, LANES - n_small)]
    w_in1_p = jnp.concatenate([w1b[:, :c_z_end], w1b[:, c_qd:c_lr], gaps[0], w1b[:, c_lr:], w1b[:, c_z_end:c_qd], gaps[1]], 1)
    gb_row = jnp.pad(mlstm_gate_bias.astype(_f32), (0, LANES - 2 * NH))[None]
    norm0 = jnp.stack([ret_norm_g, mlstm_norm_g]).astype(_f32)
    norm1 = jnp.stack([gdn_norm_g, gla_norm_g]).astype(_f32)
    lo = GLA_RANK + NH
    gdn_par = jnp.stack(
        [
            jnp.pad(gdn_a_log.astype(_f32), (lo, LANES - lo - NH)),
            jnp.pad(gdn_dt_bias.astype(_f32), (lo, LANES - lo - NH)),
        ]
    )
    w_router_t = jnp.pad(moe_w_router.astype(_f32).T, ((0, ROUTE_ROWS - N_EXPERTS), (0, 0)))
    b_router = jnp.pad(moe_b_router.astype(_f32), (0, ROUTE_ROWS - N_EXPERTS), constant_values=-jnp.inf)
    b_router_col = jnp.broadcast_to(b_router[:, None], (ROUTE_ROWS, LANES))
    moe_w_flat = [w.reshape(-1, w.shape[-1]) for w in (moe_w_gate, moe_w_up, moe_w_down)]

    def row(v):
        return v.astype(_f32)[None]

    proj0 = _proj(x[ROW_SAMPLE:], w_in0_p, tn=P0_PAD // 3)
    zeros_even = (
        jnp.zeros((1, NH, DH, DH), _f32), jnp.zeros((1, NH, DH, DH), _f32),
        jnp.zeros((1, NH, DH), _f32), jnp.zeros((1, NH, LANES), _f32),
    )
    cos_m, sin_m = _rotary_tables(jnp.arange(N_META))
    cos_p, sin_p = _rotary_tables(N_META + jnp.arange(SEQ))
    cos_s, sin_s = _rotary_tables(PAST_LEN + jnp.arange(DEC_SEQ))
    ybuf = jnp.zeros((R_ROWS, 2 * w4), _bf16)
    ybuf, *meta_even = _mixer0(proj0, ybuf, cos_m, sin_m, gb_row, norm0, zeros_even, **_GROUP_META)
    casts0 = [ffn0_w_gate, ffn0_w_up, ffn0_w_down.reshape(-1, 2 * LANES), w_out0]
    ybuf, p_ret, p_mc, p_mn, p_mm, wg0, wu0, wd0, wo0 = _mixer0(
        (x, w_in0_p), ybuf, cos_p, sin_p, gb_row, norm0, meta_even, casts=casts0, **_GROUP_PROMPT)
    init_s = (state_ret.astype(_f32), state_mlstm_c.astype(_f32), state_mlstm_n.astype(_f32), _lanes(state_mlstm_m))
    ybuf, s_ret, s_mc, s_mn, s_mm = _mixer0(proj0, ybuf, cos_s, sin_s, gb_row, norm0, init_s, **_GROUP_SAMPLE)
    x = _out_ln(ybuf, wo0, x, row(ln0_mix_g), row(ln0_mix_b))
    x = _ffn_ln(x, wg0, wu0, wd0.reshape(ffn0_w_down.shape), row(ln0_ffn_g), row(ln0_ffn_b))

    proj1 = _proj(x[ROW_SAMPLE:], w_in1_p, tn=P1_PAD // 3)
    zeros_odd = (jnp.zeros((1, NH, DH, DH), _f32), jnp.zeros((1, 8, QKV_C), _f32), jnp.zeros((1, NH, DK_D, DH), _f32))
    m1_par = (gdn_conv_w.astype(_f32), gdn_par, gla_w_alpha.astype(_f32), row(gla_b_alpha), norm1)
    ybuf, *meta_odd = _mixer1(proj1, ybuf, *m1_par, zeros_odd, **_GROUP_META)
    ybuf, p_gdn, p_conv, p_gla, *moe_w, wo1 = _mixer1((x, w_in1_p), ybuf, *m1_par, meta_odd, casts=moe_w_flat + [w_out1], **_GROUP_PROMPT)
    moe_w = [w.reshape(w32.shape) for w, w32 in zip(moe_w, (moe_w_gate, moe_w_up, moe_w_down))]
    init_s = (state_gdn.astype(_f32), _conv_rows(state_gdn_conv), state_gla.astype(_f32))
    ybuf, s_gdn, s_conv, s_gla = _mixer1(proj1, ybuf, *m1_par, init_s, **_GROUP_SAMPLE)
    x, gate_t, code_t, counts = _out_ln_route(ybuf, wo1, x, row(ln1_mix_g), row(ln1_mix_b), w_router_t, b_router_col)
    x = _moe_ln(x, gate_t, code_t, counts, *moe_w, row(ln1_ffn_g), row(ln1_ffn_b))

    y_prompt = _prompt_rows_to_batch_major(x)
    y_sample = x[ROW_SAMPLE:ROW_META].reshape(DEC_BATCH, DEC_SEQ, D_MODEL)
    tail = 8 - (CONV_W - 1)
    return (
        y_prompt, y_sample,
        p_ret, p_mc, p_mn, p_mm[..., 0], p_gdn, p_conv[:, tail:], p_gla,
        s_ret, s_mc, s_mn, s_mm[..., 0], s_gdn, s_conv[:, tail:], s_gla,
    )
```

```python
import functools
import math

import jax
import jax.numpy as jnp
import numpy as np
from jax import lax
from jax.experimental import pallas as pl
from jax.experimental.pallas import tpu as pltpu

D_MODEL = 1024
BATCH = 8
SEQ = 2048
DEC_BATCH = 128
DEC_SEQ = 4
PAST_LEN = 16384
N_META = 16
CHUNK = 64
NH = 4
DH = 128
DK_D = 64
CONV_W = 4
GLA_RANK = 16
GLA_TAU = 16.0
D_FF = 2816
N_EXPERTS = 8
MOE_FF = 1408
ROPE_BASE = 10000.0
LN_EPS = 1e-5
NORM_EPS = 1e-6
DEPTH = 2
ALPHA = (2 * DEPTH) ** 0.25
QKV_C = 3 * NH * DH

LANES = 128
N_PROMPT = BATCH * SEQ
N_SAMPLE = DEC_BATCH * DEC_SEQ
ROW_SAMPLE = N_PROMPT
ROW_META = N_PROMPT + N_SAMPLE
TM_MOE = 448
MOE_BLOCK = 128
MOE_GROUP = 2
TM = TM_MOE * MOE_GROUP
R_ROWS = ((ROW_META + N_META + TM - 1) // TM) * TM

P0_PAD = 33 * LANES
P1_MAIN = 28 * LANES
P1_PAD = 30 * LANES
SOLVE_BLOCK = 16

VMEM_LIMIT = 56 * 1024 * 1024

_bf16 = jnp.bfloat16
_f32 = jnp.float32
_HI = lax.Precision.HIGHEST


def _dot_hi(a, b):
    return jnp.dot(a, b, preferred_element_type=_f32, precision=_HI)


def _bmm(a, b):
    return jnp.einsum("nmk,nkp->nmp", a.astype(_bf16), b.astype(_bf16), preferred_element_type=_f32)


def _bmm_nt(a, b):
    return jnp.einsum("nmk,npk->nmp", a.astype(_bf16), b.astype(_bf16), preferred_element_type=_f32)


def _bmm_tn(a, b):
    return jnp.einsum("nkm,nkp->nmp", a.astype(_bf16), b.astype(_bf16), preferred_element_type=_f32)


def _split2(a):
    hi = a.astype(_bf16)
    lo = (a - hi.astype(_f32)).astype(_bf16)
    return hi, lo


def _bmm_x3(a, b):
    ah, al = _split2(a)
    bh, bl = _split2(b)
    mm = functools.partial(jnp.einsum, "nmk,nkp->nmp", preferred_element_type=_f32)
    return mm(ah, bh) + (mm(ah, bl) + mm(al, bh))


def _sigmoid(x):
    return 1.0 / (1.0 + jnp.exp(-x))


def _silu(x):
    return x * _sigmoid(x)


def _softplus(x):
    return jnp.maximum(x, 0.0) + jnp.log1p(jnp.exp(-jnp.abs(x)))


def _group_norm(o, g_row, rms):
    if not rms:
        o = o - jnp.mean(o, axis=-1, keepdims=True)
    return o * lax.rsqrt(jnp.mean(o * o, axis=-1, keepdims=True) + NORM_EPS) * g_row


def _layer_norm(x, g_row, b_row):
    mu = jnp.mean(x, axis=-1, keepdims=True)
    xc = x - mu
    var = jnp.mean(xc * xc, axis=-1, keepdims=True)
    return xc * lax.rsqrt(var + LN_EPS) * g_row + b_row


def _tri3(c):
    t = lax.broadcasted_iota(jnp.int32, (1, c, c), 1)
    s = lax.broadcasted_iota(jnp.int32, (1, c, c), 2)
    return t, s


def _cumsum_col_row(x_col, x_row, c):
    t, s = _tri3(c)
    col = jnp.sum(jnp.where(t >= s, x_row, 0.0), axis=2, keepdims=True)
    row = jnp.sum(jnp.where(t <= s, x_col, 0.0), axis=1, keepdims=True)
    return col, row


def _proj_kernel(x_ref, w_ref, o_ref, tail_ref, xb_ref):
    j = pl.program_id(1)

    @pl.when(j == 0)
    def _():
        xb_ref[...] = x_ref[...].astype(_bf16)

    acc = jnp.dot(xb_ref[...], w_ref[...], preferred_element_type=_f32)
    o_ref[...] = acc.astype(o_ref.dtype)

    @pl.when(j == pl.num_programs(1) - 1)
    def _():
        tail_ref[...] = acc[:, acc.shape[1] - LANES :]


def _proj(x, w, tn):
    rows, k = x.shape
    n = w.shape[1]
    tm = min(TM, rows)
    return pl.pallas_call(
        _proj_kernel,
        grid=(rows // tm, n // tn),
        in_specs=[pl.BlockSpec((tm, k), lambda i, j: (i, 0)), pl.BlockSpec((k, tn), lambda i, j: (0, j))],
        out_specs=[pl.BlockSpec((tm, tn), lambda i, j: (i, j)), pl.BlockSpec((tm, LANES), lambda i, j: (i, 0))],
        out_shape=[jax.ShapeDtypeStruct((rows, n), _bf16), jax.ShapeDtypeStruct((rows, LANES), _f32)],
        scratch_shapes=[pltpu.VMEM((tm, k), _bf16)],
        compiler_params=pltpu.CompilerParams(dimension_semantics=("parallel", "arbitrary"), vmem_limit_bytes=VMEM_LIMIT),
        name="in_proj",
    )(x, w)


def _out_ln_kernel(y_ref, w_ref, x_ref, g_ref, b_ref, o_ref):
    h = jnp.dot(y_ref[...], w_ref[...], preferred_element_type=_f32)
    o_ref[...] = _layer_norm(ALPHA * x_ref[...] + h, g_ref[...], b_ref[...])


def _out_ln(y, w, x, g, b):
    rows, k = y.shape
    d = w.shape[1]
    return pl.pallas_call(
        _out_ln_kernel,
        grid=(rows // TM,),
        in_specs=[
            pl.BlockSpec((TM, k), lambda i: (i, 0)),
            pl.BlockSpec((k, d), lambda i: (0, 0)),
            pl.BlockSpec((TM, d), lambda i: (i, 0)),
            pl.BlockSpec((1, d), lambda i: (0, 0)),
            pl.BlockSpec((1, d), lambda i: (0, 0)),
        ],
        out_specs=pl.BlockSpec((TM, d), lambda i: (i, 0)),
        out_shape=jax.ShapeDtypeStruct((rows, d), _f32),
        compiler_params=pltpu.CompilerParams(dimension_semantics=("parallel",), vmem_limit_bytes=VMEM_LIMIT),
        name="out_proj_ln",
    )(y, w, x, g, b)


FF_SUB = (0, 512, 1024, 1408)


def _swiglu_partial(xb, wg_ref, wu_ref, wd_ref):
    out = None
    for lo, hi in zip(FF_SUB[:-1], FF_SUB[1:]):
        hg = jnp.dot(xb, wg_ref[:, lo:hi], preferred_element_type=_f32)
        hu = jnp.dot(xb, wu_ref[:, lo:hi], preferred_element_type=_f32)
        a = (_silu(hg) * hu).astype(_bf16)
        part = jnp.dot(a, wd_ref[lo:hi, :], preferred_element_type=_f32)
        out = part if out is None else out + part
    return out


def _ffn_ln_kernel(x_ref, wg_ref, wu_ref, wd_ref, g_ref, b_ref, o_ref, xb_ref, acc_ref):
    f = pl.program_id(1)

    @pl.when(f == 0)
    def _():
        xb_ref[...] = x_ref[...].astype(_bf16)
        acc_ref[...] = jnp.zeros_like(acc_ref)

    acc_ref[...] += _swiglu_partial(xb_ref[...], wg_ref, wu_ref, wd_ref)

    @pl.when(f == pl.num_programs(1) - 1)
    def _():
        o_ref[...] = _layer_norm(ALPHA * x_ref[...] + acc_ref[...], g_ref[...], b_ref[...])


def _ffn_ln(x, wg, wu, wd, g, b):
    rows, d = x.shape
    ff = wg.shape[1]
    tf = MOE_FF
    return pl.pallas_call(
        _ffn_ln_kernel,
        grid=(rows // TM, ff // tf),
        in_specs=[
            pl.BlockSpec((TM, d), lambda i, f: (i, 0)),
            pl.BlockSpec((d, tf), lambda i, f: (0, f)),
            pl.BlockSpec((d, tf), lambda i, f: (0, f)),
            pl.BlockSpec((tf, d), lambda i, f: (f, 0)),
            pl.BlockSpec((1, d), lambda i, f: (0, 0)),
            pl.BlockSpec((1, d), lambda i, f: (0, 0)),
        ],
        out_specs=pl.BlockSpec((TM, d), lambda i, f: (i, 0)),
        out_shape=jax.ShapeDtypeStruct((rows, d), _f32),
        scratch_shapes=[pltpu.VMEM((TM, d), _bf16), pltpu.VMEM((TM, d), _f32)],
        compiler_params=pltpu.CompilerParams(dimension_semantics=("parallel", "arbitrary"), vmem_limit_bytes=VMEM_LIMIT),
        name="ffn_ln",
    )(x, wg, wu, wd, g, b)


ROUTE_ROWS = 16


def _route_tile(x, wr_t, bias_col, before):
    xh, xl = _split2(x)
    wh, wl = _split2(wr_t)
    nt = functools.partial(lax.dot_general, dimension_numbers=(((1,), (1,)), ((), ())), preferred_element_type=_f32)
    logits = nt(wh, xh) + (nt(wh, xl) + nt(wl, xh)) + bias_col
    row = lax.broadcasted_iota(jnp.int32, logits.shape, 0)
    ex = jnp.exp(logits - jnp.max(logits, axis=0, keepdims=True))
    probs = ex / jnp.sum(ex, axis=0, keepdims=True)
    p1 = jnp.max(probs, axis=0, keepdims=True)
    i1 = jnp.min(jnp.where(probs == p1, row, ROUTE_ROWS), axis=0, keepdims=True)
    rest = jnp.where(row == i1, -1.0, probs)
    p2 = jnp.max(rest, axis=0, keepdims=True)
    i2 = jnp.min(jnp.where(rest == p2, row, ROUTE_ROWS), axis=0, keepdims=True)
    tot = p1 + p2
    gate_t = jnp.where(row == i1, p1 / tot, 0.0) + jnp.where(row == i2, p2 / tot, 0.0)
    sel_t = jnp.where(row == i1, 1.0, jnp.where(row == i2, 1.0, 0.0))
    rank_t = jnp.dot(sel_t.astype(_bf16), before, preferred_element_type=_f32)
    code_t = jnp.where(sel_t > 0.0, rank_t, -1.0)
    return gate_t[:N_EXPERTS], code_t[:N_EXPERTS], jnp.sum(sel_t, axis=1, keepdims=True)


def _out_ln_route_kernel(y_ref, w_ref, x_ref, g_ref, b_ref, wr_ref, br_ref, before_ref, o_ref, gate_t_ref, code_t_ref, cnt_ref):
    h = jnp.dot(y_ref[...], w_ref[...], preferred_element_type=_f32)
    x_new = _layer_norm(ALPHA * x_ref[...] + h, g_ref[...], b_ref[...])
    o_ref[...] = x_new
    for i in range(MOE_GROUP):
        gate_t, code_t, cnt = _route_tile(x_new[i * TM_MOE : (i + 1) * TM_MOE], wr_ref[...], br_ref[:, 0:1], before_ref[...])
        gate_t_ref[i] = gate_t
        code_t_ref[i] = code_t
        cnt_ref[i] = jnp.broadcast_to(cnt, (ROUTE_ROWS, LANES)).astype(jnp.int32)


def _out_ln_route(y, w, x, g, b, wr_t, br_col):
    rows, k = y.shape
    d = w.shape[1]
    nt = rows // TM_MOE
    t = np.arange(TM_MOE)
    before = jnp.asarray(t[:, None] < t[None, :], _bf16)

    def tile_spec(shape):
        return pl.BlockSpec((MOE_GROUP,) + shape, lambda i: (i, 0, 0))

    x_new, gate_t, code_t, cnt = pl.pallas_call(
        _out_ln_route_kernel,
        grid=(rows // TM,),
        in_specs=[
            pl.BlockSpec((TM, k), lambda i: (i, 0)),
            pl.BlockSpec((k, d), lambda i: (0, 0)),
            pl.BlockSpec((TM, d), lambda i: (i, 0)),
            pl.BlockSpec((1, d), lambda i: (0, 0)),
            pl.BlockSpec((1, d), lambda i: (0, 0)),
            pl.BlockSpec((ROUTE_ROWS, d), lambda i: (0, 0)),
            pl.BlockSpec((ROUTE_ROWS, LANES), lambda i: (0, 0)),
            pl.BlockSpec((TM_MOE, TM_MOE), lambda i: (0, 0)),
        ],
        out_specs=[
            pl.BlockSpec((TM, d), lambda i: (i, 0)),
            tile_spec((N_EXPERTS, TM_MOE)),
            tile_spec((N_EXPERTS, TM_MOE)),
            tile_spec((ROUTE_ROWS, LANES)),
        ],
        out_shape=[
            jax.ShapeDtypeStruct((rows, d), _f32),
            jax.ShapeDtypeStruct((nt, N_EXPERTS, TM_MOE), _f32),
            jax.ShapeDtypeStruct((nt, N_EXPERTS, TM_MOE), _f32),
            jax.ShapeDtypeStruct((nt, ROUTE_ROWS, LANES), jnp.int32),
        ],
        compiler_params=pltpu.CompilerParams(dimension_semantics=("parallel",), vmem_limit_bytes=VMEM_LIMIT),
        name="out_proj_ln_route",
    )(y, w, x, g, b, wr_t, br_col, before)
    return x_new, gate_t, code_t, cnt[:, :N_EXPERTS, 0].reshape(-1)


def _moe_ln_kernel(cnt_ref, x_ref, gate_t_ref, code_t_ref, wg_ref, wu_ref, wd_ref, g_ref, b_ref, o_ref, xb_ref, acc_ref, first_ref):
    grp = pl.program_id(0)
    e = pl.program_id(1)

    @pl.when(e == 0)
    def _():
        xb_ref[...] = x_ref[...].astype(_bf16)
        acc_ref[...] = jnp.zeros_like(acc_ref)

    slot_col = lax.broadcasted_iota(jnp.int32, (MOE_BLOCK, 1), 0)
    counts = [cnt_ref[(grp * MOE_GROUP + i) * N_EXPERTS + e] for i in range(MOE_GROUP)]
    code_rows = [code_t_ref[i, pl.ds(e, 1), :] for i in range(MOE_GROUP)]
    gate_rows = [gate_t_ref[i, pl.ds(e, 1), :] for i in range(MOE_GROUP)]
    first_rows = pl.ds(pl.multiple_of(e * MOE_BLOCK, MOE_BLOCK), MOE_BLOCK)
    scatter_dims = (((0,), (0,)), ((), ()))

    def gather_rows(i, blk):
        hit = code_rows[i] == (slot_col + blk * MOE_BLOCK).astype(_f32)
        onehot = jnp.where(hit, 1.0, 0.0).astype(_bf16)
        xs = jnp.dot(onehot, xb_ref[i * TM_MOE : (i + 1) * TM_MOE, :], preferred_element_type=_f32).astype(_bf16)
        gate = jnp.sum(jnp.where(hit, gate_rows[i], 0.0), axis=1, keepdims=True)
        return onehot, xs, gate

    def first_pass(i):
        _, xs, gate = gather_rows(i, 0)
        first_ref[i, first_rows, :] = (_swiglu_partial(xs, wg_ref, wu_ref, wd_ref) * gate).astype(_bf16)

    def no_pass(i):
        first_ref[i, first_rows, :] = jnp.zeros((MOE_BLOCK, first_ref.shape[2]), _bf16)

    assert MOE_GROUP == 2
    has0 = counts[0] > 0
    has1 = counts[1] > 0

    @pl.when(jnp.logical_and(has0, has1))
    def _():
        _, xs0, gate0 = gather_rows(0, 0)
        _, xs1, gate1 = gather_rows(1, 0)
        out = _swiglu_partial(jnp.concatenate([xs0, xs1], axis=0), wg_ref, wu_ref, wd_ref)
        first_ref[0, first_rows, :] = (out[:MOE_BLOCK] * gate0).astype(_bf16)
        first_ref[1, first_rows, :] = (out[MOE_BLOCK:] * gate1).astype(_bf16)

    @pl.when(jnp.logical_and(has0, jnp.logical_not(has1)))
    def _():
        first_pass(0)
        no_pass(1)

    @pl.when(jnp.logical_and(has1, jnp.logical_not(has0)))
    def _():
        no_pass(0)
        first_pass(1)

    @pl.when(jnp.logical_and(jnp.logical_not(has0), jnp.logical_not(has1)))
    def _():
        no_pass(0)
        no_pass(1)

    for blk in range(1, -(-TM_MOE // MOE_BLOCK)):
        for i in range(MOE_GROUP):

            @pl.when(counts[i] > blk * MOE_BLOCK)
            def _():
                onehot, xs, gate = gather_rows(i, blk)
                out = (_swiglu_partial(xs, wg_ref, wu_ref, wd_ref) * gate).astype(_bf16)
                acc_ref[i * TM_MOE : (i + 1) * TM_MOE, :] += lax.dot_general(onehot, out, scatter_dims, preferred_element_type=_f32)

    @pl.when(e == pl.num_programs(1) - 1)
    def _():
        slot = lax.broadcasted_iota(jnp.int32, (1, MOE_BLOCK, 1), 1).astype(_f32)
        for i in range(MOE_GROUP):
            hit = code_t_ref[i][:, None, :] == slot
            onehot = jnp.where(hit, 1.0, 0.0).astype(_bf16).reshape(N_EXPERTS * MOE_BLOCK, TM_MOE)
            rows = slice(i * TM_MOE, (i + 1) * TM_MOE)
            moe = acc_ref[rows, :] + lax.dot_general(onehot, first_ref[i], scatter_dims, preferred_element_type=_f32)
            o_ref[rows, :] = _layer_norm(ALPHA * x_ref[rows, :] + moe, g_ref[...], b_ref[...])


def _moe_ln(x, gate_t, code_t, counts, wg, wu, wd, g, b):
    rows, d = x.shape
    ne, _, ff = wg.shape
    grid_spec = pltpu.PrefetchScalarGridSpec(
        num_scalar_prefetch=1,
        grid=(rows // TM, ne),
        in_specs=[
            pl.BlockSpec((TM, d), lambda i, e, cnt: (i, 0)),
            pl.BlockSpec((MOE_GROUP, N_EXPERTS, TM_MOE), lambda i, e, cnt: (i, 0, 0)),
            pl.BlockSpec((MOE_GROUP, N_EXPERTS, TM_MOE), lambda i, e, cnt: (i, 0, 0)),
            pl.BlockSpec((None, d, ff), lambda i, e, cnt: (e, 0, 0)),
            pl.BlockSpec((None, d, ff), lambda i, e, cnt: (e, 0, 0)),
            pl.BlockSpec((None, ff, d), lambda i, e, cnt: (e, 0, 0)),
            pl.BlockSpec((1, d), lambda i, e, cnt: (0, 0)),
            pl.BlockSpec((1, d), lambda i, e, cnt: (0, 0)),
        ],
        out_specs=pl.BlockSpec((TM, d), lambda i, e, cnt: (i, 0)),
        scratch_shapes=[
            pltpu.VMEM((TM, d), _bf16),
            pltpu.VMEM((TM, d), _f32),
            pltpu.VMEM((MOE_GROUP, ne * MOE_BLOCK, d), _bf16),
        ],
    )
    return pl.pallas_call(
        _moe_ln_kernel,
        grid_spec=grid_spec,
        out_shape=jax.ShapeDtypeStruct((rows, d), _f32),
        compiler_params=pltpu.CompilerParams(dimension_semantics=("parallel", "arbitrary"), vmem_limit_bytes=VMEM_LIMIT),
        name="moe_ln",
    )(counts, x, gate_t, code_t, wg, wu, wd, g, b)


def _chains(a, c, bb, width, off=0):
    return jnp.stack([a[j * c : (j + 1) * c, off + h * width : off + (h + 1) * width] for j in range(bb) for h in range(NH)])


def _chain_cols(a, c, bb, lane0):
    return jnp.stack([a[j * c : (j + 1) * c, lane0 + h : lane0 + h + 1] for j in range(bb) for h in range(NH)])


def _chain_rows(a_t, c, bb, lane0):
    return jnp.stack([a_t[lane0 + h : lane0 + h + 1, j * c : (j + 1) * c] for j in range(bb) for h in range(NH)])


def _head_rows(ref, row, bb):
    return jnp.stack([ref[row : row + 1, h * DH : (h + 1) * DH] for _ in range(bb) for h in range(NH)])


def _store_chains(y_ref, y, c, bb, off):
    for j in range(bb):
        for h in range(NH):
            y_ref[j * c : (j + 1) * c, off + h * DH : off + (h + 1) * DH] = y[j * NH + h].astype(y_ref.dtype)
    if y_ref.shape[0] > bb * c:
        y_ref[bb * c :, off : off + NH * DH] = jnp.zeros((y_ref.shape[0] - bb * c, NH * DH), y_ref.dtype)


def _drop_alias_ref(kernel, n_in):
    def body(*refs):
        return kernel(*refs[:n_in], None, *refs[n_in:])

    return body


def _y_buffer_specs(ybuf, rows, y_rows, y_row_off, nbb, width):
    if y_rows == rows:
        spec = pl.BlockSpec((rows, width), lambda i, ci: (y_row_off // rows + ci * nbb + i, 0))
    else:
        spec = pl.BlockSpec((y_rows, width), lambda i, ci: (y_row_off // y_rows, 0))
    return spec, jax.ShapeDtypeStruct((R_ROWS, width), _bf16), ([] if ybuf is None else [pl.BlockSpec(memory_space=pl.ANY)])


PIECE = NH * DH


def _stream_projection(xn_ref, xc_ref, w_ref, main_sc, tail_sc, xb_sc, n_piece):
    rows = xn_ref.shape[0]
    ci = pl.program_id(1)
    tail_col = w_ref.shape[1] - LANES

    def project(xb, row0, p):
        if p < n_piece:
            cols = slice(p * PIECE, (p + 1) * PIECE)
            main_sc[pl.ds(row0, rows), cols] = jnp.dot(xb, w_ref[:, cols], preferred_element_type=_f32).astype(_bf16)
        else:
            tail_sc[pl.ds(row0, rows), :] = jnp.dot(xb, w_ref[:, tail_col:], preferred_element_type=_f32)

    @pl.when(ci == 0)
    def _():
        xb0 = xc_ref[...].astype(_bf16)
        for p in range(n_piece + 1):
            project(xb0, 0, p)

    cur = pl.multiple_of((ci % 2) * rows, rows)
    nxt = pl.multiple_of(((ci + 1) % 2) * rows, rows)
    xb_sc[...] = xn_ref[...].astype(_bf16)
    pending = iter(range(n_piece + 1))

    def between(count=1):
        for _ in range(count):
            p = next(pending, None)
            if p is not None:
                project(xb_sc[...], nxt, p)

    def load(p):
        return main_sc[pl.ds(cur, rows), p * PIECE : (p + 1) * PIECE].astype(_f32)

    return load, tail_sc[pl.ds(cur, rows), :], between


def _retention_chunk(q, k, v, s_prev, dec, vec):
    att = _bmm_nt(q, k) * dec
    o = _bmm(att, v) + vec[:, :, 0:1] * _bmm(q, s_prev)
    s_new = vec[:, 0:1, 2:3] * s_prev + _bmm_tn(k * vec[:, :, 1:2], v)
    return o, s_new


def _mlstm_chunk(q, k, v, it_col, it_row, lf_col, lf_row, c_prev, n_prev, m_prev, c):
    t, s = _tri3(c)
    b_col, b_row = _cumsum_col_row(lf_col, lf_row, c)
    logw = jnp.where(t >= s, b_col - b_row + it_row, -jnp.inf)
    m_t = jnp.maximum(b_col + m_prev, jnp.max(logw, axis=2, keepdims=True))
    w = jnp.exp(logw - m_t)
    carry = jnp.exp(b_col + m_prev - m_t)
    qk = _bmm_nt(q, k) * w
    num = _bmm(qk, v) + carry * _bmm(q, c_prev)
    den = jnp.sum(qk, axis=2, keepdims=True) + carry * jnp.sum(q * n_prev, axis=2, keepdims=True)
    h = num / jnp.maximum(jnp.abs(den), jnp.exp(-m_t))
    m_new = m_t[:, c - 1 : c, :]
    b_last = b_col[:, c - 1 : c, :]
    w_last = jnp.exp(b_last - b_col + it_col - m_new)
    decay = jnp.exp(b_last + m_prev - m_new)
    kw = k * w_last
    c_new = decay * c_prev + _bmm_tn(kw, v)
    n_new = decay * n_prev + jnp.sum(kw, axis=1, keepdims=True)
    return h, c_new, n_new, m_new


N_PIECE0 = 8


def _mixer0_body(
    load, gate_tail, between,
    cos_ref, sin_ref, dec_ref, vec_ref, gb_ref, ng_ref, s0_ref, c0_ref, n0_ref, m0_ref,
    y_ref, s_ref, c_ref, n_ref, m_ref, *, c, bb,
):
    n = bb * NH

    @pl.when(pl.program_id(1) == 0)
    def _():
        s_ref[...] = jnp.broadcast_to(s0_ref[...], s_ref.shape)
        c_ref[...] = jnp.broadcast_to(c0_ref[...], c_ref.shape)
        n_ref[...] = jnp.broadcast_to(n0_ref[...], n_ref.shape)
        m_ref[...] = jnp.broadcast_to(m0_ref[...], m_ref.shape)

    def chains(p):
        return _chains(load(p), c, bb, DH)

    cosf = cos_ref[...][None]
    sinf = sin_ref[...][None]
    gates = gate_tail + gb_ref[...]
    lane = lax.broadcasted_iota(jnp.int32, gates.shape, 1)
    gates = jnp.where(lane < NH, gates, -_softplus(-gates))
    gates_t = jnp.transpose(gates)

    q = chains(0)
    k = chains(1)
    q = q * cosf + pltpu.roll(q, DH // 2, axis=2) * sinf
    k = (k * cosf + pltpu.roll(k, DH // 2, axis=2) * sinf) * DH**-0.5
    between(2)
    dec = jnp.concatenate([dec_ref[...]] * bb, axis=0)
    vec = jnp.concatenate([vec_ref[...]] * bb, axis=0)
    o, s_new = _retention_chunk(q, k, chains(2), s_ref[...].reshape(n, DH, DH), dec, vec)
    between(2)
    s_ref[...] = s_new.reshape(s_ref.shape)
    y_a = _group_norm(o, _head_rows(ng_ref, 0, bb), rms=False) * _silu(chains(3))
    _store_chains(y_ref, y_a, c, bb, 0)
    between(1)

    n_prev = jnp.stack([n_ref[j, h : h + 1, :] for j in range(bb) for h in range(NH)])
    m_prev = jnp.stack([m_ref[j, h : h + 1, 0:1] for j in range(bb) for h in range(NH)])
    h_b, c_new, n_new, m_new = _mlstm_chunk(
        chains(4), chains(5) * DH**-0.5, chains(6),
        _chain_cols(gates, c, bb, 0), _chain_rows(gates_t, c, bb, 0),
        _chain_cols(gates, c, bb, NH), _chain_rows(gates_t, c, bb, NH),
        c_ref[...].reshape(n, DH, DH), n_prev, m_prev, c,
    )
    between(2)
    c_ref[...] = c_new.reshape(c_ref.shape)
    for j in range(bb):
        for h in range(NH):
            n_ref[j, h : h + 1, :] = n_new[j * NH + h]
            m_ref[j, h : h + 1, :] = jnp.broadcast_to(m_new[j * NH + h], (1, LANES))
    h_b = _sigmoid(chains(7)) * h_b
    between(1)
    y_b = _group_norm(h_b, _head_rows(ng_ref, 1, bb), rms=False)
    _store_chains(y_ref, y_b, c, bb, NH * DH)
    between(N_PIECE0 + 1)


def _no_op(count=1):
    del count


def _mixer0_kernel(*refs, c, bb):
    pieces, gate_ref, rest = refs[:N_PIECE0], refs[N_PIECE0], refs[N_PIECE0 + 1 :]
    consts, outs = rest[:10], rest[11:]
    _mixer0_body(lambda p: pieces[p][...].astype(_f32), gate_ref[...], _no_op, *consts, *outs, c=c, bb=bb)


def _cast_blocks(cast_in, cast_out, cast_fns):
    for src, dst, fn in zip(cast_in, cast_out, cast_fns):
        blk = src[...]
        dst[...] = (blk if fn is None else fn(blk)).astype(dst.dtype)


def _mixer0_stream_kernel(xn_ref, xc_ref, w_ref, *rest, c, bb, cast_fns=()):
    n_cast = len(cast_fns)
    consts, rest = rest[:10], rest[11:]
    cast_in, outs, cast_out, scratch = rest[:n_cast], rest[n_cast : n_cast + 5], rest[n_cast + 5 : 2 * n_cast + 5], rest[2 * n_cast + 5 :]
    _cast_blocks(cast_in, cast_out, cast_fns)
    load, gate_tail, between = _stream_projection(xn_ref, xc_ref, w_ref, *scratch, N_PIECE0)
    _mixer0_body(load, gate_tail, between, *consts, *outs, c=c, bb=bb)


def _retention_tables(c):
    lg = np.log(1.0 - 2.0 ** (-5.0 - np.arange(NH, dtype=np.float64)))[:, None, None]
    t = np.arange(c, dtype=np.float64)
    diff = t[None, :, None] - t[None, None, :]
    dec = np.where(diff >= 0, np.exp(np.maximum(diff, 0.0) * lg), 0.0)
    vec = np.zeros((NH, c, LANES))
    vec[:, :, 0] = np.exp((t[None, :] + 1.0) * lg[:, 0])
    vec[:, :, 1] = np.exp((c - 1.0 - t[None, :]) * lg[:, 0])
    vec[:, :, 2] = np.exp(c * lg[:, 0])
    return jnp.asarray(dec, _f32), jnp.asarray(vec, _f32)


def _whole(a):
    return a, pl.BlockSpec(a.shape, lambda i, ci: (0,) * a.ndim)


def _mixer_call(name, kernels, source, n_piece, consts, init, ybuf, *, row_off, y_row_off, nb, nchunk, c, bb, bcast_init, y_rows=None, stream=False, casts=()):
    rows = bb * c
    nbb = nb // bb
    blk0 = row_off // rows
    ib = 1 if bcast_init else bb
    y_spec, y_shape, alias_spec = _y_buffer_specs(ybuf, rows, y_rows or rows, y_row_off, nbb, 2 * NH * DH)

    def row_block(col):
        return lambda i, ci: (blk0 + ci * nbb + i, col)

    def batch_block(nd, lead):
        return lambda i, ci: ((i if lead else 0),) + (0,) * nd

    if stream:
        x, w = source
        src_args = [x, x, w]
        src_specs = [
            pl.BlockSpec((rows, D_MODEL), lambda i, ci: (blk0 + jnp.minimum(ci + 1, nchunk - 1) * nbb + i, 0)),
            pl.BlockSpec((rows, D_MODEL), lambda i, ci: (blk0 + i, 0)),
            pl.BlockSpec(w.shape, lambda i, ci: (0, 0)),
        ]
        scratch = [pltpu.VMEM((2 * rows, n_piece * PIECE), _bf16), pltpu.VMEM((2 * rows, LANES), _f32), pltpu.VMEM((rows, D_MODEL), _bf16)]
    else:
        proj, tail = source
        src_args = [proj] * n_piece + [tail]
        src_specs = [pl.BlockSpec((rows, PIECE), row_block(p)) for p in range(n_piece)] + [pl.BlockSpec((rows, LANES), row_block(0))]
        scratch = []
    in_specs = src_specs + [spec for _, spec in consts]
    in_specs += [pl.BlockSpec((ib,) + a.shape[1:], batch_block(a.ndim - 1, not bcast_init)) for a in init]
    out_specs = [y_spec] + [pl.BlockSpec((bb,) + a.shape[1:], batch_block(a.ndim - 1, True)) for a in init]
    out_shape = [y_shape] + [jax.ShapeDtypeStruct((nb,) + a.shape[1:], _f32) for a in init]
    args = src_args + [a for a, _ in consts] + list(init)
    cast_in_specs, cast_out_specs, cast_shapes = [], [], []
    for a, n_blocks, out_cols, _ in casts:
        blk_rows = a.shape[0] // n_blocks
        assert n_blocks <= nbb * nchunk and blk_rows * n_blocks == a.shape[0] and blk_rows % 16 == 0
        index = (lambda last: lambda i, ci: (jnp.minimum(i * nchunk + ci, last), 0))(n_blocks - 1)
        cast_in_specs.append(pl.BlockSpec((blk_rows, a.shape[1]), index))
        cast_out_specs.append(pl.BlockSpec((blk_rows, out_cols), index))
        cast_shapes.append(jax.ShapeDtypeStruct((a.shape[0], out_cols), _bf16))
    body = functools.partial(kernels[int(stream)], c=c, bb=bb, **({"cast_fns": tuple(fn for *_, fn in casts)} if casts else {}))
    return pl.pallas_call(
        body if ybuf is not None else _drop_alias_ref(body, len(args)),
        grid=(nbb, nchunk),
        in_specs=in_specs + alias_spec + cast_in_specs,
        out_specs=out_specs + cast_out_specs,
        out_shape=out_shape + cast_shapes,
        scratch_shapes=scratch,
        input_output_aliases={len(args): 0} if ybuf is not None else {},
        compiler_params=pltpu.CompilerParams(dimension_semantics=("parallel", "arbitrary"), vmem_limit_bytes=VMEM_LIMIT),
        name=f"{name}_c{c}",
    )(*args, *([] if ybuf is None else [ybuf]), *[a for a, *_ in casts])


C1_Z_END = QKV_C + NH * DH
C1_QD = C1_Z_END + 2 * NH
C1_LR = C1_QD + 2 * NH * DK_D + 2 * NH * DH


def _regroup_w_in1(blk):
    rows, cols = blk.shape
    n_small = cols - C1_LR + 2 * NH
    gaps = [jnp.zeros((rows, n), blk.dtype) for n in (P1_PAD - LANES - P1_MAIN, LANES - n_small)]
    return jnp.concatenate([blk[:, :C1_Z_END], blk[:, C1_QD:C1_LR], gaps[0], blk[:, C1_LR:], blk[:, C1_Z_END:C1_QD], gaps[1]], axis=1)


def _side_cast(a, n_blocks, out_cols=None, fn=None):
    return a, n_blocks, out_cols or a.shape[1], fn


def _mixer0(source, ybuf, cosf, sinf, gb_row, norm_g, init, **group):
    c = group["c"]
    dec, vec = _retention_tables(c)
    by_chunk = pl.BlockSpec((c, DH), lambda i, ci: (ci, 0))
    consts = [(cosf, by_chunk), (sinf, by_chunk), _whole(dec), _whole(vec), _whole(gb_row), _whole(norm_g)]
    return _mixer_call("mixer0", (_mixer0_kernel, _mixer0_stream_kernel), source, N_PIECE0, consts, init, ybuf, **group)


def _unit_lower_solve(a, rhs, c):
    bs = min(SOLVE_BLOCK, c)
    t, s = _tri3(c)
    if c > bs:
        shift = bs.bit_length() - 1
        same = jnp.right_shift(t, shift) == jnp.right_shift(s, shift)
        d = jnp.where(same, a, 0.0)
    else:
        d = a
    inv = jnp.where(t == s, 1.0, 0.0) - d
    p = d
    span = 2
    while span < bs:
        p = _bmm_x3(p, p)
        inv = inv + _bmm_x3(inv, p)
        span *= 2
    y = _bmm(inv, rhs)
    if c == bs:
        return y
    b = _bmm(inv, jnp.where(same, 0.0, a))
    y = y - _bmm(b, y)
    p = b
    span = 2
    while span < c // bs:
        p = _bmm(p, p)
        y = y + _bmm(p, y)
        span *= 2
    return y


def _gdn_chunk(q, k, v, beta_col, g_col, g_row, s_prev, c, between=_no_op):
    t, s = _tri3(c)
    gc_col, gc_row = _cumsum_col_row(g_col, g_row, c)
    dec_incl = jnp.exp(jnp.where(t >= s, gc_col - gc_row, -jnp.inf))
    dec_strict = jnp.where(t > s, dec_incl, 0.0)
    e_col = jnp.exp(gc_col)
    a = beta_col * _bmm_nt(k, k) * dec_strict
    rhs = jnp.concatenate([beta_col * v, (beta_col * e_col) * k], axis=-1)
    between(1)
    sol = _unit_lower_solve(a, rhs, c)
    between(1)
    u = sol[:, :, :DH] - _bmm(sol[:, :, DH:], s_prev)
    qk = _bmm_nt(q, k) * dec_incl
    between(1)
    o = e_col * _bmm(q, s_prev) + _bmm(qk, u)
    gl = gc_col[:, c - 1 : c, :]
    s_new = jnp.exp(gl) * s_prev + _bmm_tn(k * jnp.exp(gl - gc_col), u)
    return o, s_new


def _gla_chunk(q, k, v, bc, s_prev, c):
    t, s = _tri3(c)
    qe = q * jnp.exp(bc)
    att = jnp.where(t >= s, _bmm_nt(qe, k * jnp.exp(-bc)), 0.0)
    o = _bmm(att, v) + _bmm(qe, s_prev)
    bl = bc[:, c - 1 : c, :]
    ti, si = _tri3(DK_D)
    el_col = jnp.sum(jnp.where(ti == si, jnp.exp(bl), 0.0), axis=2, keepdims=True)
    s_new = el_col * s_prev + _bmm_tn(k * jnp.exp(bl - bc), v)
    return o, s_new


N_PIECE1 = 7


def _mixer1_body(
    load, small, between,
    cw_ref, gp_ref, wa_ref, ba_ref, ng_ref, s0_ref, cv0_ref, d0_ref,
    y_ref, s_ref, cv_ref, d_ref, *, c, bb,
):
    n = bb * NH
    w4 = NH * DH

    @pl.when(pl.program_id(1) == 0)
    def _():
        s_ref[...] = jnp.broadcast_to(s0_ref[...], s_ref.shape)
        cv_ref[...] = jnp.broadcast_to(cv0_ref[...], cv_ref.shape)
        d_ref[...] = jnp.broadcast_to(d0_ref[...], d_ref.shape)

    beta_all = _sigmoid(small)
    g_all = -jnp.exp(gp_ref[0:1, :]) * _softplus(small + gp_ref[1:2, :])
    g_all_t = jnp.transpose(g_all)
    log_alpha = -_softplus(-(_dot_hi(small[:, 0:GLA_RANK], wa_ref[...]) + ba_ref[...])) * (1.0 / GLA_TAU)
    cw = cw_ref[...]

    acts = []
    qkv = jnp.concatenate([load(0), load(1), load(2)], axis=1)
    between(1)
    for j in range(bb):
        ext = jnp.concatenate([cv_ref[j], qkv[j * c : (j + 1) * c, :]], axis=0)
        conv = cw[3:4] * ext[8 : 8 + c] + cw[2:3] * ext[7 : 7 + c] + cw[1:2] * ext[6 : 6 + c] + cw[0:1] * ext[5 : 5 + c]
        cv_ref[j] = ext[c : c + 8]
        acts.append(_silu(conv))

    def act_chains(off):
        return jnp.stack([acts[j][:, off + h * DH : off + (h + 1) * DH] for j in range(bb) for h in range(NH)])

    qc = act_chains(0)
    kc = act_chains(w4)
    qc = qc * lax.rsqrt(jnp.sum(qc * qc, axis=-1, keepdims=True) + NORM_EPS) * DH**-0.5
    kc = kc * lax.rsqrt(jnp.sum(kc * kc, axis=-1, keepdims=True) + NORM_EPS)
    between(1)
    o, s_new = _gdn_chunk(
        qc, kc, act_chains(2 * w4),
        _chain_cols(beta_all, c, bb, GLA_RANK), _chain_cols(g_all, c, bb, GLA_RANK + NH),
        _chain_rows(g_all_t, c, bb, GLA_RANK + NH), s_ref[...].reshape(n, DH, DH), c, between,
    )
    s_ref[...] = s_new.reshape(s_ref.shape)
    y_c = _group_norm(o, _head_rows(ng_ref, 0, bb), rms=True) * _silu(_chains(load(3), c, bb, DH))
    _store_chains(y_ref, y_c, c, bb, 0)
    between(1)

    tt = lax.broadcasted_iota(jnp.int32, (c, c), 0)
    ss = lax.broadcasted_iota(jnp.int32, (c, c), 1)
    ones_lt = jnp.where(tt >= ss, 1.0, 0.0).astype(_bf16)
    bcs = []
    for j in range(bb):
        la = log_alpha[j * c : (j + 1) * c, :]
        hi = la.astype(_bf16)
        r1 = la - hi.astype(_f32)
        mid = r1.astype(_bf16)
        lo = (r1 - mid.astype(_f32)).astype(_bf16)
        cum = functools.partial(jnp.dot, ones_lt, preferred_element_type=_f32)
        bcs.append(cum(hi) + (cum(mid) + cum(lo)))
    bc = jnp.stack([bcs[j][:, h * DK_D : (h + 1) * DK_D] for j in range(bb) for h in range(NH)])
    qkd = load(4)
    between(1)
    o, d_new = _gla_chunk(
        _chains(qkd, c, bb, DK_D) * DK_D**-0.5, _chains(qkd, c, bb, DK_D, off=NH * DK_D),
        _chains(load(5), c, bb, DH), bc, d_ref[...].reshape(n, DK_D, DH), c,
    )
    d_ref[...] = d_new.reshape(d_ref.shape)
    between(1)
    y_d = _group_norm(o, _head_rows(ng_ref, 1, bb), rms=False) * _silu(_chains(load(6), c, bb, DH))
    _store_chains(y_ref, y_d, c, bb, w4)
    between(N_PIECE1 + 1)


def _mixer1_kernel(*refs, c, bb):
    pieces, small_ref, rest = refs[:N_PIECE1], refs[N_PIECE1], refs[N_PIECE1 + 1 :]
    consts, outs = rest[:8], rest[9:]
    _mixer1_body(lambda p: pieces[p][...].astype(_f32), small_ref[...], _no_op, *consts, *outs, c=c, bb=bb)


def _mixer1_stream_kernel(xn_ref, xc_ref, w_ref, *rest, c, bb, cast_fns=()):
    n_cast = len(cast_fns)
    consts, rest = rest[:8], rest[9:]
    cast_in, outs, cast_out, scratch = rest[:n_cast], rest[n_cast : n_cast + 4], rest[n_cast + 4 : 2 * n_cast + 4], rest[2 * n_cast + 4 :]
    _cast_blocks(cast_in, cast_out, cast_fns)
    load, small, between = _stream_projection(xn_ref, xc_ref, w_ref, *scratch, N_PIECE1)
    _mixer1_body(load, small, between, *consts, *outs, c=c, bb=bb)


def _mixer1(source, ybuf, conv_w, gdn_par, w_alpha, b_alpha, norm_g, init, **group):
    consts = [_whole(a) for a in (conv_w, gdn_par, w_alpha, b_alpha, norm_g)]
    return _mixer_call("mixer1", (_mixer1_kernel, _mixer1_stream_kernel), source, N_PIECE1, consts, init, ybuf, **group)


_GROUP_META = dict(
    row_off=ROW_META - ROW_SAMPLE, y_row_off=ROW_META, nb=1, nchunk=1, c=N_META, bb=1, bcast_init=True,
    y_rows=R_ROWS - ROW_META,
)
_GROUP_PROMPT = dict(row_off=0, y_row_off=0, nb=BATCH, nchunk=SEQ // CHUNK, c=CHUNK, bb=4, bcast_init=True, stream=True)
_GROUP_SAMPLE = dict(row_off=0, y_row_off=ROW_SAMPLE, nb=DEC_BATCH, nchunk=1, c=DEC_SEQ, bb=16, bcast_init=False)


def _rows_to_batch_major_kernel(x_ref, o_ref):
    o_ref[...] = x_ref[...].reshape(o_ref.shape)


def _prompt_rows_to_batch_major(x):
    nchunk = SEQ // CHUNK
    out = pl.pallas_call(
        _rows_to_batch_major_kernel,
        grid=(nchunk,),
        in_specs=[pl.BlockSpec((BATCH * CHUNK, D_MODEL), lambda ci: (ci, 0))],
        out_specs=pl.BlockSpec((BATCH, None, CHUNK, D_MODEL), lambda ci: (0, ci, 0, 0)),
        out_shape=jax.ShapeDtypeStruct((BATCH, nchunk, CHUNK, D_MODEL), x.dtype),
        compiler_params=pltpu.CompilerParams(dimension_semantics=("parallel",)),
        name="rows_to_batch_major",
    )(x)
    return out.reshape(BATCH, SEQ, D_MODEL)


def _rotary_tables(pos):
    half = DH // 2
    inv = ROPE_BASE ** (-jnp.arange(half, dtype=_f32) / half)
    ang = pos.astype(_f32)[:, None] * inv[None, :]
    cos, sin = jnp.cos(ang), jnp.sin(ang)
    return jnp.concatenate([cos, cos], -1), jnp.concatenate([-sin, sin], -1)


def _lanes(m):
    return jnp.broadcast_to(m.astype(_f32)[..., None], m.shape + (LANES,))


def _conv_rows(s):
    return jnp.pad(s.astype(_f32), ((0, 0), (8 - (CONV_W - 1), 0), (0, 0)))


def kernel(x_prompt, x_sample, state_ret, state_mlstm_c, state_mlstm_n, state_mlstm_m, state_gdn, state_gdn_conv, state_gla, meta_tokens, w_in0, ret_norm_g, mlstm_gate_bias, mlstm_norm_g, w_out0, ln0_mix_g, ln0_mix_b, ffn0_w_gate, ffn0_w_up, ffn0_w_down, ln0_ffn_g, ln0_ffn_b, w_in1, gdn_conv_w, gdn_a_log, gdn_dt_bias, gdn_norm_g, gla_w_alpha, gla_b_alpha, gla_norm_g, w_out1, ln1_mix_g, ln1_mix_b, moe_w_router, moe_b_router, moe_w_gate, moe_w_up, moe_w_down, ln1_ffn_g, ln1_ffn_b):
    w4 = NH * DH
    nchunk = SEQ // CHUNK
    xp = x_prompt.reshape(BATCH, nchunk, CHUNK, D_MODEL).transpose(1, 0, 2, 3).reshape(N_PROMPT, D_MODEL)
    x = jnp.concatenate(
        [
            xp,
            x_sample.reshape(N_SAMPLE, D_MODEL),
            meta_tokens.astype(x_prompt.dtype),
            jnp.zeros((R_ROWS - ROW_META - N_META, D_MODEL), x_prompt.dtype),
        ],
        0,
    )

    w_in0_p = jnp.pad(w_in0.astype(_bf16), ((0, 0), (0, P0_PAD - w_in0.shape[1])))
    gb_row = jnp.pad(mlstm_gate_bias.astype(_f32), (0, LANES - 2 * NH))[None]
    norm0 = jnp.stack([ret_norm_g, mlstm_norm_g]).astype(_f32)
    norm1 = jnp.stack([gdn_norm_g, gla_norm_g]).astype(_f32)
    lo = GLA_RANK + NH
    gdn_par = jnp.stack(
        [
            jnp.pad(gdn_a_log.astype(_f32), (lo, LANES - lo - NH)),
            jnp.pad(gdn_dt_bias.astype(_f32), (lo, LANES - lo - NH)),
        ]
    )
    w_router_t = jnp.pad(moe_w_router.astype(_f32).T, ((0, ROUTE_ROWS - N_EXPERTS), (0, 0)))
    b_router = jnp.pad(moe_b_router.astype(_f32), (0, ROUTE_ROWS - N_EXPERTS), constant_values=-jnp.inf)
    b_router_col = jnp.broadcast_to(b_router[:, None], (ROUTE_ROWS, LANES))
    moe_w_flat = [w.reshape(-1, w.shape[-1]) for w in (moe_w_gate, moe_w_up, moe_w_down)]

    def row(v):
        return v.astype(_f32)[None]

    proj0 = _proj(x[ROW_SAMPLE:], w_in0_p, tn=P0_PAD // 3)
    zeros_even = (
        jnp.zeros((1, NH, DH, DH), _f32), jnp.zeros((1, NH, DH, DH), _f32),
        jnp.zeros((1, NH, DH), _f32), jnp.zeros((1, NH, LANES), _f32),
    )
    cos_m, sin_m = _rotary_tables(jnp.arange(N_META))
    cos_p, sin_p = _rotary_tables(N_META + jnp.arange(SEQ))
    cos_s, sin_s = _rotary_tables(PAST_LEN + jnp.arange(DEC_SEQ))
    ybuf = jnp.zeros((R_ROWS, 2 * w4), _bf16)
    ybuf, *meta_even = _mixer0(proj0, ybuf, cos_m, sin_m, gb_row, norm0, zeros_even, **_GROUP_META)
    steps = _GROUP_PROMPT["nb"] // _GROUP_PROMPT["bb"] * _GROUP_PROMPT["nchunk"]
    casts0 = [
        _side_cast(ffn0_w_gate, steps), _side_cast(ffn0_w_up, steps), _side_cast(ffn0_w_down, D_FF // 64),
        _side_cast(w_out0, steps), _side_cast(w_in1, steps, P1_PAD, _regroup_w_in1),
    ]
    ybuf, p_ret, p_mc, p_mn, p_mm, wg0, wu0, wd0, wo0, w_in1_p = _mixer0(
        (x, w_in0_p), ybuf, cos_p, sin_p, gb_row, norm0, meta_even, casts=casts0, **_GROUP_PROMPT)
    init_s = (state_ret.astype(_f32), state_mlstm_c.astype(_f32), state_mlstm_n.astype(_f32), _lanes(state_mlstm_m))
    ybuf, s_ret, s_mc, s_mn, s_mm = _mixer0(proj0, ybuf, cos_s, sin_s, gb_row, norm0, init_s, **_GROUP_SAMPLE)
    x = _out_ln(ybuf, wo0, x, row(ln0_mix_g), row(ln0_mix_b))
    x = _ffn_ln(x, wg0, wu0, wd0, row(ln0_ffn_g), row(ln0_ffn_b))

    proj1 = _proj(x[ROW_SAMPLE:], w_in1_p, tn=P1_PAD // 3)
    zeros_odd = (jnp.zeros((1, NH, DH, DH), _f32), jnp.zeros((1, 8, QKV_C), _f32), jnp.zeros((1, NH, DK_D, DH), _f32))
    m1_par = (gdn_conv_w.astype(_f32), gdn_par, gla_w_alpha.astype(_f32), row(gla_b_alpha), norm1)
    ybuf, *meta_odd = _mixer1(proj1, ybuf, *m1_par, zeros_odd, **_GROUP_META)
    casts1 = [_side_cast(w, steps) for w in moe_w_flat + [w_out1]]
    ybuf, p_gdn, p_conv, p_gla, *moe_w, wo1 = _mixer1((x, w_in1_p), ybuf, *m1_par, meta_odd, casts=casts1, **_GROUP_PROMPT)
    moe_w = [w.reshape(w32.shape) for w, w32 in zip(moe_w, (moe_w_gate, moe_w_up, moe_w_down))]
    init_s = (state_gdn.astype(_f32), _conv_rows(state_gdn_conv), state_gla.astype(_f32))
    ybuf, s_gdn, s_conv, s_gla = _mixer1(proj1, ybuf, *m1_par, init_s, **_GROUP_SAMPLE)
    x, gate_t, code_t, counts = _out_ln_route(ybuf, wo1, x, row(ln1_mix_g), row(ln1_mix_b), w_router_t, b_router_col)
    x = _moe_ln(x, gate_t, code_t, counts, *moe_w, row(ln1_ffn_g), row(ln1_ffn_b))

    y_prompt = _prompt_rows_to_batch_major(x)
    y_sample = x[ROW_SAMPLE:ROW_META].reshape(DEC_BATCH, DEC_SEQ, D_MODEL)
    tail = 8 - (CONV_W - 1)
    return (
        y_prompt, y_sample,
        p_ret, p_mc, p_mn, p_mm[..., 0], p_gdn, p_conv[:, tail:], p_gla,
        s_ret, s_mc, s_mn, s_mm[..., 0], s_gdn, s_conv[:, tail:], s_gla,
    )
```

```python
import functools
import math

import jax
import jax.numpy as jnp
import numpy as np
from jax import lax
from jax.experimental import pallas as pl
from jax.experimental.pallas import tpu as pltpu

D_MODEL = 1024
BATCH = 8
SEQ = 2048
DEC_BATCH = 128
DEC_SEQ = 4
PAST_LEN = 16384
N_META = 16
CHUNK = 64
NH = 4
DH = 128
DK_D = 64
CONV_W = 4
GLA_RANK = 16
GLA_TAU = 16.0
D_FF = 2816
N_EXPERTS = 8
MOE_FF = 1408
ROPE_BASE = 10000.0
LN_EPS = 1e-5
NORM_EPS = 1e-6
DEPTH = 2
ALPHA = (2 * DEPTH) ** 0.25
QKV_C = 3 * NH * DH

LANES = 128
N_PROMPT = BATCH * SEQ
N_SAMPLE = DEC_BATCH * DEC_SEQ
ROW_SAMPLE = N_PROMPT
ROW_META = N_PROMPT + N_SAMPLE
TM_MOE = 448
MOE_BLOCK = 128
MOE_GROUP = 2
TM = TM_MOE * MOE_GROUP
R_ROWS = ((ROW_META + N_META + TM - 1) // TM) * TM

P0_PAD = 33 * LANES
P1_MAIN = 28 * LANES
P1_PAD = 30 * LANES
SOLVE_BLOCK = 16

VMEM_LIMIT = 56 * 1024 * 1024

_bf16 = jnp.bfloat16
_f32 = jnp.float32
_HI = lax.Precision.HIGHEST


def _dot_hi(a, b):
    return jnp.dot(a, b, preferred_element_type=_f32, precision=_HI)


def _bmm(a, b):
    return jnp.einsum("nmk,nkp->nmp", a.astype(_bf16), b.astype(_bf16), preferred_element_type=_f32)


def _bmm_nt(a, b):
    return jnp.einsum("nmk,npk->nmp", a.astype(_bf16), b.astype(_bf16), preferred_element_type=_f32)


def _bmm_tn(a, b):
    return jnp.einsum("nkm,nkp->nmp", a.astype(_bf16), b.astype(_bf16), preferred_element_type=_f32)


def _split2(a):
    hi = a.astype(_bf16)
    lo = (a - hi.astype(_f32)).astype(_bf16)
    return hi, lo


def _bmm_x3(a, b):
    ah, al = _split2(a)
    bh, bl = _split2(b)
    mm = functools.partial(jnp.einsum, "nmk,nkp->nmp", preferred_element_type=_f32)
    return mm(ah, bh) + (mm(ah, bl) + mm(al, bh))


def _sigmoid(x):
    return 1.0 / (1.0 + jnp.exp(-x))


def _silu(x):
    return x * _sigmoid(x)


def _softplus(x):
    return jnp.maximum(x, 0.0) + jnp.log1p(jnp.exp(-jnp.abs(x)))


def _group_norm(o, g_row, rms):
    if not rms:
        o = o - jnp.mean(o, axis=-1, keepdims=True)
    return o * lax.rsqrt(jnp.mean(o * o, axis=-1, keepdims=True) + NORM_EPS) * g_row


def _layer_norm(x, g_row, b_row):
    mu = jnp.mean(x, axis=-1, keepdims=True)
    xc = x - mu
    var = jnp.mean(xc * xc, axis=-1, keepdims=True)
    return xc * lax.rsqrt(var + LN_EPS) * g_row + b_row


def _tri3(c):
    t = lax.broadcasted_iota(jnp.int32, (1, c, c), 1)
    s = lax.broadcasted_iota(jnp.int32, (1, c, c), 2)
    return t, s


def _cumsum_col_row(x_col, x_row, c):
    t, s = _tri3(c)
    col = jnp.sum(jnp.where(t >= s, x_row, 0.0), axis=2, keepdims=True)
    row = jnp.sum(jnp.where(t <= s, x_col, 0.0), axis=1, keepdims=True)
    return col, row


def _proj_kernel(x_ref, w_ref, o_ref, tail_ref, xb_ref):
    j = pl.program_id(1)

    @pl.when(j == 0)
    def _():
        xb_ref[...] = x_ref[...].astype(_bf16)

    acc = jnp.dot(xb_ref[...], w_ref[...], preferred_element_type=_f32)
    o_ref[...] = acc.astype(o_ref.dtype)

    @pl.when(j == pl.num_programs(1) - 1)
    def _():
        tail_ref[...] = acc[:, acc.shape[1] - LANES :]


def _proj(x, w, tn):
    rows, k = x.shape
    n = w.shape[1]
    tm = min(TM, rows)
    return pl.pallas_call(
        _proj_kernel,
        grid=(rows // tm, n // tn),
        in_specs=[pl.BlockSpec((tm, k), lambda i, j: (i, 0)), pl.BlockSpec((k, tn), lambda i, j: (0, j))],
        out_specs=[pl.BlockSpec((tm, tn), lambda i, j: (i, j)), pl.BlockSpec((tm, LANES), lambda i, j: (i, 0))],
        out_shape=[jax.ShapeDtypeStruct((rows, n), _bf16), jax.ShapeDtypeStruct((rows, LANES), _f32)],
        scratch_shapes=[pltpu.VMEM((tm, k), _bf16)],
        compiler_params=pltpu.CompilerParams(dimension_semantics=("parallel", "arbitrary"), vmem_limit_bytes=VMEM_LIMIT),
        name="in_proj",
    )(x, w)


def _out_ln_kernel(y_ref, w_ref, x_ref, g_ref, b_ref, o_ref):
    h = jnp.dot(y_ref[...], w_ref[...], preferred_element_type=_f32)
    o_ref[...] = _layer_norm(ALPHA * x_ref[...] + h, g_ref[...], b_ref[...])


def _out_ln(y, w, x, g, b):
    rows, k = y.shape
    d = w.shape[1]
    return pl.pallas_call(
        _out_ln_kernel,
        grid=(rows // TM,),
        in_specs=[
            pl.BlockSpec((TM, k), lambda i: (i, 0)),
            pl.BlockSpec((k, d), lambda i: (0, 0)),
            pl.BlockSpec((TM, d), lambda i: (i, 0)),
            pl.BlockSpec((1, d), lambda i: (0, 0)),
            pl.BlockSpec((1, d), lambda i: (0, 0)),
        ],
        out_specs=pl.BlockSpec((TM, d), lambda i: (i, 0)),
        out_shape=jax.ShapeDtypeStruct((rows, d), _f32),
        compiler_params=pltpu.CompilerParams(dimension_semantics=("parallel",), vmem_limit_bytes=VMEM_LIMIT),
        name="out_proj_ln",
    )(y, w, x, g, b)


FF_SUB = (0, 512, 1024, 1408)


def _swiglu_partial(xb, wg_ref, wu_ref, wd_ref):
    acts = []
    for lo, hi in zip(FF_SUB[:-1], FF_SUB[1:]):
        hg = jnp.dot(xb, wg_ref[:, lo:hi], preferred_element_type=_f32)
        hu = jnp.dot(xb, wu_ref[:, lo:hi], preferred_element_type=_f32)
        acts.append((_silu(hg) * hu).astype(_bf16))
    return jnp.dot(jnp.concatenate(acts, axis=1), wd_ref[...], preferred_element_type=_f32)


def _ffn_ln_kernel(x_ref, wg_ref, wu_ref, wd_ref, g_ref, b_ref, o_ref, xb_ref, acc_ref):
    f = pl.program_id(1)

    @pl.when(f == 0)
    def _():
        xb_ref[...] = x_ref[...].astype(_bf16)
        acc_ref[...] = jnp.zeros_like(acc_ref)

    acc_ref[...] += _swiglu_partial(xb_ref[...], wg_ref, wu_ref, wd_ref)

    @pl.when(f == pl.num_programs(1) - 1)
    def _():
        o_ref[...] = _layer_norm(ALPHA * x_ref[...] + acc_ref[...], g_ref[...], b_ref[...])


def _ffn_ln(x, wg, wu, wd, g, b):
    rows, d = x.shape
    ff = wg.shape[1]
    tf = MOE_FF
    return pl.pallas_call(
        _ffn_ln_kernel,
        grid=(rows // TM, ff // tf),
        in_specs=[
            pl.BlockSpec((TM, d), lambda i, f: (i, 0)),
            pl.BlockSpec((d, tf), lambda i, f: (0, f)),
            pl.BlockSpec((d, tf), lambda i, f: (0, f)),
            pl.BlockSpec((tf, d), lambda i, f: (f, 0)),
            pl.BlockSpec((1, d), lambda i, f: (0, 0)),
            pl.BlockSpec((1, d), lambda i, f: (0, 0)),
        ],
        out_specs=pl.BlockSpec((TM, d), lambda i, f: (i, 0)),
        out_shape=jax.ShapeDtypeStruct((rows, d), _f32),
        scratch_shapes=[pltpu.VMEM((TM, d), _bf16), pltpu.VMEM((TM, d), _f32)],
        compiler_params=pltpu.CompilerParams(dimension_semantics=("parallel", "arbitrary"), vmem_limit_bytes=VMEM_LIMIT),
        name="ffn_ln",
    )(x, wg, wu, wd, g, b)


ROUTE_ROWS = 16


def _route_tile(x, wr_t, bias_col, before):
    xh, xl = _split2(x)
    wh, wl = _split2(wr_t)
    nt = functools.partial(lax.dot_general, dimension_numbers=(((1,), (1,)), ((), ())), preferred_element_type=_f32)
    logits = nt(wh, xh) + (nt(wh, xl) + nt(wl, xh)) + bias_col
    row = lax.broadcasted_iota(jnp.int32, logits.shape, 0)
    ex = jnp.exp(logits - jnp.max(logits, axis=0, keepdims=True))
    probs = ex / jnp.sum(ex, axis=0, keepdims=True)
    p1 = jnp.max(probs, axis=0, keepdims=True)
    i1 = jnp.min(jnp.where(probs == p1, row, ROUTE_ROWS), axis=0, keepdims=True)
    rest = jnp.where(row == i1, -1.0, probs)
    p2 = jnp.max(rest, axis=0, keepdims=True)
    i2 = jnp.min(jnp.where(rest == p2, row, ROUTE_ROWS), axis=0, keepdims=True)
    tot = p1 + p2
    gate_t = jnp.where(row == i1, p1 / tot, 0.0) + jnp.where(row == i2, p2 / tot, 0.0)
    sel_t = jnp.where(row == i1, 1.0, jnp.where(row == i2, 1.0, 0.0))
    rank_t = jnp.dot(sel_t.astype(_bf16), before, preferred_element_type=_f32)
    code_t = jnp.where(sel_t > 0.0, rank_t, -1.0)
    return gate_t[:N_EXPERTS], code_t[:N_EXPERTS], jnp.sum(sel_t, axis=1, keepdims=True)


def _out_ln_route_kernel(y_ref, w_ref, x_ref, g_ref, b_ref, wr_ref, br_ref, before_ref, o_ref, gate_t_ref, code_t_ref, cnt_ref):
    h = jnp.dot(y_ref[...], w_ref[...], preferred_element_type=_f32)
    x_new = _layer_norm(ALPHA * x_ref[...] + h, g_ref[...], b_ref[...])
    o_ref[...] = x_new
    for i in range(MOE_GROUP):
        gate_t, code_t, cnt = _route_tile(x_new[i * TM_MOE : (i + 1) * TM_MOE], wr_ref[...], br_ref[:, 0:1], before_ref[...])
        gate_t_ref[i] = gate_t
        code_t_ref[i] = code_t
        cnt_ref[i] = jnp.broadcast_to(cnt, (ROUTE_ROWS, LANES)).astype(jnp.int32)


def _out_ln_route(y, w, x, g, b, wr_t, br_col):
    rows, k = y.shape
    d = w.shape[1]
    nt = rows // TM_MOE
    t = np.arange(TM_MOE)
    before = jnp.asarray(t[:, None] < t[None, :], _bf16)

    def tile_spec(shape):
        return pl.BlockSpec((MOE_GROUP,) + shape, lambda i: (i, 0, 0))

    x_new, gate_t, code_t, cnt = pl.pallas_call(
        _out_ln_route_kernel,
        grid=(rows // TM,),
        in_specs=[
            pl.BlockSpec((TM, k), lambda i: (i, 0)),
            pl.BlockSpec((k, d), lambda i: (0, 0)),
            pl.BlockSpec((TM, d), lambda i: (i, 0)),
            pl.BlockSpec((1, d), lambda i: (0, 0)),
            pl.BlockSpec((1, d), lambda i: (0, 0)),
            pl.BlockSpec((ROUTE_ROWS, d), lambda i: (0, 0)),
            pl.BlockSpec((ROUTE_ROWS, LANES), lambda i: (0, 0)),
            pl.BlockSpec((TM_MOE, TM_MOE), lambda i: (0, 0)),
        ],
        out_specs=[
            pl.BlockSpec((TM, d), lambda i: (i, 0)),
            tile_spec((N_EXPERTS, TM_MOE)),
            tile_spec((N_EXPERTS, TM_MOE)),
            tile_spec((ROUTE_ROWS, LANES)),
        ],
        out_shape=[
            jax.ShapeDtypeStruct((rows, d), _f32),
            jax.ShapeDtypeStruct((nt, N_EXPERTS, TM_MOE), _f32),
            jax.ShapeDtypeStruct((nt, N_EXPERTS, TM_MOE), _f32),
            jax.ShapeDtypeStruct((nt, ROUTE_ROWS, LANES), jnp.int32),
        ],
        compiler_params=pltpu.CompilerParams(dimension_semantics=("parallel",), vmem_limit_bytes=VMEM_LIMIT),
        name="out_proj_ln_route",
    )(y, w, x, g, b, wr_t, br_col, before)
    return x_new, gate_t, code_t, cnt[:, :N_EXPERTS, 0].reshape(-1)


def _moe_ln_kernel(cnt_ref, x_ref, gate_t_ref, code_t_ref, wg_ref, wu_ref, wd_ref, g_ref, b_ref, o_ref, xb_ref, acc_ref, first_ref):
    grp = pl.program_id(0)
    e = pl.program_id(1)

    @pl.when(e == 0)
    def _():
        xb_ref[...] = x_ref[...].astype(_bf16)
        acc_ref[...] = jnp.zeros_like(acc_ref)

    slot_col = lax.broadcasted_iota(jnp.int32, (MOE_BLOCK, 1), 0)
    counts = [cnt_ref[(grp * MOE_GROUP + i) * N_EXPERTS + e] for i in range(MOE_GROUP)]
    code_rows = [code_t_ref[i, pl.ds(e, 1), :] for i in range(MOE_GROUP)]
    gate_rows = [gate_t_ref[i, pl.ds(e, 1), :] for i in range(MOE_GROUP)]
    first_rows = pl.ds(pl.multiple_of(e * MOE_BLOCK, MOE_BLOCK), MOE_BLOCK)
    scatter_dims = (((0,), (0,)), ((), ()))

    def gather_rows(i, blk):
        hit = code_rows[i] == (slot_col + blk * MOE_BLOCK).astype(_f32)
        onehot = jnp.where(hit, 1.0, 0.0).astype(_bf16)
        xs = jnp.dot(onehot, xb_ref[i * TM_MOE : (i + 1) * TM_MOE, :], preferred_element_type=_f32).astype(_bf16)
        gate = jnp.sum(jnp.where(hit, gate_rows[i], 0.0), axis=1, keepdims=True)
        return onehot, xs, gate

    def first_pass(i):
        _, xs, gate = gather_rows(i, 0)
        first_ref[i, first_rows, :] = (_swiglu_partial(xs, wg_ref, wu_ref, wd_ref) * gate).astype(_bf16)

    def no_pass(i):
        first_ref[i, first_rows, :] = jnp.zeros((MOE_BLOCK, first_ref.shape[2]), _bf16)

    assert MOE_GROUP == 2
    has0 = counts[0] > 0
    has1 = counts[1] > 0

    @pl.when(jnp.logical_and(has0, has1))
    def _():
        _, xs0, gate0 = gather_rows(0, 0)
        _, xs1, gate1 = gather_rows(1, 0)
        out = _swiglu_partial(jnp.concatenate([xs0, xs1], axis=0), wg_ref, wu_ref, wd_ref)
        first_ref[0, first_rows, :] = (out[:MOE_BLOCK] * gate0).astype(_bf16)
        first_ref[1, first_rows, :] = (out[MOE_BLOCK:] * gate1).astype(_bf16)

    @pl.when(jnp.logical_and(has0, jnp.logical_not(has1)))
    def _():
        first_pass(0)
        no_pass(1)

    @pl.when(jnp.logical_and(has1, jnp.logical_not(has0)))
    def _():
        no_pass(0)
        first_pass(1)

    @pl.when(jnp.logical_and(jnp.logical_not(has0), jnp.logical_not(has1)))
    def _():
        no_pass(0)
        no_pass(1)

    for blk in range(1, -(-TM_MOE // MOE_BLOCK)):
        for i in range(MOE_GROUP):

            @pl.when(counts[i] > blk * MOE_BLOCK)
            def _():
                onehot, xs, gate = gather_rows(i, blk)
                out = (_swiglu_partial(xs, wg_ref, wu_ref, wd_ref) * gate).astype(_bf16)
                acc_ref[i * TM_MOE : (i + 1) * TM_MOE, :] += lax.dot_general(onehot, out, scatter_dims, preferred_element_type=_f32)

    @pl.when(e == pl.num_programs(1) - 1)
    def _():
        slot = lax.broadcasted_iota(jnp.int32, (1, MOE_BLOCK, 1), 1).astype(_f32)
        for i in range(MOE_GROUP):
            hit = code_t_ref[i][:, None, :] == slot
            onehot = jnp.where(hit, 1.0, 0.0).astype(_bf16).reshape(N_EXPERTS * MOE_BLOCK, TM_MOE)
            rows = slice(i * TM_MOE, (i + 1) * TM_MOE)
            moe = acc_ref[rows, :] + lax.dot_general(onehot, first_ref[i], scatter_dims, preferred_element_type=_f32)
            o_ref[rows, :] = _layer_norm(ALPHA * x_ref[rows, :] + moe, g_ref[...], b_ref[...])


def _moe_ln(x, gate_t, code_t, counts, wg, wu, wd, g, b):
    rows, d = x.shape
    ne, _, ff = wg.shape
    grid_spec = pltpu.PrefetchScalarGridSpec(
        num_scalar_prefetch=1,
        grid=(rows // TM, ne),
        in_specs=[
            pl.BlockSpec((TM, d), lambda i, e, cnt: (i, 0)),
            pl.BlockSpec((MOE_GROUP, N_EXPERTS, TM_MOE), lambda i, e, cnt: (i, 0, 0)),
            pl.BlockSpec((MOE_GROUP, N_EXPERTS, TM_MOE), lambda i, e, cnt: (i, 0, 0)),
            pl.BlockSpec((None, d, ff), lambda i, e, cnt: (e, 0, 0)),
            pl.BlockSpec((None, d, ff), lambda i, e, cnt: (e, 0, 0)),
            pl.BlockSpec((None, ff, d), lambda i, e, cnt: (e, 0, 0)),
            pl.BlockSpec((1, d), lambda i, e, cnt: (0, 0)),
            pl.BlockSpec((1, d), lambda i, e, cnt: (0, 0)),
        ],
        out_specs=pl.BlockSpec((TM, d), lambda i, e, cnt: (i, 0)),
        scratch_shapes=[
            pltpu.VMEM((TM, d), _bf16),
            pltpu.VMEM((TM, d), _f32),
            pltpu.VMEM((MOE_GROUP, ne * MOE_BLOCK, d), _bf16),
        ],
    )
    return pl.pallas_call(
        _moe_ln_kernel,
        grid_spec=grid_spec,
        out_shape=jax.ShapeDtypeStruct((rows, d), _f32),
        compiler_params=pltpu.CompilerParams(dimension_semantics=("parallel", "arbitrary"), vmem_limit_bytes=VMEM_LIMIT),
        name="moe_ln",
    )(counts, x, gate_t, code_t, wg, wu, wd, g, b)


def _chains(a, c, bb, width, off=0):
    return jnp.stack([a[j * c : (j + 1) * c, off + h * width : off + (h + 1) * width] for j in range(bb) for h in range(NH)])


def _chain_cols(a, c, bb, lane0):
    return jnp.stack([a[j * c : (j + 1) * c, lane0 + h : lane0 + h + 1] for j in range(bb) for h in range(NH)])


def _chain_rows(a_t, c, bb, lane0):
    return jnp.stack([a_t[lane0 + h : lane0 + h + 1, j * c : (j + 1) * c] for j in range(bb) for h in range(NH)])


def _head_rows(ref, row, bb):
    return jnp.stack([ref[row : row + 1, h * DH : (h + 1) * DH] for _ in range(bb) for h in range(NH)])


def _store_chains(y_ref, y, c, bb, off):
    for j in range(bb):
        for h in range(NH):
            y_ref[j * c : (j + 1) * c, off + h * DH : off + (h + 1) * DH] = y[j * NH + h].astype(y_ref.dtype)
    if y_ref.shape[0] > bb * c:
        y_ref[bb * c :, off : off + NH * DH] = jnp.zeros((y_ref.shape[0] - bb * c, NH * DH), y_ref.dtype)


def _drop_alias_ref(kernel, n_in):
    def body(*refs):
        return kernel(*refs[:n_in], None, *refs[n_in:])

    return body


def _y_buffer_specs(ybuf, rows, y_rows, y_row_off, nbb, width):
    if y_rows == rows:
        spec = pl.BlockSpec((rows, width), lambda i, ci: (y_row_off // rows + ci * nbb + i, 0))
    else:
        spec = pl.BlockSpec((y_rows, width), lambda i, ci: (y_row_off // y_rows, 0))
    return spec, jax.ShapeDtypeStruct((R_ROWS, width), _bf16), ([] if ybuf is None else [pl.BlockSpec(memory_space=pl.ANY)])


PIECE = NH * DH


def _stream_projection(xn_ref, xc_ref, w_ref, main_sc, tail_sc, xb_sc, n_piece):
    rows = xn_ref.shape[0]
    ci = pl.program_id(1)
    tail_col = w_ref.shape[1] - LANES

    def project(xb, row0, p):
        if p < n_piece:
            cols = slice(p * PIECE, (p + 1) * PIECE)
            main_sc[pl.ds(row0, rows), cols] = jnp.dot(xb, w_ref[:, cols], preferred_element_type=_f32).astype(_bf16)
        else:
            tail_sc[pl.ds(row0, rows), :] = jnp.dot(xb, w_ref[:, tail_col:], preferred_element_type=_f32)

    @pl.when(ci == 0)
    def _():
        xb0 = xc_ref[...].astype(_bf16)
        for p in range(n_piece + 1):
            project(xb0, 0, p)

    cur = pl.multiple_of((ci % 2) * rows, rows)
    nxt = pl.multiple_of(((ci + 1) % 2) * rows, rows)
    xb_sc[...] = xn_ref[...].astype(_bf16)
    pending = iter(range(n_piece + 1))

    def between(count=1):
        for _ in range(count):
            p = next(pending, None)
            if p is not None:
                project(xb_sc[...], nxt, p)

    def load(p):
        return main_sc[pl.ds(cur, rows), p * PIECE : (p + 1) * PIECE].astype(_f32)

    return load, tail_sc[pl.ds(cur, rows), :], between


def _retention_chunk(q, k, v, s_prev, dec, vec):
    att = _bmm_nt(q, k) * dec
    o = _bmm(att, v) + vec[:, :, 0:1] * _bmm(q, s_prev)
    s_new = vec[:, 0:1, 2:3] * s_prev + _bmm_tn(k * vec[:, :, 1:2], v)
    return o, s_new


def _mlstm_chunk(q, k, v, it_col, it_row, lf_col, lf_row, c_prev, n_prev, m_prev, c):
    t, s = _tri3(c)
    b_col, b_row = _cumsum_col_row(lf_col, lf_row, c)
    logw = jnp.where(t >= s, b_col - b_row + it_row, -jnp.inf)
    m_t = jnp.maximum(b_col + m_prev, jnp.max(logw, axis=2, keepdims=True))
    w = jnp.exp(logw - m_t)
    carry = jnp.exp(b_col + m_prev - m_t)
    qk = _bmm_nt(q, k) * w
    num = _bmm(qk, v) + carry * _bmm(q, c_prev)
    den = jnp.sum(qk, axis=2, keepdims=True) + carry * jnp.sum(q * n_prev, axis=2, keepdims=True)
    h = num / jnp.maximum(jnp.abs(den), jnp.exp(-m_t))
    m_new = m_t[:, c - 1 : c, :]
    b_last = b_col[:, c - 1 : c, :]
    w_last = jnp.exp(b_last - b_col + it_col - m_new)
    decay = jnp.exp(b_last + m_prev - m_new)
    kw = k * w_last
    c_new = decay * c_prev + _bmm_tn(kw, v)
    n_new = decay * n_prev + jnp.sum(kw, axis=1, keepdims=True)
    return h, c_new, n_new, m_new


N_PIECE0 = 8


def _mixer0_body(
    load, gate_tail, between,
    cos_ref, sin_ref, dec_ref, vec_ref, gb_ref, ng_ref, s0_ref, c0_ref, n0_ref, m0_ref,
    y_ref, s_ref, c_ref, n_ref, m_ref, *, c, bb,
):
    n = bb * NH

    @pl.when(pl.program_id(1) == 0)
    def _():
        s_ref[...] = jnp.broadcast_to(s0_ref[...], s_ref.shape)
        c_ref[...] = jnp.broadcast_to(c0_ref[...], c_ref.shape)
        n_ref[...] = jnp.broadcast_to(n0_ref[...], n_ref.shape)
        m_ref[...] = jnp.broadcast_to(m0_ref[...], m_ref.shape)

    def chains(p):
        return _chains(load(p), c, bb, DH)

    cosf = cos_ref[...][None]
    sinf = sin_ref[...][None]
    gates = gate_tail + gb_ref[...]
    lane = lax.broadcasted_iota(jnp.int32, gates.shape, 1)
    gates = jnp.where(lane < NH, gates, -_softplus(-gates))
    gates_t = jnp.transpose(gates)

    q = chains(0)
    k = chains(1)
    q = q * cosf + pltpu.roll(q, DH // 2, axis=2) * sinf
    k = (k * cosf + pltpu.roll(k, DH // 2, axis=2) * sinf) * DH**-0.5
    between(2)
    dec = jnp.concatenate([dec_ref[...]] * bb, axis=0)
    vec = jnp.concatenate([vec_ref[...]] * bb, axis=0)
    o, s_new = _retention_chunk(q, k, chains(2), s_ref[...].reshape(n, DH, DH), dec, vec)
    between(2)
    s_ref[...] = s_new.reshape(s_ref.shape)
    y_a = _group_norm(o, _head_rows(ng_ref, 0, bb), rms=False) * _silu(chains(3))
    _store_chains(y_ref, y_a, c, bb, 0)
    between(1)

    n_prev = jnp.stack([n_ref[j, h : h + 1, :] for j in range(bb) for h in range(NH)])
    m_prev = jnp.stack([m_ref[j, h : h + 1, 0:1] for j in range(bb) for h in range(NH)])
    h_b, c_new, n_new, m_new = _mlstm_chunk(
        chains(4), chains(5) * DH**-0.5, chains(6),
        _chain_cols(gates, c, bb, 0), _chain_rows(gates_t, c, bb, 0),
        _chain_cols(gates, c, bb, NH), _chain_rows(gates_t, c, bb, NH),
        c_ref[...].reshape(n, DH, DH), n_prev, m_prev, c,
    )
    between(2)
    c_ref[...] = c_new.reshape(c_ref.shape)
    for j in range(bb):
        for h in range(NH):
            n_ref[j, h : h + 1, :] = n_new[j * NH + h]
            m_ref[j, h : h + 1, :] = jnp.broadcast_to(m_new[j * NH + h], (1, LANES))
    h_b = _sigmoid(chains(7)) * h_b
    between(1)
    y_b = _group_norm(h_b, _head_rows(ng_ref, 1, bb), rms=False)
    _store_chains(y_ref, y_b, c, bb, NH * DH)
    between(N_PIECE0 + 1)


def _no_op(count=1):
    del count


def _mixer0_kernel(*refs, c, bb):
    pieces, gate_ref, rest = refs[:N_PIECE0], refs[N_PIECE0], refs[N_PIECE0 + 1 :]
    consts, outs = rest[:10], rest[11:]
    _mixer0_body(lambda p: pieces[p][...].astype(_f32), gate_ref[...], _no_op, *consts, *outs, c=c, bb=bb)


def _cast_blocks(cast_in, cast_out, cast_fns):
    for src, dst, fn in zip(cast_in, cast_out, cast_fns):
        blk = src[...]
        dst[...] = (blk if fn is None else fn(blk)).astype(dst.dtype)


def _mixer0_stream_kernel(xn_ref, xc_ref, w_ref, *rest, c, bb, cast_fns=()):
    n_cast = len(cast_fns)
    consts, rest = rest[:10], rest[11:]
    cast_in, outs, cast_out, scratch = rest[:n_cast], rest[n_cast : n_cast + 5], rest[n_cast + 5 : 2 * n_cast + 5], rest[2 * n_cast + 5 :]
    _cast_blocks(cast_in, cast_out, cast_fns)
    load, gate_tail, between = _stream_projection(xn_ref, xc_ref, w_ref, *scratch, N_PIECE0)
    _mixer0_body(load, gate_tail, between, *consts, *outs, c=c, bb=bb)


def _retention_tables(c):
    lg = np.log(1.0 - 2.0 ** (-5.0 - np.arange(NH, dtype=np.float64)))[:, None, None]
    t = np.arange(c, dtype=np.float64)
    diff = t[None, :, None] - t[None, None, :]
    dec = np.where(diff >= 0, np.exp(np.maximum(diff, 0.0) * lg), 0.0)
    vec = np.zeros((NH, c, LANES))
    vec[:, :, 0] = np.exp((t[None, :] + 1.0) * lg[:, 0])
    vec[:, :, 1] = np.exp((c - 1.0 - t[None, :]) * lg[:, 0])
    vec[:, :, 2] = np.exp(c * lg[:, 0])
    return jnp.asarray(dec, _f32), jnp.asarray(vec, _f32)


def _whole(a):
    return a, pl.BlockSpec(a.shape, lambda i, ci: (0,) * a.ndim)


def _mixer_call(name, kernels, source, n_piece, consts, init, ybuf, *, row_off, y_row_off, nb, nchunk, c, bb, bcast_init, y_rows=None, stream=False, casts=()):
    rows = bb * c
    nbb = nb // bb
    blk0 = row_off // rows
    ib = 1 if bcast_init else bb
    y_spec, y_shape, alias_spec = _y_buffer_specs(ybuf, rows, y_rows or rows, y_row_off, nbb, 2 * NH * DH)

    def row_block(col):
        return lambda i, ci: (blk0 + ci * nbb + i, col)

    def batch_block(nd, lead):
        return lambda i, ci: ((i if lead else 0),) + (0,) * nd

    if stream:
        x, w = source
        src_args = [x, x, w]
        src_specs = [
            pl.BlockSpec((rows, D_MODEL), lambda i, ci: (blk0 + jnp.minimum(ci + 1, nchunk - 1) * nbb + i, 0)),
            pl.BlockSpec((rows, D_MODEL), lambda i, ci: (blk0 + i, 0)),
            pl.BlockSpec(w.shape, lambda i, ci: (0, 0)),
        ]
        scratch = [pltpu.VMEM((2 * rows, n_piece * PIECE), _bf16), pltpu.VMEM((2 * rows, LANES), _f32), pltpu.VMEM((rows, D_MODEL), _bf16)]
    else:
        proj, tail = source
        src_args = [proj] * n_piece + [tail]
        src_specs = [pl.BlockSpec((rows, PIECE), row_block(p)) for p in range(n_piece)] + [pl.BlockSpec((rows, LANES), row_block(0))]
        scratch = []
    in_specs = src_specs + [spec for _, spec in consts]
    in_specs += [pl.BlockSpec((ib,) + a.shape[1:], batch_block(a.ndim - 1, not bcast_init)) for a in init]
    out_specs = [y_spec] + [pl.BlockSpec((bb,) + a.shape[1:], batch_block(a.ndim - 1, True)) for a in init]
    out_shape = [y_shape] + [jax.ShapeDtypeStruct((nb,) + a.shape[1:], _f32) for a in init]
    args = src_args + [a for a, _ in consts] + list(init)
    cast_in_specs, cast_out_specs, cast_shapes = [], [], []
    for a, n_blocks, out_cols, _ in casts:
        blk_rows = a.shape[0] // n_blocks
        assert n_blocks <= nbb * nchunk and blk_rows * n_blocks == a.shape[0] and blk_rows % 16 == 0
        index = (lambda last: lambda i, ci: (jnp.minimum(i * nchunk + ci, last), 0))(n_blocks - 1)
        cast_in_specs.append(pl.BlockSpec((blk_rows, a.shape[1]), index))
        cast_out_specs.append(pl.BlockSpec((blk_rows, out_cols), index))
        cast_shapes.append(jax.ShapeDtypeStruct((a.shape[0], out_cols), _bf16))
    body = functools.partial(kernels[int(stream)], c=c, bb=bb, **({"cast_fns": tuple(fn for *_, fn in casts)} if casts else {}))
    return pl.pallas_call(
        body if ybuf is not None else _drop_alias_ref(body, len(args)),
        grid=(nbb, nchunk),
        in_specs=in_specs + alias_spec + cast_in_specs,
        out_specs=out_specs + cast_out_specs,
        out_shape=out_shape + cast_shapes,
        scratch_shapes=scratch,
        input_output_aliases={len(args): 0} if ybuf is not None else {},
        compiler_params=pltpu.CompilerParams(dimension_semantics=("parallel", "arbitrary"), vmem_limit_bytes=VMEM_LIMIT),
        name=f"{name}_c{c}",
    )(*args, *([] if ybuf is None else [ybuf]), *[a for a, *_ in casts])


C1_Z_END = QKV_C + NH * DH
C1_QD = C1_Z_END + 2 * NH
C1_LR = C1_QD + 2 * NH * DK_D + 2 * NH * DH


def _regroup_w_in1(blk):
    rows, cols = blk.shape
    n_small = cols - C1_LR + 2 * NH
    gaps = [jnp.zeros((rows, n), blk.dtype) for n in (P1_PAD - LANES - P1_MAIN, LANES - n_small)]
    return jnp.concatenate([blk[:, :C1_Z_END], blk[:, C1_QD:C1_LR], gaps[0], blk[:, C1_LR:], blk[:, C1_Z_END:C1_QD], gaps[1]], axis=1)


def _side_cast(a, n_blocks, out_cols=None, fn=None):
    return a, n_blocks, out_cols or a.shape[1], fn


def _mixer0(source, ybuf, cosf, sinf, gb_row, norm_g, init, **group):
    c = group["c"]
    dec, vec = _retention_tables(c)
    by_chunk = pl.BlockSpec((c, DH), lambda i, ci: (ci, 0))
    consts = [(cosf, by_chunk), (sinf, by_chunk), _whole(dec), _whole(vec), _whole(gb_row), _whole(norm_g)]
    return _mixer_call("mixer0", (_mixer0_kernel, _mixer0_stream_kernel), source, N_PIECE0, consts, init, ybuf, **group)


def _unit_lower_solve(a, rhs, c):
    bs = min(SOLVE_BLOCK, c)
    t, s = _tri3(c)
    if c > bs:
        shift = bs.bit_length() - 1
        same = jnp.right_shift(t, shift) == jnp.right_shift(s, shift)
        d = jnp.where(same, a, 0.0)
    else:
        d = a
    inv = jnp.where(t == s, 1.0, 0.0) - d
    p = d
    span = 2
    while span < bs:
        p = _bmm_x3(p, p)
        inv = inv + _bmm_x3(inv, p)
        span *= 2
    y = _bmm(inv, rhs)
    if c == bs:
        return y
    b = _bmm(inv, jnp.where(same, 0.0, a))
    y = y - _bmm(b, y)
    p = b
    span = 2
    while span < c // bs:
        p = _bmm(p, p)
        y = y + _bmm(p, y)
        span *= 2
    return y


def _gdn_chunk(q, k, v, beta_col, g_col, g_row, s_prev, c, between=_no_op):
    t, s = _tri3(c)
    gc_col, gc_row = _cumsum_col_row(g_col, g_row, c)
    dec_incl = jnp.exp(jnp.where(t >= s, gc_col - gc_row, -jnp.inf))
    dec_strict = jnp.where(t > s, dec_incl, 0.0)
    e_col = jnp.exp(gc_col)
    a = beta_col * _bmm_nt(k, k) * dec_strict
    rhs = jnp.concatenate([beta_col * v, (beta_col * e_col) * k], axis=-1)
    between(1)
    sol = _unit_lower_solve(a, rhs, c)
    between(1)
    u = sol[:, :, :DH] - _bmm(sol[:, :, DH:], s_prev)
    qk = _bmm_nt(q, k) * dec_incl
    between(1)
    o = e_col * _bmm(q, s_prev) + _bmm(qk, u)
    gl = gc_col[:, c - 1 : c, :]
    s_new = jnp.exp(gl) * s_prev + _bmm_tn(k * jnp.exp(gl - gc_col), u)
    return o, s_new


def _gla_chunk(q, k, v, bc, s_prev, c):
    t, s = _tri3(c)
    qe = q * jnp.exp(bc)
    att = jnp.where(t >= s, _bmm_nt(qe, k * jnp.exp(-bc)), 0.0)
    o = _bmm(att, v) + _bmm(qe, s_prev)
    bl = bc[:, c - 1 : c, :]
    ti, si = _tri3(DK_D)
    el_col = jnp.sum(jnp.where(ti == si, jnp.exp(bl), 0.0), axis=2, keepdims=True)
    s_new = el_col * s_prev + _bmm_tn(k * jnp.exp(bl - bc), v)
    return o, s_new


N_PIECE1 = 7


def _mixer1_body(
    load, small, between,
    cw_ref, gp_ref, wa_ref, ba_ref, ng_ref, s0_ref, cv0_ref, d0_ref,
    y_ref, s_ref, cv_ref, d_ref, *, c, bb,
):
    n = bb * NH
    w4 = NH * DH

    @pl.when(pl.program_id(1) == 0)
    def _():
        s_ref[...] = jnp.broadcast_to(s0_ref[...], s_ref.shape)
        cv_ref[...] = jnp.broadcast_to(cv0_ref[...], cv_ref.shape)
        d_ref[...] = jnp.broadcast_to(d0_ref[...], d_ref.shape)

    beta_all = _sigmoid(small)
    g_all = -jnp.exp(gp_ref[0:1, :]) * _softplus(small + gp_ref[1:2, :])
    g_all_t = jnp.transpose(g_all)
    log_alpha = -_softplus(-(_dot_hi(small[:, 0:GLA_RANK], wa_ref[...]) + ba_ref[...])) * (1.0 / GLA_TAU)
    cw = cw_ref[...]

    acts = []
    qkv = jnp.concatenate([load(0), load(1), load(2)], axis=1)
    between(1)
    for j in range(bb):
        ext = jnp.concatenate([cv_ref[j], qkv[j * c : (j + 1) * c, :]], axis=0)
        conv = cw[3:4] * ext[8 : 8 + c] + cw[2:3] * ext[7 : 7 + c] + cw[1:2] * ext[6 : 6 + c] + cw[0:1] * ext[5 : 5 + c]
        cv_ref[j] = ext[c : c + 8]
        acts.append(_silu(conv))

    def act_chains(off):
        return jnp.stack([acts[j][:, off + h * DH : off + (h + 1) * DH] for j in range(bb) for h in range(NH)])

    qc = act_chains(0)
    kc = act_chains(w4)
    qc = qc * lax.rsqrt(jnp.sum(qc * qc, axis=-1, keepdims=True) + NORM_EPS) * DH**-0.5
    kc = kc * lax.rsqrt(jnp.sum(kc * kc, axis=-1, keepdims=True) + NORM_EPS)
    between(1)
    o, s_new = _gdn_chunk(
        qc, kc, act_chains(2 * w4),
        _chain_cols(beta_all, c, bb, GLA_RANK), _chain_cols(g_all, c, bb, GLA_RANK + NH),
        _chain_rows(g_all_t, c, bb, GLA_RANK + NH), s_ref[...].reshape(n, DH, DH), c, between,
    )
    s_ref[...] = s_new.reshape(s_ref.shape)
    y_c = _group_norm(o, _head_rows(ng_ref, 0, bb), rms=True) * _silu(_chains(load(3), c, bb, DH))
    _store_chains(y_ref, y_c, c, bb, 0)
    between(1)

    tt = lax.broadcasted_iota(jnp.int32, (c, c), 0)
    ss = lax.broadcasted_iota(jnp.int32, (c, c), 1)
    ones_lt = jnp.where(tt >= ss, 1.0, 0.0).astype(_bf16)
    bcs = []
    for j in range(bb):
        la = log_alpha[j * c : (j + 1) * c, :]
        hi = la.astype(_bf16)
        r1 = la - hi.astype(_f32)
        mid = r1.astype(_bf16)
        lo = (r1 - mid.astype(_f32)).astype(_bf16)
        cum = functools.partial(jnp.dot, ones_lt, preferred_element_type=_f32)
        bcs.append(cum(hi) + (cum(mid) + cum(lo)))
    bc = jnp.stack([bcs[j][:, h * DK_D : (h + 1) * DK_D] for j in range(bb) for h in range(NH)])
    qkd = load(4)
    between(1)
    o, d_new = _gla_chunk(
        _chains(qkd, c, bb, DK_D) * DK_D**-0.5, _chains(qkd, c, bb, DK_D, off=NH * DK_D),
        _chains(load(5), c, bb, DH), bc, d_ref[...].reshape(n, DK_D, DH), c,
    )
    d_ref[...] = d_new.reshape(d_ref.shape)
    between(1)
    y_d = _group_norm(o, _head_rows(ng_ref, 1, bb), rms=False) * _silu(_chains(load(6), c, bb, DH))
    _store_chains(y_ref, y_d, c, bb, w4)
    between(N_PIECE1 + 1)


def _mixer1_kernel(*refs, c, bb):
    pieces, small_ref, rest = refs[:N_PIECE1], refs[N_PIECE1], refs[N_PIECE1 + 1 :]
    consts, outs = rest[:8], rest[9:]
    _mixer1_body(lambda p: pieces[p][...].astype(_f32), small_ref[...], _no_op, *consts, *outs, c=c, bb=bb)


def _mixer1_stream_kernel(xn_ref, xc_ref, w_ref, *rest, c, bb, cast_fns=()):
    n_cast = len(cast_fns)
    consts, rest = rest[:8], rest[9:]
    cast_in, outs, cast_out, scratch = rest[:n_cast], rest[n_cast : n_cast + 4], rest[n_cast + 4 : 2 * n_cast + 4], rest[2 * n_cast + 4 :]
    _cast_blocks(cast_in, cast_out, cast_fns)
    load, small, between = _stream_projection(xn_ref, xc_ref, w_ref, *scratch, N_PIECE1)
    _mixer1_body(load, small, between, *consts, *outs, c=c, bb=bb)


def _mixer1(source, ybuf, conv_w, gdn_par, w_alpha, b_alpha, norm_g, init, **group):
    consts = [_whole(a) for a in (conv_w, gdn_par, w_alpha, b_alpha, norm_g)]
    return _mixer_call("mixer1", (_mixer1_kernel, _mixer1_stream_kernel), source, N_PIECE1, consts, init, ybuf, **group)


_GROUP_META = dict(
    row_off=ROW_META - ROW_SAMPLE, y_row_off=ROW_META, nb=1, nchunk=1, c=N_META, bb=1, bcast_init=True,
    y_rows=R_ROWS - ROW_META,
)
_GROUP_PROMPT = dict(row_off=0, y_row_off=0, nb=BATCH, nchunk=SEQ // CHUNK, c=CHUNK, bb=4, bcast_init=True, stream=True)
_GROUP_SAMPLE = dict(row_off=0, y_row_off=ROW_SAMPLE, nb=DEC_BATCH, nchunk=1, c=DEC_SEQ, bb=16, bcast_init=False)


def _rows_to_batch_major_kernel(x_ref, o_ref):
    o_ref[...] = x_ref[...].reshape(o_ref.shape)


def _prompt_rows_to_batch_major(x):
    nchunk = SEQ // CHUNK
    out = pl.pallas_call(
        _rows_to_batch_major_kernel,
        grid=(nchunk,),
        in_specs=[pl.BlockSpec((BATCH * CHUNK, D_MODEL), lambda ci: (ci, 0))],
        out_specs=pl.BlockSpec((BATCH, None, CHUNK, D_MODEL), lambda ci: (0, ci, 0, 0)),
        out_shape=jax.ShapeDtypeStruct((BATCH, nchunk, CHUNK, D_MODEL), x.dtype),
        compiler_params=pltpu.CompilerParams(dimension_semantics=("parallel",)),
        name="rows_to_batch_major",
    )(x)
    return out.reshape(BATCH, SEQ, D_MODEL)


def _rotary_tables(pos):
    half = DH // 2
    inv = ROPE_BASE ** (-jnp.arange(half, dtype=_f32) / half)
    ang = pos.astype(_f32)[:, None] * inv[None, :]
    cos, sin = jnp.cos(ang), jnp.sin(ang)
    return jnp.concatenate([cos, cos], -1), jnp.concatenate([-sin, sin], -1)


def _lanes(m):
    return jnp.broadcast_to(m.astype(_f32)[..., None], m.shape + (LANES,))


def _conv_rows(s):
    return jnp.pad(s.astype(_f32), ((0, 0), (8 - (CONV_W - 1), 0), (0, 0)))


def kernel(x_prompt, x_sample, state_ret, state_mlstm_c, state_mlstm_n, state_mlstm_m, state_gdn, state_gdn_conv, state_gla, meta_tokens, w_in0, ret_norm_g, mlstm_gate_bias, mlstm_norm_g, w_out0, ln0_mix_g, ln0_mix_b, ffn0_w_gate, ffn0_w_up, ffn0_w_down, ln0_ffn_g, ln0_ffn_b, w_in1, gdn_conv_w, gdn_a_log, gdn_dt_bias, gdn_norm_g, gla_w_alpha, gla_b_alpha, gla_norm_g, w_out1, ln1_mix_g, ln1_mix_b, moe_w_router, moe_b_router, moe_w_gate, moe_w_up, moe_w_down, ln1_ffn_g, ln1_ffn_b):
    w4 = NH * DH
    nchunk = SEQ // CHUNK
    xp = x_prompt.reshape(BATCH, nchunk, CHUNK, D_MODEL).transpose(1, 0, 2, 3).reshape(N_PROMPT, D_MODEL)
    x = jnp.concatenate(
        [
            xp,
            x_sample.reshape(N_SAMPLE, D_MODEL),
            meta_tokens.astype(x_prompt.dtype),
            jnp.zeros((R_ROWS - ROW_META - N_META, D_MODEL), x_prompt.dtype),
        ],
        0,
    )

    w_in0_p = jnp.pad(w_in0.astype(_bf16), ((0, 0), (0, P0_PAD - w_in0.shape[1])))
    gb_row = jnp.pad(mlstm_gate_bias.astype(_f32), (0, LANES - 2 * NH))[None]
    norm0 = jnp.stack([ret_norm_g, mlstm_norm_g]).astype(_f32)
    norm1 = jnp.stack([gdn_norm_g, gla_norm_g]).astype(_f32)
    lo = GLA_RANK + NH
    gdn_par = jnp.stack(
        [
            jnp.pad(gdn_a_log.astype(_f32), (lo, LANES - lo - NH)),
            jnp.pad(gdn_dt_bias.astype(_f32), (lo, LANES - lo - NH)),
        ]
    )
    w_router_t = jnp.pad(moe_w_router.astype(_f32).T, ((0, ROUTE_ROWS - N_EXPERTS), (0, 0)))
    b_router = jnp.pad(moe_b_router.astype(_f32), (0, ROUTE_ROWS - N_EXPERTS), constant_values=-jnp.inf)
    b_router_col = jnp.broadcast_to(b_router[:, None], (ROUTE_ROWS, LANES))
    moe_w_flat = [w.reshape(-1, w.shape[-1]) for w in (moe_w_gate, moe_w_up, moe_w_down)]

    def row(v):
        return v.astype(_f32)[None]

    proj0 = _proj(x[ROW_SAMPLE:], w_in0_p, tn=P0_PAD // 3)
    zeros_even = (
        jnp.zeros((1, NH, DH, DH), _f32), jnp.zeros((1, NH, DH, DH), _f32),
        jnp.zeros((1, NH, DH), _f32), jnp.zeros((1, NH, LANES), _f32),
    )
    cos_m, sin_m = _rotary_tables(jnp.arange(N_META))
    cos_p, sin_p = _rotary_tables(N_META + jnp.arange(SEQ))
    cos_s, sin_s = _rotary_tables(PAST_LEN + jnp.arange(DEC_SEQ))
    ybuf = jnp.zeros((R_ROWS, 2 * w4), _bf16)
    ybuf, *meta_even = _mixer0(proj0, ybuf, cos_m, sin_m, gb_row, norm0, zeros_even, **_GROUP_META)
    steps = _GROUP_PROMPT["nb"] // _GROUP_PROMPT["bb"] * _GROUP_PROMPT["nchunk"]
    casts0 = [
        _side_cast(ffn0_w_gate, steps), _side_cast(ffn0_w_up, steps), _side_cast(ffn0_w_down, D_FF // 64),
        _side_cast(w_out0, steps), _side_cast(w_in1, steps, P1_PAD, _regroup_w_in1),
    ]
    ybuf, p_ret, p_mc, p_mn, p_mm, wg0, wu0, wd0, wo0, w_in1_p = _mixer0(
        (x, w_in0_p), ybuf, cos_p, sin_p, gb_row, norm0, meta_even, casts=casts0, **_GROUP_PROMPT)
    init_s = (state_ret.astype(_f32), state_mlstm_c.astype(_f32), state_mlstm_n.astype(_f32), _lanes(state_mlstm_m))
    ybuf, s_ret, s_mc, s_mn, s_mm = _mixer0(proj0, ybuf, cos_s, sin_s, gb_row, norm0, init_s, **_GROUP_SAMPLE)
    x = _out_ln(ybuf, wo0, x, row(ln0_mix_g), row(ln0_mix_b))
    x = _ffn_ln(x, wg0, wu0, wd0, row(ln0_ffn_g), row(ln0_ffn_b))

    proj1 = _proj(x[ROW_SAMPLE:], w_in1_p, tn=P1_PAD // 3)
    zeros_odd = (jnp.zeros((1, NH, DH, DH), _f32), jnp.zeros((1, 8, QKV_C), _f32), jnp.zeros((1, NH, DK_D, DH), _f32))
    m1_par = (gdn_conv_w.astype(_f32), gdn_par, gla_w_alpha.astype(_f32), row(gla_b_alpha), norm1)
    ybuf, *meta_odd = _mixer1(proj1, ybuf, *m1_par, zeros_odd, **_GROUP_META)
    casts1 = [_side_cast(w, steps) for w in moe_w_flat + [w_out1]]
    ybuf, p_gdn, p_conv, p_gla, *moe_w, wo1 = _mixer1((x, w_in1_p), ybuf, *m1_par, meta_odd, casts=casts1, **_GROUP_PROMPT)
    moe_w = [w.reshape(w32.shape) for w, w32 in zip(moe_w, (moe_w_gate, moe_w_up, moe_w_down))]
    init_s = (state_gdn.astype(_f32), _conv_rows(state_gdn_conv), state_gla.astype(_f32))
    ybuf, s_gdn, s_conv, s_gla = _mixer1(proj1, ybuf, *m1_par, init_s, **_GROUP_SAMPLE)
    x, gate_t, code_t, counts = _out_ln_route(ybuf, wo1, x, row(ln1_mix_g), row(ln1_mix_b), w_router_t, b_router_col)
    x = _moe_ln(x, gate_t, code_t, counts, *moe_w, row(ln1_ffn_g), row(ln1_ffn_b))

    y_prompt = _prompt_rows_to_batch_major(x)
    y_sample = x[ROW_SAMPLE:ROW_META].reshape(DEC_BATCH, DEC_SEQ, D_MODEL)
    tail = 8 - (CONV_W - 1)
    return (
        y_prompt, y_sample,
        p_ret, p_mc, p_mn, p_mm[..., 0], p_gdn, p_conv[:, tail:], p_gla,
        s_ret, s_mc, s_mn, s_mm[..., 0], s_gdn, s_conv[:, tail:], s_gla,
    )
```

```python
import functools

import jax
import jax.numpy as jnp
import numpy as np
from jax import lax
from jax.experimental import pallas as pl
from jax.experimental.pallas import tpu as pltpu

D_MODEL = 1024
BATCH = 8
SEQ = 2048
DEC_BATCH = 128
DEC_SEQ = 4
PAST_LEN = 16384
N_META = 16
CHUNK = 64
NH = 4
DH = 128
DK_D = 64
CONV_W = 4
GLA_RANK = 16
GLA_TAU = 16.0
D_FF = 2816
N_EXPERTS = 8
MOE_FF = 1408
ROPE_BASE = 10000.0
LN_EPS = 1e-5
NORM_EPS = 1e-6
DEPTH = 2
ALPHA = (2 * DEPTH) ** 0.25
QKV_C = 3 * NH * DH

LANES = 128
N_PROMPT = BATCH * SEQ
N_SAMPLE = DEC_BATCH * DEC_SEQ
ROW_SAMPLE = N_PROMPT
ROW_META = N_PROMPT + N_SAMPLE
TM_MOE = 448
MOE_BLOCK = 128
MOE_GROUP = 2
TM = TM_MOE * MOE_GROUP
R_ROWS = ((ROW_META + N_META + TM - 1) // TM) * TM

P0_PAD = 33 * LANES
P1_MAIN = 28 * LANES
P1_PAD = 30 * LANES
SOLVE_BLOCK = 16

VMEM_LIMIT = 56 * 1024 * 1024

_bf16 = jnp.bfloat16
_f32 = jnp.float32
_HI = lax.Precision.HIGHEST


def _dot_hi(a, b):
    return jnp.dot(a, b, preferred_element_type=_f32, precision=_HI)


def _bmm(a, b):
    return jnp.einsum("nmk,nkp->nmp", a.astype(_bf16), b.astype(_bf16), preferred_element_type=_f32)


def _bmm_nt(a, b):
    return jnp.einsum("nmk,npk->nmp", a.astype(_bf16), b.astype(_bf16), preferred_element_type=_f32)


def _bmm_tn(a, b):
    return jnp.einsum("nkm,nkp->nmp", a.astype(_bf16), b.astype(_bf16), preferred_element_type=_f32)


def _split2(a):
    hi = a.astype(_bf16)
    lo = (a - hi.astype(_f32)).astype(_bf16)
    return hi, lo


def _bmm_x3(a, b):
    ah, al = _split2(a)
    bh, bl = _split2(b)
    mm = functools.partial(jnp.einsum, "nmk,nkp->nmp", preferred_element_type=_f32)
    return mm(ah, bh) + (mm(ah, bl) + mm(al, bh))


def _sigmoid(x):
    return 1.0 / (1.0 + jnp.exp(-x))


def _silu(x):
    return x * _sigmoid(x)


def _softplus(x):
    return jnp.maximum(x, 0.0) + jnp.log1p(jnp.exp(-jnp.abs(x)))


def _group_norm(o, g_row, rms):
    if not rms:
        o = o - jnp.mean(o, axis=-1, keepdims=True)
    return o * lax.rsqrt(jnp.mean(o * o, axis=-1, keepdims=True) + NORM_EPS) * g_row


def _layer_norm(x, g_row, b_row):
    mu = jnp.mean(x, axis=-1, keepdims=True)
    xc = x - mu
    var = jnp.mean(xc * xc, axis=-1, keepdims=True)
    return xc * lax.rsqrt(var + LN_EPS) * g_row + b_row


def _tri3(c):
    t = lax.broadcasted_iota(jnp.int32, (1, c, c), 1)
    s = lax.broadcasted_iota(jnp.int32, (1, c, c), 2)
    return t, s


def _cumsum_col_row(x_col, x_row, c):
    t, s = _tri3(c)
    col = jnp.sum(jnp.where(t >= s, x_row, 0.0), axis=2, keepdims=True)
    row = jnp.sum(jnp.where(t <= s, x_col, 0.0), axis=1, keepdims=True)
    return col, row


def _proj_kernel(x_ref, w_ref, o_ref, tail_ref, xb_ref):
    j = pl.program_id(1)

    @pl.when(j == 0)
    def _():
        xb_ref[...] = x_ref[...].astype(_bf16)

    acc = jnp.dot(xb_ref[...], w_ref[...], preferred_element_type=_f32)
    o_ref[...] = acc.astype(o_ref.dtype)

    @pl.when(j == pl.num_programs(1) - 1)
    def _():
        tail_ref[...] = acc[:, acc.shape[1] - LANES :]


def _proj(x, w, tn):
    rows, k = x.shape
    n = w.shape[1]
    tm = min(TM, rows)
    return pl.pallas_call(
        _proj_kernel,
        grid=(rows // tm, n // tn),
        in_specs=[pl.BlockSpec((tm, k), lambda i, j: (i, 0)), pl.BlockSpec((k, tn), lambda i, j: (0, j))],
        out_specs=[pl.BlockSpec((tm, tn), lambda i, j: (i, j)), pl.BlockSpec((tm, LANES), lambda i, j: (i, 0))],
        out_shape=[jax.ShapeDtypeStruct((rows, n), _bf16), jax.ShapeDtypeStruct((rows, LANES), _f32)],
        scratch_shapes=[pltpu.VMEM((tm, k), _bf16)],
        compiler_params=pltpu.CompilerParams(dimension_semantics=("parallel", "arbitrary"), vmem_limit_bytes=VMEM_LIMIT),
        name="in_proj",
    )(x, w)


def _out_ln_kernel(y_ref, w_ref, x_ref, g_ref, b_ref, o_ref):
    h = jnp.dot(y_ref[...], w_ref[...], preferred_element_type=_f32)
    o_ref[...] = _layer_norm(ALPHA * x_ref[...] + h, g_ref[...], b_ref[...])


def _out_ln(y, w, x, g, b):
    rows, k = y.shape
    d = w.shape[1]
    return pl.pallas_call(
        _out_ln_kernel,
        grid=(rows // TM,),
        in_specs=[
            pl.BlockSpec((TM, k), lambda i: (i, 0)),
            pl.BlockSpec((k, d), lambda i: (0, 0)),
            pl.BlockSpec((TM, d), lambda i: (i, 0)),
            pl.BlockSpec((1, d), lambda i: (0, 0)),
            pl.BlockSpec((1, d), lambda i: (0, 0)),
        ],
        out_specs=pl.BlockSpec((TM, d), lambda i: (i, 0)),
        out_shape=jax.ShapeDtypeStruct((rows, d), _f32),
        compiler_params=pltpu.CompilerParams(dimension_semantics=("parallel",), vmem_limit_bytes=VMEM_LIMIT),
        name="out_proj_ln",
    )(y, w, x, g, b)


FF_SUB = (0, 512, 1024, 1408)


def _swiglu_partial(xb, wg_ref, wu_ref, wd_ref):
    acts = []
    for lo, hi in zip(FF_SUB[:-1], FF_SUB[1:]):
        hg = jnp.dot(xb, wg_ref[:, lo:hi], preferred_element_type=_f32)
        hu = jnp.dot(xb, wu_ref[:, lo:hi], preferred_element_type=_f32)
        acts.append((_silu(hg) * hu).astype(_bf16))
    return jnp.dot(jnp.concatenate(acts, axis=1), wd_ref[...], preferred_element_type=_f32)


def _ffn_ln_kernel(x_ref, wg_ref, wu_ref, wd_ref, g_ref, b_ref, o_ref, xb_ref, acc_ref):
    f = pl.program_id(1)
    last = pl.num_programs(1) - 1

    @pl.when(f == 0)
    def _():
        xb_ref[...] = x_ref[...].astype(_bf16)

    part = _swiglu_partial(xb_ref[...], wg_ref, wu_ref, wd_ref)

    @pl.when(f == 0)
    def _():
        acc_ref[...] = part

    @pl.when(jnp.logical_and(f > 0, f < last))
    def _():
        acc_ref[...] += part

    @pl.when(f == last)
    def _():
        o_ref[...] = _layer_norm(ALPHA * x_ref[...] + (acc_ref[...] + part), g_ref[...], b_ref[...])


def _ffn_ln(x, wg, wu, wd, g, b):
    rows, d = x.shape
    ff = wg.shape[1]
    tf = MOE_FF
    assert ff % tf == 0 and ff // tf >= 2
    return pl.pallas_call(
        _ffn_ln_kernel,
        grid=(rows // TM, ff // tf),
        in_specs=[
            pl.BlockSpec((TM, d), lambda i, f: (i, 0)),
            pl.BlockSpec((d, tf), lambda i, f: (0, f)),
            pl.BlockSpec((d, tf), lambda i, f: (0, f)),
            pl.BlockSpec((tf, d), lambda i, f: (f, 0)),
            pl.BlockSpec((1, d), lambda i, f: (0, 0)),
            pl.BlockSpec((1, d), lambda i, f: (0, 0)),
        ],
        out_specs=pl.BlockSpec((TM, d), lambda i, f: (i, 0)),
        out_shape=jax.ShapeDtypeStruct((rows, d), _f32),
        scratch_shapes=[pltpu.VMEM((TM, d), _bf16), pltpu.VMEM((TM, d), _f32)],
        compiler_params=pltpu.CompilerParams(dimension_semantics=("parallel", "arbitrary"), vmem_limit_bytes=VMEM_LIMIT),
        name="ffn_ln",
    )(x, wg, wu, wd, g, b)


ROUTE_ROWS = 16


def _route_tile(x, wr_t, bias_col, before):
    xh, xl = _split2(x)
    wh, wl = _split2(wr_t)
    nt = functools.partial(lax.dot_general, dimension_numbers=(((1,), (1,)), ((), ())), preferred_element_type=_f32)
    logits = nt(wh, xh) + (nt(wh, xl) + nt(wl, xh)) + bias_col
    row = lax.broadcasted_iota(jnp.int32, logits.shape, 0)
    ex = jnp.exp(logits - jnp.max(logits, axis=0, keepdims=True))
    probs = ex / jnp.sum(ex, axis=0, keepdims=True)
    p1 = jnp.max(probs, axis=0, keepdims=True)
    i1 = jnp.min(jnp.where(probs == p1, row, ROUTE_ROWS), axis=0, keepdims=True)
    rest = jnp.where(row == i1, -1.0, probs)
    p2 = jnp.max(rest, axis=0, keepdims=True)
    i2 = jnp.min(jnp.where(rest == p2, row, ROUTE_ROWS), axis=0, keepdims=True)
    tot = p1 + p2
    gate_t = jnp.where(row == i1, p1 / tot, 0.0) + jnp.where(row == i2, p2 / tot, 0.0)
    sel_t = jnp.where(row == i1, 1.0, jnp.where(row == i2, 1.0, 0.0))
    rank_t = jnp.dot(sel_t.astype(_bf16), before, preferred_element_type=_f32)
    code_t = jnp.where(sel_t > 0.0, rank_t, -1.0)
    return gate_t[:N_EXPERTS], code_t[:N_EXPERTS], jnp.sum(sel_t, axis=1, keepdims=True)


def _out_ln_route_kernel(y_ref, w_ref, x_ref, g_ref, b_ref, wr_ref, br_ref, before_ref, o_ref, gate_t_ref, code_t_ref, cnt_ref):
    h = jnp.dot(y_ref[...], w_ref[...], preferred_element_type=_f32)
    x_new = _layer_norm(ALPHA * x_ref[...] + h, g_ref[...], b_ref[...])
    o_ref[...] = x_new
    for i in range(MOE_GROUP):
        gate_t, code_t, cnt = _route_tile(x_new[i * TM_MOE : (i + 1) * TM_MOE], wr_ref[...], br_ref[:, 0:1], before_ref[...])
        gate_t_ref[i] = gate_t
        code_t_ref[i] = code_t
        cnt_ref[i] = jnp.broadcast_to(cnt, (ROUTE_ROWS, LANES)).astype(jnp.int32)


def _out_ln_route(y, w, x, g, b, wr_t, br_col):
    rows, k = y.shape
    d = w.shape[1]
    nt = rows // TM_MOE
    t = np.arange(TM_MOE)
    before = jnp.asarray(t[:, None] < t[None, :], _bf16)

    def tile_spec(shape):
        return pl.BlockSpec((MOE_GROUP,) + shape, lambda i: (i, 0, 0))

    x_new, gate_t, code_t, cnt = pl.pallas_call(
        _out_ln_route_kernel,
        grid=(rows // TM,),
        in_specs=[
            pl.BlockSpec((TM, k), lambda i: (i, 0)),
            pl.BlockSpec((k, d), lambda i: (0, 0)),
            pl.BlockSpec((TM, d), lambda i: (i, 0)),
            pl.BlockSpec((1, d), lambda i: (0, 0)),
            pl.BlockSpec((1, d), lambda i: (0, 0)),
            pl.BlockSpec((ROUTE_ROWS, d), lambda i: (0, 0)),
            pl.BlockSpec((ROUTE_ROWS, LANES), lambda i: (0, 0)),
            pl.BlockSpec((TM_MOE, TM_MOE), lambda i: (0, 0)),
        ],
        out_specs=[
            pl.BlockSpec((TM, d), lambda i: (i, 0)),
            tile_spec((N_EXPERTS, TM_MOE)),
            tile_spec((N_EXPERTS, TM_MOE)),
            tile_spec((ROUTE_ROWS, LANES)),
        ],
        out_shape=[
            jax.ShapeDtypeStruct((rows, d), _f32),
            jax.ShapeDtypeStruct((nt, N_EXPERTS, TM_MOE), _f32),
            jax.ShapeDtypeStruct((nt, N_EXPERTS, TM_MOE), _f32),
            jax.ShapeDtypeStruct((nt, ROUTE_ROWS, LANES), jnp.int32),
        ],
        compiler_params=pltpu.CompilerParams(dimension_semantics=("parallel",), vmem_limit_bytes=VMEM_LIMIT),
        name="out_proj_ln_route",
    )(y, w, x, g, b, wr_t, br_col, before)
    return x_new, gate_t, code_t, cnt[:, :N_EXPERTS, 0].reshape(-1)


def _moe_ln_kernel(cnt_ref, x_ref, gate_t_ref, code_t_ref, wg_ref, wu_ref, wd_ref, g_ref, b_ref, o_ref, xb_ref, acc_ref, first_ref):
    grp = pl.program_id(0)
    e = pl.program_id(1)

    @pl.when(e == 0)
    def _():
        xb_ref[...] = x_ref[...].astype(_bf16)
        acc_ref[...] = jnp.zeros_like(acc_ref)

    slot_col = lax.broadcasted_iota(jnp.int32, (MOE_BLOCK, 1), 0)
    counts = [cnt_ref[(grp * MOE_GROUP + i) * N_EXPERTS + e] for i in range(MOE_GROUP)]
    code_rows = [code_t_ref[i, pl.ds(e, 1), :] for i in range(MOE_GROUP)]
    gate_rows = [gate_t_ref[i, pl.ds(e, 1), :] for i in range(MOE_GROUP)]
    first_rows = pl.ds(pl.multiple_of(e * MOE_BLOCK, MOE_BLOCK), MOE_BLOCK)
    scatter_dims = (((0,), (0,)), ((), ()))

    def gather_rows(i, blk):
        hit = code_rows[i] == (slot_col + blk * MOE_BLOCK).astype(_f32)
        onehot = jnp.where(hit, 1.0, 0.0).astype(_bf16)
        xs = jnp.dot(onehot, xb_ref[i * TM_MOE : (i + 1) * TM_MOE, :], preferred_element_type=_f32).astype(_bf16)
        gate = jnp.sum(jnp.where(hit, gate_rows[i], 0.0), axis=1, keepdims=True)
        return onehot, xs, gate

    def first_pass(i):
        _, xs, gate = gather_rows(i, 0)
        first_ref[i, first_rows, :] = (_swiglu_partial(xs, wg_ref, wu_ref, wd_ref) * gate).astype(_bf16)

    def no_pass(i):
        first_ref[i, first_rows, :] = jnp.zeros((MOE_BLOCK, first_ref.shape[2]), _bf16)

    assert MOE_GROUP == 2
    has0 = counts[0] > 0
    has1 = counts[1] > 0

    @pl.when(jnp.logical_and(has0, has1))
    def _():
        _, xs0, gate0 = gather_rows(0, 0)
        _, xs1, gate1 = gather_rows(1, 0)
        out = _swiglu_partial(jnp.concatenate([xs0, xs1], axis=0), wg_ref, wu_ref, wd_ref)
        first_ref[0, first_rows, :] = (out[:MOE_BLOCK] * gate0).astype(_bf16)
        first_ref[1, first_rows, :] = (out[MOE_BLOCK:] * gate1).astype(_bf16)

    @pl.when(jnp.logical_and(has0, jnp.logical_not(has1)))
    def _():
        first_pass(0)
        no_pass(1)

    @pl.when(jnp.logical_and(has1, jnp.logical_not(has0)))
    def _():
        no_pass(0)
        first_pass(1)

    @pl.when(jnp.logical_and(jnp.logical_not(has0), jnp.logical_not(has1)))
    def _():
        no_pass(0)
        no_pass(1)

    for blk in range(1, -(-TM_MOE // MOE_BLOCK)):
        for i in range(MOE_GROUP):

            @pl.when(counts[i] > blk * MOE_BLOCK)
            def _():
                onehot, xs, gate = gather_rows(i, blk)
                out = (_swiglu_partial(xs, wg_ref, wu_ref, wd_ref) * gate).astype(_bf16)
                acc_ref[i * TM_MOE : (i + 1) * TM_MOE, :] += lax.dot_general(onehot, out, scatter_dims, preferred_element_type=_f32)

    @pl.when(e == pl.num_programs(1) - 1)
    def _():
        slot = lax.broadcasted_iota(jnp.int32, (1, MOE_BLOCK, 1), 1).astype(_f32)
        for i in range(MOE_GROUP):
            hit = code_t_ref[i][:, None, :] == slot
            onehot = jnp.where(hit, 1.0, 0.0).astype(_bf16).reshape(N_EXPERTS * MOE_BLOCK, TM_MOE)
            rows = slice(i * TM_MOE, (i + 1) * TM_MOE)
            moe = acc_ref[rows, :] + lax.dot_general(onehot, first_ref[i], scatter_dims, preferred_element_type=_f32)
            o_ref[rows, :] = _layer_norm(ALPHA * x_ref[rows, :] + moe, g_ref[...], b_ref[...])


def _moe_ln(x, gate_t, code_t, counts, wg, wu, wd, g, b):
    rows, d = x.shape
    ne, _, ff = wg.shape
    grid_spec = pltpu.PrefetchScalarGridSpec(
        num_scalar_prefetch=1,
        grid=(rows // TM, ne),
        in_specs=[
            pl.BlockSpec((TM, d), lambda i, e, cnt: (i, 0)),
            pl.BlockSpec((MOE_GROUP, N_EXPERTS, TM_MOE), lambda i, e, cnt: (i, 0, 0)),
            pl.BlockSpec((MOE_GROUP, N_EXPERTS, TM_MOE), lambda i, e, cnt: (i, 0, 0)),
            pl.BlockSpec((None, d, ff), lambda i, e, cnt: (e, 0, 0)),
            pl.BlockSpec((None, d, ff), lambda i, e, cnt: (e, 0, 0)),
            pl.BlockSpec((None, ff, d), lambda i, e, cnt: (e, 0, 0)),
            pl.BlockSpec((1, d), lambda i, e, cnt: (0, 0)),
            pl.BlockSpec((1, d), lambda i, e, cnt: (0, 0)),
        ],
        out_specs=pl.BlockSpec((TM, d), lambda i, e, cnt: (i, 0)),
        scratch_shapes=[
            pltpu.VMEM((TM, d), _bf16),
            pltpu.VMEM((TM, d), _f32),
            pltpu.VMEM((MOE_GROUP, ne * MOE_BLOCK, d), _bf16),
        ],
    )
    return pl.pallas_call(
        _moe_ln_kernel,
        grid_spec=grid_spec,
        out_shape=jax.ShapeDtypeStruct((rows, d), _f32),
        compiler_params=pltpu.CompilerParams(dimension_semantics=("parallel", "arbitrary"), vmem_limit_bytes=VMEM_LIMIT),
        name="moe_ln",
    )(counts, x, gate_t, code_t, wg, wu, wd, g, b)


def _chains(a, c, bb, width, off=0):
    return jnp.stack([a[j * c : (j + 1) * c, off + h * width : off + (h + 1) * width] for j in range(bb) for h in range(NH)])


def _chain_cols(a, c, bb, lane0):
    return jnp.stack([a[j * c : (j + 1) * c, lane0 + h : lane0 + h + 1] for j in range(bb) for h in range(NH)])


def _chain_rows(a_t, c, bb, lane0):
    return jnp.stack([a_t[lane0 + h : lane0 + h + 1, j * c : (j + 1) * c] for j in range(bb) for h in range(NH)])


def _head_rows(ref, row, bb):
    return jnp.stack([ref[row : row + 1, h * DH : (h + 1) * DH] for _ in range(bb) for h in range(NH)])


def _store_chains(y_ref, y, c, bb, off):
    for j in range(bb):
        for h in range(NH):
            y_ref[j * c : (j + 1) * c, off + h * DH : off + (h + 1) * DH] = y[j * NH + h].astype(y_ref.dtype)


PIECE = NH * DH


def _stream_projection(xn_ref, xc_ref, w_ref, main_sc, tail_sc, xb_sc, n_piece):
    rows = xn_ref.shape[0]
    ci = pl.program_id(1)
    tail_col = w_ref.shape[1] - LANES

    def project(xb, row0, p):
        if p < n_piece:
            cols = slice(p * PIECE, (p + 1) * PIECE)
            main_sc[pl.ds(row0, rows), cols] = jnp.dot(xb, w_ref[:, cols], preferred_element_type=_f32).astype(_bf16)
        else:
            tail_sc[pl.ds(row0, rows), :] = jnp.dot(xb, w_ref[:, tail_col:], preferred_element_type=_f32)

    @pl.when(ci == 0)
    def _():
        xb0 = xc_ref[...].astype(_bf16)
        for p in range(n_piece + 1):
            project(xb0, 0, p)

    cur = pl.multiple_of((ci % 2) * rows, rows)
    nxt = pl.multiple_of(((ci + 1) % 2) * rows, rows)
    xb_sc[...] = xn_ref[...].astype(_bf16)
    pending = iter(range(n_piece + 1))

    def between(count=1):
        for _ in range(count):
            p = next(pending, None)
            if p is not None:
                project(xb_sc[...], nxt, p)

    def load(p):
        return main_sc[pl.ds(cur, rows), p * PIECE : (p + 1) * PIECE].astype(_f32)

    return load, tail_sc[pl.ds(cur, rows), :], between


def _retention_chunk(q, k, v, s_prev, dec, vec):
    att = _bmm_nt(q, k) * dec
    o = _bmm(att, v) + vec[:, :, 0:1] * _bmm(q, s_prev)
    s_new = vec[:, 0:1, 2:3] * s_prev + _bmm_tn(k * vec[:, :, 1:2], v)
    return o, s_new


def _mlstm_chunk(q, k, v, it_col, it_row, lf_col, lf_row, c_prev, n_prev, m_prev, c):
    t, s = _tri3(c)
    b_col, b_row = _cumsum_col_row(lf_col, lf_row, c)
    logw = jnp.where(t >= s, b_col - b_row + it_row, -jnp.inf)
    m_t = jnp.maximum(b_col + m_prev, jnp.max(logw, axis=2, keepdims=True))
    w = jnp.exp(logw - m_t)
    carry = jnp.exp(b_col + m_prev - m_t)
    qk = _bmm_nt(q, k) * w
    num = _bmm(qk, v) + carry * _bmm(q, c_prev)
    den = jnp.sum(qk, axis=2, keepdims=True) + carry * jnp.sum(q * n_prev, axis=2, keepdims=True)
    h = num / jnp.maximum(jnp.abs(den), jnp.exp(-m_t))
    m_new = m_t[:, c - 1 : c, :]
    b_last = b_col[:, c - 1 : c, :]
    w_last = jnp.exp(b_last - b_col + it_col - m_new)
    decay = jnp.exp(b_last + m_prev - m_new)
    kw = k * w_last
    c_new = decay * c_prev + _bmm_tn(kw, v)
    n_new = decay * n_prev + jnp.sum(kw, axis=1, keepdims=True)
    return h, c_new, n_new, m_new


N_PIECE0 = 8


def _mixer0_body(
    load, gate_tail, between,
    cos_ref, sin_ref, dec_ref, vec_ref, gb_ref, ng_ref, s0_ref, c0_ref, n0_ref, m0_ref,
    y_ref, s_ref, c_ref, n_ref, m_ref, *, c, bb,
):
    n = bb * NH

    @pl.when(pl.program_id(1) == 0)
    def _():
        s_ref[...] = jnp.broadcast_to(s0_ref[...], s_ref.shape)
        c_ref[...] = jnp.broadcast_to(c0_ref[...], c_ref.shape)
        n_ref[...] = jnp.broadcast_to(n0_ref[...], n_ref.shape)
        m_ref[...] = jnp.broadcast_to(m0_ref[...], m_ref.shape)

    def chains(p):
        return _chains(load(p), c, bb, DH)

    cosf = cos_ref[...][None]
    sinf = sin_ref[...][None]
    gates = gate_tail + gb_ref[...]
    lane = lax.broadcasted_iota(jnp.int32, gates.shape, 1)
    gates = jnp.where(lane < NH, gates, -_softplus(-gates))
    gates_t = jnp.transpose(gates)

    q = chains(0)
    k = chains(1)
    q = q * cosf + pltpu.roll(q, DH // 2, axis=2) * sinf
    k = (k * cosf + pltpu.roll(k, DH // 2, axis=2) * sinf) * DH**-0.5
    between(2)
    dec = jnp.concatenate([dec_ref[...]] * bb, axis=0)
    vec = jnp.concatenate([vec_ref[...]] * bb, axis=0)
    o, s_new = _retention_chunk(q, k, chains(2), s_ref[...].reshape(n, DH, DH), dec, vec)
    between(2)
    s_ref[...] = s_new.reshape(s_ref.shape)
    y_a = _group_norm(o, _head_rows(ng_ref, 0, bb), rms=False) * _silu(chains(3))
    _store_chains(y_ref, y_a, c, bb, 0)
    between(1)

    n_prev = jnp.stack([n_ref[j, h : h + 1, :] for j in range(bb) for h in range(NH)])
    m_prev = jnp.stack([m_ref[j, h : h + 1, 0:1] for j in range(bb) for h in range(NH)])
    h_b, c_new, n_new, m_new = _mlstm_chunk(
        chains(4), chains(5) * DH**-0.5, chains(6),
        _chain_cols(gates, c, bb, 0), _chain_rows(gates_t, c, bb, 0),
        _chain_cols(gates, c, bb, NH), _chain_rows(gates_t, c, bb, NH),
        c_ref[...].reshape(n, DH, DH), n_prev, m_prev, c,
    )
    between(2)
    c_ref[...] = c_new.reshape(c_ref.shape)
    for j in range(bb):
        for h in range(NH):
            n_ref[j, h : h + 1, :] = n_new[j * NH + h]
            m_ref[j, h : h + 1, :] = jnp.broadcast_to(m_new[j * NH + h], (1, LANES))
    h_b = _sigmoid(chains(7)) * h_b
    between(1)
    y_b = _group_norm(h_b, _head_rows(ng_ref, 1, bb), rms=False)
    _store_chains(y_ref, y_b, c, bb, NH * DH)
    between(N_PIECE0 + 1)


def _no_op(count=1):
    del count


def _mixer0_kernel(*refs, c, bb):
    pieces, gate_ref, rest = refs[:N_PIECE0], refs[N_PIECE0], refs[N_PIECE0 + 1 :]
    consts, outs = rest[:10], rest[11:]
    _mixer0_body(lambda p: pieces[p][...].astype(_f32), gate_ref[...], _no_op, *consts, *outs, c=c, bb=bb)


def _cast_blocks(cast_in, cast_out, cast_fns):
    for src, dst, fn in zip(cast_in, cast_out, cast_fns):
        blk = src[...]
        dst[...] = (blk if fn is None else fn(blk)).astype(dst.dtype)


def _mixer0_stream_kernel(xn_ref, xc_ref, w_ref, *rest, c, bb, cast_fns=()):
    n_cast = len(cast_fns)
    consts, rest = rest[:10], rest[11:]
    cast_in, outs, cast_out, scratch = rest[:n_cast], rest[n_cast : n_cast + 5], rest[n_cast + 5 : 2 * n_cast + 5], rest[2 * n_cast + 5 :]
    _cast_blocks(cast_in, cast_out, cast_fns)
    load, gate_tail, between = _stream_projection(xn_ref, xc_ref, w_ref, *scratch, N_PIECE0)
    _mixer0_body(load, gate_tail, between, *consts, *outs, c=c, bb=bb)


def _retention_tables(c):
    lg = np.log(1.0 - 2.0 ** (-5.0 - np.arange(NH, dtype=np.float64)))[:, None, None]
    t = np.arange(c, dtype=np.float64)
    diff = t[None, :, None] - t[None, None, :]
    dec = np.where(diff >= 0, np.exp(np.maximum(diff, 0.0) * lg), 0.0)
    vec = np.zeros((NH, c, LANES))
    vec[:, :, 0] = np.exp((t[None, :] + 1.0) * lg[:, 0])
    vec[:, :, 1] = np.exp((c - 1.0 - t[None, :]) * lg[:, 0])
    vec[:, :, 2] = np.exp(c * lg[:, 0])
    return jnp.asarray(dec, _f32), jnp.asarray(vec, _f32)


def _whole(a):
    return a, pl.BlockSpec(a.shape, lambda i, ci: (0,) * a.ndim)


def _mixer_call(name, kernels, source, n_piece, consts, init, ybuf, *, row_off, y_row_off, nb, nchunk, c, bb, bcast_init, stream=False, casts=()):
    rows = bb * c
    nbb = nb // bb
    blk0 = row_off // rows
    ib = 1 if bcast_init else bb
    y_spec = pl.BlockSpec((rows, ybuf.shape[1]), lambda i, ci: (y_row_off // rows + ci * nbb + i, 0))

    def row_block(col):
        return lambda i, ci: (blk0 + ci * nbb + i, col)

    def batch_block(nd, lead):
        return lambda i, ci: ((i if lead else 0),) + (0,) * nd

    if stream:
        x, w = source
        src_args = [x, x, w]
        src_specs = [
            pl.BlockSpec((rows, D_MODEL), lambda i, ci: (blk0 + jnp.minimum(ci + 1, nchunk - 1) * nbb + i, 0)),
            pl.BlockSpec((rows, D_MODEL), lambda i, ci: (blk0 + i, 0)),
            pl.BlockSpec(w.shape, lambda i, ci: (0, 0)),
        ]
        scratch = [pltpu.VMEM((2 * rows, n_piece * PIECE), _bf16), pltpu.VMEM((2 * rows, LANES), _f32), pltpu.VMEM((rows, D_MODEL), _bf16)]
    else:
        proj, tail = source
        src_args = [proj] * n_piece + [tail]
        src_specs = [pl.BlockSpec((rows, PIECE), row_block(p)) for p in range(n_piece)] + [pl.BlockSpec((rows, LANES), row_block(0))]
        scratch = []
    in_specs = src_specs + [spec for _, spec in consts]
    in_specs += [pl.BlockSpec((ib,) + a.shape[1:], batch_block(a.ndim - 1, not bcast_init)) for a in init]
    out_specs = [y_spec] + [pl.BlockSpec((bb,) + a.shape[1:], batch_block(a.ndim - 1, True)) for a in init]
    out_shape = [jax.ShapeDtypeStruct(ybuf.shape, ybuf.dtype)] + [jax.ShapeDtypeStruct((nb,) + a.shape[1:], _f32) for a in init]
    args = src_args + [a for a, _ in consts] + list(init)
    cast_in_specs, cast_out_specs, cast_shapes = [], [], []
    for a, n_blocks, out_cols, _ in casts:
        blk_rows = a.shape[0] // n_blocks
        assert n_blocks <= nbb * nchunk and blk_rows * n_blocks == a.shape[0] and blk_rows % 16 == 0
        index = (lambda last: lambda i, ci: (jnp.minimum(i * nchunk + ci, last), 0))(n_blocks - 1)
        cast_in_specs.append(pl.BlockSpec((blk_rows, a.shape[1]), index))
        cast_out_specs.append(pl.BlockSpec((blk_rows, out_cols), index))
        cast_shapes.append(jax.ShapeDtypeStruct((a.shape[0], out_cols), _bf16))
    body = functools.partial(kernels[int(stream)], c=c, bb=bb, **({"cast_fns": tuple(fn for *_, fn in casts)} if casts else {}))
    return pl.pallas_call(
        body,
        grid=(nbb, nchunk),
        in_specs=in_specs + [pl.BlockSpec(memory_space=pl.ANY)] + cast_in_specs,
        out_specs=out_specs + cast_out_specs,
        out_shape=out_shape + cast_shapes,
        scratch_shapes=scratch,
        input_output_aliases={len(args): 0},
        compiler_params=pltpu.CompilerParams(dimension_semantics=("parallel", "arbitrary"), vmem_limit_bytes=VMEM_LIMIT),
        name=f"{name}_c{c}",
    )(*args, ybuf, *[a for a, *_ in casts])


C1_Z_END = QKV_C + NH * DH
C1_QD = C1_Z_END + 2 * NH
C1_LR = C1_QD + 2 * NH * DK_D + 2 * NH * DH


def _regroup_w_in1(blk):
    rows, cols = blk.shape
    n_small = cols - C1_LR + 2 * NH
    gaps = [jnp.zeros((rows, n), blk.dtype) for n in (P1_PAD - LANES - P1_MAIN, LANES - n_small)]
    return jnp.concatenate([blk[:, :C1_Z_END], blk[:, C1_QD:C1_LR], gaps[0], blk[:, C1_LR:], blk[:, C1_Z_END:C1_QD], gaps[1]], axis=1)


def _side_cast(a, n_blocks, out_cols=None, fn=None):
    return a, n_blocks, out_cols or a.shape[1], fn


def _mixer0(source, ybuf, cosf, sinf, gb_row, norm_g, init, **group):
    c = group["c"]
    dec, vec = _retention_tables(c)
    by_chunk = pl.BlockSpec((c, DH), lambda i, ci: (ci, 0))
    consts = [(cosf, by_chunk), (sinf, by_chunk), _whole(dec), _whole(vec), _whole(gb_row), _whole(norm_g)]
    return _mixer_call("mixer0", (_mixer0_kernel, _mixer0_stream_kernel), source, N_PIECE0, consts, init, ybuf, **group)


def _unit_lower_solve(a, rhs, c):
    bs = min(SOLVE_BLOCK, c)
    t, s = _tri3(c)
    if c > bs:
        shift = bs.bit_length() - 1
        same = jnp.right_shift(t, shift) == jnp.right_shift(s, shift)
        d = jnp.where(same, a, 0.0)
    else:
        d = a
    inv = jnp.where(t == s, 1.0, 0.0) - d
    p = d
    span = 2
    while span < bs:
        p = _bmm_x3(p, p)
        inv = inv + _bmm_x3(inv, p)
        span *= 2
    y = _bmm(inv, rhs)
    if c == bs:
        return y
    b = _bmm(inv, jnp.where(same, 0.0, a))
    y = y - _bmm(b, y)
    p = b
    span = 2
    while span < c // bs:
        p = _bmm(p, p)
        y = y + _bmm(p, y)
        span *= 2
    return y


def _gdn_chunk(q, k, v, beta_col, g_col, g_row, s_prev, c, between=_no_op):
    t, s = _tri3(c)
    gc_col, gc_row = _cumsum_col_row(g_col, g_row, c)
    dec_incl = jnp.exp(jnp.where(t >= s, gc_col - gc_row, -jnp.inf))
    dec_strict = jnp.where(t > s, dec_incl, 0.0)
    e_col = jnp.exp(gc_col)
    a = beta_col * _bmm_nt(k, k) * dec_strict
    rhs = jnp.concatenate([beta_col * v, (beta_col * e_col) * k], axis=-1)
    between(1)
    sol = _unit_lower_solve(a, rhs, c)
    between(1)
    u = sol[:, :, :DH] - _bmm(sol[:, :, DH:], s_prev)
    qk = _bmm_nt(q, k) * dec_incl
    between(1)
    o = e_col * _bmm(q, s_prev) + _bmm(qk, u)
    gl = gc_col[:, c - 1 : c, :]
    s_new = jnp.exp(gl) * s_prev + _bmm_tn(k * jnp.exp(gl - gc_col), u)
    return o, s_new


def _gla_chunk(q, k, v, bc, s_prev, c):
    t, s = _tri3(c)
    qe = q * jnp.exp(bc)
    att = jnp.where(t >= s, _bmm_nt(qe, k * jnp.exp(-bc)), 0.0)
    o = _bmm(att, v) + _bmm(qe, s_prev)
    bl = bc[:, c - 1 : c, :]
    ti, si = _tri3(DK_D)
    el_col = jnp.sum(jnp.where(ti == si, jnp.exp(bl), 0.0), axis=2, keepdims=True)
    s_new = el_col * s_prev + _bmm_tn(k * jnp.exp(bl - bc), v)
    return o, s_new


N_PIECE1 = 7


def _mixer1_body(
    load, small, between,
    cw_ref, gp_ref, wa_ref, ba_ref, ng_ref, s0_ref, cv0_ref, d0_ref,
    y_ref, s_ref, cv_ref, d_ref, *, c, bb,
):
    n = bb * NH
    w4 = NH * DH

    @pl.when(pl.program_id(1) == 0)
    def _():
        s_ref[...] = jnp.broadcast_to(s0_ref[...], s_ref.shape)
        cv_ref[...] = jnp.broadcast_to(cv0_ref[...], cv_ref.shape)
        d_ref[...] = jnp.broadcast_to(d0_ref[...], d_ref.shape)

    beta_all = _sigmoid(small)
    g_all = -jnp.exp(gp_ref[0:1, :]) * _softplus(small + gp_ref[1:2, :])
    g_all_t = jnp.transpose(g_all)
    log_alpha = -_softplus(-(_dot_hi(small[:, 0:GLA_RANK], wa_ref[...]) + ba_ref[...])) * (1.0 / GLA_TAU)
    cw = cw_ref[...]

    acts = []
    qkv = jnp.concatenate([load(0), load(1), load(2)], axis=1)
    between(1)
    for j in range(bb):
        ext = jnp.concatenate([cv_ref[j], qkv[j * c : (j + 1) * c, :]], axis=0)
        conv = cw[3:4] * ext[8 : 8 + c] + cw[2:3] * ext[7 : 7 + c] + cw[1:2] * ext[6 : 6 + c] + cw[0:1] * ext[5 : 5 + c]
        cv_ref[j] = ext[c : c + 8]
        acts.append(_silu(conv))

    def act_chains(off):
        return jnp.stack([acts[j][:, off + h * DH : off + (h + 1) * DH] for j in range(bb) for h in range(NH)])

    qc = act_chains(0)
    kc = act_chains(w4)
    qc = qc * lax.rsqrt(jnp.sum(qc * qc, axis=-1, keepdims=True) + NORM_EPS) * DH**-0.5
    kc = kc * lax.rsqrt(jnp.sum(kc * kc, axis=-1, keepdims=True) + NORM_EPS)
    between(1)
    o, s_new = _gdn_chunk(
        qc, kc, act_chains(2 * w4),
        _chain_cols(beta_all, c, bb, GLA_RANK), _chain_cols(g_all, c, bb, GLA_RANK + NH),
        _chain_rows(g_all_t, c, bb, GLA_RANK + NH), s_ref[...].reshape(n, DH, DH), c, between,
    )
    s_ref[...] = s_new.reshape(s_ref.shape)
    y_c = _group_norm(o, _head_rows(ng_ref, 0, bb), rms=True) * _silu(_chains(load(3), c, bb, DH))
    _store_chains(y_ref, y_c, c, bb, 0)
    between(1)

    tt = lax.broadcasted_iota(jnp.int32, (c, c), 0)
    ss = lax.broadcasted_iota(jnp.int32, (c, c), 1)
    ones_lt = jnp.where(tt >= ss, 1.0, 0.0).astype(_bf16)
    bcs = []
    for j in range(bb):
        la = log_alpha[j * c : (j + 1) * c, :]
        hi = la.astype(_bf16)
        r1 = la - hi.astype(_f32)
        mid = r1.astype(_bf16)
        lo = (r1 - mid.astype(_f32)).astype(_bf16)
        cum = functools.partial(jnp.dot, ones_lt, preferred_element_type=_f32)
        bcs.append(cum(hi) + (cum(mid) + cum(lo)))
    bc = jnp.stack([bcs[j][:, h * DK_D : (h + 1) * DK_D] for j in range(bb) for h in range(NH)])
    qkd = load(4)
    between(1)
    o, d_new = _gla_chunk(
        _chains(qkd, c, bb, DK_D) * DK_D**-0.5, _chains(qkd, c, bb, DK_D, off=NH * DK_D),
        _chains(load(5), c, bb, DH), bc, d_ref[...].reshape(n, DK_D, DH), c,
    )
    d_ref[...] = d_new.reshape(d_ref.shape)
    between(1)
    y_d = _group_norm(o, _head_rows(ng_ref, 1, bb), rms=False) * _silu(_chains(load(6), c, bb, DH))
    _store_chains(y_ref, y_d, c, bb, w4)
    between(N_PIECE1 + 1)


def _mixer1_kernel(*refs, c, bb):
    pieces, small_ref, rest = refs[:N_PIECE1], refs[N_PIECE1], refs[N_PIECE1 + 1 :]
    consts, outs = rest[:8], rest[9:]
    _mixer1_body(lambda p: pieces[p][...].astype(_f32), small_ref[...], _no_op, *consts, *outs, c=c, bb=bb)


def _mixer1_stream_kernel(xn_ref, xc_ref, w_ref, *rest, c, bb, cast_fns=()):
    n_cast = len(cast_fns)
    consts, rest = rest[:8], rest[9:]
    cast_in, outs, cast_out, scratch = rest[:n_cast], rest[n_cast : n_cast + 4], rest[n_cast + 4 : 2 * n_cast + 4], rest[2 * n_cast + 4 :]
    _cast_blocks(cast_in, cast_out, cast_fns)
    load, small, between = _stream_projection(xn_ref, xc_ref, w_ref, *scratch, N_PIECE1)
    _mixer1_body(load, small, between, *consts, *outs, c=c, bb=bb)


def _mixer1(source, ybuf, conv_w, gdn_par, w_alpha, b_alpha, norm_g, init, **group):
    consts = [_whole(a) for a in (conv_w, gdn_par, w_alpha, b_alpha, norm_g)]
    return _mixer_call("mixer1", (_mixer1_kernel, _mixer1_stream_kernel), source, N_PIECE1, consts, init, ybuf, **group)


_GROUP_META = dict(row_off=ROW_META - ROW_SAMPLE, y_row_off=ROW_META, nb=1, nchunk=1, c=N_META, bb=1, bcast_init=True)
_GROUP_PROMPT = dict(row_off=0, y_row_off=0, nb=BATCH, nchunk=SEQ // CHUNK, c=CHUNK, bb=4, bcast_init=True, stream=True)
_GROUP_SAMPLE = dict(row_off=0, y_row_off=ROW_SAMPLE, nb=DEC_BATCH, nchunk=1, c=DEC_SEQ, bb=16, bcast_init=False)


def _rows_to_batch_major_kernel(x_ref, o_ref):
    o_ref[...] = x_ref[...].reshape(o_ref.shape)


def _prompt_rows_to_batch_major(x):
    nchunk = SEQ // CHUNK
    out = pl.pallas_call(
        _rows_to_batch_major_kernel,
        grid=(nchunk,),
        in_specs=[pl.BlockSpec((BATCH * CHUNK, D_MODEL), lambda ci: (ci, 0))],
        out_specs=pl.BlockSpec((BATCH, None, CHUNK, D_MODEL), lambda ci: (0, ci, 0, 0)),
        out_shape=jax.ShapeDtypeStruct((BATCH, nchunk, CHUNK, D_MODEL), x.dtype),
        compiler_params=pltpu.CompilerParams(dimension_semantics=("parallel",)),
        name="rows_to_batch_major",
    )(x)
    return out.reshape(BATCH, SEQ, D_MODEL)


def _rotary_tables(pos):
    half = DH // 2
    inv = ROPE_BASE ** (-jnp.arange(half, dtype=_f32) / half)
    ang = pos.astype(_f32)[:, None] * inv[None, :]
    cos, sin = jnp.cos(ang), jnp.sin(ang)
    return jnp.concatenate([cos, cos], -1), jnp.concatenate([-sin, sin], -1)


def _lanes(m):
    return jnp.broadcast_to(m.astype(_f32)[..., None], m.shape + (LANES,))


def _conv_rows(s):
    return jnp.pad(s.astype(_f32), ((0, 0), (8 - (CONV_W - 1), 0), (0, 0)))


def kernel(x_prompt, x_sample, state_ret, state_mlstm_c, state_mlstm_n, state_mlstm_m, state_gdn, state_gdn_conv, state_gla, meta_tokens, w_in0, ret_norm_g, mlstm_gate_bias, mlstm_norm_g, w_out0, ln0_mix_g, ln0_mix_b, ffn0_w_gate, ffn0_w_up, ffn0_w_down, ln0_ffn_g, ln0_ffn_b, w_in1, gdn_conv_w, gdn_a_log, gdn_dt_bias, gdn_norm_g, gla_w_alpha, gla_b_alpha, gla_norm_g, w_out1, ln1_mix_g, ln1_mix_b, moe_w_router, moe_b_router, moe_w_gate, moe_w_up, moe_w_down, ln1_ffn_g, ln1_ffn_b):
    w4 = NH * DH
    nchunk = SEQ // CHUNK
    xp = x_prompt.reshape(BATCH, nchunk, CHUNK, D_MODEL).transpose(1, 0, 2, 3).reshape(N_PROMPT, D_MODEL)
    x = jnp.concatenate(
        [
            xp,
            x_sample.reshape(N_SAMPLE, D_MODEL),
            meta_tokens.astype(x_prompt.dtype),
            jnp.zeros((R_ROWS - ROW_META - N_META, D_MODEL), x_prompt.dtype),
        ],
        0,
    )

    w_in0_p = jnp.pad(w_in0.astype(_bf16), ((0, 0), (0, P0_PAD - w_in0.shape[1])))
    gb_row = jnp.pad(mlstm_gate_bias.astype(_f32), (0, LANES - 2 * NH))[None]
    norm0 = jnp.stack([ret_norm_g, mlstm_norm_g]).astype(_f32)
    norm1 = jnp.stack([gdn_norm_g, gla_norm_g]).astype(_f32)
    lo = GLA_RANK + NH
    gdn_par = jnp.stack(
        [
            jnp.pad(gdn_a_log.astype(_f32), (lo, LANES - lo - NH)),
            jnp.pad(gdn_dt_bias.astype(_f32), (lo, LANES - lo - NH)),
        ]
    )
    w_router_t = jnp.pad(moe_w_router.astype(_f32).T, ((0, ROUTE_ROWS - N_EXPERTS), (0, 0)))
    b_router = jnp.pad(moe_b_router.astype(_f32), (0, ROUTE_ROWS - N_EXPERTS), constant_values=-jnp.inf)
    b_router_col = jnp.broadcast_to(b_router[:, None], (ROUTE_ROWS, LANES))
    moe_w_flat = [w.reshape(-1, w.shape[-1]) for w in (moe_w_gate, moe_w_up, moe_w_down)]

    def row(v):
        return v.astype(_f32)[None]

    proj0 = _proj(x[ROW_SAMPLE:], w_in0_p, tn=P0_PAD // 3)
    zeros_even = (
        jnp.zeros((1, NH, DH, DH), _f32), jnp.zeros((1, NH, DH, DH), _f32),
        jnp.zeros((1, NH, DH), _f32), jnp.zeros((1, NH, LANES), _f32),
    )
    cos_m, sin_m = _rotary_tables(jnp.arange(N_META))
    cos_p, sin_p = _rotary_tables(N_META + jnp.arange(SEQ))
    cos_s, sin_s = _rotary_tables(PAST_LEN + jnp.arange(DEC_SEQ))
    ybuf = jnp.zeros((R_ROWS, 2 * w4), _bf16)
    ybuf, *meta_even = _mixer0(proj0, ybuf, cos_m, sin_m, gb_row, norm0, zeros_even, **_GROUP_META)
    steps = _GROUP_PROMPT["nb"] // _GROUP_PROMPT["bb"] * _GROUP_PROMPT["nchunk"]
    casts0 = [
        _side_cast(ffn0_w_gate, steps), _side_cast(ffn0_w_up, steps), _side_cast(ffn0_w_down, D_FF // 64),
        _side_cast(w_out0, steps), _side_cast(w_in1, steps, P1_PAD, _regroup_w_in1),
    ]
    ybuf, p_ret, p_mc, p_mn, p_mm, wg0, wu0, wd0, wo0, w_in1_p = _mixer0(
        (x, w_in0_p), ybuf, cos_p, sin_p, gb_row, norm0, meta_even, casts=casts0, **_GROUP_PROMPT)
    init_s = (state_ret.astype(_f32), state_mlstm_c.astype(_f32), state_mlstm_n.astype(_f32), _lanes(state_mlstm_m))
    ybuf, s_ret, s_mc, s_mn, s_mm = _mixer0(proj0, ybuf, cos_s, sin_s, gb_row, norm0, init_s, **_GROUP_SAMPLE)
    x = _out_ln(ybuf, wo0, x, row(ln0_mix_g), row(ln0_mix_b))
    x = _ffn_ln(x, wg0, wu0, wd0, row(ln0_ffn_g), row(ln0_ffn_b))

    proj1 = _proj(x[ROW_SAMPLE:], w_in1_p, tn=P1_PAD // 3)
    zeros_odd = (jnp.zeros((1, NH, DH, DH), _f32), jnp.zeros((1, 8, QKV_C), _f32), jnp.zeros((1, NH, DK_D, DH), _f32))
    m1_par = (gdn_conv_w.astype(_f32), gdn_par, gla_w_alpha.astype(_f32), row(gla_b_alpha), norm1)
    ybuf, *meta_odd = _mixer1(proj1, ybuf, *m1_par, zeros_odd, **_GROUP_META)
    casts1 = [_side_cast(w, steps) for w in moe_w_flat + [w_out1]]
    ybuf, p_gdn, p_conv, p_gla, *moe_w, wo1 = _mixer1((x, w_in1_p), ybuf, *m1_par, meta_odd, casts=casts1, **_GROUP_PROMPT)
    moe_w = [w.reshape(w32.shape) for w, w32 in zip(moe_w, (moe_w_gate, moe_w_up, moe_w_down))]
    init_s = (state_gdn.astype(_f32), _conv_rows(state_gdn_conv), state_gla.astype(_f32))
    ybuf, s_gdn, s_conv, s_gla = _mixer1(proj1, ybuf, *m1_par, init_s, **_GROUP_SAMPLE)
    x, gate_t, code_t, counts = _out_ln_route(ybuf, wo1, x, row(ln1_mix_g), row(ln1_mix_b), w_router_t, b_router_col)
    x = _moe_ln(x, gate_t, code_t, counts, *moe_w, row(ln1_ffn_g), row(ln1_ffn_b))

    y_prompt = _prompt_rows_to_batch_major(x)
    y_sample = x[ROW_SAMPLE:ROW_META].reshape(DEC_BATCH, DEC_SEQ, D_MODEL)
    tail = 8 - (CONV_W - 1)
    return (
        y_prompt, y_sample,
        p_ret, p_mc, p_mn, p_mm[..., 0], p_gdn, p_conv[:, tail:], p_gla,
        s_ret, s_mc, s_mn, s_mm[..., 0], s_gdn, s_conv[:, tail:], s_gla,
    )
```

```python
import functools

import jax
import jax.numpy as jnp
import numpy as np
from jax import lax
from jax.experimental import pallas as pl
from jax.experimental.pallas import tpu as pltpu

D_MODEL = 1024
BATCH = 8
SEQ = 2048
DEC_BATCH = 128
DEC_SEQ = 4
PAST_LEN = 16384
N_META = 16
CHUNK = 64
NH = 4
DH = 128
DK_D = 64
CONV_W = 4
GLA_RANK = 16
GLA_TAU = 16.0
D_FF = 2816
N_EXPERTS = 8
MOE_FF = 1408
ROPE_BASE = 10000.0
LN_EPS = 1e-5
NORM_EPS = 1e-6
DEPTH = 2
ALPHA = (2 * DEPTH) ** 0.25
QKV_C = 3 * NH * DH

LANES = 128
N_PROMPT = BATCH * SEQ
N_SAMPLE = DEC_BATCH * DEC_SEQ
ROW_SAMPLE = N_PROMPT
ROW_META = N_PROMPT + N_SAMPLE
TM_MOE = 448
MOE_BLOCK = 128
MOE_GROUP = 2
TM = TM_MOE * MOE_GROUP
R_ROWS = ((ROW_META + N_META + TM - 1) // TM) * TM

P0_PAD = 33 * LANES
P1_MAIN = 28 * LANES
P1_PAD = 30 * LANES
SOLVE_BLOCK = 16

VMEM_LIMIT = 56 * 1024 * 1024

_bf16 = jnp.bfloat16
_f32 = jnp.float32
_HI = lax.Precision.HIGHEST


def _dot_hi(a, b):
    return jnp.dot(a, b, preferred_element_type=_f32, precision=_HI)


def _bmm(a, b):
    return jnp.einsum("nmk,nkp->nmp", a.astype(_bf16), b.astype(_bf16), preferred_element_type=_f32)


def _bmm_nt(a, b):
    return jnp.einsum("nmk,npk->nmp", a.astype(_bf16), b.astype(_bf16), preferred_element_type=_f32)


def _bmm_tn(a, b):
    return jnp.einsum("nkm,nkp->nmp", a.astype(_bf16), b.astype(_bf16), preferred_element_type=_f32)


def _split2(a):
    hi = a.astype(_bf16)
    lo = (a - hi.astype(_f32)).astype(_bf16)
    return hi, lo


def _bmm_x3(a, b):
    ah, al = _split2(a)
    bh, bl = _split2(b)
    mm = functools.partial(jnp.einsum, "nmk,nkp->nmp", preferred_element_type=_f32)
    return mm(ah, bh) + (mm(ah, bl) + mm(al, bh))


def _sigmoid(x):
    return 1.0 / (1.0 + jnp.exp(-x))


def _silu(x):
    return x * _sigmoid(x)


def _softplus(x):
    return jnp.maximum(x, 0.0) + jnp.log1p(jnp.exp(-jnp.abs(x)))


def _group_norm(o, g_row, rms):
    if not rms:
        o = o - jnp.mean(o, axis=-1, keepdims=True)
    return o * lax.rsqrt(jnp.mean(o * o, axis=-1, keepdims=True) + NORM_EPS) * g_row


def _layer_norm(x, g_row, b_row):
    mu = jnp.mean(x, axis=-1, keepdims=True)
    xc = x - mu
    var = jnp.mean(xc * xc, axis=-1, keepdims=True)
    return xc * lax.rsqrt(var + LN_EPS) * g_row + b_row


def _tri3(c):
    t = lax.broadcasted_iota(jnp.int32, (1, c, c), 1)
    s = lax.broadcasted_iota(jnp.int32, (1, c, c), 2)
    return t, s


def _cumsum_col_row(x_col, x_row, c):
    t, s = _tri3(c)
    col = jnp.sum(jnp.where(t >= s, x_row, 0.0), axis=2, keepdims=True)
    row = jnp.sum(jnp.where(t <= s, x_col, 0.0), axis=1, keepdims=True)
    return col, row


def _proj_kernel(x_ref, w_ref, o_ref, tail_ref, xb_ref):
    j = pl.program_id(1)

    @pl.when(j == 0)
    def _():
        xb_ref[...] = x_ref[...].astype(_bf16)

    acc = jnp.dot(xb_ref[...], w_ref[...], preferred_element_type=_f32)
    o_ref[...] = acc.astype(o_ref.dtype)

    @pl.when(j == pl.num_programs(1) - 1)
    def _():
        tail_ref[...] = acc[:, acc.shape[1] - LANES :]


def _proj(x, w, tn):
    rows, k = x.shape
    n = w.shape[1]
    tm = min(TM, rows)
    return pl.pallas_call(
        _proj_kernel,
        grid=(rows // tm, n // tn),
        in_specs=[pl.BlockSpec((tm, k), lambda i, j: (i, 0)), pl.BlockSpec((k, tn), lambda i, j: (0, j))],
        out_specs=[pl.BlockSpec((tm, tn), lambda i, j: (i, j)), pl.BlockSpec((tm, LANES), lambda i, j: (i, 0))],
        out_shape=[jax.ShapeDtypeStruct((rows, n), _bf16), jax.ShapeDtypeStruct((rows, LANES), _f32)],
        scratch_shapes=[pltpu.VMEM((tm, k), _bf16)],
        compiler_params=pltpu.CompilerParams(dimension_semantics=("parallel", "arbitrary"), vmem_limit_bytes=VMEM_LIMIT),
        name="in_proj",
    )(x, w)


def _out_ln_kernel(y_ref, w_ref, x_ref, g_ref, b_ref, o_ref):
    h = jnp.dot(y_ref[...], w_ref[...], preferred_element_type=_f32)
    o_ref[...] = _layer_norm(ALPHA * x_ref[...] + h, g_ref[...], b_ref[...])


def _out_ln(y, w, x, g, b):
    rows, k = y.shape
    d = w.shape[1]
    return pl.pallas_call(
        _out_ln_kernel,
        grid=(rows // TM,),
        in_specs=[
            pl.BlockSpec((TM, k), lambda i: (i, 0)),
            pl.BlockSpec((k, d), lambda i: (0, 0)),
            pl.BlockSpec((TM, d), lambda i: (i, 0)),
            pl.BlockSpec((1, d), lambda i: (0, 0)),
            pl.BlockSpec((1, d), lambda i: (0, 0)),
        ],
        out_specs=pl.BlockSpec((TM, d), lambda i: (i, 0)),
        out_shape=jax.ShapeDtypeStruct((rows, d), _f32),
        compiler_params=pltpu.CompilerParams(dimension_semantics=("parallel",), vmem_limit_bytes=VMEM_LIMIT),
        name="out_proj_ln",
    )(y, w, x, g, b)


FF_SUB = (0, 512, 1024, 1408)


def _swiglu_partial(xb, wg_ref, wu_ref, wd_ref):
    acts = []
    for lo, hi in zip(FF_SUB[:-1], FF_SUB[1:]):
        hg = jnp.dot(xb, wg_ref[:, lo:hi], preferred_element_type=_f32)
        hu = jnp.dot(xb, wu_ref[:, lo:hi], preferred_element_type=_f32)
        acts.append((_silu(hg) * hu).astype(_bf16))
    return jnp.dot(jnp.concatenate(acts, axis=1), wd_ref[...], preferred_element_type=_f32)


def _ffn_ln_kernel(x_ref, wg_ref, wu_ref, wd_ref, g_ref, b_ref, o_ref, xb_ref, acc_ref):
    f = pl.program_id(1)

    @pl.when(f == 0)
    def _():
        xb_ref[...] = x_ref[...].astype(_bf16)
        acc_ref[...] = jnp.zeros_like(acc_ref)

    acc_ref[...] += _swiglu_partial(xb_ref[...], wg_ref, wu_ref, wd_ref)

    @pl.when(f == pl.num_programs(1) - 1)
    def _():
        o_ref[...] = _layer_norm(ALPHA * x_ref[...] + acc_ref[...], g_ref[...], b_ref[...])


def _ffn_ln(x, wg, wu, wd, g, b):
    rows, d = x.shape
    ff = wg.shape[1]
    tf = MOE_FF
    return pl.pallas_call(
        _ffn_ln_kernel,
        grid=(rows // TM, ff // tf),
        in_specs=[
            pl.BlockSpec((TM, d), lambda i, f: (i, 0)),
            pl.BlockSpec((d, tf), lambda i, f: (0, f)),
            pl.BlockSpec((d, tf), lambda i, f: (0, f)),
            pl.BlockSpec((tf, d), lambda i, f: (f, 0)),
            pl.BlockSpec((1, d), lambda i, f: (0, 0)),
            pl.BlockSpec((1, d), lambda i, f: (0, 0)),
        ],
        out_specs=pl.BlockSpec((TM, d), lambda i, f: (i, 0)),
        out_shape=jax.ShapeDtypeStruct((rows, d), _f32),
        scratch_shapes=[pltpu.VMEM((TM, d), _bf16), pltpu.VMEM((TM, d), _f32)],
        compiler_params=pltpu.CompilerParams(dimension_semantics=("parallel", "arbitrary"), vmem_limit_bytes=VMEM_LIMIT),
        name="ffn_ln",
    )(x, wg, wu, wd, g, b)


ROUTE_ROWS = 16


def _route_tile(x, wr_t, bias_col, before):
    xh, xl = _split2(x)
    wh, wl = _split2(wr_t)
    nt = functools.partial(lax.dot_general, dimension_numbers=(((1,), (1,)), ((), ())), preferred_element_type=_f32)
    logits = nt(wh, xh) + (nt(wh, xl) + nt(wl, xh)) + bias_col
    row = lax.broadcasted_iota(jnp.int32, logits.shape, 0)
    ex = jnp.exp(logits - jnp.max(logits, axis=0, keepdims=True))
    probs = ex / jnp.sum(ex, axis=0, keepdims=True)
    p1 = jnp.max(probs, axis=0, keepdims=True)
    i1 = jnp.min(jnp.where(probs == p1, row, ROUTE_ROWS), axis=0, keepdims=True)
    rest = jnp.where(row == i1, -1.0, probs)
    p2 = jnp.max(rest, axis=0, keepdims=True)
    i2 = jnp.min(jnp.where(rest == p2, row, ROUTE_ROWS), axis=0, keepdims=True)
    tot = p1 + p2
    gate_t = jnp.where(row == i1, p1 / tot, 0.0) + jnp.where(row == i2, p2 / tot, 0.0)
    sel_t = jnp.where(row == i1, 1.0, jnp.where(row == i2, 1.0, 0.0))
    rank_t = jnp.dot(sel_t.astype(_bf16), before, preferred_element_type=_f32)
    code_t = jnp.where(sel_t > 0.0, rank_t, -1.0)
    return gate_t[:N_EXPERTS], code_t[:N_EXPERTS], jnp.sum(sel_t, axis=1, keepdims=True)


def _out_ln_route_kernel(y_ref, w_ref, x_ref, g_ref, b_ref, wr_ref, br_ref, before_ref, o_ref, gate_t_ref, code_t_ref, cnt_ref):
    h = jnp.dot(y_ref[...], w_ref[...], preferred_element_type=_f32)
    x_new = _layer_norm(ALPHA * x_ref[...] + h, g_ref[...], b_ref[...])
    o_ref[...] = x_new
    for i in range(MOE_GROUP):
        gate_t, code_t, cnt = _route_tile(x_new[i * TM_MOE : (i + 1) * TM_MOE], wr_ref[...], br_ref[:, 0:1], before_ref[...])
        gate_t_ref[i] = gate_t
        code_t_ref[i] = code_t
        cnt_ref[i] = jnp.broadcast_to(cnt, (ROUTE_ROWS, LANES)).astype(jnp.int32)


def _out_ln_route(y, w, x, g, b, wr_t, br_col):
    rows, k = y.shape
    d = w.shape[1]
    nt = rows // TM_MOE
    t = np.arange(TM_MOE)
    before = jnp.asarray(t[:, None] < t[None, :], _bf16)

    def tile_spec(shape):
        return pl.BlockSpec((MOE_GROUP,) + shape, lambda i: (i, 0, 0))

    x_new, gate_t, code_t, cnt = pl.pallas_call(
        _out_ln_route_kernel,
        grid=(rows // TM,),
        in_specs=[
            pl.BlockSpec((TM, k), lambda i: (i, 0)),
            pl.BlockSpec((k, d), lambda i: (0, 0)),
            pl.BlockSpec((TM, d), lambda i: (i, 0)),
            pl.BlockSpec((1, d), lambda i: (0, 0)),
            pl.BlockSpec((1, d), lambda i: (0, 0)),
            pl.BlockSpec((ROUTE_ROWS, d), lambda i: (0, 0)),
            pl.BlockSpec((ROUTE_ROWS, LANES), lambda i: (0, 0)),
            pl.BlockSpec((TM_MOE, TM_MOE), lambda i: (0, 0)),
        ],
        out_specs=[
            pl.BlockSpec((TM, d), lambda i: (i, 0)),
            tile_spec((N_EXPERTS, TM_MOE)),
            tile_spec((N_EXPERTS, TM_MOE)),
            tile_spec((ROUTE_ROWS, LANES)),
        ],
        out_shape=[
            jax.ShapeDtypeStruct((rows, d), _f32),
            jax.ShapeDtypeStruct((nt, N_EXPERTS, TM_MOE), _f32),
            jax.ShapeDtypeStruct((nt, N_EXPERTS, TM_MOE), _f32),
            jax.ShapeDtypeStruct((nt, ROUTE_ROWS, LANES), jnp.int32),
        ],
        compiler_params=pltpu.CompilerParams(dimension_semantics=("parallel",), vmem_limit_bytes=VMEM_LIMIT),
        name="out_proj_ln_route",
    )(y, w, x, g, b, wr_t, br_col, before)
    return x_new, gate_t, code_t, cnt[:, :N_EXPERTS, 0].reshape(-1)


def _moe_ln_kernel(cnt_ref, x_ref, gate_t_ref, code_t_ref, wg_ref, wu_ref, wd_ref, g_ref, b_ref, o_ref, xb_ref, acc_ref, first_ref):
    grp = pl.program_id(0)
    e = pl.program_id(1)

    @pl.when(e == 0)
    def _():
        xb_ref[...] = x_ref[...].astype(_bf16)
        acc_ref[...] = jnp.zeros_like(acc_ref)

    slot_col = lax.broadcasted_iota(jnp.int32, (MOE_BLOCK, 1), 0)
    counts = [cnt_ref[(grp * MOE_GROUP + i) * N_EXPERTS + e] for i in range(MOE_GROUP)]
    code_rows = [code_t_ref[i, pl.ds(e, 1), :] for i in range(MOE_GROUP)]
    gate_rows = [gate_t_ref[i, pl.ds(e, 1), :] for i in range(MOE_GROUP)]
    first_rows = pl.ds(pl.multiple_of(e * MOE_BLOCK, MOE_BLOCK), MOE_BLOCK)
    scatter_dims = (((0,), (0,)), ((), ()))

    def gather_rows(i, blk):
        hit = code_rows[i] == (slot_col + blk * MOE_BLOCK).astype(_f32)
        onehot = jnp.where(hit, 1.0, 0.0).astype(_bf16)
        xs = jnp.dot(onehot, xb_ref[i * TM_MOE : (i + 1) * TM_MOE, :], preferred_element_type=_f32).astype(_bf16)
        gate = jnp.sum(jnp.where(hit, gate_rows[i], 0.0), axis=1, keepdims=True)
        return onehot, xs, gate

    def first_pass(i):
        _, xs, gate = gather_rows(i, 0)
        first_ref[i, first_rows, :] = (_swiglu_partial(xs, wg_ref, wu_ref, wd_ref) * gate).astype(_bf16)

    def no_pass(i):
        first_ref[i, first_rows, :] = jnp.zeros((MOE_BLOCK, first_ref.shape[2]), _bf16)

    assert MOE_GROUP == 2
    has0 = counts[0] > 0
    has1 = counts[1] > 0

    @pl.when(jnp.logical_and(has0, has1))
    def _():
        _, xs0, gate0 = gather_rows(0, 0)
        _, xs1, gate1 = gather_rows(1, 0)
        out = _swiglu_partial(jnp.concatenate([xs0, xs1], axis=0), wg_ref, wu_ref, wd_ref)
        first_ref[0, first_rows, :] = (out[:MOE_BLOCK] * gate0).astype(_bf16)
        first_ref[1, first_rows, :] = (out[MOE_BLOCK:] * gate1).astype(_bf16)

    @pl.when(jnp.logical_and(has0, jnp.logical_not(has1)))
    def _():
        first_pass(0)
        no_pass(1)

    @pl.when(jnp.logical_and(has1, jnp.logical_not(has0)))
    def _():
        no_pass(0)
        first_pass(1)

    @pl.when(jnp.logical_and(jnp.logical_not(has0), jnp.logical_not(has1)))
    def _():
        no_pass(0)
        no_pass(1)

    for blk in range(1, -(-TM_MOE // MOE_BLOCK)):
        for i in range(MOE_GROUP):

            @pl.when(counts[i] > blk * MOE_BLOCK)
            def _():
                onehot, xs, gate = gather_rows(i, blk)
                out = (_swiglu_partial(xs, wg_ref, wu_ref, wd_ref) * gate).astype(_bf16)
                acc_ref[i * TM_MOE : (i + 1) * TM_MOE, :] += lax.dot_general(onehot, out, scatter_dims, preferred_element_type=_f32)

    @pl.when(e == pl.num_programs(1) - 1)
    def _():
        slot = lax.broadcasted_iota(jnp.int32, (1, MOE_BLOCK, 1), 1).astype(_f32)
        for i in range(MOE_GROUP):
            hit = code_t_ref[i][:, None, :] == slot
            onehot = jnp.where(hit, 1.0, 0.0).astype(_bf16).reshape(N_EXPERTS * MOE_BLOCK, TM_MOE)
            rows = slice(i * TM_MOE, (i + 1) * TM_MOE)
            moe = acc_ref[rows, :] + lax.dot_general(onehot, first_ref[i], scatter_dims, preferred_element_type=_f32)
            o_ref[rows, :] = _layer_norm(ALPHA * x_ref[rows, :] + moe, g_ref[...], b_ref[...])


def _moe_ln(x, gate_t, code_t, counts, wg, wu, wd, g, b):
    rows, d = x.shape
    ne, _, ff = wg.shape
    grid_spec = pltpu.PrefetchScalarGridSpec(
        num_scalar_prefetch=1,
        grid=(rows // TM, ne),
        in_specs=[
            pl.BlockSpec((TM, d), lambda i, e, cnt: (i, 0)),
            pl.BlockSpec((MOE_GROUP, N_EXPERTS, TM_MOE), lambda i, e, cnt: (i, 0, 0)),
            pl.BlockSpec((MOE_GROUP, N_EXPERTS, TM_MOE), lambda i, e, cnt: (i, 0, 0)),
            pl.BlockSpec((None, d, ff), lambda i, e, cnt: (e, 0, 0)),
            pl.BlockSpec((None, d, ff), lambda i, e, cnt: (e, 0, 0)),
            pl.BlockSpec((None, ff, d), lambda i, e, cnt: (e, 0, 0)),
            pl.BlockSpec((1, d), lambda i, e, cnt: (0, 0)),
            pl.BlockSpec((1, d), lambda i, e, cnt: (0, 0)),
        ],
        out_specs=pl.BlockSpec((TM, d), lambda i, e, cnt: (i, 0)),
        scratch_shapes=[
            pltpu.VMEM((TM, d), _bf16),
            pltpu.VMEM((TM, d), _f32),
            pltpu.VMEM((MOE_GROUP, ne * MOE_BLOCK, d), _bf16),
        ],
    )
    return pl.pallas_call(
        _moe_ln_kernel,
        grid_spec=grid_spec,
        out_shape=jax.ShapeDtypeStruct((rows, d), _f32),
        compiler_params=pltpu.CompilerParams(dimension_semantics=("parallel", "arbitrary"), vmem_limit_bytes=VMEM_LIMIT),
        name="moe_ln",
    )(counts, x, gate_t, code_t, wg, wu, wd, g, b)


def _chains(a, c, bb, width, off=0):
    return jnp.stack([a[j * c : (j + 1) * c, off + h * width : off + (h + 1) * width] for j in range(bb) for h in range(NH)])


def _chain_cols(a, c, bb, lane0):
    return jnp.stack([a[j * c : (j + 1) * c, lane0 + h : lane0 + h + 1] for j in range(bb) for h in range(NH)])


def _chain_rows(a_t, c, bb, lane0):
    return jnp.stack([a_t[lane0 + h : lane0 + h + 1, j * c : (j + 1) * c] for j in range(bb) for h in range(NH)])


def _head_rows(ref, row, bb):
    return jnp.stack([ref[row : row + 1, h * DH : (h + 1) * DH] for _ in range(bb) for h in range(NH)])


def _store_chains(y_ref, y, c, bb, off):
    for j in range(bb):
        for h in range(NH):
            y_ref[j * c : (j + 1) * c, off + h * DH : off + (h + 1) * DH] = y[j * NH + h].astype(y_ref.dtype)


PIECE = NH * DH


def _stream_projection(xn_ref, xc_ref, w_ref, main_sc, tail_sc, xb_sc, n_piece):
    rows = xn_ref.shape[0]
    ci = pl.program_id(1)
    tail_col = w_ref.shape[1] - LANES

    def project(xb, row0, p):
        if p < n_piece:
            cols = slice(p * PIECE, (p + 1) * PIECE)
            main_sc[pl.ds(row0, rows), cols] = jnp.dot(xb, w_ref[:, cols], preferred_element_type=_f32).astype(_bf16)
        else:
            tail_sc[pl.ds(row0, rows), :] = jnp.dot(xb, w_ref[:, tail_col:], preferred_element_type=_f32)

    @pl.when(ci == 0)
    def _():
        xb0 = xc_ref[...].astype(_bf16)
        for p in range(n_piece + 1):
            project(xb0, 0, p)

    cur = pl.multiple_of((ci % 2) * rows, rows)
    nxt = pl.multiple_of(((ci + 1) % 2) * rows, rows)
    xb_sc[...] = xn_ref[...].astype(_bf16)
    pending = iter(range(n_piece + 1))

    def between(count=1):
        for _ in range(count):
            p = next(pending, None)
            if p is not None:
                project(xb_sc[...], nxt, p)

    def load(p):
        return main_sc[pl.ds(cur, rows), p * PIECE : (p + 1) * PIECE].astype(_f32)

    return load, tail_sc[pl.ds(cur, rows), :], between


def _retention_chunk(q, k, v, s_prev, dec, vec):
    att = _bmm_nt(q, k) * dec
    o = _bmm(att, v) + vec[:, :, 0:1] * _bmm(q, s_prev)
    s_new = vec[:, 0:1, 2:3] * s_prev + _bmm_tn(k * vec[:, :, 1:2], v)
    return o, s_new


def _mlstm_chunk(q, k, v, it_col, it_row, lf_col, lf_row, c_prev, n_prev, m_prev, c):
    t, s = _tri3(c)
    b_col, b_row = _cumsum_col_row(lf_col, lf_row, c)
    logw = jnp.where(t >= s, b_col - b_row + it_row, -jnp.inf)
    m_t = jnp.maximum(b_col + m_prev, jnp.max(logw, axis=2, keepdims=True))
    w = jnp.exp(logw - m_t)
    carry = jnp.exp(b_col + m_prev - m_t)
    qk = _bmm_nt(q, k) * w
    num = _bmm(qk, v) + carry * _bmm(q, c_prev)
    den = jnp.sum(qk, axis=2, keepdims=True) + carry * jnp.sum(q * n_prev, axis=2, keepdims=True)
    h = num / jnp.maximum(jnp.abs(den), jnp.exp(-m_t))
    m_new = m_t[:, c - 1 : c, :]
    b_last = b_col[:, c - 1 : c, :]
    w_last = jnp.exp(b_last - b_col + it_col - m_new)
    decay = jnp.exp(b_last + m_prev - m_new)
    kw = k * w_last
    c_new = decay * c_prev + _bmm_tn(kw, v)
    n_new = decay * n_prev + jnp.sum(kw, axis=1, keepdims=True)
    return h, c_new, n_new, m_new


N_PIECE0 = 8


def _mixer0_body(
    load, gate_tail, between,
    cos_ref, sin_ref, dec_ref, vec_ref, gb_ref, ng_ref, s0_ref, c0_ref, n0_ref, m0_ref,
    y_ref, s_ref, c_ref, n_ref, m_ref, *, c, bb,
):
    n = bb * NH

    @pl.when(pl.program_id(1) == 0)
    def _():
        s_ref[...] = jnp.broadcast_to(s0_ref[...], s_ref.shape)
        c_ref[...] = jnp.broadcast_to(c0_ref[...], c_ref.shape)
        n_ref[...] = jnp.broadcast_to(n0_ref[...], n_ref.shape)
        m_ref[...] = jnp.broadcast_to(m0_ref[...], m_ref.shape)

    def chains(p):
        return _chains(load(p), c, bb, DH)

    cosf = cos_ref[...][None]
    sinf = sin_ref[...][None]
    gates = gate_tail + gb_ref[...]
    lane = lax.broadcasted_iota(jnp.int32, gates.shape, 1)
    gates = jnp.where(lane < NH, gates, -_softplus(-gates))
    gates_t = jnp.transpose(gates)

    q = chains(0)
    k = chains(1)
    q = q * cosf + pltpu.roll(q, DH // 2, axis=2) * sinf
    k = (k * cosf + pltpu.roll(k, DH // 2, axis=2) * sinf) * DH**-0.5
    between(2)
    dec = jnp.concatenate([dec_ref[...]] * bb, axis=0)
    vec = jnp.concatenate([vec_ref[...]] * bb, axis=0)
    o, s_new = _retention_chunk(q, k, chains(2), s_ref[...].reshape(n, DH, DH), dec, vec)
    between(2)
    s_ref[...] = s_new.reshape(s_ref.shape)
    y_a = _group_norm(o, _head_rows(ng_ref, 0, bb), rms=False) * _silu(chains(3))
    _store_chains(y_ref, y_a, c, bb, 0)
    between(1)

    n_prev = jnp.stack([n_ref[j, h : h + 1, :] for j in range(bb) for h in range(NH)])
    m_prev = jnp.stack([m_ref[j, h : h + 1, 0:1] for j in range(bb) for h in range(NH)])
    h_b, c_new, n_new, m_new = _mlstm_chunk(
        chains(4), chains(5) * DH**-0.5, chains(6),
        _chain_cols(gates, c, bb, 0), _chain_rows(gates_t, c, bb, 0),
        _chain_cols(gates, c, bb, NH), _chain_rows(gates_t, c, bb, NH),
        c_ref[...].reshape(n, DH, DH), n_prev, m_prev, c,
    )
    between(2)
    c_ref[...] = c_new.reshape(c_ref.shape)
    for j in range(bb):
        for h in range(NH):
            n_ref[j, h : h + 1, :] = n_new[j * NH + h]
            m_ref[j, h : h + 1, :] = jnp.broadcast_to(m_new[j * NH + h], (1, LANES))
    h_b = _sigmoid(chains(7)) * h_b
    between(1)
    y_b = _group_norm(h_b, _head_rows(ng_ref, 1, bb), rms=False)
    _store_chains(y_ref, y_b, c, bb, NH * DH)
    between(N_PIECE0 + 1)


def _no_op(count=1):
    del count


def _mixer0_kernel(*refs, c, bb):
    pieces, gate_ref, rest = refs[:N_PIECE0], refs[N_PIECE0], refs[N_PIECE0 + 1 :]
    consts, outs = rest[:10], rest[11:]
    _mixer0_body(lambda p: pieces[p][...].astype(_f32), gate_ref[...], _no_op, *consts, *outs, c=c, bb=bb)


def _cast_blocks(cast_in, cast_out, cast_fns):
    for src, dst, fn in zip(cast_in, cast_out, cast_fns):
        blk = src[...]
        dst[...] = (blk if fn is None else fn(blk)).astype(dst.dtype)


def _mixer0_stream_kernel(xn_ref, xc_ref, w_ref, *rest, c, bb, cast_fns=()):
    n_cast = len(cast_fns)
    consts, rest = rest[:10], rest[11:]
    cast_in, outs, cast_out, scratch = rest[:n_cast], rest[n_cast : n_cast + 5], rest[n_cast + 5 : 2 * n_cast + 5], rest[2 * n_cast + 5 :]
    _cast_blocks(cast_in, cast_out, cast_fns)
    load, gate_tail, between = _stream_projection(xn_ref, xc_ref, w_ref, *scratch, N_PIECE0)
    _mixer0_body(load, gate_tail, between, *consts, *outs, c=c, bb=bb)


def _retention_tables(c):
    lg = np.log(1.0 - 2.0 ** (-5.0 - np.arange(NH, dtype=np.float64)))[:, None, None]
    t = np.arange(c, dtype=np.float64)
    diff = t[None, :, None] - t[None, None, :]
    dec = np.where(diff >= 0, np.exp(np.maximum(diff, 0.0) * lg), 0.0)
    vec = np.zeros((NH, c, LANES))
    vec[:, :, 0] = np.exp((t[None, :] + 1.0) * lg[:, 0])
    vec[:, :, 1] = np.exp((c - 1.0 - t[None, :]) * lg[:, 0])
    vec[:, :, 2] = np.exp(c * lg[:, 0])
    return jnp.asarray(dec, _f32), jnp.asarray(vec, _f32)


def _whole(a):
    return a, pl.BlockSpec(a.shape, lambda i, ci: (0,) * a.ndim)


def _mixer_call(name, kernels, source, n_piece, consts, init, ybuf, *, row_off, y_row_off, nb, nchunk, c, bb, bcast_init, stream=False, casts=()):
    rows = bb * c
    nbb = nb // bb
    blk0 = row_off // rows
    ib = 1 if bcast_init else bb
    y_spec = pl.BlockSpec((rows, ybuf.shape[1]), lambda i, ci: (y_row_off // rows + ci * nbb + i, 0))

    def row_block(col):
        return lambda i, ci: (blk0 + ci * nbb + i, col)

    def batch_block(nd, lead):
        return lambda i, ci: ((i if lead else 0),) + (0,) * nd

    if stream:
        x, w = source
        src_args = [x, x, w]
        src_specs = [
            pl.BlockSpec((rows, D_MODEL), lambda i, ci: (blk0 + jnp.minimum(ci + 1, nchunk - 1) * nbb + i, 0)),
            pl.BlockSpec((rows, D_MODEL), lambda i, ci: (blk0 + i, 0)),
            pl.BlockSpec(w.shape, lambda i, ci: (0, 0)),
        ]
        scratch = [pltpu.VMEM((2 * rows, n_piece * PIECE), _bf16), pltpu.VMEM((2 * rows, LANES), _f32), pltpu.VMEM((rows, D_MODEL), _bf16)]
    else:
        proj, tail = source
        src_args = [proj] * n_piece + [tail]
        src_specs = [pl.BlockSpec((rows, PIECE), row_block(p)) for p in range(n_piece)] + [pl.BlockSpec((rows, LANES), row_block(0))]
        scratch = []
    in_specs = src_specs + [spec for _, spec in consts]
    in_specs += [pl.BlockSpec((ib,) + a.shape[1:], batch_block(a.ndim - 1, not bcast_init)) for a in init]
    out_specs = [y_spec] + [pl.BlockSpec((bb,) + a.shape[1:], batch_block(a.ndim - 1, True)) for a in init]
    out_shape = [jax.ShapeDtypeStruct(ybuf.shape, ybuf.dtype)] + [jax.ShapeDtypeStruct((nb,) + a.shape[1:], _f32) for a in init]
    args = src_args + [a for a, _ in consts] + list(init)
    cast_in_specs, cast_out_specs, cast_shapes = [], [], []
    for a, n_blocks, out_cols, _ in casts:
        blk_rows = a.shape[0] // n_blocks
        assert n_blocks <= nbb * nchunk and blk_rows * n_blocks == a.shape[0] and blk_rows % 16 == 0
        index = (lambda last: lambda i, ci: (jnp.minimum(i * nchunk + ci, last), 0))(n_blocks - 1)
        cast_in_specs.append(pl.BlockSpec((blk_rows, a.shape[1]), index))
        cast_out_specs.append(pl.BlockSpec((blk_rows, out_cols), index))
        cast_shapes.append(jax.ShapeDtypeStruct((a.shape[0], out_cols), _bf16))
    body = functools.partial(kernels[int(stream)], c=c, bb=bb, **({"cast_fns": tuple(fn for *_, fn in casts)} if casts else {}))
    return pl.pallas_call(
        body,
        grid=(nbb, nchunk),
        in_specs=in_specs + [pl.BlockSpec(memory_space=pl.ANY)] + cast_in_specs,
        out_specs=out_specs + cast_out_specs,
        out_shape=out_shape + cast_shapes,
        scratch_shapes=scratch,
        input_output_aliases={len(args): 0},
        compiler_params=pltpu.CompilerParams(dimension_semantics=("parallel", "arbitrary"), vmem_limit_bytes=VMEM_LIMIT),
        name=f"{name}_c{c}",
    )(*args, ybuf, *[a for a, *_ in casts])


C1_Z_END = QKV_C + NH * DH
C1_QD = C1_Z_END + 2 * NH
C1_LR = C1_QD + 2 * NH * DK_D + 2 * NH * DH


def _regroup_w_in1(blk):
    rows, cols = blk.shape
    n_small = cols - C1_LR + 2 * NH
    gaps = [jnp.zeros((rows, n), blk.dtype) for n in (P1_PAD - LANES - P1_MAIN, LANES - n_small)]
    return jnp.concatenate([blk[:, :C1_Z_END], blk[:, C1_QD:C1_LR], gaps[0], blk[:, C1_LR:], blk[:, C1_Z_END:C1_QD], gaps[1]], axis=1)


def _side_cast(a, n_blocks, out_cols=None, fn=None):
    return a, n_blocks, out_cols or a.shape[1], fn


def _mixer0(source, ybuf, cosf, sinf, gb_row, norm_g, init, **group):
    c = group["c"]
    dec, vec = _retention_tables(c)
    by_chunk = pl.BlockSpec((c, DH), lambda i, ci: (ci, 0))
    consts = [(cosf, by_chunk), (sinf, by_chunk), _whole(dec), _whole(vec), _whole(gb_row), _whole(norm_g)]
    return _mixer_call("mixer0", (_mixer0_kernel, _mixer0_stream_kernel), source, N_PIECE0, consts, init, ybuf, **group)


def _unit_lower_solve(a, rhs, c):
    bs = min(SOLVE_BLOCK, c)
    t, s = _tri3(c)
    if c > bs:
        shift = bs.bit_length() - 1
        same = jnp.right_shift(t, shift) == jnp.right_shift(s, shift)
        d = jnp.where(same, a, 0.0)
    else:
        d = a
    inv = jnp.where(t == s, 1.0, 0.0) - d
    p = d
    span = 2
    while span < bs:
        p = _bmm_x3(p, p)
        inv = inv + _bmm_x3(inv, p)
        span *= 2
    y = _bmm(inv, rhs)
    if c == bs:
        return y
    b = _bmm(inv, jnp.where(same, 0.0, a))
    y = y - _bmm(b, y)
    p = b
    span = 2
    while span < c // bs:
        p = _bmm(p, p)
        y = y + _bmm(p, y)
        span *= 2
    return y


def _gdn_chunk(q, k, v, beta_col, g_col, g_row, s_prev, c, between=_no_op):
    t, s = _tri3(c)
    gc_col, gc_row = _cumsum_col_row(g_col, g_row, c)
    dec_incl = jnp.exp(jnp.where(t >= s, gc_col - gc_row, -jnp.inf))
    dec_strict = jnp.where(t > s, dec_incl, 0.0)
    e_col = jnp.exp(gc_col)
    a = beta_col * _bmm_nt(k, k) * dec_strict
    rhs = jnp.concatenate([beta_col * v, (beta_col * e_col) * k], axis=-1)
    between(1)
    sol = _unit_lower_solve(a, rhs, c)
    between(1)
    u = sol[:, :, :DH] - _bmm(sol[:, :, DH:], s_prev)
    qk = _bmm_nt(q, k) * dec_incl
    between(1)
    o = e_col * _bmm(q, s_prev) + _bmm(qk, u)
    gl = gc_col[:, c - 1 : c, :]
    s_new = jnp.exp(gl) * s_prev + _bmm_tn(k * jnp.exp(gl - gc_col), u)
    return o, s_new


GLA_SUB = 8


def _gla_chunk(q, k, v, bc, s_prev, c):
    t, s = _tri3(c)
    sub = min(GLA_SUB, c)
    atts = []
    for lo in range(0, c, sub):
        hi = lo + sub
        ref = bc[:, lo - 1 : lo, :] if lo else jnp.zeros_like(bc[:, 0:1, :])
        att = _bmm_nt(q[:, lo:hi] * jnp.exp(bc[:, lo:hi] - ref), k[:, :hi] * jnp.exp(ref - bc[:, :hi]))
        atts.append(att if hi == c else jnp.concatenate([att, jnp.zeros((att.shape[0], sub, c - hi), _f32)], axis=2))
    att = jnp.where(t >= s, jnp.concatenate(atts, axis=1), 0.0)
    o = _bmm(att, v) + _bmm(q * jnp.exp(bc), s_prev)
    bl = bc[:, c - 1 : c, :]
    ti, si = _tri3(DK_D)
    el_col = jnp.sum(jnp.where(ti == si, jnp.exp(bl), 0.0), axis=2, keepdims=True)
    s_new = el_col * s_prev + _bmm_tn(k * jnp.exp(bl - bc), v)
    return o, s_new


N_PIECE1 = 7


def _mixer1_body(
    load, small, between,
    cw_ref, gp_ref, wa_ref, ba_ref, ng_ref, s0_ref, cv0_ref, d0_ref,
    y_ref, s_ref, cv_ref, d_ref, *, c, bb,
):
    n = bb * NH
    w4 = NH * DH

    @pl.when(pl.program_id(1) == 0)
    def _():
        s_ref[...] = jnp.broadcast_to(s0_ref[...], s_ref.shape)
        cv_ref[...] = jnp.broadcast_to(cv0_ref[...], cv_ref.shape)
        d_ref[...] = jnp.broadcast_to(d0_ref[...], d_ref.shape)

    beta_all = _sigmoid(small)
    g_all = -jnp.exp(gp_ref[0:1, :]) * _softplus(small + gp_ref[1:2, :])
    g_all_t = jnp.transpose(g_all)
    log_alpha = -_softplus(-(_dot_hi(small[:, 0:GLA_RANK], wa_ref[...]) + ba_ref[...])) * (1.0 / GLA_TAU)
    cw = cw_ref[...]

    acts = []
    qkv = jnp.concatenate([load(0), load(1), load(2)], axis=1)
    between(1)
    for j in range(bb):
        ext = jnp.concatenate([cv_ref[j], qkv[j * c : (j + 1) * c, :]], axis=0)
        conv = cw[3:4] * ext[8 : 8 + c] + cw[2:3] * ext[7 : 7 + c] + cw[1:2] * ext[6 : 6 + c] + cw[0:1] * ext[5 : 5 + c]
        cv_ref[j] = ext[c : c + 8]
        acts.append(_silu(conv))

    def act_chains(off):
        return jnp.stack([acts[j][:, off + h * DH : off + (h + 1) * DH] for j in range(bb) for h in range(NH)])

    qc = act_chains(0)
    kc = act_chains(w4)
    qc = qc * lax.rsqrt(jnp.sum(qc * qc, axis=-1, keepdims=True) + NORM_EPS) * DH**-0.5
    kc = kc * lax.rsqrt(jnp.sum(kc * kc, axis=-1, keepdims=True) + NORM_EPS)
    between(1)
    o, s_new = _gdn_chunk(
        qc, kc, act_chains(2 * w4),
        _chain_cols(beta_all, c, bb, GLA_RANK), _chain_cols(g_all, c, bb, GLA_RANK + NH),
        _chain_rows(g_all_t, c, bb, GLA_RANK + NH), s_ref[...].reshape(n, DH, DH), c, between,
    )
    s_ref[...] = s_new.reshape(s_ref.shape)
    y_c = _group_norm(o, _head_rows(ng_ref, 0, bb), rms=True) * _silu(_chains(load(3), c, bb, DH))
    _store_chains(y_ref, y_c, c, bb, 0)
    between(1)

    tt = lax.broadcasted_iota(jnp.int32, (c, c), 0)
    ss = lax.broadcasted_iota(jnp.int32, (c, c), 1)
    ones_lt = jnp.where(tt >= ss, 1.0, 0.0).astype(_bf16)
    bcs = []
    for j in range(bb):
        la = log_alpha[j * c : (j + 1) * c, :]
        hi = la.astype(_bf16)
        r1 = la - hi.astype(_f32)
        mid = r1.astype(_bf16)
        lo = (r1 - mid.astype(_f32)).astype(_bf16)
        cum = functools.partial(jnp.dot, ones_lt, preferred_element_type=_f32)
        bcs.append(cum(hi) + (cum(mid) + cum(lo)))
    bc = jnp.stack([bcs[j][:, h * DK_D : (h + 1) * DK_D] for j in range(bb) for h in range(NH)])
    qkd = load(4)
    between(1)
    o, d_new = _gla_chunk(
        _chains(qkd, c, bb, DK_D) * DK_D**-0.5, _chains(qkd, c, bb, DK_D, off=NH * DK_D),
        _chains(load(5), c, bb, DH), bc, d_ref[...].reshape(n, DK_D, DH), c,
    )
    d_ref[...] = d_new.reshape(d_ref.shape)
    between(1)
    y_d = _group_norm(o, _head_rows(ng_ref, 1, bb), rms=False) * _silu(_chains(load(6), c, bb, DH))
    _store_chains(y_ref, y_d, c, bb, w4)
    between(N_PIECE1 + 1)


def _mixer1_kernel(*refs, c, bb):
    pieces, small_ref, rest = refs[:N_PIECE1], refs[N_PIECE1], refs[N_PIECE1 + 1 :]
    consts, outs = rest[:8], rest[9:]
    _mixer1_body(lambda p: pieces[p][...].astype(_f32), small_ref[...], _no_op, *consts, *outs, c=c, bb=bb)


def _mixer1_stream_kernel(xn_ref, xc_ref, w_ref, *rest, c, bb, cast_fns=()):
    n_cast = len(cast_fns)
    consts, rest = rest[:8], rest[9:]
    cast_in, outs, cast_out, scratch = rest[:n_cast], rest[n_cast : n_cast + 4], rest[n_cast + 4 : 2 * n_cast + 4], rest[2 * n_cast + 4 :]
    _cast_blocks(cast_in, cast_out, cast_fns)
    load, small, between = _stream_projection(xn_ref, xc_ref, w_ref, *scratch, N_PIECE1)
    _mixer1_body(load, small, between, *consts, *outs, c=c, bb=bb)


def _mixer1(source, ybuf, conv_w, gdn_par, w_alpha, b_alpha, norm_g, init, **group):
    consts = [_whole(a) for a in (conv_w, gdn_par, w_alpha, b_alpha, norm_g)]
    return _mixer_call("mixer1", (_mixer1_kernel, _mixer1_stream_kernel), source, N_PIECE1, consts, init, ybuf, **group)


_GROUP_META = dict(row_off=ROW_META - ROW_SAMPLE, y_row_off=ROW_META, nb=1, nchunk=1, c=N_META, bb=1, bcast_init=True)
_GROUP_PROMPT = dict(row_off=0, y_row_off=0, nb=BATCH, nchunk=SEQ // CHUNK, c=CHUNK, bb=4, bcast_init=True, stream=True)
_GROUP_SAMPLE = dict(row_off=0, y_row_off=ROW_SAMPLE, nb=DEC_BATCH, nchunk=1, c=DEC_SEQ, bb=16, bcast_init=False)


def _rows_to_batch_major_kernel(x_ref, o_ref):
    o_ref[...] = x_ref[...].reshape(o_ref.shape)


def _prompt_rows_to_batch_major(x):
    nchunk = SEQ // CHUNK
    out = pl.pallas_call(
        _rows_to_batch_major_kernel,
        grid=(nchunk,),
        in_specs=[pl.BlockSpec((BATCH * CHUNK, D_MODEL), lambda ci: (ci, 0))],
        out_specs=pl.BlockSpec((BATCH, None, CHUNK, D_MODEL), lambda ci: (0, ci, 0, 0)),
        out_shape=jax.ShapeDtypeStruct((BATCH, nchunk, CHUNK, D_MODEL), x.dtype),
        compiler_params=pltpu.CompilerParams(dimension_semantics=("parallel",)),
        name="rows_to_batch_major",
    )(x)
    return out.reshape(BATCH, SEQ, D_MODEL)


def _rotary_tables(pos):
    half = DH // 2
    inv = ROPE_BASE ** (-jnp.arange(half, dtype=_f32) / half)
    ang = pos.astype(_f32)[:, None] * inv[None, :]
    cos, sin = jnp.cos(ang), jnp.sin(ang)
    return jnp.concatenate([cos, cos], -1), jnp.concatenate([-sin, sin], -1)


def _lanes(m):
    return jnp.broadcast_to(m.astype(_f32)[..., None], m.shape + (LANES,))


def _conv_rows(s):
    return jnp.pad(s.astype(_f32), ((0, 0), (8 - (CONV_W - 1), 0), (0, 0)))


def kernel(x_prompt, x_sample, state_ret, state_mlstm_c, state_mlstm_n, state_mlstm_m, state_gdn, state_gdn_conv, state_gla, meta_tokens, w_in0, ret_norm_g, mlstm_gate_bias, mlstm_norm_g, w_out0, ln0_mix_g, ln0_mix_b, ffn0_w_gate, ffn0_w_up, ffn0_w_down, ln0_ffn_g, ln0_ffn_b, w_in1, gdn_conv_w, gdn_a_log, gdn_dt_bias, gdn_norm_g, gla_w_alpha, gla_b_alpha, gla_norm_g, w_out1, ln1_mix_g, ln1_mix_b, moe_w_router, moe_b_router, moe_w_gate, moe_w_up, moe_w_down, ln1_ffn_g, ln1_ffn_b):
    w4 = NH * DH
    nchunk = SEQ // CHUNK
    xp = x_prompt.reshape(BATCH, nchunk, CHUNK, D_MODEL).transpose(1, 0, 2, 3).reshape(N_PROMPT, D_MODEL)
    x = jnp.concatenate(
        [
            xp,
            x_sample.reshape(N_SAMPLE, D_MODEL),
            meta_tokens.astype(x_prompt.dtype),
            jnp.zeros((R_ROWS - ROW_META - N_META, D_MODEL), x_prompt.dtype),
        ],
        0,
    )

    w_in0_p = jnp.pad(w_in0.astype(_bf16), ((0, 0), (0, P0_PAD - w_in0.shape[1])))
    gb_row = jnp.pad(mlstm_gate_bias.astype(_f32), (0, LANES - 2 * NH))[None]
    norm0 = jnp.stack([ret_norm_g, mlstm_norm_g]).astype(_f32)
    norm1 = jnp.stack([gdn_norm_g, gla_norm_g]).astype(_f32)
    lo = GLA_RANK + NH
    gdn_par = jnp.stack(
        [
            jnp.pad(gdn_a_log.astype(_f32), (lo, LANES - lo - NH)),
            jnp.pad(gdn_dt_bias.astype(_f32), (lo, LANES - lo - NH)),
        ]
    )
    w_router_t = jnp.pad(moe_w_router.astype(_f32).T, ((0, ROUTE_ROWS - N_EXPERTS), (0, 0)))
    b_router = jnp.pad(moe_b_router.astype(_f32), (0, ROUTE_ROWS - N_EXPERTS), constant_values=-jnp.inf)
    b_router_col = jnp.broadcast_to(b_router[:, None], (ROUTE_ROWS, LANES))
    moe_w_flat = [w.reshape(-1, w.shape[-1]) for w in (moe_w_gate, moe_w_up, moe_w_down)]

    def row(v):
        return v.astype(_f32)[None]

    proj0 = _proj(x[ROW_SAMPLE:], w_in0_p, tn=P0_PAD // 3)
    zeros_even = (
        jnp.zeros((1, NH, DH, DH), _f32), jnp.zeros((1, NH, DH, DH), _f32),
        jnp.zeros((1, NH, DH), _f32), jnp.zeros((1, NH, LANES), _f32),
    )
    cos_m, sin_m = _rotary_tables(jnp.arange(N_META))
    cos_p, sin_p = _rotary_tables(N_META + jnp.arange(SEQ))
    cos_s, sin_s = _rotary_tables(PAST_LEN + jnp.arange(DEC_SEQ))
    ybuf = jnp.zeros((R_ROWS, 2 * w4), _bf16)
    ybuf, *meta_even = _mixer0(proj0, ybuf, cos_m, sin_m, gb_row, norm0, zeros_even, **_GROUP_META)
    steps = _GROUP_PROMPT["nb"] // _GROUP_PROMPT["bb"] * _GROUP_PROMPT["nchunk"]
    casts0 = [
        _side_cast(ffn0_w_gate, steps), _side_cast(ffn0_w_up, steps), _side_cast(ffn0_w_down, D_FF // 64),
        _side_cast(w_out0, steps), _side_cast(w_in1, steps, P1_PAD, _regroup_w_in1),
    ]
    ybuf, p_ret, p_mc, p_mn, p_mm, wg0, wu0, wd0, wo0, w_in1_p = _mixer0(
        (x, w_in0_p), ybuf, cos_p, sin_p, gb_row, norm0, meta_even, casts=casts0, **_GROUP_PROMPT)
    init_s = (state_ret.astype(_f32), state_mlstm_c.astype(_f32), state_mlstm_n.astype(_f32), _lanes(state_mlstm_m))
    ybuf, s_ret, s_mc, s_mn, s_mm = _mixer0(proj0, ybuf, cos_s, sin_s, gb_row, norm0, init_s, **_GROUP_SAMPLE)
    x = _out_ln(ybuf, wo0, x, row(ln0_mix_g), row(ln0_mix_b))
    x = _ffn_ln(x, wg0, wu0, wd0, row(ln0_ffn_g), row(ln0_ffn_b))

    proj1 = _proj(x[ROW_SAMPLE:], w_in1_p, tn=P1_PAD // 3)
    zeros_odd = (jnp.zeros((1, NH, DH, DH), _f32), jnp.zeros((1, 8, QKV_C), _f32), jnp.zeros((1, NH, DK_D, DH), _f32))
    m1_par = (gdn_conv_w.astype(_f32), gdn_par, gla_w_alpha.astype(_f32), row(gla_b_alpha), norm1)
    ybuf, *meta_odd = _mixer1(proj1, ybuf, *m1_par, zeros_odd, **_GROUP_META)
    casts1 = [_side_cast(w, steps) for w in moe_w_flat + [w_out1]]
    ybuf, p_gdn, p_conv, p_gla, *moe_w, wo1 = _mixer1((x, w_in1_p), ybuf, *m1_par, meta_odd, casts=casts1, **_GROUP_PROMPT)
    moe_w = [w.reshape(w32.shape) for w, w32 in zip(moe_w, (moe_w_gate, moe_w_up, moe_w_down))]
    init_s = (state_gdn.astype(_f32), _conv_rows(state_gdn_conv), state_gla.astype(_f32))
    ybuf, s_gdn, s_conv, s_gla = _mixer1(proj1, ybuf, *m1_par, init_s, **_GROUP_SAMPLE)
    x, gate_t, code_t, counts = _out_ln_route(ybuf, wo1, x, row(ln1_mix_g), row(ln1_mix_b), w_router_t, b_router_col)
    x = _moe_ln(x, gate_t, code_t, counts, *moe_w, row(ln1_ffn_g), row(ln1_ffn_b))

    y_prompt = _prompt_rows_to_batch_major(x)
    y_sample = x[ROW_SAMPLE:ROW_META].reshape(DEC_BATCH, DEC_SEQ, D_MODEL)
    tail = 8 - (CONV_W - 1)
    return (
        y_prompt, y_sample,
        p_ret, p_mc, p_mn, p_mm[..., 0], p_gdn, p_conv[:, tail:], p_gla,
        s_ret, s_mc, s_mn, s_mm[..., 0], s_gdn, s_conv[:, tail:], s_gla,
    )
```

```python
import functools

import jax
import jax.numpy as jnp
import numpy as np
from jax import lax
from jax.experimental import pallas as pl
from jax.experimental.pallas import tpu as pltpu

D_MODEL = 1024
BATCH = 8
SEQ = 2048
DEC_BATCH = 128
DEC_SEQ = 4
PAST_LEN = 16384
N_META = 16
CHUNK = 64
NH = 4
DH = 128
DK_D = 64
CONV_W = 4
GLA_RANK = 16
GLA_TAU = 16.0
D_FF = 2816
N_EXPERTS = 8
MOE_FF = 1408
ROPE_BASE = 10000.0
LN_EPS = 1e-5
NORM_EPS = 1e-6
DEPTH = 2
ALPHA = (2 * DEPTH) ** 0.25
QKV_C = 3 * NH * DH

LANES = 128
N_PROMPT = BATCH * SEQ
N_SAMPLE = DEC_BATCH * DEC_SEQ
ROW_SAMPLE = N_PROMPT
ROW_META = N_PROMPT + N_SAMPLE
TM_MOE = 448
MOE_BLOCK = 128
MOE_GROUP = 2
TM = TM_MOE * MOE_GROUP
R_ROWS = ((ROW_META + N_META + TM - 1) // TM) * TM

P0_PAD = 33 * LANES
P1_MAIN = 28 * LANES
P1_PAD = 30 * LANES
SOLVE_BLOCK = 16

VMEM_LIMIT = 56 * 1024 * 1024

_bf16 = jnp.bfloat16
_f32 = jnp.float32
_HI = lax.Precision.HIGHEST


def _dot_hi(a, b):
    return jnp.dot(a, b, preferred_element_type=_f32, precision=_HI)


def _bmm(a, b):
    return jnp.einsum("nmk,nkp->nmp", a.astype(_bf16), b.astype(_bf16), preferred_element_type=_f32)


def _bmm_nt(a, b):
    return jnp.einsum("nmk,npk->nmp", a.astype(_bf16), b.astype(_bf16), preferred_element_type=_f32)


def _bmm_tn(a, b):
    return jnp.einsum("nkm,nkp->nmp", a.astype(_bf16), b.astype(_bf16), preferred_element_type=_f32)


def _split2(a):
    hi = a.astype(_bf16)
    lo = (a - hi.astype(_f32)).astype(_bf16)
    return hi, lo


def _bmm_x3(a, b):
    ah, al = _split2(a)
    bh, bl = _split2(b)
    mm = functools.partial(jnp.einsum, "nmk,nkp->nmp", preferred_element_type=_f32)
    return mm(ah, bh) + (mm(ah, bl) + mm(al, bh))


def _sigmoid(x):
    return 1.0 / (1.0 + jnp.exp(-x))


def _silu(x):
    return x * _sigmoid(x)


def _softplus(x):
    return jnp.maximum(x, 0.0) + jnp.log1p(jnp.exp(-jnp.abs(x)))


def _group_norm(o, g_row, rms):
    if not rms:
        o = o - jnp.mean(o, axis=-1, keepdims=True)
    return o * lax.rsqrt(jnp.mean(o * o, axis=-1, keepdims=True) + NORM_EPS) * g_row


def _layer_norm(x, g_row, b_row):
    mu = jnp.mean(x, axis=-1, keepdims=True)
    xc = x - mu
    var = jnp.mean(xc * xc, axis=-1, keepdims=True)
    return xc * lax.rsqrt(var + LN_EPS) * g_row + b_row


def _tri3(c):
    t = lax.broadcasted_iota(jnp.int32, (1, c, c), 1)
    s = lax.broadcasted_iota(jnp.int32, (1, c, c), 2)
    return t, s


def _cumsum_col_row(x_col, x_row, c):
    t, s = _tri3(c)
    col = jnp.sum(jnp.where(t >= s, x_row, 0.0), axis=2, keepdims=True)
    row = jnp.sum(jnp.where(t <= s, x_col, 0.0), axis=1, keepdims=True)
    return col, row


def _proj_kernel(x_ref, w_ref, o_ref, tail_ref, xb_ref):
    j = pl.program_id(1)

    @pl.when(j == 0)
    def _():
        xb_ref[...] = x_ref[...].astype(_bf16)

    acc = jnp.dot(xb_ref[...], w_ref[...], preferred_element_type=_f32)
    o_ref[...] = acc.astype(o_ref.dtype)

    @pl.when(j == pl.num_programs(1) - 1)
    def _():
        tail_ref[...] = acc[:, acc.shape[1] - LANES :]


def _proj(x, w, tn):
    rows, k = x.shape
    n = w.shape[1]
    tm = min(TM, rows)
    return pl.pallas_call(
        _proj_kernel,
        grid=(rows // tm, n // tn),
        in_specs=[pl.BlockSpec((tm, k), lambda i, j: (i, 0)), pl.BlockSpec((k, tn), lambda i, j: (0, j))],
        out_specs=[pl.BlockSpec((tm, tn), lambda i, j: (i, j)), pl.BlockSpec((tm, LANES), lambda i, j: (i, 0))],
        out_shape=[jax.ShapeDtypeStruct((rows, n), _bf16), jax.ShapeDtypeStruct((rows, LANES), _f32)],
        scratch_shapes=[pltpu.VMEM((tm, k), _bf16)],
        compiler_params=pltpu.CompilerParams(dimension_semantics=("parallel", "arbitrary"), vmem_limit_bytes=VMEM_LIMIT),
        name="in_proj",
    )(x, w)


def _out_ln_kernel(y_ref, w_ref, x_ref, g_ref, b_ref, o_ref):
    h = jnp.dot(y_ref[...], w_ref[...], preferred_element_type=_f32)
    o_ref[...] = _layer_norm(ALPHA * x_ref[...] + h, g_ref[...], b_ref[...])


def _out_ln(y, w, x, g, b):
    rows, k = y.shape
    d = w.shape[1]
    return pl.pallas_call(
        _out_ln_kernel,
        grid=(rows // TM,),
        in_specs=[
            pl.BlockSpec((TM, k), lambda i: (i, 0)),
            pl.BlockSpec((k, d), lambda i: (0, 0)),
            pl.BlockSpec((TM, d), lambda i: (i, 0)),
            pl.BlockSpec((1, d), lambda i: (0, 0)),
            pl.BlockSpec((1, d), lambda i: (0, 0)),
        ],
        out_specs=pl.BlockSpec((TM, d), lambda i: (i, 0)),
        out_shape=jax.ShapeDtypeStruct((rows, d), _f32),
        compiler_params=pltpu.CompilerParams(dimension_semantics=("parallel",), vmem_limit_bytes=VMEM_LIMIT),
        name="out_proj_ln",
    )(y, w, x, g, b)


FF_SUB = (0, 512, 1024, 1408)


def _swiglu_partial(xb, wg_ref, wu_ref, wd_ref):
    acts = []
    for lo, hi in zip(FF_SUB[:-1], FF_SUB[1:]):
        hg = jnp.dot(xb, wg_ref[:, lo:hi], preferred_element_type=_f32)
        hu = jnp.dot(xb, wu_ref[:, lo:hi], preferred_element_type=_f32)
        acts.append((_silu(hg) * hu).astype(_bf16))
    return jnp.dot(jnp.concatenate(acts, axis=1), wd_ref[...], preferred_element_type=_f32)


def _ffn_ln_kernel(x_ref, wg_ref, wu_ref, wd_ref, g_ref, b_ref, o_ref, xb_ref, acc_ref):
    f = pl.program_id(1)

    @pl.when(f == 0)
    def _():
        xb_ref[...] = x_ref[...].astype(_bf16)
        acc_ref[...] = jnp.zeros_like(acc_ref)

    acc_ref[...] += _swiglu_partial(xb_ref[...], wg_ref, wu_ref, wd_ref)

    @pl.when(f == pl.num_programs(1) - 1)
    def _():
        o_ref[...] = _layer_norm(ALPHA * x_ref[...] + acc_ref[...], g_ref[...], b_ref[...])


def _ffn_ln(x, wg, wu, wd, g, b):
    rows, d = x.shape
    ff = wg.shape[1]
    tf = MOE_FF
    return pl.pallas_call(
        _ffn_ln_kernel,
        grid=(rows // TM, ff // tf),
        in_specs=[
            pl.BlockSpec((TM, d), lambda i, f: (i, 0)),
            pl.BlockSpec((d, tf), lambda i, f: (0, f)),
            pl.BlockSpec((d, tf), lambda i, f: (0, f)),
            pl.BlockSpec((tf, d), lambda i, f: (f, 0)),
            pl.BlockSpec((1, d), lambda i, f: (0, 0)),
            pl.BlockSpec((1, d), lambda i, f: (0, 0)),
        ],
        out_specs=pl.BlockSpec((TM, d), lambda i, f: (i, 0)),
        out_shape=jax.ShapeDtypeStruct((rows, d), _f32),
        scratch_shapes=[pltpu.VMEM((TM, d), _bf16), pltpu.VMEM((TM, d), _f32)],
        compiler_params=pltpu.CompilerParams(dimension_semantics=("parallel", "arbitrary"), vmem_limit_bytes=VMEM_LIMIT),
        name="ffn_ln",
    )(x, wg, wu, wd, g, b)


ROUTE_ROWS = 16


def _route_tile(x, wr_t, bias_col, before):
    xh, xl = _split2(x)
    wh, wl = _split2(wr_t)
    nt = functools.partial(lax.dot_general, dimension_numbers=(((1,), (1,)), ((), ())), preferred_element_type=_f32)
    logits = nt(wh, xh) + (nt(wh, xl) + nt(wl, xh)) + bias_col
    row = lax.broadcasted_iota(jnp.int32, logits.shape, 0)
    ex = jnp.exp(logits - jnp.max(logits, axis=0, keepdims=True))
    probs = ex / jnp.sum(ex, axis=0, keepdims=True)
    p1 = jnp.max(probs, axis=0, keepdims=True)
    i1 = jnp.min(jnp.where(probs == p1, row, ROUTE_ROWS), axis=0, keepdims=True)
    rest = jnp.where(row == i1, -1.0, probs)
    p2 = jnp.max(rest, axis=0, keepdims=True)
    i2 = jnp.min(jnp.where(rest == p2, row, ROUTE_ROWS), axis=0, keepdims=True)
    tot = p1 + p2
    gate_t = jnp.where(row == i1, p1 / tot, 0.0) + jnp.where(row == i2, p2 / tot, 0.0)
    sel_t = jnp.where(row == i1, 1.0, jnp.where(row == i2, 1.0, 0.0))
    rank_t = jnp.dot(sel_t.astype(_bf16), before, preferred_element_type=_f32)
    code_t = jnp.where(sel_t > 0.0, rank_t, -1.0)
    return gate_t[:N_EXPERTS], code_t[:N_EXPERTS], jnp.sum(sel_t, axis=1, keepdims=True)


def _out_ln_route_kernel(y_ref, w_ref, x_ref, g_ref, b_ref, wr_ref, br_ref, before_ref, o_ref, gate_t_ref, code_t_ref, cnt_ref):
    h = jnp.dot(y_ref[...], w_ref[...], preferred_element_type=_f32)
    x_new = _layer_norm(ALPHA * x_ref[...] + h, g_ref[...], b_ref[...])
    o_ref[...] = x_new
    for i in range(MOE_GROUP):
        gate_t, code_t, cnt = _route_tile(x_new[i * TM_MOE : (i + 1) * TM_MOE], wr_ref[...], br_ref[:, 0:1], before_ref[...])
        gate_t_ref[i] = gate_t
        code_t_ref[i] = code_t
        cnt_ref[i] = jnp.broadcast_to(cnt, (ROUTE_ROWS, LANES)).astype(jnp.int32)


def _out_ln_route(y, w, x, g, b, wr_t, br_col):
    rows, k = y.shape
    d = w.shape[1]
    nt = rows // TM_MOE
    t = np.arange(TM_MOE)
    before = jnp.asarray(t[:, None] < t[None, :], _bf16)

    def tile_spec(shape):
        return pl.BlockSpec((MOE_GROUP,) + shape, lambda i: (i, 0, 0))

    x_new, gate_t, code_t, cnt = pl.pallas_call(
        _out_ln_route_kernel,
        grid=(rows // TM,),
        in_specs=[
            pl.BlockSpec((TM, k), lambda i: (i, 0)),
            pl.BlockSpec((k, d), lambda i: (0, 0)),
            pl.BlockSpec((TM, d), lambda i: (i, 0)),
            pl.BlockSpec((1, d), lambda i: (0, 0)),
            pl.BlockSpec((1, d), lambda i: (0, 0)),
            pl.BlockSpec((ROUTE_ROWS, d), lambda i: (0, 0)),
            pl.BlockSpec((ROUTE_ROWS, LANES), lambda i: (0, 0)),
            pl.BlockSpec((TM_MOE, TM_MOE), lambda i: (0, 0)),
        ],
        out_specs=[
            pl.BlockSpec((TM, d), lambda i: (i, 0)),
            tile_spec((N_EXPERTS, TM_MOE)),
            tile_spec((N_EXPERTS, TM_MOE)),
            tile_spec((ROUTE_ROWS, LANES)),
        ],
        out_shape=[
            jax.ShapeDtypeStruct((rows, d), _f32),
            jax.ShapeDtypeStruct((nt, N_EXPERTS, TM_MOE), _f32),
            jax.ShapeDtypeStruct((nt, N_EXPERTS, TM_MOE), _f32),
            jax.ShapeDtypeStruct((nt, ROUTE_ROWS, LANES), jnp.int32),
        ],
        compiler_params=pltpu.CompilerParams(dimension_semantics=("parallel",), vmem_limit_bytes=VMEM_LIMIT),
        name="out_proj_ln_route",
    )(y, w, x, g, b, wr_t, br_col, before)
    return x_new, gate_t, code_t, cnt[:, :N_EXPERTS, 0].reshape(-1)


def _moe_ln_kernel(cnt_ref, x_ref, gate_t_ref, code_t_ref, wg_ref, wu_ref, wd_ref, g_ref, b_ref, o_ref, xb_ref, acc_ref, first_ref):
    grp = pl.program_id(0)
    e = pl.program_id(1)

    @pl.when(e == 0)
    def _():
        xb_ref[...] = x_ref[...].astype(_bf16)
        acc_ref[...] = jnp.zeros_like(acc_ref)

    slot_col = lax.broadcasted_iota(jnp.int32, (MOE_BLOCK, 1), 0)
    counts = [cnt_ref[(grp * MOE_GROUP + i) * N_EXPERTS + e] for i in range(MOE_GROUP)]
    code_rows = [code_t_ref[i, pl.ds(e, 1), :] for i in range(MOE_GROUP)]
    gate_rows = [gate_t_ref[i, pl.ds(e, 1), :] for i in range(MOE_GROUP)]
    first_rows = pl.ds(pl.multiple_of(e * MOE_BLOCK, MOE_BLOCK), MOE_BLOCK)
    scatter_dims = (((0,), (0,)), ((), ()))

    def gather_rows(i, blk):
        hit = code_rows[i] == (slot_col + blk * MOE_BLOCK).astype(_f32)
        onehot = jnp.where(hit, 1.0, 0.0).astype(_bf16)
        xs = jnp.dot(onehot, xb_ref[i * TM_MOE : (i + 1) * TM_MOE, :], preferred_element_type=_f32).astype(_bf16)
        gate = jnp.sum(jnp.where(hit, gate_rows[i], 0.0), axis=1, keepdims=True)
        return onehot, xs, gate

    def first_pass(i):
        _, xs, gate = gather_rows(i, 0)
        first_ref[i, first_rows, :] = (_swiglu_partial(xs, wg_ref, wu_ref, wd_ref) * gate).astype(_bf16)

    def no_pass(i):
        first_ref[i, first_rows, :] = jnp.zeros((MOE_BLOCK, first_ref.shape[2]), _bf16)

    assert MOE_GROUP == 2
    has0 = counts[0] > 0
    has1 = counts[1] > 0

    @pl.when(jnp.logical_and(has0, has1))
    def _():
        _, xs0, gate0 = gather_rows(0, 0)
        _, xs1, gate1 = gather_rows(1, 0)
        out = _swiglu_partial(jnp.concatenate([xs0, xs1], axis=0), wg_ref, wu_ref, wd_ref)
        first_ref[0, first_rows, :] = (out[:MOE_BLOCK] * gate0).astype(_bf16)
        first_ref[1, first_rows, :] = (out[MOE_BLOCK:] * gate1).astype(_bf16)

    @pl.when(jnp.logical_and(has0, jnp.logical_not(has1)))
    def _():
        first_pass(0)
        no_pass(1)

    @pl.when(jnp.logical_and(has1, jnp.logical_not(has0)))
    def _():
        no_pass(0)
        first_pass(1)

    @pl.when(jnp.logical_and(jnp.logical_not(has0), jnp.logical_not(has1)))
    def _():
        no_pass(0)
        no_pass(1)

    for blk in range(1, -(-TM_MOE // MOE_BLOCK)):
        for i in range(MOE_GROUP):

            @pl.when(counts[i] > blk * MOE_BLOCK)
            def _():
                onehot, xs, gate = gather_rows(i, blk)
                out = (_swiglu_partial(xs, wg_ref, wu_ref, wd_ref) * gate).astype(_bf16)
                acc_ref[i * TM_MOE : (i + 1) * TM_MOE, :] += lax.dot_general(onehot, out, scatter_dims, preferred_element_type=_f32)

    @pl.when(e == pl.num_programs(1) - 1)
    def _():
        slot = lax.broadcasted_iota(jnp.int32, (1, MOE_BLOCK, 1), 1).astype(_f32)
        for i in range(MOE_GROUP):
            hit = code_t_ref[i][:, None, :] == slot
            onehot = jnp.where(hit, 1.0, 0.0).astype(_bf16).reshape(N_EXPERTS * MOE_BLOCK, TM_MOE)
            rows = slice(i * TM_MOE, (i + 1) * TM_MOE)
            moe = acc_ref[rows, :] + lax.dot_general(onehot, first_ref[i], scatter_dims, preferred_element_type=_f32)
            o_ref[rows, :] = _layer_norm(ALPHA * x_ref[rows, :] + moe, g_ref[...], b_ref[...])


def _moe_ln(x, gate_t, code_t, counts, wg, wu, wd, g, b):
    rows, d = x.shape
    ne, _, ff = wg.shape
    grid_spec = pltpu.PrefetchScalarGridSpec(
        num_scalar_prefetch=1,
        grid=(rows // TM, ne),
        in_specs=[
            pl.BlockSpec((TM, d), lambda i, e, cnt: (i, 0)),
            pl.BlockSpec((MOE_GROUP, N_EXPERTS, TM_MOE), lambda i, e, cnt: (i, 0, 0)),
            pl.BlockSpec((MOE_GROUP, N_EXPERTS, TM_MOE), lambda i, e, cnt: (i, 0, 0)),
            pl.BlockSpec((None, d, ff), lambda i, e, cnt: (e, 0, 0)),
            pl.BlockSpec((None, d, ff), lambda i, e, cnt: (e, 0, 0)),
            pl.BlockSpec((None, ff, d), lambda i, e, cnt: (e, 0, 0)),
            pl.BlockSpec((1, d), lambda i, e, cnt: (0, 0)),
            pl.BlockSpec((1, d), lambda i, e, cnt: (0, 0)),
        ],
        out_specs=pl.BlockSpec((TM, d), lambda i, e, cnt: (i, 0)),
        scratch_shapes=[
            pltpu.VMEM((TM, d), _bf16),
            pltpu.VMEM((TM, d), _f32),
            pltpu.VMEM((MOE_GROUP, ne * MOE_BLOCK, d), _bf16),
        ],
    )
    return pl.pallas_call(
        _moe_ln_kernel,
        grid_spec=grid_spec,
        out_shape=jax.ShapeDtypeStruct((rows, d), _f32),
        compiler_params=pltpu.CompilerParams(dimension_semantics=("parallel", "arbitrary"), vmem_limit_bytes=VMEM_LIMIT),
        name="moe_ln",
    )(counts, x, gate_t, code_t, wg, wu, wd, g, b)


def _chains(a, c, bb, width, off=0):
    return jnp.stack([a[j * c : (j + 1) * c, off + h * width : off + (h + 1) * width] for j in range(bb) for h in range(NH)])


def _chain_cols(a, c, bb, lane0):
    return jnp.stack([a[j * c : (j + 1) * c, lane0 + h : lane0 + h + 1] for j in range(bb) for h in range(NH)])


def _chain_rows(a_t, c, bb, lane0):
    return jnp.stack([a_t[lane0 + h : lane0 + h + 1, j * c : (j + 1) * c] for j in range(bb) for h in range(NH)])


def _head_rows(ref, row, bb):
    return jnp.stack([ref[row : row + 1, h * DH : (h + 1) * DH] for _ in range(bb) for h in range(NH)])


def _store_chains(y_ref, y, c, bb, off):
    for j in range(bb):
        for h in range(NH):
            y_ref[j * c : (j + 1) * c, off + h * DH : off + (h + 1) * DH] = y[j * NH + h].astype(y_ref.dtype)


PIECE = NH * DH


def _stream_projection(xn_ref, xc_ref, w_ref, main_sc, tail_sc, xb_sc, n_piece):
    rows = xn_ref.shape[0]
    ci = pl.program_id(1)
    tail_col = w_ref.shape[1] - LANES

    def project(xb, row0, p):
        if p < n_piece:
            cols = slice(p * PIECE, (p + 1) * PIECE)
            main_sc[row0 : row0 + rows, cols] = jnp.dot(xb, w_ref[:, cols], preferred_element_type=_f32).astype(_bf16)
        else:
            tail_sc[row0 : row0 + rows, :] = jnp.dot(xb, w_ref[:, tail_col:], preferred_element_type=_f32)

    @pl.when(ci == 0)
    def _():
        xb0 = xc_ref[...].astype(_bf16)
        for p in range(n_piece + 1):
            project(xb0, 0, p)

    @pl.when(ci > 0)
    def _():
        main_sc[0:rows, :] = main_sc[rows : 2 * rows, :]
        tail_sc[0:rows, :] = tail_sc[rows : 2 * rows, :]

    xb_sc[...] = xn_ref[...].astype(_bf16)
    pending = iter(range(n_piece + 1))

    def between(count=1):
        for _ in range(count):
            p = next(pending, None)
            if p is not None:
                project(xb_sc[...], rows, p)

    def load(p):
        return main_sc[0:rows, p * PIECE : (p + 1) * PIECE].astype(_f32)

    return load, tail_sc[0:rows, :], between


def _retention_chunk(q, k, v, s_prev, dec, vec):
    att = _bmm_nt(q, k) * dec
    o = _bmm(att, v) + vec[:, :, 0:1] * _bmm(q, s_prev)
    s_new = vec[:, 0:1, 2:3] * s_prev + _bmm_tn(k * vec[:, :, 1:2], v)
    return o, s_new


def _mlstm_chunk(q, k, v, it_col, it_row, lf_col, lf_row, c_prev, n_prev, m_prev, c):
    t, s = _tri3(c)
    b_col, b_row = _cumsum_col_row(lf_col, lf_row, c)
    logw = jnp.where(t >= s, b_col - b_row + it_row, -jnp.inf)
    m_t = jnp.maximum(b_col + m_prev, jnp.max(logw, axis=2, keepdims=True))
    w = jnp.exp(logw - m_t)
    carry = jnp.exp(b_col + m_prev - m_t)
    qk = _bmm_nt(q, k) * w
    num = _bmm(qk, v) + carry * _bmm(q, c_prev)
    den = jnp.sum(qk, axis=2, keepdims=True) + carry * jnp.sum(q * n_prev, axis=2, keepdims=True)
    h = num / jnp.maximum(jnp.abs(den), jnp.exp(-m_t))
    m_new = m_t[:, c - 1 : c, :]
    b_last = b_col[:, c - 1 : c, :]
    w_last = jnp.exp(b_last - b_col + it_col - m_new)
    decay = jnp.exp(b_last + m_prev - m_new)
    kw = k * w_last
    c_new = decay * c_prev + _bmm_tn(kw, v)
    n_new = decay * n_prev + jnp.sum(kw, axis=1, keepdims=True)
    return h, c_new, n_new, m_new


N_PIECE0 = 8


def _mixer0_body(
    load, gate_tail, between,
    cos_ref, sin_ref, dec_ref, vec_ref, gb_ref, ng_ref, s0_ref, c0_ref, n0_ref, m0_ref,
    y_ref, s_ref, c_ref, n_ref, m_ref, *, c, bb,
):
    n = bb * NH

    @pl.when(pl.program_id(1) == 0)
    def _():
        s_ref[...] = jnp.broadcast_to(s0_ref[...], s_ref.shape)
        c_ref[...] = jnp.broadcast_to(c0_ref[...], c_ref.shape)
        n_ref[...] = jnp.broadcast_to(n0_ref[...], n_ref.shape)
        m_ref[...] = jnp.broadcast_to(m0_ref[...], m_ref.shape)

    def chains(p):
        return _chains(load(p), c, bb, DH)

    cosf = cos_ref[...][None]
    sinf = sin_ref[...][None]
    gates = gate_tail + gb_ref[...]
    lane = lax.broadcasted_iota(jnp.int32, gates.shape, 1)
    gates = jnp.where(lane < NH, gates, -_softplus(-gates))
    gates_t = jnp.transpose(gates)

    q = chains(0)
    k = chains(1)
    q = q * cosf + pltpu.roll(q, DH // 2, axis=2) * sinf
    k = (k * cosf + pltpu.roll(k, DH // 2, axis=2) * sinf) * DH**-0.5
    between(2)
    dec = jnp.concatenate([dec_ref[...]] * bb, axis=0)
    vec = jnp.concatenate([vec_ref[...]] * bb, axis=0)
    o, s_new = _retention_chunk(q, k, chains(2), s_ref[...].reshape(n, DH, DH), dec, vec)
    between(2)
    s_ref[...] = s_new.reshape(s_ref.shape)
    y_a = _group_norm(o, _head_rows(ng_ref, 0, bb), rms=False) * _silu(chains(3))
    _store_chains(y_ref, y_a, c, bb, 0)
    between(1)

    n_prev = jnp.stack([n_ref[j, h : h + 1, :] for j in range(bb) for h in range(NH)])
    m_prev = jnp.stack([m_ref[j, h : h + 1, 0:1] for j in range(bb) for h in range(NH)])
    h_b, c_new, n_new, m_new = _mlstm_chunk(
        chains(4), chains(5) * DH**-0.5, chains(6),
        _chain_cols(gates, c, bb, 0), _chain_rows(gates_t, c, bb, 0),
        _chain_cols(gates, c, bb, NH), _chain_rows(gates_t, c, bb, NH),
        c_ref[...].reshape(n, DH, DH), n_prev, m_prev, c,
    )
    between(2)
    c_ref[...] = c_new.reshape(c_ref.shape)
    for j in range(bb):
        for h in range(NH):
            n_ref[j, h : h + 1, :] = n_new[j * NH + h]
            m_ref[j, h : h + 1, :] = jnp.broadcast_to(m_new[j * NH + h], (1, LANES))
    h_b = _sigmoid(chains(7)) * h_b
    between(1)
    y_b = _group_norm(h_b, _head_rows(ng_ref, 1, bb), rms=False)
    _store_chains(y_ref, y_b, c, bb, NH * DH)
    between(N_PIECE0 + 1)


def _no_op(count=1):
    del count


def _mixer0_kernel(*refs, c, bb):
    pieces, gate_ref, rest = refs[:N_PIECE0], refs[N_PIECE0], refs[N_PIECE0 + 1 :]
    consts, outs = rest[:10], rest[11:]
    _mixer0_body(lambda p: pieces[p][...].astype(_f32), gate_ref[...], _no_op, *consts, *outs, c=c, bb=bb)


def _cast_blocks(cast_in, cast_out, cast_fns):
    for src, dst, fn in zip(cast_in, cast_out, cast_fns):
        blk = src[...]
        dst[...] = (blk if fn is None else fn(blk)).astype(dst.dtype)


def _mixer0_stream_kernel(xn_ref, xc_ref, w_ref, *rest, c, bb, cast_fns=()):
    n_cast = len(cast_fns)
    consts, rest = rest[:10], rest[11:]
    cast_in, outs, cast_out, scratch = rest[:n_cast], rest[n_cast : n_cast + 5], rest[n_cast + 5 : 2 * n_cast + 5], rest[2 * n_cast + 5 :]
    _cast_blocks(cast_in, cast_out, cast_fns)
    load, gate_tail, between = _stream_projection(xn_ref, xc_ref, w_ref, *scratch, N_PIECE0)
    _mixer0_body(load, gate_tail, between, *consts, *outs, c=c, bb=bb)


def _retention_tables(c):
    lg = np.log(1.0 - 2.0 ** (-5.0 - np.arange(NH, dtype=np.float64)))[:, None, None]
    t = np.arange(c, dtype=np.float64)
    diff = t[None, :, None] - t[None, None, :]
    dec = np.where(diff >= 0, np.exp(np.maximum(diff, 0.0) * lg), 0.0)
    vec = np.zeros((NH, c, LANES))
    vec[:, :, 0] = np.exp((t[None, :] + 1.0) * lg[:, 0])
    vec[:, :, 1] = np.exp((c - 1.0 - t[None, :]) * lg[:, 0])
    vec[:, :, 2] = np.exp(c * lg[:, 0])
    return jnp.asarray(dec, _f32), jnp.asarray(vec, _f32)


def _whole(a):
    return a, pl.BlockSpec(a.shape, lambda i, ci: (0,) * a.ndim)


def _mixer_call(name, kernels, source, n_piece, consts, init, ybuf, *, row_off, y_row_off, nb, nchunk, c, bb, bcast_init, stream=False, casts=()):
    rows = bb * c
    nbb = nb // bb
    blk0 = row_off // rows
    ib = 1 if bcast_init else bb
    y_spec = pl.BlockSpec((rows, ybuf.shape[1]), lambda i, ci: (y_row_off // rows + ci * nbb + i, 0))

    def row_block(col):
        return lambda i, ci: (blk0 + ci * nbb + i, col)

    def batch_block(nd, lead):
        return lambda i, ci: ((i if lead else 0),) + (0,) * nd

    if stream:
        x, w = source
        src_args = [x, x, w]
        src_specs = [
            pl.BlockSpec((rows, D_MODEL), lambda i, ci: (blk0 + jnp.minimum(ci + 1, nchunk - 1) * nbb + i, 0)),
            pl.BlockSpec((rows, D_MODEL), lambda i, ci: (blk0 + i, 0)),
            pl.BlockSpec(w.shape, lambda i, ci: (0, 0)),
        ]
        scratch = [pltpu.VMEM((2 * rows, n_piece * PIECE), _bf16), pltpu.VMEM((2 * rows, LANES), _f32), pltpu.VMEM((rows, D_MODEL), _bf16)]
    else:
        proj, tail = source
        src_args = [proj] * n_piece + [tail]
        src_specs = [pl.BlockSpec((rows, PIECE), row_block(p)) for p in range(n_piece)] + [pl.BlockSpec((rows, LANES), row_block(0))]
        scratch = []
    in_specs = src_specs + [spec for _, spec in consts]
    in_specs += [pl.BlockSpec((ib,) + a.shape[1:], batch_block(a.ndim - 1, not bcast_init)) for a in init]
    out_specs = [y_spec] + [pl.BlockSpec((bb,) + a.shape[1:], batch_block(a.ndim - 1, True)) for a in init]
    out_shape = [jax.ShapeDtypeStruct(ybuf.shape, ybuf.dtype)] + [jax.ShapeDtypeStruct((nb,) + a.shape[1:], _f32) for a in init]
    args = src_args + [a for a, _ in consts] + list(init)
    cast_in_specs, cast_out_specs, cast_shapes = [], [], []
    for a, n_blocks, out_cols, _ in casts:
        blk_rows = a.shape[0] // n_blocks
        assert n_blocks <= nbb * nchunk and blk_rows * n_blocks == a.shape[0] and blk_rows % 16 == 0
        index = (lambda last: lambda i, ci: (jnp.minimum(i * nchunk + ci, last), 0))(n_blocks - 1)
        cast_in_specs.append(pl.BlockSpec((blk_rows, a.shape[1]), index))
        cast_out_specs.append(pl.BlockSpec((blk_rows, out_cols), index))
        cast_shapes.append(jax.ShapeDtypeStruct((a.shape[0], out_cols), _bf16))
    body = functools.partial(kernels[int(stream)], c=c, bb=bb, **({"cast_fns": tuple(fn for *_, fn in casts)} if casts else {}))
    return pl.pallas_call(
        body,
        grid=(nbb, nchunk),
        in_specs=in_specs + [pl.BlockSpec(memory_space=pl.ANY)] + cast_in_specs,
        out_specs=out_specs + cast_out_specs,
        out_shape=out_shape + cast_shapes,
        scratch_shapes=scratch,
        input_output_aliases={len(args): 0},
        compiler_params=pltpu.CompilerParams(dimension_semantics=("parallel", "arbitrary"), vmem_limit_bytes=VMEM_LIMIT),
        name=f"{name}_c{c}",
    )(*args, ybuf, *[a for a, *_ in casts])


C1_Z_END = QKV_C + NH * DH
C1_QD = C1_Z_END + 2 * NH
C1_LR = C1_QD + 2 * NH * DK_D + 2 * NH * DH


def _regroup_w_in1(blk):
    rows, cols = blk.shape
    n_small = cols - C1_LR + 2 * NH
    gaps = [jnp.zeros((rows, n), blk.dtype) for n in (P1_PAD - LANES - P1_MAIN, LANES - n_small)]
    return jnp.concatenate([blk[:, :C1_Z_END], blk[:, C1_QD:C1_LR], gaps[0], blk[:, C1_LR:], blk[:, C1_Z_END:C1_QD], gaps[1]], axis=1)


def _side_cast(a, n_blocks, out_cols=None, fn=None):
    return a, n_blocks, out_cols or a.shape[1], fn


def _mixer0(source, ybuf, cosf, sinf, gb_row, norm_g, init, **group):
    c = group["c"]
    dec, vec = _retention_tables(c)
    by_chunk = pl.BlockSpec((c, DH), lambda i, ci: (ci, 0))
    consts = [(cosf, by_chunk), (sinf, by_chunk), _whole(dec), _whole(vec), _whole(gb_row), _whole(norm_g)]
    return _mixer_call("mixer0", (_mixer0_kernel, _mixer0_stream_kernel), source, N_PIECE0, consts, init, ybuf, **group)


def _unit_lower_solve(a, rhs, c):
    bs = min(SOLVE_BLOCK, c)
    t, s = _tri3(c)
    if c > bs:
        shift = bs.bit_length() - 1
        same = jnp.right_shift(t, shift) == jnp.right_shift(s, shift)
        d = jnp.where(same, a, 0.0)
    else:
        d = a
    inv = jnp.where(t == s, 1.0, 0.0) - d
    p = d
    span = 2
    while span < bs:
        p = _bmm_x3(p, p)
        inv = inv + _bmm_x3(inv, p)
        span *= 2
    y = _bmm(inv, rhs)
    if c == bs:
        return y
    b = _bmm(inv, jnp.where(same, 0.0, a))
    y = y - _bmm(b, y)
    p = b
    span = 2
    while span < c // bs:
        p = _bmm(p, p)
        y = y + _bmm(p, y)
        span *= 2
    return y


def _gdn_chunk(q, k, v, beta_col, g_col, g_row, s_prev, c, between=_no_op):
    t, s = _tri3(c)
    gc_col, gc_row = _cumsum_col_row(g_col, g_row, c)
    dec_incl = jnp.exp(jnp.where(t >= s, gc_col - gc_row, -jnp.inf))
    dec_strict = jnp.where(t > s, dec_incl, 0.0)
    e_col = jnp.exp(gc_col)
    a = beta_col * _bmm_nt(k, k) * dec_strict
    rhs = jnp.concatenate([beta_col * v, (beta_col * e_col) * k], axis=-1)
    between(1)
    sol = _unit_lower_solve(a, rhs, c)
    between(1)
    u = sol[:, :, :DH] - _bmm(sol[:, :, DH:], s_prev)
    qk = _bmm_nt(q, k) * dec_incl
    between(1)
    o = e_col * _bmm(q, s_prev) + _bmm(qk, u)
    gl = gc_col[:, c - 1 : c, :]
    s_new = jnp.exp(gl) * s_prev + _bmm_tn(k * jnp.exp(gl - gc_col), u)
    return o, s_new


GLA_SUB = 8


def _gla_chunk(q, k, v, bc, s_prev, c):
    t, s = _tri3(c)
    sub = min(GLA_SUB, c)
    atts = []
    for lo in range(0, c, sub):
        hi = lo + sub
        ref = bc[:, lo - 1 : lo, :] if lo else jnp.zeros_like(bc[:, 0:1, :])
        att = _bmm_nt(q[:, lo:hi] * jnp.exp(bc[:, lo:hi] - ref), k[:, :hi] * jnp.exp(ref - bc[:, :hi]))
        atts.append(att if hi == c else jnp.concatenate([att, jnp.zeros((att.shape[0], sub, c - hi), _f32)], axis=2))
    att = jnp.where(t >= s, jnp.concatenate(atts, axis=1), 0.0)
    o = _bmm(att, v) + _bmm(q * jnp.exp(bc), s_prev)
    bl = bc[:, c - 1 : c, :]
    ti, si = _tri3(DK_D)
    el_col = jnp.sum(jnp.where(ti == si, jnp.exp(bl), 0.0), axis=2, keepdims=True)
    s_new = el_col * s_prev + _bmm_tn(k * jnp.exp(bl - bc), v)
    return o, s_new


N_PIECE1 = 7


def _mixer1_body(
    load, small, between,
    cw_ref, gp_ref, wa_ref, ba_ref, ng_ref, s0_ref, cv0_ref, d0_ref,
    y_ref, s_ref, cv_ref, d_ref, *, c, bb,
):
    n = bb * NH
    w4 = NH * DH

    @pl.when(pl.program_id(1) == 0)
    def _():
        s_ref[...] = jnp.broadcast_to(s0_ref[...], s_ref.shape)
        cv_ref[...] = jnp.broadcast_to(cv0_ref[...], cv_ref.shape)
        d_ref[...] = jnp.broadcast_to(d0_ref[...], d_ref.shape)

    beta_all = _sigmoid(small)
    g_all = -jnp.exp(gp_ref[0:1, :]) * _softplus(small + gp_ref[1:2, :])
    g_all_t = jnp.transpose(g_all)
    log_alpha = -_softplus(-(_dot_hi(small[:, 0:GLA_RANK], wa_ref[...]) + ba_ref[...])) * (1.0 / GLA_TAU)
    cw = cw_ref[...]

    acts = []
    qkv = jnp.concatenate([load(0), load(1), load(2)], axis=1)
    between(1)
    for j in range(bb):
        ext = jnp.concatenate([cv_ref[j], qkv[j * c : (j + 1) * c, :]], axis=0)
        conv = cw[3:4] * ext[8 : 8 + c] + cw[2:3] * ext[7 : 7 + c] + cw[1:2] * ext[6 : 6 + c] + cw[0:1] * ext[5 : 5 + c]
        cv_ref[j] = ext[c : c + 8]
        acts.append(_silu(conv))

    def act_chains(off):
        return jnp.stack([acts[j][:, off + h * DH : off + (h + 1) * DH] for j in range(bb) for h in range(NH)])

    qc = act_chains(0)
    kc = act_chains(w4)
    qc = qc * lax.rsqrt(jnp.sum(qc * qc, axis=-1, keepdims=True) + NORM_EPS) * DH**-0.5
    kc = kc * lax.rsqrt(jnp.sum(kc * kc, axis=-1, keepdims=True) + NORM_EPS)
    between(1)
    o, s_new = _gdn_chunk(
        qc, kc, act_chains(2 * w4),
        _chain_cols(beta_all, c, bb, GLA_RANK), _chain_cols(g_all, c, bb, GLA_RANK + NH),
        _chain_rows(g_all_t, c, bb, GLA_RANK + NH), s_ref[...].reshape(n, DH, DH), c, between,
    )
    s_ref[...] = s_new.reshape(s_ref.shape)
    y_c = _group_norm(o, _head_rows(ng_ref, 0, bb), rms=True) * _silu(_chains(load(3), c, bb, DH))
    _store_chains(y_ref, y_c, c, bb, 0)
    between(1)

    tt = lax.broadcasted_iota(jnp.int32, (c, c), 0)
    ss = lax.broadcasted_iota(jnp.int32, (c, c), 1)
    ones_lt = jnp.where(tt >= ss, 1.0, 0.0).astype(_bf16)
    bcs = []
    for j in range(bb):
        la = log_alpha[j * c : (j + 1) * c, :]
        hi = la.astype(_bf16)
        r1 = la - hi.astype(_f32)
        mid = r1.astype(_bf16)
        lo = (r1 - mid.astype(_f32)).astype(_bf16)
        cum = functools.partial(jnp.dot, ones_lt, preferred_element_type=_f32)
        bcs.append(cum(hi) + (cum(mid) + cum(lo)))
    bc = jnp.stack([bcs[j][:, h * DK_D : (h + 1) * DK_D] for j in range(bb) for h in range(NH)])
    qkd = load(4)
    between(1)
    o, d_new = _gla_chunk(
        _chains(qkd, c, bb, DK_D) * DK_D**-0.5, _chains(qkd, c, bb, DK_D, off=NH * DK_D),
        _chains(load(5), c, bb, DH), bc, d_ref[...].reshape(n, DK_D, DH), c,
    )
    d_ref[...] = d_new.reshape(d_ref.shape)
    between(1)
    y_d = _group_norm(o, _head_rows(ng_ref, 1, bb), rms=False) * _silu(_chains(load(6), c, bb, DH))
    _store_chains(y_ref, y_d, c, bb, w4)
    between(N_PIECE1 + 1)


def _mixer1_kernel(*refs, c, bb):
    pieces, small_ref, rest = refs[:N_PIECE1], refs[N_PIECE1], refs[N_PIECE1 + 1 :]
    consts, outs = rest[:8], rest[9:]
    _mixer1_body(lambda p: pieces[p][...].astype(_f32), small_ref[...], _no_op, *consts, *outs, c=c, bb=bb)


def _mixer1_stream_kernel(xn_ref, xc_ref, w_ref, *rest, c, bb, cast_fns=()):
    n_cast = len(cast_fns)
    consts, rest = rest[:8], rest[9:]
    cast_in, outs, cast_out, scratch = rest[:n_cast], rest[n_cast : n_cast + 4], rest[n_cast + 4 : 2 * n_cast + 4], rest[2 * n_cast + 4 :]
    _cast_blocks(cast_in, cast_out, cast_fns)
    load, small, between = _stream_projection(xn_ref, xc_ref, w_ref, *scratch, N_PIECE1)
    _mixer1_body(load, small, between, *consts, *outs, c=c, bb=bb)


def _mixer1(source, ybuf, conv_w, gdn_par, w_alpha, b_alpha, norm_g, init, **group):
    consts = [_whole(a) for a in (conv_w, gdn_par, w_alpha, b_alpha, norm_g)]
    return _mixer_call("mixer1", (_mixer1_kernel, _mixer1_stream_kernel), source, N_PIECE1, consts, init, ybuf, **group)


_GROUP_META = dict(row_off=ROW_META - ROW_SAMPLE, y_row_off=ROW_META, nb=1, nchunk=1, c=N_META, bb=1, bcast_init=True)
_GROUP_PROMPT = dict(row_off=0, y_row_off=0, nb=BATCH, nchunk=SEQ // CHUNK, c=CHUNK, bb=4, bcast_init=True, stream=True)
_GROUP_SAMPLE = dict(row_off=0, y_row_off=ROW_SAMPLE, nb=DEC_BATCH, nchunk=1, c=DEC_SEQ, bb=16, bcast_init=False)


def _rows_to_batch_major_kernel(x_ref, o_ref):
    o_ref[...] = x_ref[...].reshape(o_ref.shape)


def _prompt_rows_to_batch_major(x):
    nchunk = SEQ // CHUNK
    out = pl.pallas_call(
        _rows_to_batch_major_kernel,
        grid=(nchunk,),
        in_specs=[pl.BlockSpec((BATCH * CHUNK, D_MODEL), lambda ci: (ci, 0))],
        out_specs=pl.BlockSpec((BATCH, None, CHUNK, D_MODEL), lambda ci: (0, ci, 0, 0)),
        out_shape=jax.ShapeDtypeStruct((BATCH, nchunk, CHUNK, D_MODEL), x.dtype),
        compiler_params=pltpu.CompilerParams(dimension_semantics=("parallel",)),
        name="rows_to_batch_major",
    )(x)
    return out.reshape(BATCH, SEQ, D_MODEL)


def _rotary_tables(pos):
    half = DH // 2
    inv = ROPE_BASE ** (-jnp.arange(half, dtype=_f32) / half)
    ang = pos.astype(_f32)[:, None] * inv[None, :]
    cos, sin = jnp.cos(ang), jnp.sin(ang)
    return jnp.concatenate([cos, cos], -1), jnp.concatenate([-sin, sin], -1)


def _lanes(m):
    return jnp.broadcast_to(m.astype(_f32)[..., None], m.shape + (LANES,))


def _conv_rows(s):
    return jnp.pad(s.astype(_f32), ((0, 0), (8 - (CONV_W - 1), 0), (0, 0)))


def kernel(x_prompt, x_sample, state_ret, state_mlstm_c, state_mlstm_n, state_mlstm_m, state_gdn, state_gdn_conv, state_gla, meta_tokens, w_in0, ret_norm_g, mlstm_gate_bias, mlstm_norm_g, w_out0, ln0_mix_g, ln0_mix_b, ffn0_w_gate, ffn0_w_up, ffn0_w_down, ln0_ffn_g, ln0_ffn_b, w_in1, gdn_conv_w, gdn_a_log, gdn_dt_bias, gdn_norm_g, gla_w_alpha, gla_b_alpha, gla_norm_g, w_out1, ln1_mix_g, ln1_mix_b, moe_w_router, moe_b_router, moe_w_gate, moe_w_up, moe_w_down, ln1_ffn_g, ln1_ffn_b):
    w4 = NH * DH
    nchunk = SEQ // CHUNK
    xp = x_prompt.reshape(BATCH, nchunk, CHUNK, D_MODEL).transpose(1, 0, 2, 3).reshape(N_PROMPT, D_MODEL)
    x = jnp.concatenate(
        [
            xp,
            x_sample.reshape(N_SAMPLE, D_MODEL),
            meta_tokens.astype(x_prompt.dtype),
            jnp.zeros((R_ROWS - ROW_META - N_META, D_MODEL), x_prompt.dtype),
        ],
        0,
    )

    w_in0_p = jnp.pad(w_in0.astype(_bf16), ((0, 0), (0, P0_PAD - w_in0.shape[1])))
    gb_row = jnp.pad(mlstm_gate_bias.astype(_f32), (0, LANES - 2 * NH))[None]
    norm0 = jnp.stack([ret_norm_g, mlstm_norm_g]).astype(_f32)
    norm1 = jnp.stack([gdn_norm_g, gla_norm_g]).astype(_f32)
    lo = GLA_RANK + NH
    gdn_par = jnp.stack(
        [
            jnp.pad(gdn_a_log.astype(_f32), (lo, LANES - lo - NH)),
            jnp.pad(gdn_dt_bias.astype(_f32), (lo, LANES - lo - NH)),
        ]
    )
    w_router_t = jnp.pad(moe_w_router.astype(_f32).T, ((0, ROUTE_ROWS - N_EXPERTS), (0, 0)))
    b_router = jnp.pad(moe_b_router.astype(_f32), (0, ROUTE_ROWS - N_EXPERTS), constant_values=-jnp.inf)
    b_router_col = jnp.broadcast_to(b_router[:, None], (ROUTE_ROWS, LANES))
    moe_w_flat = [w.reshape(-1, w.shape[-1]) for w in (moe_w_gate, moe_w_up, moe_w_down)]

    def row(v):
        return v.astype(_f32)[None]

    proj0 = _proj(x[ROW_SAMPLE:], w_in0_p, tn=P0_PAD // 3)
    zeros_even = (
        jnp.zeros((1, NH, DH, DH), _f32), jnp.zeros((1, NH, DH, DH), _f32),
        jnp.zeros((1, NH, DH), _f32), jnp.zeros((1, NH, LANES), _f32),
    )
    cos_m, sin_m = _rotary_tables(jnp.arange(N_META))
    cos_p, sin_p = _rotary_tables(N_META + jnp.arange(SEQ))
    cos_s, sin_s = _rotary_tables(PAST_LEN + jnp.arange(DEC_SEQ))
    ybuf = jnp.zeros((R_ROWS, 2 * w4), _bf16)
    ybuf, *meta_even = _mixer0(proj0, ybuf, cos_m, sin_m, gb_row, norm0, zeros_even, **_GROUP_META)
    steps = _GROUP_PROMPT["nb"] // _GROUP_PROMPT["bb"] * _GROUP_PROMPT["nchunk"]
    casts0 = [
        _side_cast(ffn0_w_gate, steps), _side_cast(ffn0_w_up, steps), _side_cast(ffn0_w_down, D_FF // 64),
        _side_cast(w_out0, steps), _side_cast(w_in1, steps, P1_PAD, _regroup_w_in1),
    ]
    ybuf, p_ret, p_mc, p_mn, p_mm, wg0, wu0, wd0, wo0, w_in1_p = _mixer0(
        (x, w_in0_p), ybuf, cos_p, sin_p, gb_row, norm0, meta_even, casts=casts0, **_GROUP_PROMPT)
    init_s = (state_ret.astype(_f32), state_mlstm_c.astype(_f32), state_mlstm_n.astype(_f32), _lanes(state_mlstm_m))
    ybuf, s_ret, s_mc, s_mn, s_mm = _mixer0(proj0, ybuf, cos_s, sin_s, gb_row, norm0, init_s, **_GROUP_SAMPLE)
    x = _out_ln(ybuf, wo0, x, row(ln0_mix_g), row(ln0_mix_b))
    x = _ffn_ln(x, wg0, wu0, wd0, row(ln0_ffn_g), row(ln0_ffn_b))

    proj1 = _proj(x[ROW_SAMPLE:], w_in1_p, tn=P1_PAD // 3)
    zeros_odd = (jnp.zeros((1, NH, DH, DH), _f32), jnp.zeros((1, 8, QKV_C), _f32), jnp.zeros((1, NH, DK_D, DH), _f32))
    m1_par = (gdn_conv_w.astype(_f32), gdn_par, gla_w_alpha.astype(_f32), row(gla_b_alpha), norm1)
    ybuf, *meta_odd = _mixer1(proj1, ybuf, *m1_par, zeros_odd, **_GROUP_META)
    casts1 = [_side_cast(w, steps) for w in moe_w_flat + [w_out1]]
    ybuf, p_gdn, p_conv, p_gla, *moe_w, wo1 = _mixer1((x, w_in1_p), ybuf, *m1_par, meta_odd, casts=casts1, **_GROUP_PROMPT)
    moe_w = [w.reshape(w32.shape) for w, w32 in zip(moe_w, (moe_w_gate, moe_w_up, moe_w_down))]
    init_s = (state_gdn.astype(_f32), _conv_rows(state_gdn_conv), state_gla.astype(_f32))
    ybuf, s_gdn, s_conv, s_gla = _mixer1(proj1, ybuf, *m1_par, init_s, **_GROUP_SAMPLE)
    x, gate_t, code_t, counts = _out_ln_route(ybuf, wo1, x, row(ln1_mix_g), row(ln1_mix_b), w_router_t, b_router_col)
    x = _moe_ln(x, gate_t, code_t, counts, *moe_w, row(ln1_ffn_g), row(ln1_ffn_b))

    y_prompt = _prompt_rows_to_batch_major(x)
    y_sample = x[ROW_SAMPLE:ROW_META].reshape(DEC_BATCH, DEC_SEQ, D_MODEL)
    tail = 8 - (CONV_W - 1)
    return (
        y_prompt, y_sample,
        p_ret, p_mc, p_mn, p_mm[..., 0], p_gdn, p_conv[:, tail:], p_gla,
        s_ret, s_mc, s_mn, s_mm[..., 0], s_gdn, s_conv[:, tail:], s_gla,
    )
```

```python
import functools

import jax
import jax.numpy as jnp
import numpy as np
from jax import lax
from jax.experimental import pallas as pl
from jax.experimental.pallas import tpu as pltpu

D_MODEL = 1024
BATCH = 8
SEQ = 2048
DEC_BATCH = 128
DEC_SEQ = 4
PAST_LEN = 16384
N_META = 16
CHUNK = 64
NH = 4
DH = 128
DK_D = 64
CONV_W = 4
GLA_RANK = 16
GLA_TAU = 16.0
D_FF = 2816
N_EXPERTS = 8
MOE_FF = 1408
ROPE_BASE = 10000.0
LN_EPS = 1e-5
NORM_EPS = 1e-6
DEPTH = 2
ALPHA = (2 * DEPTH) ** 0.25
QKV_C = 3 * NH * DH

LANES = 128
N_PROMPT = BATCH * SEQ
N_SAMPLE = DEC_BATCH * DEC_SEQ
ROW_SAMPLE = N_PROMPT
ROW_META = N_PROMPT + N_SAMPLE
TM_MOE = 448
MOE_BLOCK = 128
MOE_GROUP = 2
TM = TM_MOE * MOE_GROUP
R_ROWS = ((ROW_META + N_META + TM - 1) // TM) * TM

P0_PAD = 33 * LANES
P1_MAIN = 28 * LANES
P1_PAD = 30 * LANES
SOLVE_BLOCK = 16

VMEM_LIMIT = 56 * 1024 * 1024

_bf16 = jnp.bfloat16
_f32 = jnp.float32
_HI = lax.Precision.HIGHEST


def _dot_hi(a, b):
    return jnp.dot(a, b, preferred_element_type=_f32, precision=_HI)


def _bmm(a, b):
    return jnp.einsum("nmk,nkp->nmp", a.astype(_bf16), b.astype(_bf16), preferred_element_type=_f32)


def _bmm_nt(a, b):
    return jnp.einsum("nmk,npk->nmp", a.astype(_bf16), b.astype(_bf16), preferred_element_type=_f32)


def _bmm_tn(a, b):
    return jnp.einsum("nkm,nkp->nmp", a.astype(_bf16), b.astype(_bf16), preferred_element_type=_f32)


def _split2(a):
    hi = a.astype(_bf16)
    lo = (a - hi.astype(_f32)).astype(_bf16)
    return hi, lo


def _bmm_x3(a, b):
    ah, al = _split2(a)
    bh, bl = _split2(b)
    mm = functools.partial(jnp.einsum, "nmk,nkp->nmp", preferred_element_type=_f32)
    return mm(ah, bh) + (mm(ah, bl) + mm(al, bh))


def _sigmoid(x):
    return 1.0 / (1.0 + jnp.exp(-x))


def _silu(x):
    return x * _sigmoid(x)


def _softplus(x):
    return jnp.maximum(x, 0.0) + jnp.log1p(jnp.exp(-jnp.abs(x)))


def _group_norm(o, g_row, rms):
    if not rms:
        o = o - jnp.mean(o, axis=-1, keepdims=True)
    return o * lax.rsqrt(jnp.mean(o * o, axis=-1, keepdims=True) + NORM_EPS) * g_row


def _layer_norm(x, g_row, b_row):
    mu = jnp.mean(x, axis=-1, keepdims=True)
    xc = x - mu
    var = jnp.mean(xc * xc, axis=-1, keepdims=True)
    return xc * lax.rsqrt(var + LN_EPS) * g_row + b_row


def _tri3(c):
    t = lax.broadcasted_iota(jnp.int32, (1, c, c), 1)
    s = lax.broadcasted_iota(jnp.int32, (1, c, c), 2)
    return t, s


def _cumsum_col_row(x_col, x_row, c):
    t, s = _tri3(c)
    col = jnp.sum(jnp.where(t >= s, x_row, 0.0), axis=2, keepdims=True)
    row = jnp.sum(jnp.where(t <= s, x_col, 0.0), axis=1, keepdims=True)
    return col, row


def _proj_kernel(x_ref, w_ref, o_ref, tail_ref, xb_ref):
    j = pl.program_id(1)

    @pl.when(j == 0)
    def _():
        xb_ref[...] = x_ref[...].astype(_bf16)

    acc = jnp.dot(xb_ref[...], w_ref[...], preferred_element_type=_f32)
    o_ref[...] = acc.astype(o_ref.dtype)

    @pl.when(j == pl.num_programs(1) - 1)
    def _():
        tail_ref[...] = acc[:, acc.shape[1] - LANES :]


def _proj(x, w, tn):
    rows, k = x.shape
    n = w.shape[1]
    tm = min(TM, rows)
    return pl.pallas_call(
        _proj_kernel,
        grid=(rows // tm, n // tn),
        in_specs=[pl.BlockSpec((tm, k), lambda i, j: (i, 0)), pl.BlockSpec((k, tn), lambda i, j: (0, j))],
        out_specs=[pl.BlockSpec((tm, tn), lambda i, j: (i, j)), pl.BlockSpec((tm, LANES), lambda i, j: (i, 0))],
        out_shape=[jax.ShapeDtypeStruct((rows, n), _bf16), jax.ShapeDtypeStruct((rows, LANES), _f32)],
        scratch_shapes=[pltpu.VMEM((tm, k), _bf16)],
        compiler_params=pltpu.CompilerParams(dimension_semantics=("parallel", "arbitrary"), vmem_limit_bytes=VMEM_LIMIT),
        name="in_proj",
    )(x, w)


def _out_ln_kernel(y_ref, w_ref, x_ref, g_ref, b_ref, o_ref):
    h = jnp.dot(y_ref[...], w_ref[...], preferred_element_type=_f32)
    o_ref[...] = _layer_norm(ALPHA * x_ref[...] + h, g_ref[...], b_ref[...])


def _out_ln(y, w, x, g, b):
    rows, k = y.shape
    d = w.shape[1]
    return pl.pallas_call(
        _out_ln_kernel,
        grid=(rows // TM,),
        in_specs=[
            pl.BlockSpec((TM, k), lambda i: (i, 0)),
            pl.BlockSpec((k, d), lambda i: (0, 0)),
            pl.BlockSpec((TM, d), lambda i: (i, 0)),
            pl.BlockSpec((1, d), lambda i: (0, 0)),
            pl.BlockSpec((1, d), lambda i: (0, 0)),
        ],
        out_specs=pl.BlockSpec((TM, d), lambda i: (i, 0)),
        out_shape=jax.ShapeDtypeStruct((rows, d), _f32),
        compiler_params=pltpu.CompilerParams(dimension_semantics=("parallel",), vmem_limit_bytes=VMEM_LIMIT),
        name="out_proj_ln",
    )(y, w, x, g, b)


def _swiglu_partial(xb, wg_ref, wu_ref, wd_ref):
    hg = jnp.dot(xb, wg_ref[...], preferred_element_type=_f32)
    hu = jnp.dot(xb, wu_ref[...], preferred_element_type=_f32)
    return jnp.dot((_silu(hg) * hu).astype(_bf16), wd_ref[...], preferred_element_type=_f32)


def _ffn_ln_kernel(x_ref, wg_ref, wu_ref, wd_ref, g_ref, b_ref, o_ref, xb_ref, acc_ref):
    f = pl.program_id(1)

    @pl.when(f == 0)
    def _():
        xb_ref[...] = x_ref[...].astype(_bf16)
        acc_ref[...] = jnp.zeros_like(acc_ref)

    acc_ref[...] += _swiglu_partial(xb_ref[...], wg_ref, wu_ref, wd_ref)

    @pl.when(f == pl.num_programs(1) - 1)
    def _():
        o_ref[...] = _layer_norm(ALPHA * x_ref[...] + acc_ref[...], g_ref[...], b_ref[...])


def _ffn_ln(x, wg, wu, wd, g, b):
    rows, d = x.shape
    ff = wg.shape[1]
    tf = MOE_FF
    return pl.pallas_call(
        _ffn_ln_kernel,
        grid=(rows // TM, ff // tf),
        in_specs=[
            pl.BlockSpec((TM, d), lambda i, f: (i, 0)),
            pl.BlockSpec((d, tf), lambda i, f: (0, f)),
            pl.BlockSpec((d, tf), lambda i, f: (0, f)),
            pl.BlockSpec((tf, d), lambda i, f: (f, 0)),
            pl.BlockSpec((1, d), lambda i, f: (0, 0)),
            pl.BlockSpec((1, d), lambda i, f: (0, 0)),
        ],
        out_specs=pl.BlockSpec((TM, d), lambda i, f: (i, 0)),
        out_shape=jax.ShapeDtypeStruct((rows, d), _f32),
        scratch_shapes=[pltpu.VMEM((TM, d), _bf16), pltpu.VMEM((TM, d), _f32)],
        compiler_params=pltpu.CompilerParams(dimension_semantics=("parallel", "arbitrary"), vmem_limit_bytes=VMEM_LIMIT),
        name="ffn_ln",
    )(x, wg, wu, wd, g, b)


ROUTE_ROWS = 16


def _route_tile(x, wr_t, bias_col, before):
    xh, xl = _split2(x)
    wh, wl = _split2(wr_t)
    nt = functools.partial(lax.dot_general, dimension_numbers=(((1,), (1,)), ((), ())), preferred_element_type=_f32)
    logits = nt(wh, xh) + (nt(wh, xl) + nt(wl, xh)) + bias_col
    row = lax.broadcasted_iota(jnp.int32, logits.shape, 0)
    ex = jnp.exp(logits - jnp.max(logits, axis=0, keepdims=True))
    probs = ex / jnp.sum(ex, axis=0, keepdims=True)
    p1 = jnp.max(probs, axis=0, keepdims=True)
    i1 = jnp.min(jnp.where(probs == p1, row, ROUTE_ROWS), axis=0, keepdims=True)
    rest = jnp.where(row == i1, -1.0, probs)
    p2 = jnp.max(rest, axis=0, keepdims=True)
    i2 = jnp.min(jnp.where(rest == p2, row, ROUTE_ROWS), axis=0, keepdims=True)
    tot = p1 + p2
    gate_t = jnp.where(row == i1, p1 / tot, 0.0) + jnp.where(row == i2, p2 / tot, 0.0)
    sel_t = jnp.where(row == i1, 1.0, jnp.where(row == i2, 1.0, 0.0))
    rank_t = jnp.dot(sel_t.astype(_bf16), before, preferred_element_type=_f32)
    code_t = jnp.where(sel_t > 0.0, rank_t, -1.0)
    return gate_t[:N_EXPERTS], code_t[:N_EXPERTS], jnp.sum(sel_t, axis=1, keepdims=True)


def _out_ln_route_kernel(y_ref, w_ref, x_ref, g_ref, b_ref, wr_ref, br_ref, before_ref, o_ref, gate_t_ref, code_t_ref, cnt_ref):
    h = jnp.dot(y_ref[...], w_ref[...], preferred_element_type=_f32)
    x_new = _layer_norm(ALPHA * x_ref[...] + h, g_ref[...], b_ref[...])
    o_ref[...] = x_new
    for i in range(MOE_GROUP):
        gate_t, code_t, cnt = _route_tile(x_new[i * TM_MOE : (i + 1) * TM_MOE], wr_ref[...], br_ref[:, 0:1], before_ref[...])
        gate_t_ref[i] = gate_t
        code_t_ref[i] = code_t
        cnt_ref[i] = jnp.broadcast_to(cnt, (ROUTE_ROWS, LANES)).astype(jnp.int32)


def _out_ln_route(y, w, x, g, b, wr_t, br_col):
    rows, k = y.shape
    d = w.shape[1]
    nt = rows // TM_MOE
    t = np.arange(TM_MOE)
    before = jnp.asarray(t[:, None] < t[None, :], _bf16)

    def tile_spec(shape):
        return pl.BlockSpec((MOE_GROUP,) + shape, lambda i: (i, 0, 0))

    x_new, gate_t, code_t, cnt = pl.pallas_call(
        _out_ln_route_kernel,
        grid=(rows // TM,),
        in_specs=[
            pl.BlockSpec((TM, k), lambda i: (i, 0)),
            pl.BlockSpec((k, d), lambda i: (0, 0)),
            pl.BlockSpec((TM, d), lambda i: (i, 0)),
            pl.BlockSpec((1, d), lambda i: (0, 0)),
            pl.BlockSpec((1, d), lambda i: (0, 0)),
            pl.BlockSpec((ROUTE_ROWS, d), lambda i: (0, 0)),
            pl.BlockSpec((ROUTE_ROWS, LANES), lambda i: (0, 0)),
            pl.BlockSpec((TM_MOE, TM_MOE), lambda i: (0, 0)),
        ],
        out_specs=[
            pl.BlockSpec((TM, d), lambda i: (i, 0)),
            tile_spec((N_EXPERTS, TM_MOE)),
            tile_spec((N_EXPERTS, TM_MOE)),
            tile_spec((ROUTE_ROWS, LANES)),
        ],
        out_shape=[
            jax.ShapeDtypeStruct((rows, d), _f32),
            jax.ShapeDtypeStruct((nt, N_EXPERTS, TM_MOE), _f32),
            jax.ShapeDtypeStruct((nt, N_EXPERTS, TM_MOE), _f32),
            jax.ShapeDtypeStruct((nt, ROUTE_ROWS, LANES), jnp.int32),
        ],
        compiler_params=pltpu.CompilerParams(dimension_semantics=("parallel",), vmem_limit_bytes=VMEM_LIMIT),
        name="out_proj_ln_route",
    )(y, w, x, g, b, wr_t, br_col, before)
    return x_new, gate_t, code_t, cnt[:, :N_EXPERTS, 0].reshape(-1)


def _moe_ln_kernel(cnt_ref, x_ref, gate_t_ref, code_t_ref, wg_ref, wu_ref, wd_ref, g_ref, b_ref, o_ref, xb_ref, acc_ref, first_ref):
    grp = pl.program_id(0)
    e = pl.program_id(1)

    @pl.when(e == 0)
    def _():
        xb_ref[...] = x_ref[...].astype(_bf16)
        acc_ref[...] = jnp.zeros_like(acc_ref)

    slot_col = lax.broadcasted_iota(jnp.int32, (MOE_BLOCK, 1), 0)
    counts = [cnt_ref[(grp * MOE_GROUP + i) * N_EXPERTS + e] for i in range(MOE_GROUP)]
    code_rows = [code_t_ref[i, pl.ds(e, 1), :] for i in range(MOE_GROUP)]
    gate_rows = [gate_t_ref[i, pl.ds(e, 1), :] for i in range(MOE_GROUP)]
    first_rows = pl.ds(pl.multiple_of(e * MOE_BLOCK, MOE_BLOCK), MOE_BLOCK)
    scatter_dims = (((0,), (0,)), ((), ()))

    def gather_rows(i, blk):
        hit = code_rows[i] == (slot_col + blk * MOE_BLOCK).astype(_f32)
        onehot = jnp.where(hit, 1.0, 0.0).astype(_bf16)
        xs = jnp.dot(onehot, xb_ref[i * TM_MOE : (i + 1) * TM_MOE, :], preferred_element_type=_f32).astype(_bf16)
        gate = jnp.sum(jnp.where(hit, gate_rows[i], 0.0), axis=1, keepdims=True)
        return onehot, xs, gate

    def first_pass(i):
        _, xs, gate = gather_rows(i, 0)
        first_ref[i, first_rows, :] = (_swiglu_partial(xs, wg_ref, wu_ref, wd_ref) * gate).astype(_bf16)

    def no_pass(i):
        first_ref[i, first_rows, :] = jnp.zeros((MOE_BLOCK, first_ref.shape[2]), _bf16)

    assert MOE_GROUP == 2
    has0 = counts[0] > 0
    has1 = counts[1] > 0

    @pl.when(jnp.logical_and(has0, has1))
    def _():
        _, xs0, gate0 = gather_rows(0, 0)
        _, xs1, gate1 = gather_rows(1, 0)
        out = _swiglu_partial(jnp.concatenate([xs0, xs1], axis=0), wg_ref, wu_ref, wd_ref)
        first_ref[0, first_rows, :] = (out[:MOE_BLOCK] * gate0).astype(_bf16)
        first_ref[1, first_rows, :] = (out[MOE_BLOCK:] * gate1).astype(_bf16)

    @pl.when(jnp.logical_and(has0, jnp.logical_not(has1)))
    def _():
        first_pass(0)
        no_pass(1)

    @pl.when(jnp.logical_and(has1, jnp.logical_not(has0)))
    def _():
        no_pass(0)
        first_pass(1)

    @pl.when(jnp.logical_and(jnp.logical_not(has0), jnp.logical_not(has1)))
    def _():
        no_pass(0)
        no_pass(1)

    for blk in range(1, -(-TM_MOE // MOE_BLOCK)):
        for i in range(MOE_GROUP):

            @pl.when(counts[i] > blk * MOE_BLOCK)
            def _():
                onehot, xs, gate = gather_rows(i, blk)
                out = (_swiglu_partial(xs, wg_ref, wu_ref, wd_ref) * gate).astype(_bf16)
                acc_ref[i * TM_MOE : (i + 1) * TM_MOE, :] += lax.dot_general(onehot, out, scatter_dims, preferred_element_type=_f32)

    @pl.when(e == pl.num_programs(1) - 1)
    def _():
        slot = lax.broadcasted_iota(jnp.int32, (1, MOE_BLOCK, 1), 1).astype(_f32)
        for i in range(MOE_GROUP):
            hit = code_t_ref[i][:, None, :] == slot
            onehot = jnp.where(hit, 1.0, 0.0).astype(_bf16).reshape(N_EXPERTS * MOE_BLOCK, TM_MOE)
            rows = slice(i * TM_MOE, (i + 1) * TM_MOE)
            moe = acc_ref[rows, :] + lax.dot_general(onehot, first_ref[i], scatter_dims, preferred_element_type=_f32)
            o_ref[rows, :] = _layer_norm(ALPHA * x_ref[rows, :] + moe, g_ref[...], b_ref[...])


def _moe_ln(x, gate_t, code_t, counts, wg, wu, wd, g, b):
    rows, d = x.shape
    ne, _, ff = wg.shape
    grid_spec = pltpu.PrefetchScalarGridSpec(
        num_scalar_prefetch=1,
        grid=(rows // TM, ne),
        in_specs=[
            pl.BlockSpec((TM, d), lambda i, e, cnt: (i, 0)),
            pl.BlockSpec((MOE_GROUP, N_EXPERTS, TM_MOE), lambda i, e, cnt: (i, 0, 0)),
            pl.BlockSpec((MOE_GROUP, N_EXPERTS, TM_MOE), lambda i, e, cnt: (i, 0, 0)),
            pl.BlockSpec((None, d, ff), lambda i, e, cnt: (e, 0, 0)),
            pl.BlockSpec((None, d, ff), lambda i, e, cnt: (e, 0, 0)),
            pl.BlockSpec((None, ff, d), lambda i, e, cnt: (e, 0, 0)),
            pl.BlockSpec((1, d), lambda i, e, cnt: (0, 0)),
            pl.BlockSpec((1, d), lambda i, e, cnt: (0, 0)),
        ],
        out_specs=pl.BlockSpec((TM, d), lambda i, e, cnt: (i, 0)),
        scratch_shapes=[
            pltpu.VMEM((TM, d), _bf16),
            pltpu.VMEM((TM, d), _f32),
            pltpu.VMEM((MOE_GROUP, ne * MOE_BLOCK, d), _bf16),
        ],
    )
    return pl.pallas_call(
        _moe_ln_kernel,
        grid_spec=grid_spec,
        out_shape=jax.ShapeDtypeStruct((rows, d), _f32),
        compiler_params=pltpu.CompilerParams(dimension_semantics=("parallel", "arbitrary"), vmem_limit_bytes=VMEM_LIMIT),
        name="moe_ln",
    )(counts, x, gate_t, code_t, wg, wu, wd, g, b)


def _chains(a, c, bb, width, off=0):
    return jnp.stack([a[j * c : (j + 1) * c, off + h * width : off + (h + 1) * width] for j in range(bb) for h in range(NH)])


def _chain_cols(a, c, bb, lane0):
    return jnp.stack([a[j * c : (j + 1) * c, lane0 + h : lane0 + h + 1] for j in range(bb) for h in range(NH)])


def _chain_rows(a_t, c, bb, lane0):
    return jnp.stack([a_t[lane0 + h : lane0 + h + 1, j * c : (j + 1) * c] for j in range(bb) for h in range(NH)])


def _head_rows(ref, row, bb):
    return jnp.stack([ref[row : row + 1, h * DH : (h + 1) * DH] for _ in range(bb) for h in range(NH)])


def _store_chains(y_ref, y, c, bb, off):
    for j in range(bb):
        for h in range(NH):
            y_ref[j * c : (j + 1) * c, off + h * DH : off + (h + 1) * DH] = y[j * NH + h].astype(y_ref.dtype)


PIECE = NH * DH


def _stream_projection(xn_ref, xc_ref, w_ref, main_sc, tail_sc, xb_sc, n_piece):
    rows = xn_ref.shape[0]
    ci = pl.program_id(1)
    tail_col = w_ref.shape[1] - LANES

    def project(xb, row0, p):
        if p < n_piece:
            cols = slice(p * PIECE, (p + 1) * PIECE)
            main_sc[pl.ds(row0, rows), cols] = jnp.dot(xb, w_ref[:, cols], preferred_element_type=_f32).astype(_bf16)
        else:
            tail_sc[pl.ds(row0, rows), :] = jnp.dot(xb, w_ref[:, tail_col:], preferred_element_type=_f32)

    @pl.when(ci == 0)
    def _():
        xb0 = xc_ref[...].astype(_bf16)
        for p in range(n_piece + 1):
            project(xb0, 0, p)

    cur = pl.multiple_of((ci % 2) * rows, rows)
    nxt = pl.multiple_of(((ci + 1) % 2) * rows, rows)
    xb_sc[...] = xn_ref[...].astype(_bf16)
    pending = iter(range(n_piece + 1))

    def between(count=1):
        for _ in range(count):
            p = next(pending, None)
            if p is not None:
                project(xb_sc[...], nxt, p)

    def load(p):
        return main_sc[pl.ds(cur, rows), p * PIECE : (p + 1) * PIECE].astype(_f32)

    return load, tail_sc[pl.ds(cur, rows), :], between


def _retention_chunk(q, k, v, s_prev, dec, vec):
    att = _bmm_nt(q, k) * dec
    o = _bmm(att, v) + vec[:, :, 0:1] * _bmm(q, s_prev)
    s_new = vec[:, 0:1, 2:3] * s_prev + _bmm_tn(k * vec[:, :, 1:2], v)
    return o, s_new


def _mlstm_chunk(q, k, v, it_col, it_row, lf_col, lf_row, c_prev, n_prev, m_prev, c):
    t, s = _tri3(c)
    b_col, b_row = _cumsum_col_row(lf_col, lf_row, c)
    logw = jnp.where(t >= s, b_col - b_row + it_row, -jnp.inf)
    m_t = jnp.maximum(b_col + m_prev, jnp.max(logw, axis=2, keepdims=True))
    w = jnp.exp(logw - m_t)
    carry = jnp.exp(b_col + m_prev - m_t)
    qk = _bmm_nt(q, k) * w
    num = _bmm(qk, v) + carry * _bmm(q, c_prev)
    den = jnp.sum(qk, axis=2, keepdims=True) + carry * jnp.sum(q * n_prev, axis=2, keepdims=True)
    h = num / jnp.maximum(jnp.abs(den), jnp.exp(-m_t))
    m_new = m_t[:, c - 1 : c, :]
    b_last = b_col[:, c - 1 : c, :]
    w_last = jnp.exp(b_last - b_col + it_col - m_new)
    decay = jnp.exp(b_last + m_prev - m_new)
    kw = k * w_last
    c_new = decay * c_prev + _bmm_tn(kw, v)
    n_new = decay * n_prev + jnp.sum(kw, axis=1, keepdims=True)
    return h, c_new, n_new, m_new


N_PIECE0 = 8


def _mixer0_body(
    load, gate_tail, between,
    cos_ref, sin_ref, dec_ref, vec_ref, gb_ref, ng_ref, s0_ref, c0_ref, n0_ref, m0_ref,
    y_ref, s_ref, c_ref, n_ref, m_ref, *, c, bb,
):
    n = bb * NH

    @pl.when(pl.program_id(1) == 0)
    def _():
        s_ref[...] = jnp.broadcast_to(s0_ref[...], s_ref.shape)
        c_ref[...] = jnp.broadcast_to(c0_ref[...], c_ref.shape)
        n_ref[...] = jnp.broadcast_to(n0_ref[...], n_ref.shape)
        m_ref[...] = jnp.broadcast_to(m0_ref[...], m_ref.shape)

    def chains(p):
        return _chains(load(p), c, bb, DH)

    cosf = cos_ref[...][None]
    sinf = sin_ref[...][None]
    gates = gate_tail + gb_ref[...]
    lane = lax.broadcasted_iota(jnp.int32, gates.shape, 1)
    gates = jnp.where(lane < NH, gates, -_softplus(-gates))
    gates_t = jnp.transpose(gates)

    q = chains(0)
    k = chains(1)
    q = q * cosf + pltpu.roll(q, DH // 2, axis=2) * sinf
    k = (k * cosf + pltpu.roll(k, DH // 2, axis=2) * sinf) * DH**-0.5
    between(2)
    dec = jnp.concatenate([dec_ref[...]] * bb, axis=0)
    vec = jnp.concatenate([vec_ref[...]] * bb, axis=0)
    o, s_new = _retention_chunk(q, k, chains(2), s_ref[...].reshape(n, DH, DH), dec, vec)
    between(2)
    s_ref[...] = s_new.reshape(s_ref.shape)
    y_a = _group_norm(o, _head_rows(ng_ref, 0, bb), rms=False) * _silu(chains(3))
    _store_chains(y_ref, y_a, c, bb, 0)
    between(1)

    n_prev = jnp.stack([n_ref[j, h : h + 1, :] for j in range(bb) for h in range(NH)])
    m_prev = jnp.stack([m_ref[j, h : h + 1, 0:1] for j in range(bb) for h in range(NH)])
    h_b, c_new, n_new, m_new = _mlstm_chunk(
        chains(4), chains(5) * DH**-0.5, chains(6),
        _chain_cols(gates, c, bb, 0), _chain_rows(gates_t, c, bb, 0),
        _chain_cols(gates, c, bb, NH), _chain_rows(gates_t, c, bb, NH),
        c_ref[...].reshape(n, DH, DH), n_prev, m_prev, c,
    )
    between(2)
    c_ref[...] = c_new.reshape(c_ref.shape)
    for j in range(bb):
        for h in range(NH):
            n_ref[j, h : h + 1, :] = n_new[j * NH + h]
            m_ref[j, h : h + 1, :] = jnp.broadcast_to(m_new[j * NH + h], (1, LANES))
    h_b = _sigmoid(chains(7)) * h_b
    between(1)
    y_b = _group_norm(h_b, _head_rows(ng_ref, 1, bb), rms=False)
    _store_chains(y_ref, y_b, c, bb, NH * DH)
    between(N_PIECE0 + 1)


def _no_op(count=1):
    del count


def _mixer0_kernel(*refs, c, bb):
    pieces, gate_ref, rest = refs[:N_PIECE0], refs[N_PIECE0], refs[N_PIECE0 + 1 :]
    consts, outs = rest[:10], rest[11:]
    _mixer0_body(lambda p: pieces[p][...].astype(_f32), gate_ref[...], _no_op, *consts, *outs, c=c, bb=bb)


def _cast_blocks(cast_in, cast_out, cast_fns):
    for src, dst, fn in zip(cast_in, cast_out, cast_fns):
        blk = src[...]
        dst[...] = (blk if fn is None else fn(blk)).astype(dst.dtype)


def _mixer0_stream_kernel(xn_ref, xc_ref, w_ref, *rest, c, bb, cast_fns=()):
    n_cast = len(cast_fns)
    consts, rest = rest[:10], rest[11:]
    cast_in, outs, cast_out, scratch = rest[:n_cast], rest[n_cast : n_cast + 5], rest[n_cast + 5 : 2 * n_cast + 5], rest[2 * n_cast + 5 :]
    _cast_blocks(cast_in, cast_out, cast_fns)
    load, gate_tail, between = _stream_projection(xn_ref, xc_ref, w_ref, *scratch, N_PIECE0)
    _mixer0_body(load, gate_tail, between, *consts, *outs, c=c, bb=bb)


def _retention_tables(c):
    lg = np.log(1.0 - 2.0 ** (-5.0 - np.arange(NH, dtype=np.float64)))[:, None, None]
    t = np.arange(c, dtype=np.float64)
    diff = t[None, :, None] - t[None, None, :]
    dec = np.where(diff >= 0, np.exp(np.maximum(diff, 0.0) * lg), 0.0)
    vec = np.zeros((NH, c, LANES))
    vec[:, :, 0] = np.exp((t[None, :] + 1.0) * lg[:, 0])
    vec[:, :, 1] = np.exp((c - 1.0 - t[None, :]) * lg[:, 0])
    vec[:, :, 2] = np.exp(c * lg[:, 0])
    return jnp.asarray(dec, _f32), jnp.asarray(vec, _f32)


def _whole(a):
    return a, pl.BlockSpec(a.shape, lambda i, ci: (0,) * a.ndim)


def _mixer_call(name, kernels, source, n_piece, consts, init, ybuf, *, row_off, y_row_off, nb, nchunk, c, bb, bcast_init, stream=False, casts=()):
    rows = bb * c
    nbb = nb // bb
    blk0 = row_off // rows
    ib = 1 if bcast_init else bb
    y_spec = pl.BlockSpec((rows, ybuf.shape[1]), lambda i, ci: (y_row_off // rows + ci * nbb + i, 0))

    def row_block(col):
        return lambda i, ci: (blk0 + ci * nbb + i, col)

    def batch_block(nd, lead):
        return lambda i, ci: ((i if lead else 0),) + (0,) * nd

    if stream:
        x, w = source
        src_args = [x, x, w]
        src_specs = [
            pl.BlockSpec((rows, D_MODEL), lambda i, ci: (blk0 + jnp.minimum(ci + 1, nchunk - 1) * nbb + i, 0)),
            pl.BlockSpec((rows, D_MODEL), lambda i, ci: (blk0 + i, 0)),
            pl.BlockSpec(w.shape, lambda i, ci: (0, 0)),
        ]
        scratch = [pltpu.VMEM((2 * rows, n_piece * PIECE), _bf16), pltpu.VMEM((2 * rows, LANES), _f32), pltpu.VMEM((rows, D_MODEL), _bf16)]
    else:
        proj, tail = source
        src_args = [proj] * n_piece + [tail]
        src_specs = [pl.BlockSpec((rows, PIECE), row_block(p)) for p in range(n_piece)] + [pl.BlockSpec((rows, LANES), row_block(0))]
        scratch = []
    in_specs = src_specs + [spec for _, spec in consts]
    in_specs += [pl.BlockSpec((ib,) + a.shape[1:], batch_block(a.ndim - 1, not bcast_init)) for a in init]
    out_specs = [y_spec] + [pl.BlockSpec((bb,) + a.shape[1:], batch_block(a.ndim - 1, True)) for a in init]
    out_shape = [jax.ShapeDtypeStruct(ybuf.shape, ybuf.dtype)] + [jax.ShapeDtypeStruct((nb,) + a.shape[1:], _f32) for a in init]
    args = src_args + [a for a, _ in consts] + list(init)
    cast_in_specs, cast_out_specs, cast_shapes = [], [], []
    for a, n_blocks, out_cols, _ in casts:
        blk_rows = a.shape[0] // n_blocks
        assert n_blocks <= nbb * nchunk and blk_rows * n_blocks == a.shape[0] and blk_rows % 16 == 0
        index = (lambda last: lambda i, ci: (jnp.minimum(i * nchunk + ci, last), 0))(n_blocks - 1)
        cast_in_specs.append(pl.BlockSpec((blk_rows, a.shape[1]), index))
        cast_out_specs.append(pl.BlockSpec((blk_rows, out_cols), index))
        cast_shapes.append(jax.ShapeDtypeStruct((a.shape[0], out_cols), _bf16))
    body = functools.partial(kernels[int(stream)], c=c, bb=bb, **({"cast_fns": tuple(fn for *_, fn in casts)} if casts else {}))
    return pl.pallas_call(
        body,
        grid=(nbb, nchunk),
        in_specs=in_specs + [pl.BlockSpec(memory_space=pl.ANY)] + cast_in_specs,
        out_specs=out_specs + cast_out_specs,
        out_shape=out_shape + cast_shapes,
        scratch_shapes=scratch,
        input_output_aliases={len(args): 0},
        compiler_params=pltpu.CompilerParams(dimension_semantics=("parallel", "arbitrary"), vmem_limit_bytes=VMEM_LIMIT),
        name=f"{name}_c{c}",
    )(*args, ybuf, *[a for a, *_ in casts])


C1_Z_END = QKV_C + NH * DH
C1_QD = C1_Z_END + 2 * NH
C1_LR = C1_QD + 2 * NH * DK_D + 2 * NH * DH


def _regroup_w_in1(blk):
    rows, cols = blk.shape
    n_small = cols - C1_LR + 2 * NH
    gaps = [jnp.zeros((rows, n), blk.dtype) for n in (P1_PAD - LANES - P1_MAIN, LANES - n_small)]
    return jnp.concatenate([blk[:, :C1_Z_END], blk[:, C1_QD:C1_LR], gaps[0], blk[:, C1_LR:], blk[:, C1_Z_END:C1_QD], gaps[1]], axis=1)


def _side_cast(a, n_blocks, out_cols=None, fn=None):
    return a, n_blocks, out_cols or a.shape[1], fn


def _mixer0(source, ybuf, cosf, sinf, gb_row, norm_g, init, **group):
    c = group["c"]
    dec, vec = _retention_tables(c)
    by_chunk = pl.BlockSpec((c, DH), lambda i, ci: (ci, 0))
    consts = [(cosf, by_chunk), (sinf, by_chunk), _whole(dec), _whole(vec), _whole(gb_row), _whole(norm_g)]
    return _mixer_call("mixer0", (_mixer0_kernel, _mixer0_stream_kernel), source, N_PIECE0, consts, init, ybuf, **group)


def _unit_lower_solve(a, rhs, c):
    bs = min(SOLVE_BLOCK, c)
    t, s = _tri3(c)
    if c > bs:
        shift = bs.bit_length() - 1
        same = jnp.right_shift(t, shift) == jnp.right_shift(s, shift)
        d = jnp.where(same, a, 0.0)
    else:
        d = a
    inv = jnp.where(t == s, 1.0, 0.0) - d
    p = d
    span = 2
    while span < bs:
        p = _bmm_x3(p, p)
        inv = inv + _bmm_x3(inv, p)
        span *= 2
    y = _bmm(inv, rhs)
    if c == bs:
        return y
    b = _bmm(inv, jnp.where(same, 0.0, a))
    y = y - _bmm(b, y)
    p = b
    span = 2
    while span < c // bs:
        p = _bmm(p, p)
        y = y + _bmm(p, y)
        span *= 2
    return y


def _gdn_chunk(q, k, v, beta_col, g_col, g_row, s_prev, c, between=_no_op):
    t, s = _tri3(c)
    gc_col, gc_row = _cumsum_col_row(g_col, g_row, c)
    dec_incl = jnp.exp(jnp.where(t >= s, gc_col - gc_row, -jnp.inf))
    dec_strict = jnp.where(t > s, dec_incl, 0.0)
    e_col = jnp.exp(gc_col)
    a = beta_col * _bmm_nt(k, k) * dec_strict
    rhs = jnp.concatenate([beta_col * v, (beta_col * e_col) * k], axis=-1)
    between(1)
    sol = _unit_lower_solve(a, rhs, c)
    between(1)
    u = sol[:, :, :DH] - _bmm(sol[:, :, DH:], s_prev)
    qk = _bmm_nt(q, k) * dec_incl
    between(1)
    o = e_col * _bmm(q, s_prev) + _bmm(qk, u)
    gl = gc_col[:, c - 1 : c, :]
    s_new = jnp.exp(gl) * s_prev + _bmm_tn(k * jnp.exp(gl - gc_col), u)
    return o, s_new


GLA_SUB = 8


def _gla_chunk(q, k, v, bc, s_prev, c):
    t, s = _tri3(c)
    sub = min(GLA_SUB, c)
    atts = []
    for lo in range(0, c, sub):
        hi = lo + sub
        ref = bc[:, lo - 1 : lo, :] if lo else jnp.zeros_like(bc[:, 0:1, :])
        att = _bmm_nt(q[:, lo:hi] * jnp.exp(bc[:, lo:hi] - ref), k[:, :hi] * jnp.exp(ref - bc[:, :hi]))
        atts.append(att if hi == c else jnp.concatenate([att, jnp.zeros((att.shape[0], sub, c - hi), _f32)], axis=2))
    att = jnp.where(t >= s, jnp.concatenate(atts, axis=1), 0.0)
    o = _bmm(att, v) + _bmm(q * jnp.exp(bc), s_prev)
    bl = bc[:, c - 1 : c, :]
    ti, si = _tri3(DK_D)
    el_col = jnp.sum(jnp.where(ti == si, jnp.exp(bl), 0.0), axis=2, keepdims=True)
    s_new = el_col * s_prev + _bmm_tn(k * jnp.exp(bl - bc), v)
    return o, s_new


N_PIECE1 = 7


def _mixer1_body(
    load, small, between,
    cw_ref, gp_ref, wa_ref, ba_ref, ng_ref, s0_ref, cv0_ref, d0_ref,
    y_ref, s_ref, cv_ref, d_ref, *, c, bb,
):
    n = bb * NH
    w4 = NH * DH

    @pl.when(pl.program_id(1) == 0)
    def _():
        s_ref[...] = jnp.broadcast_to(s0_ref[...], s_ref.shape)
        cv_ref[...] = jnp.broadcast_to(cv0_ref[...], cv_ref.shape)
        d_ref[...] = jnp.broadcast_to(d0_ref[...], d_ref.shape)

    beta_all = _sigmoid(small)
    g_all = -jnp.exp(gp_ref[0:1, :]) * _softplus(small + gp_ref[1:2, :])
    g_all_t = jnp.transpose(g_all)
    log_alpha = -_softplus(-(_dot_hi(small[:, 0:GLA_RANK], wa_ref[...]) + ba_ref[...])) * (1.0 / GLA_TAU)
    cw = cw_ref[...]

    acts = []
    qkv = jnp.concatenate([load(0), load(1), load(2)], axis=1)
    between(1)
    for j in range(bb):
        ext = jnp.concatenate([cv_ref[j], qkv[j * c : (j + 1) * c, :]], axis=0)
        conv = cw[3:4] * ext[8 : 8 + c] + cw[2:3] * ext[7 : 7 + c] + cw[1:2] * ext[6 : 6 + c] + cw[0:1] * ext[5 : 5 + c]
        cv_ref[j] = ext[c : c + 8]
        acts.append(_silu(conv))

    def act_chains(off):
        return jnp.stack([acts[j][:, off + h * DH : off + (h + 1) * DH] for j in range(bb) for h in range(NH)])

    qc = act_chains(0)
    kc = act_chains(w4)
    qc = qc * lax.rsqrt(jnp.sum(qc * qc, axis=-1, keepdims=True) + NORM_EPS) * DH**-0.5
    kc = kc * lax.rsqrt(jnp.sum(kc * kc, axis=-1, keepdims=True) + NORM_EPS)
    between(1)
    o, s_new = _gdn_chunk(
        qc, kc, act_chains(2 * w4),
        _chain_cols(beta_all, c, bb, GLA_RANK), _chain_cols(g_all, c, bb, GLA_RANK + NH),
        _chain_rows(g_all_t, c, bb, GLA_RANK + NH), s_ref[...].reshape(n, DH, DH), c, between,
    )
    s_ref[...] = s_new.reshape(s_ref.shape)
    y_c = _group_norm(o, _head_rows(ng_ref, 0, bb), rms=True) * _silu(_chains(load(3), c, bb, DH))
    _store_chains(y_ref, y_c, c, bb, 0)
    between(1)

    tt = lax.broadcasted_iota(jnp.int32, (c, c), 0)
    ss = lax.broadcasted_iota(jnp.int32, (c, c), 1)
    ones_lt = jnp.where(tt >= ss, 1.0, 0.0).astype(_bf16)
    bcs = []
    for j in range(bb):
        la = log_alpha[j * c : (j + 1) * c, :]
        hi = la.astype(_bf16)
        r1 = la - hi.astype(_f32)
        mid = r1.astype(_bf16)
        lo = (r1 - mid.astype(_f32)).astype(_bf16)
        cum = functools.partial(jnp.dot, ones_lt, preferred_element_type=_f32)
        bcs.append(cum(hi) + (cum(mid) + cum(lo)))
    bc = jnp.stack([bcs[j][:, h * DK_D : (h + 1) * DK_D] for j in range(bb) for h in range(NH)])
    qkd = load(4)
    between(1)
    o, d_new = _gla_chunk(
        _chains(qkd, c, bb, DK_D) * DK_D**-0.5, _chains(qkd, c, bb, DK_D, off=NH * DK_D),
        _chains(load(5), c, bb, DH), bc, d_ref[...].reshape(n, DK_D, DH), c,
    )
    d_ref[...] = d_new.reshape(d_ref.shape)
    between(1)
    y_d = _group_norm(o, _head_rows(ng_ref, 1, bb), rms=False) * _silu(_chains(load(6), c, bb, DH))
    _store_chains(y_ref, y_d, c, bb, w4)
    between(N_PIECE1 + 1)


def _mixer1_kernel(*refs, c, bb):
    pieces, small_ref, rest = refs[:N_PIECE1], refs[N_PIECE1], refs[N_PIECE1 + 1 :]
    consts, outs = rest[:8], rest[9:]
    _mixer1_body(lambda p: pieces[p][...].astype(_f32), small_ref[...], _no_op, *consts, *outs, c=c, bb=bb)


def _mixer1_stream_kernel(xn_ref, xc_ref, w_ref, *rest, c, bb, cast_fns=()):
    n_cast = len(cast_fns)
    consts, rest = rest[:8], rest[9:]
    cast_in, outs, cast_out, scratch = rest[:n_cast], rest[n_cast : n_cast + 4], rest[n_cast + 4 : 2 * n_cast + 4], rest[2 * n_cast + 4 :]
    _cast_blocks(cast_in, cast_out, cast_fns)
    load, small, between = _stream_projection(xn_ref, xc_ref, w_ref, *scratch, N_PIECE1)
    _mixer1_body(load, small, between, *consts, *outs, c=c, bb=bb)


def _mixer1(source, ybuf, conv_w, gdn_par, w_alpha, b_alpha, norm_g, init, **group):
    consts = [_whole(a) for a in (conv_w, gdn_par, w_alpha, b_alpha, norm_g)]
    return _mixer_call("mixer1", (_mixer1_kernel, _mixer1_stream_kernel), source, N_PIECE1, consts, init, ybuf, **group)


_GROUP_META = dict(row_off=ROW_META - ROW_SAMPLE, y_row_off=ROW_META, nb=1, nchunk=1, c=N_META, bb=1, bcast_init=True)
_GROUP_PROMPT = dict(row_off=0, y_row_off=0, nb=BATCH, nchunk=SEQ // CHUNK, c=CHUNK, bb=4, bcast_init=True, stream=True)
_GROUP_SAMPLE = dict(row_off=0, y_row_off=ROW_SAMPLE, nb=DEC_BATCH, nchunk=1, c=DEC_SEQ, bb=16, bcast_init=False)


def _rows_to_batch_major_kernel(x_ref, o_ref):
    o_ref[...] = x_ref[...].reshape(o_ref.shape)


def _prompt_rows_to_batch_major(x):
    nchunk = SEQ // CHUNK
    out = pl.pallas_call(
        _rows_to_batch_major_kernel,
        grid=(nchunk,),
        in_specs=[pl.BlockSpec((BATCH * CHUNK, D_MODEL), lambda ci: (ci, 0))],
        out_specs=pl.BlockSpec((BATCH, None, CHUNK, D_MODEL), lambda ci: (0, ci, 0, 0)),
        out_shape=jax.ShapeDtypeStruct((BATCH, nchunk, CHUNK, D_MODEL), x.dtype),
        compiler_params=pltpu.CompilerParams(dimension_semantics=("parallel",)),
        name="rows_to_batch_major",
    )(x)
    return out.reshape(BATCH, SEQ, D_MODEL)


def _rotary_tables(pos):
    half = DH // 2
    inv = ROPE_BASE ** (-jnp.arange(half, dtype=_f32) / half)
    ang = pos.astype(_f32)[:, None] * inv[None, :]
    cos, sin = jnp.cos(ang), jnp.sin(ang)
    return jnp.concatenate([cos, cos], -1), jnp.concatenate([-sin, sin], -1)


def _lanes(m):
    return jnp.broadcast_to(m.astype(_f32)[..., None], m.shape + (LANES,))


def _conv_rows(s):
    return jnp.pad(s.astype(_f32), ((0, 0), (8 - (CONV_W - 1), 0), (0, 0)))


def kernel(x_prompt, x_sample, state_ret, state_mlstm_c, state_mlstm_n, state_mlstm_m, state_gdn, state_gdn_conv, state_gla, meta_tokens, w_in0, ret_norm_g, mlstm_gate_bias, mlstm_norm_g, w_out0, ln0_mix_g, ln0_mix_b, ffn0_w_gate, ffn0_w_up, ffn0_w_down, ln0_ffn_g, ln0_ffn_b, w_in1, gdn_conv_w, gdn_a_log, gdn_dt_bias, gdn_norm_g, gla_w_alpha, gla_b_alpha, gla_norm_g, w_out1, ln1_mix_g, ln1_mix_b, moe_w_router, moe_b_router, moe_w_gate, moe_w_up, moe_w_down, ln1_ffn_g, ln1_ffn_b):
    w4 = NH * DH
    nchunk = SEQ // CHUNK
    xp = x_prompt.reshape(BATCH, nchunk, CHUNK, D_MODEL).transpose(1, 0, 2, 3).reshape(N_PROMPT, D_MODEL)
    x = jnp.concatenate(
        [
            xp,
            x_sample.reshape(N_SAMPLE, D_MODEL),
            meta_tokens.astype(x_prompt.dtype),
            jnp.zeros((R_ROWS - ROW_META - N_META, D_MODEL), x_prompt.dtype),
        ],
        0,
    )

    w_in0_p = jnp.pad(w_in0.astype(_bf16), ((0, 0), (0, P0_PAD - w_in0.shape[1])))
    gb_row = jnp.pad(mlstm_gate_bias.astype(_f32), (0, LANES - 2 * NH))[None]
    norm0 = jnp.stack([ret_norm_g, mlstm_norm_g]).astype(_f32)
    norm1 = jnp.stack([gdn_norm_g, gla_norm_g]).astype(_f32)
    lo = GLA_RANK + NH
    gdn_par = jnp.stack(
        [
            jnp.pad(gdn_a_log.astype(_f32), (lo, LANES - lo - NH)),
            jnp.pad(gdn_dt_bias.astype(_f32), (lo, LANES - lo - NH)),
        ]
    )
    w_router_t = jnp.pad(moe_w_router.astype(_f32).T, ((0, ROUTE_ROWS - N_EXPERTS), (0, 0)))
    b_router = jnp.pad(moe_b_router.astype(_f32), (0, ROUTE_ROWS - N_EXPERTS), constant_values=-jnp.inf)
    b_router_col = jnp.broadcast_to(b_router[:, None], (ROUTE_ROWS, LANES))
    moe_w_flat = [w.reshape(-1, w.shape[-1]) for w in (moe_w_gate, moe_w_up, moe_w_down)]

    def row(v):
        return v.astype(_f32)[None]

    proj0 = _proj(x[ROW_SAMPLE:], w_in0_p, tn=P0_PAD // 3)
    zeros_even = (
        jnp.zeros((1, NH, DH, DH), _f32), jnp.zeros((1, NH, DH, DH), _f32),
        jnp.zeros((1, NH, DH), _f32), jnp.zeros((1, NH, LANES), _f32),
    )
    cos_m, sin_m = _rotary_tables(jnp.arange(N_META))
    cos_p, sin_p = _rotary_tables(N_META + jnp.arange(SEQ))
    cos_s, sin_s = _rotary_tables(PAST_LEN + jnp.arange(DEC_SEQ))
    ybuf = jnp.zeros((R_ROWS, 2 * w4), _bf16)
    ybuf, *meta_even = _mixer0(proj0, ybuf, cos_m, sin_m, gb_row, norm0, zeros_even, **_GROUP_META)
    steps = _GROUP_PROMPT["nb"] // _GROUP_PROMPT["bb"] * _GROUP_PROMPT["nchunk"]
    casts0 = [
        _side_cast(ffn0_w_gate, steps), _side_cast(ffn0_w_up, steps), _side_cast(ffn0_w_down, D_FF // 64),
        _side_cast(w_out0, steps), _side_cast(w_in1, steps, P1_PAD, _regroup_w_in1),
    ]
    ybuf, p_ret, p_mc, p_mn, p_mm, wg0, wu0, wd0, wo0, w_in1_p = _mixer0(
        (x, w_in0_p), ybuf, cos_p, sin_p, gb_row, norm0, meta_even, casts=casts0, **_GROUP_PROMPT)
    init_s = (state_ret.astype(_f32), state_mlstm_c.astype(_f32), state_mlstm_n.astype(_f32), _lanes(state_mlstm_m))
    ybuf, s_ret, s_mc, s_mn, s_mm = _mixer0(proj0, ybuf, cos_s, sin_s, gb_row, norm0, init_s, **_GROUP_SAMPLE)
    x = _out_ln(ybuf, wo0, x, row(ln0_mix_g), row(ln0_mix_b))
    x = _ffn_ln(x, wg0, wu0, wd0, row(ln0_ffn_g), row(ln0_ffn_b))

    proj1 = _proj(x[ROW_SAMPLE:], w_in1_p, tn=P1_PAD // 3)
    zeros_odd = (jnp.zeros((1, NH, DH, DH), _f32), jnp.zeros((1, 8, QKV_C), _f32), jnp.zeros((1, NH, DK_D, DH), _f32))
    m1_par = (gdn_conv_w.astype(_f32), gdn_par, gla_w_alpha.astype(_f32), row(gla_b_alpha), norm1)
    ybuf, *meta_odd = _mixer1(proj1, ybuf, *m1_par, zeros_odd, **_GROUP_META)
    casts1 = [_side_cast(w, steps) for w in moe_w_flat + [w_out1]]
    ybuf, p_gdn, p_conv, p_gla, *moe_w, wo1 = _mixer1((x, w_in1_p), ybuf, *m1_par, meta_odd, casts=casts1, **_GROUP_PROMPT)
    moe_w = [w.reshape(w32.shape) for w, w32 in zip(moe_w, (moe_w_gate, moe_w_up, moe_w_down))]
    init_s = (state_gdn.astype(_f32), _conv_rows(state_gdn_conv), state_gla.astype(_f32))
    ybuf, s_gdn, s_conv, s_gla = _mixer1(proj1, ybuf, *m1_par, init_s, **_GROUP_SAMPLE)
    x, gate_t, code_t, counts = _out_ln_route(ybuf, wo1, x, row(ln1_mix_g), row(ln1_mix_b), w_router_t, b_router_col)
    x = _moe_ln(x, gate_t, code_t, counts, *moe_w, row(ln1_ffn_g), row(ln1_ffn_b))

    y_prompt = _prompt_rows_to_batch_major(x)
    y_sample = x[ROW_SAMPLE:ROW_META].reshape(DEC_BATCH, DEC_SEQ, D_MODEL)
    tail = 8 - (CONV_W - 1)
    return (
        y_prompt, y_sample,
        p_ret, p_mc, p_mn, p_mm[..., 0], p_gdn, p_conv[:, tail:], p_gla,
        s_ret, s_mc, s_mn, s_mm[..., 0], s_gdn, s_conv[:, tail:], s_gla,
    )
```

```python
import functools

import jax
import jax.numpy as jnp
import numpy as np
from jax import lax
from jax.experimental import pallas as pl
from jax.experimental.pallas import tpu as pltpu

D_MODEL = 1024
BATCH = 8
SEQ = 2048
DEC_BATCH = 128
DEC_SEQ = 4
PAST_LEN = 16384
N_META = 16
CHUNK = 64
NH = 4
DH = 128
DK_D = 64
CONV_W = 4
GLA_RANK = 16
GLA_TAU = 16.0
D_FF = 2816
N_EXPERTS = 8
MOE_FF = 1408
ROPE_BASE = 10000.0
LN_EPS = 1e-5
NORM_EPS = 1e-6
DEPTH = 2
ALPHA = (2 * DEPTH) ** 0.25
QKV_C = 3 * NH * DH

LANES = 128
N_PROMPT = BATCH * SEQ
N_SAMPLE = DEC_BATCH * DEC_SEQ
ROW_SAMPLE = N_PROMPT
ROW_META = N_PROMPT + N_SAMPLE
TM_MOE = 448
MOE_BLOCK = 128
MOE_GROUP = 2
TM = TM_MOE * MOE_GROUP
R_ROWS = ((ROW_META + N_META + TM - 1) // TM) * TM

P0_PAD = 33 * LANES
P1_MAIN = 28 * LANES
P1_PAD = 30 * LANES
SOLVE_BLOCK = 16

VMEM_LIMIT = 56 * 1024 * 1024

_bf16 = jnp.bfloat16
_f32 = jnp.float32
_HI = lax.Precision.HIGHEST


def _dot_hi(a, b):
    return jnp.dot(a, b, preferred_element_type=_f32, precision=_HI)


def _bmm(a, b):
    return jnp.einsum("nmk,nkp->nmp", a.astype(_bf16), b.astype(_bf16), preferred_element_type=_f32)


def _bmm_nt(a, b):
    return jnp.einsum("nmk,npk->nmp", a.astype(_bf16), b.astype(_bf16), preferred_element_type=_f32)


def _bmm_tn(a, b):
    return jnp.einsum("nkm,nkp->nmp", a.astype(_bf16), b.astype(_bf16), preferred_element_type=_f32)


def _split2(a):
    hi = a.astype(_bf16)
    lo = (a - hi.astype(_f32)).astype(_bf16)
    return hi, lo


def _bmm_x3(a, b):
    ah, al = _split2(a)
    bh, bl = _split2(b)
    mm = functools.partial(jnp.einsum, "nmk,nkp->nmp", preferred_element_type=_f32)
    return mm(ah, bh) + (mm(ah, bl) + mm(al, bh))


def _sigmoid(x):
    return 1.0 / (1.0 + jnp.exp(-x))


def _silu(x):
    return x * _sigmoid(x)


def _softplus(x):
    return jnp.maximum(x, 0.0) + jnp.log1p(jnp.exp(-jnp.abs(x)))


def _group_norm(o, g_row, rms):
    if not rms:
        o = o - jnp.mean(o, axis=-1, keepdims=True)
    return o * lax.rsqrt(jnp.mean(o * o, axis=-1, keepdims=True) + NORM_EPS) * g_row


def _layer_norm(x, g_row, b_row):
    mu = jnp.mean(x, axis=-1, keepdims=True)
    xc = x - mu
    var = jnp.mean(xc * xc, axis=-1, keepdims=True)
    return xc * lax.rsqrt(var + LN_EPS) * g_row + b_row


def _tri3(c):
    t = lax.broadcasted_iota(jnp.int32, (1, c, c), 1)
    s = lax.broadcasted_iota(jnp.int32, (1, c, c), 2)
    return t, s


def _cumsum_col_row(x_col, x_row, c):
    t, s = _tri3(c)
    col = jnp.sum(jnp.where(t >= s, x_row, 0.0), axis=2, keepdims=True)
    row = jnp.sum(jnp.where(t <= s, x_col, 0.0), axis=1, keepdims=True)
    return col, row


def _proj_kernel(x_ref, w_ref, o_ref, tail_ref, xb_ref):
    j = pl.program_id(1)

    @pl.when(j == 0)
    def _():
        xb_ref[...] = x_ref[...].astype(_bf16)

    acc = jnp.dot(xb_ref[...], w_ref[...], preferred_element_type=_f32)
    o_ref[...] = acc.astype(o_ref.dtype)

    @pl.when(j == pl.num_programs(1) - 1)
    def _():
        tail_ref[...] = acc[:, acc.shape[1] - LANES :]


def _proj(x, w, tn):
    rows, k = x.shape
    n = w.shape[1]
    tm = min(TM, rows)
    return pl.pallas_call(
        _proj_kernel,
        grid=(rows // tm, n // tn),
        in_specs=[pl.BlockSpec((tm, k), lambda i, j: (i, 0)), pl.BlockSpec((k, tn), lambda i, j: (0, j))],
        out_specs=[pl.BlockSpec((tm, tn), lambda i, j: (i, j)), pl.BlockSpec((tm, LANES), lambda i, j: (i, 0))],
        out_shape=[jax.ShapeDtypeStruct((rows, n), _bf16), jax.ShapeDtypeStruct((rows, LANES), _f32)],
        scratch_shapes=[pltpu.VMEM((tm, k), _bf16)],
        compiler_params=pltpu.CompilerParams(dimension_semantics=("parallel", "arbitrary"), vmem_limit_bytes=VMEM_LIMIT),
        name="in_proj",
    )(x, w)


def _out_ln_kernel(y_ref, w_ref, x_ref, g_ref, b_ref, o_ref):
    h = jnp.dot(y_ref[...], w_ref[...], preferred_element_type=_f32)
    o_ref[...] = _layer_norm(ALPHA * x_ref[...] + h, g_ref[...], b_ref[...])


def _out_ln(y, w, x, g, b):
    rows, k = y.shape
    d = w.shape[1]
    return pl.pallas_call(
        _out_ln_kernel,
        grid=(rows // TM,),
        in_specs=[
            pl.BlockSpec((TM, k), lambda i: (i, 0)),
            pl.BlockSpec((k, d), lambda i: (0, 0)),
            pl.BlockSpec((TM, d), lambda i: (i, 0)),
            pl.BlockSpec((1, d), lambda i: (0, 0)),
            pl.BlockSpec((1, d), lambda i: (0, 0)),
        ],
        out_specs=pl.BlockSpec((TM, d), lambda i: (i, 0)),
        out_shape=jax.ShapeDtypeStruct((rows, d), _f32),
        compiler_params=pltpu.CompilerParams(dimension_semantics=("parallel",), vmem_limit_bytes=VMEM_LIMIT),
        name="out_proj_ln",
    )(y, w, x, g, b)


def _swiglu_partial(xb, wg_ref, wu_ref, wd_ref):
    hg = jnp.dot(xb, wg_ref[...], preferred_element_type=_f32)
    hu = jnp.dot(xb, wu_ref[...], preferred_element_type=_f32)
    return jnp.dot((_silu(hg) * hu).astype(_bf16), wd_ref[...], preferred_element_type=_f32)


def _ffn_ln_kernel(x_ref, wg_ref, wu_ref, wd_ref, g_ref, b_ref, o_ref, xb_ref, acc_ref):
    f = pl.program_id(1)

    @pl.when(f == 0)
    def _():
        xb_ref[...] = x_ref[...].astype(_bf16)
        acc_ref[...] = jnp.zeros_like(acc_ref)

    acc_ref[...] += _swiglu_partial(xb_ref[...], wg_ref, wu_ref, wd_ref)

    @pl.when(f == pl.num_programs(1) - 1)
    def _():
        o_ref[...] = _layer_norm(ALPHA * x_ref[...] + acc_ref[...], g_ref[...], b_ref[...])


def _ffn_ln(x, wg, wu, wd, g, b):
    rows, d = x.shape
    ff = wg.shape[1]
    tf = MOE_FF
    return pl.pallas_call(
        _ffn_ln_kernel,
        grid=(rows // TM, ff // tf),
        in_specs=[
            pl.BlockSpec((TM, d), lambda i, f: (i, 0)),
            pl.BlockSpec((d, tf), lambda i, f: (0, f)),
            pl.BlockSpec((d, tf), lambda i, f: (0, f)),
            pl.BlockSpec((tf, d), lambda i, f: (f, 0)),
            pl.BlockSpec((1, d), lambda i, f: (0, 0)),
            pl.BlockSpec((1, d), lambda i, f: (0, 0)),
        ],
        out_specs=pl.BlockSpec((TM, d), lambda i, f: (i, 0)),
        out_shape=jax.ShapeDtypeStruct((rows, d), _f32),
        scratch_shapes=[pltpu.VMEM((TM, d), _bf16), pltpu.VMEM((TM, d), _f32)],
        compiler_params=pltpu.CompilerParams(dimension_semantics=("parallel", "arbitrary"), vmem_limit_bytes=VMEM_LIMIT),
        name="ffn_ln",
    )(x, wg, wu, wd, g, b)


ROUTE_ROWS = 16


def _route_tile(x, wr_t, bias_col, before):
    xh, xl = _split2(x)
    wh, wl = _split2(wr_t)
    nt = functools.partial(lax.dot_general, dimension_numbers=(((1,), (1,)), ((), ())), preferred_element_type=_f32)
    logits = nt(wh, xh) + (nt(wh, xl) + nt(wl, xh)) + bias_col
    row = lax.broadcasted_iota(jnp.int32, logits.shape, 0)
    ex = jnp.exp(logits - jnp.max(logits, axis=0, keepdims=True))
    probs = ex / jnp.sum(ex, axis=0, keepdims=True)
    p1 = jnp.max(probs, axis=0, keepdims=True)
    i1 = jnp.min(jnp.where(probs == p1, row, ROUTE_ROWS), axis=0, keepdims=True)
    rest = jnp.where(row == i1, -1.0, probs)
    p2 = jnp.max(rest, axis=0, keepdims=True)
    i2 = jnp.min(jnp.where(rest == p2, row, ROUTE_ROWS), axis=0, keepdims=True)
    tot = p1 + p2
    gate_t = jnp.where(row == i1, p1 / tot, 0.0) + jnp.where(row == i2, p2 / tot, 0.0)
    sel_t = jnp.where(row == i1, 1.0, jnp.where(row == i2, 1.0, 0.0))
    rank_t = jnp.dot(sel_t.astype(_bf16), before, preferred_element_type=_f32)
    code_t = jnp.where(sel_t > 0.0, rank_t, -1.0)
    return gate_t[:N_EXPERTS], code_t[:N_EXPERTS], jnp.sum(sel_t, axis=1, keepdims=True)


def _out_ln_route_kernel(y_ref, w_ref, x_ref, g_ref, b_ref, wr_ref, br_ref, before_ref, o_ref, gate_t_ref, code_t_ref, cnt_ref):
    h = jnp.dot(y_ref[...], w_ref[...], preferred_element_type=_f32)
    x_new = _layer_norm(ALPHA * x_ref[...] + h, g_ref[...], b_ref[...])
    o_ref[...] = x_new
    for i in range(MOE_GROUP):
        gate_t, code_t, cnt = _route_tile(x_new[i * TM_MOE : (i + 1) * TM_MOE], wr_ref[...], br_ref[:, 0:1], before_ref[...])
        gate_t_ref[i] = gate_t
        code_t_ref[i] = code_t
        cnt_ref[i] = jnp.broadcast_to(cnt, (ROUTE_ROWS, LANES)).astype(jnp.int32)


def _out_ln_route(y, w, x, g, b, wr_t, br_col):
    rows, k = y.shape
    d = w.shape[1]
    nt = rows // TM_MOE
    t = np.arange(TM_MOE)
    before = jnp.asarray(t[:, None] < t[None, :], _bf16)

    def tile_spec(shape):
        return pl.BlockSpec((MOE_GROUP,) + shape, lambda i: (i, 0, 0))

    x_new, gate_t, code_t, cnt = pl.pallas_call(
        _out_ln_route_kernel,
        grid=(rows // TM,),
        in_specs=[
            pl.BlockSpec((TM, k), lambda i: (i, 0)),
            pl.BlockSpec((k, d), lambda i: (0, 0)),
            pl.BlockSpec((TM, d), lambda i: (i, 0)),
            pl.BlockSpec((1, d), lambda i: (0, 0)),
            pl.BlockSpec((1, d), lambda i: (0, 0)),
            pl.BlockSpec((ROUTE_ROWS, d), lambda i: (0, 0)),
            pl.BlockSpec((ROUTE_ROWS, LANES), lambda i: (0, 0)),
            pl.BlockSpec((TM_MOE, TM_MOE), lambda i: (0, 0)),
        ],
        out_specs=[
            pl.BlockSpec((TM, d), lambda i: (i, 0)),
            tile_spec((N_EXPERTS, TM_MOE)),
            tile_spec((N_EXPERTS, TM_MOE)),
            tile_spec((ROUTE_ROWS, LANES)),
        ],
        out_shape=[
            jax.ShapeDtypeStruct((rows, d), _f32),
            jax.ShapeDtypeStruct((nt, N_EXPERTS, TM_MOE), _f32),
            jax.ShapeDtypeStruct((nt, N_EXPERTS, TM_MOE), _f32),
            jax.ShapeDtypeStruct((nt, ROUTE_ROWS, LANES), jnp.int32),
        ],
        compiler_params=pltpu.CompilerParams(dimension_semantics=("parallel",), vmem_limit_bytes=VMEM_LIMIT),
        name="out_proj_ln_route",
    )(y, w, x, g, b, wr_t, br_col, before)
    return x_new, gate_t, code_t, cnt[:, :N_EXPERTS, 0].reshape(-1)


def _moe_ln_kernel(cnt_ref, x_ref, gate_t_ref, code_t_ref, wg_ref, wu_ref, wd_ref, g_ref, b_ref, o_ref, xb_ref, acc_ref, first_ref):
    grp = pl.program_id(0)
    e = pl.program_id(1)

    @pl.when(e == 0)
    def _():
        xb_ref[...] = x_ref[...].astype(_bf16)
        acc_ref[...] = jnp.zeros_like(acc_ref)

    slot_col = lax.broadcasted_iota(jnp.int32, (MOE_BLOCK, 1), 0)
    counts = [cnt_ref[(grp * MOE_GROUP + i) * N_EXPERTS + e] for i in range(MOE_GROUP)]
    code_rows = [code_t_ref[i, pl.ds(e, 1), :] for i in range(MOE_GROUP)]
    gate_rows = [gate_t_ref[i, pl.ds(e, 1), :] for i in range(MOE_GROUP)]
    first_rows = pl.ds(pl.multiple_of(e * MOE_BLOCK, MOE_BLOCK), MOE_BLOCK)
    scatter_dims = (((0,), (0,)), ((), ()))

    def gather_rows(i, blk):
        hit = code_rows[i] == (slot_col + blk * MOE_BLOCK).astype(_f32)
        onehot = jnp.where(hit, 1.0, 0.0).astype(_bf16)
        xs = jnp.dot(onehot, xb_ref[i * TM_MOE : (i + 1) * TM_MOE, :], preferred_element_type=_f32).astype(_bf16)
        gate = jnp.sum(jnp.where(hit, gate_rows[i], 0.0), axis=1, keepdims=True)
        return onehot, xs, gate

    def first_pass(i):
        _, xs, gate = gather_rows(i, 0)
        first_ref[i, first_rows, :] = (_swiglu_partial(xs, wg_ref, wu_ref, wd_ref) * gate).astype(_bf16)

    def no_pass(i):
        first_ref[i, first_rows, :] = jnp.zeros((MOE_BLOCK, first_ref.shape[2]), _bf16)

    assert MOE_GROUP == 2
    has0 = counts[0] > 0
    has1 = counts[1] > 0

    @pl.when(jnp.logical_and(has0, has1))
    def _():
        _, xs0, gate0 = gather_rows(0, 0)
        _, xs1, gate1 = gather_rows(1, 0)
        out = _swiglu_partial(jnp.concatenate([xs0, xs1], axis=0), wg_ref, wu_ref, wd_ref)
        first_ref[0, first_rows, :] = (out[:MOE_BLOCK] * gate0).astype(_bf16)
        first_ref[1, first_rows, :] = (out[MOE_BLOCK:] * gate1).astype(_bf16)

    @pl.when(jnp.logical_and(has0, jnp.logical_not(has1)))
    def _():
        first_pass(0)
        no_pass(1)

    @pl.when(jnp.logical_and(has1, jnp.logical_not(has0)))
    def _():
        no_pass(0)
        first_pass(1)

    @pl.when(jnp.logical_and(jnp.logical_not(has0), jnp.logical_not(has1)))
    def _():
        no_pass(0)
        no_pass(1)

    for blk in range(1, -(-TM_MOE // MOE_BLOCK)):
        for i in range(MOE_GROUP):

            @pl.when(counts[i] > blk * MOE_BLOCK)
            def _():
                onehot, xs, gate = gather_rows(i, blk)
                out = (_swiglu_partial(xs, wg_ref, wu_ref, wd_ref) * gate).astype(_bf16)
                acc_ref[i * TM_MOE : (i + 1) * TM_MOE, :] += lax.dot_general(onehot, out, scatter_dims, preferred_element_type=_f32)

    @pl.when(e == pl.num_programs(1) - 1)
    def _():
        slot = lax.broadcasted_iota(jnp.int32, (1, MOE_BLOCK, 1), 1).astype(_f32)
        for i in range(MOE_GROUP):
            hit = code_t_ref[i][:, None, :] == slot
            onehot = jnp.where(hit, 1.0, 0.0).astype(_bf16).reshape(N_EXPERTS * MOE_BLOCK, TM_MOE)
            rows = slice(i * TM_MOE, (i + 1) * TM_MOE)
            moe = acc_ref[rows, :] + lax.dot_general(onehot, first_ref[i], scatter_dims, preferred_element_type=_f32)
            o_ref[rows, :] = _layer_norm(ALPHA * x_ref[rows, :] + moe, g_ref[...], b_ref[...])


def _moe_ln(x, gate_t, code_t, counts, wg, wu, wd, g, b):
    rows, d = x.shape
    ne, _, ff = wg.shape
    grid_spec = pltpu.PrefetchScalarGridSpec(
        num_scalar_prefetch=1,
        grid=(rows // TM, ne),
        in_specs=[
            pl.BlockSpec((TM, d), lambda i, e, cnt: (i, 0)),
            pl.BlockSpec((MOE_GROUP, N_EXPERTS, TM_MOE), lambda i, e, cnt: (i, 0, 0)),
            pl.BlockSpec((MOE_GROUP, N_EXPERTS, TM_MOE), lambda i, e, cnt: (i, 0, 0)),
            pl.BlockSpec((None, d, ff), lambda i, e, cnt: (e, 0, 0)),
            pl.BlockSpec((None, d, ff), lambda i, e, cnt: (e, 0, 0)),
            pl.BlockSpec((None, ff, d), lambda i, e, cnt: (e, 0, 0)),
            pl.BlockSpec((1, d), lambda i, e, cnt: (0, 0)),
            pl.BlockSpec((1, d), lambda i, e, cnt: (0, 0)),
        ],
        out_specs=pl.BlockSpec((TM, d), lambda i, e, cnt: (i, 0)),
        scratch_shapes=[
            pltpu.VMEM((TM, d), _bf16),
            pltpu.VMEM((TM, d), _f32),
            pltpu.VMEM((MOE_GROUP, ne * MOE_BLOCK, d), _bf16),
        ],
    )
    return pl.pallas_call(
        _moe_ln_kernel,
        grid_spec=grid_spec,
        out_shape=jax.ShapeDtypeStruct((rows, d), _f32),
        compiler_params=pltpu.CompilerParams(dimension_semantics=("parallel", "arbitrary"), vmem_limit_bytes=VMEM_LIMIT),
        name="moe_ln",
    )(counts, x, gate_t, code_t, wg, wu, wd, g, b)


def _chains(a, c, bb, width, off=0):
    return jnp.stack([a[j * c : (j + 1) * c, off + h * width : off + (h + 1) * width] for j in range(bb) for h in range(NH)])


def _chain_cols(a, c, bb, lane0):
    return jnp.stack([a[j * c : (j + 1) * c, lane0 + h : lane0 + h + 1] for j in range(bb) for h in range(NH)])


def _chain_rows(a_t, c, bb, lane0):
    return jnp.stack([a_t[lane0 + h : lane0 + h + 1, j * c : (j + 1) * c] for j in range(bb) for h in range(NH)])


def _head_rows(ref, row, bb):
    return jnp.stack([ref[row : row + 1, h * DH : (h + 1) * DH] for _ in range(bb) for h in range(NH)])


def _store_chains(y_ref, y, c, bb, off):
    for j in range(bb):
        for h in range(NH):
            y_ref[j * c : (j + 1) * c, off + h * DH : off + (h + 1) * DH] = y[j * NH + h].astype(y_ref.dtype)


PIECE = NH * DH


def _stream_projection(xn_ref, xc_ref, w_ref, main_sc, tail_sc, xb_sc, n_piece):
    rows = xn_ref.shape[0]
    ci = pl.program_id(1)
    tail_col = w_ref.shape[1] - LANES

    def project(xb, row0, p):
        if p < n_piece:
            cols = slice(p * PIECE, (p + 1) * PIECE)
            main_sc[pl.ds(row0, rows), cols] = jnp.dot(xb, w_ref[:, cols], preferred_element_type=_f32).astype(_bf16)
        else:
            tail_sc[pl.ds(row0, rows), :] = jnp.dot(xb, w_ref[:, tail_col:], preferred_element_type=_f32)

    @pl.when(ci == 0)
    def _():
        xb0 = xc_ref[...].astype(_bf16)
        for p in range(n_piece + 1):
            project(xb0, 0, p)

    cur = pl.multiple_of((ci % 2) * rows, rows)
    nxt = pl.multiple_of(((ci + 1) % 2) * rows, rows)
    xb_sc[...] = xn_ref[...].astype(_bf16)
    pending = iter(range(n_piece + 1))

    def between(count=1):
        for _ in range(count):
            p = next(pending, None)
            if p is not None:
                project(xb_sc[...], nxt, p)

    def load(p):
        return main_sc[pl.ds(cur, rows), p * PIECE : (p + 1) * PIECE].astype(_f32)

    return load, tail_sc[pl.ds(cur, rows), :], between


def _retention_chunk(q, k, v, s_prev, dec, vec):
    att = _bmm_nt(q, k) * dec
    o = _bmm(att, v) + vec[:, :, 0:1] * _bmm(q, s_prev)
    s_new = vec[:, 0:1, 2:3] * s_prev + _bmm_tn(k * vec[:, :, 1:2], v)
    return o, s_new


def _mlstm_chunk(q, k, v, it_col, it_row, lf_col, lf_row, c_prev, n_prev, m_prev, c):
    t, s = _tri3(c)
    b_col, b_row = _cumsum_col_row(lf_col, lf_row, c)
    logw = jnp.where(t >= s, b_col - b_row + it_row, -jnp.inf)
    m_t = jnp.maximum(b_col + m_prev, jnp.max(logw, axis=2, keepdims=True))
    w = jnp.exp(logw - m_t)
    carry = jnp.exp(b_col + m_prev - m_t)
    qk = _bmm_nt(q, k) * w
    num = _bmm(qk, v) + carry * _bmm(q, c_prev)
    den = jnp.sum(qk, axis=2, keepdims=True) + carry * jnp.sum(q * n_prev, axis=2, keepdims=True)
    h = num / jnp.maximum(jnp.abs(den), jnp.exp(-m_t))
    m_new = m_t[:, c - 1 : c, :]
    b_last = b_col[:, c - 1 : c, :]
    w_last = jnp.exp(b_last - b_col + it_col - m_new)
    decay = jnp.exp(b_last + m_prev - m_new)
    kw = k * w_last
    c_new = decay * c_prev + _bmm_tn(kw, v)
    n_new = decay * n_prev + jnp.sum(kw, axis=1, keepdims=True)
    return h, c_new, n_new, m_new


N_PIECE0 = 8


def _mixer0_body(
    load, gate_tail, between,
    cos_ref, sin_ref, dec_ref, vec_ref, gb_ref, ng_ref, s0_ref, c0_ref, n0_ref, m0_ref,
    y_ref, s_ref, c_ref, n_ref, m_ref, *, c, bb,
):
    n = bb * NH

    @pl.when(pl.program_id(1) == 0)
    def _():
        s_ref[...] = jnp.broadcast_to(s0_ref[...], s_ref.shape)
        c_ref[...] = jnp.broadcast_to(c0_ref[...], c_ref.shape)
        n_ref[...] = jnp.broadcast_to(n0_ref[...], n_ref.shape)
        m_ref[...] = jnp.broadcast_to(m0_ref[...], m_ref.shape)

    def chains(p):
        return _chains(load(p), c, bb, DH)

    cosf = cos_ref[...][None]
    sinf = sin_ref[...][None]
    gates = gate_tail + gb_ref[...]
    lane = lax.broadcasted_iota(jnp.int32, gates.shape, 1)
    gates = jnp.where(lane < NH, gates, -_softplus(-gates))
    gates_t = jnp.transpose(gates)

    q = chains(0)
    k = chains(1)
    q = q * cosf + pltpu.roll(q, DH // 2, axis=2) * sinf
    k = (k * cosf + pltpu.roll(k, DH // 2, axis=2) * sinf) * DH**-0.5
    between(2)
    dec = jnp.concatenate([dec_ref[...]] * bb, axis=0)
    vec = jnp.concatenate([vec_ref[...]] * bb, axis=0)
    o, s_new = _retention_chunk(q, k, chains(2), s_ref[...].reshape(n, DH, DH), dec, vec)
    between(2)
    s_ref[...] = s_new.reshape(s_ref.shape)
    y_a = _group_norm(o, _head_rows(ng_ref, 0, bb), rms=False) * _silu(chains(3))
    _store_chains(y_ref, y_a, c, bb, 0)
    between(1)

    n_prev = jnp.stack([n_ref[j, h : h + 1, :] for j in range(bb) for h in range(NH)])
    m_prev = jnp.stack([m_ref[j, h : h + 1, 0:1] for j in range(bb) for h in range(NH)])
    h_b, c_new, n_new, m_new = _mlstm_chunk(
        chains(4), chains(5) * DH**-0.5, chains(6),
        _chain_cols(gates, c, bb, 0), _chain_rows(gates_t, c, bb, 0),
        _chain_cols(gates, c, bb, NH), _chain_rows(gates_t, c, bb, NH),
        c_ref[...].reshape(n, DH, DH), n_prev, m_prev, c,
    )
    between(2)
    c_ref[...] = c_new.reshape(c_ref.shape)
    for j in range(bb):
        for h in range(NH):
            n_ref[j, h : h + 1, :] = n_new[j * NH + h]
            m_ref[j, h : h + 1, :] = jnp.broadcast_to(m_new[j * NH + h], (1, LANES))
    h_b = _sigmoid(chains(7)) * h_b
    between(1)
    y_b = _group_norm(h_b, _head_rows(ng_ref, 1, bb), rms=False)
    _store_chains(y_ref, y_b, c, bb, NH * DH)
    between(N_PIECE0 + 1)


def _no_op(count=1):
    del count


def _mixer0_kernel(*refs, c, bb):
    pieces, gate_ref, rest = refs[:N_PIECE0], refs[N_PIECE0], refs[N_PIECE0 + 1 :]
    consts, outs = rest[:10], rest[11:]
    _mixer0_body(lambda p: pieces[p][...].astype(_f32), gate_ref[...], _no_op, *consts, *outs, c=c, bb=bb)


def _cast_blocks(cast_in, cast_out, cast_fns):
    for src, dst, fn in zip(cast_in, cast_out, cast_fns):
        blk = src[...]
        dst[...] = (blk if fn is None else fn(blk)).astype(dst.dtype)


def _mixer0_stream_kernel(xn_ref, xc_ref, w_ref, *rest, c, bb, cast_fns=()):
    n_cast = len(cast_fns)
    consts, rest = rest[:10], rest[11:]
    cast_in, outs, cast_out, scratch = rest[:n_cast], rest[n_cast : n_cast + 5], rest[n_cast + 5 : 2 * n_cast + 5], rest[2 * n_cast + 5 :]
    _cast_blocks(cast_in, cast_out, cast_fns)
    load, gate_tail, between = _stream_projection(xn_ref, xc_ref, w_ref, *scratch, N_PIECE0)
    _mixer0_body(load, gate_tail, between, *consts, *outs, c=c, bb=bb)


def _retention_tables(c):
    lg = np.log(1.0 - 2.0 ** (-5.0 - np.arange(NH, dtype=np.float64)))[:, None, None]
    t = np.arange(c, dtype=np.float64)
    diff = t[None, :, None] - t[None, None, :]
    dec = np.where(diff >= 0, np.exp(np.maximum(diff, 0.0) * lg), 0.0)
    vec = np.zeros((NH, c, LANES))
    vec[:, :, 0] = np.exp((t[None, :] + 1.0) * lg[:, 0])
    vec[:, :, 1] = np.exp((c - 1.0 - t[None, :]) * lg[:, 0])
    vec[:, :, 2] = np.exp(c * lg[:, 0])
    return jnp.asarray(dec, _f32), jnp.asarray(vec, _f32)


def _whole(a):
    return a, pl.BlockSpec(a.shape, lambda i, ci: (0,) * a.ndim)


def _mixer_call(name, kernels, source, n_piece, consts, init, ybuf, *, row_off, y_row_off, nb, nchunk, c, bb, bcast_init, stream=False, casts=()):
    rows = bb * c
    nbb = nb // bb
    blk0 = row_off // rows
    ib = 1 if bcast_init else bb
    y_spec = pl.BlockSpec((rows, ybuf.shape[1]), lambda i, ci: (y_row_off // rows + ci * nbb + i, 0))

    def row_block(col):
        return lambda i, ci: (blk0 + ci * nbb + i, col)

    def batch_block(nd, lead):
        return lambda i, ci: ((i if lead else 0),) + (0,) * nd

    if stream:
        x, w = source
        src_args = [x, x, w]
        src_specs = [
            pl.BlockSpec((rows, D_MODEL), lambda i, ci: (blk0 + jnp.minimum(ci + 1, nchunk - 1) * nbb + i, 0)),
            pl.BlockSpec((rows, D_MODEL), lambda i, ci: (blk0 + i, 0)),
            pl.BlockSpec(w.shape, lambda i, ci: (0, 0)),
        ]
        scratch = [pltpu.VMEM((2 * rows, n_piece * PIECE), _bf16), pltpu.VMEM((2 * rows, LANES), _f32), pltpu.VMEM((rows, D_MODEL), _bf16)]
    else:
        proj, tail = source
        src_args = [proj] * n_piece + [tail]
        src_specs = [pl.BlockSpec((rows, PIECE), row_block(p)) for p in range(n_piece)] + [pl.BlockSpec((rows, LANES), row_block(0))]
        scratch = []
    in_specs = src_specs + [spec for _, spec in consts]
    in_specs += [pl.BlockSpec((ib,) + a.shape[1:], batch_block(a.ndim - 1, not bcast_init)) for a in init]
    out_specs = [y_spec] + [pl.BlockSpec((bb,) + a.shape[1:], batch_block(a.ndim - 1, True)) for a in init]
    out_shape = [jax.ShapeDtypeStruct(ybuf.shape, ybuf.dtype)] + [jax.ShapeDtypeStruct((nb,) + a.shape[1:], _f32) for a in init]
    args = src_args + [a for a, _ in consts] + list(init)
    cast_in_specs, cast_out_specs, cast_shapes = [], [], []
    for a, n_blocks, out_cols, _ in casts:
        blk_rows = a.shape[0] // n_blocks
        assert n_blocks <= nbb * nchunk and blk_rows * n_blocks == a.shape[0] and blk_rows % 16 == 0
        index = (lambda last: lambda i, ci: (jnp.minimum(i * nchunk + ci, last), 0))(n_blocks - 1)
        cast_in_specs.append(pl.BlockSpec((blk_rows, a.shape[1]), index))
        cast_out_specs.append(pl.BlockSpec((blk_rows, out_cols), index))
        cast_shapes.append(jax.ShapeDtypeStruct((a.shape[0], out_cols), _bf16))
    body = functools.partial(kernels[int(stream)], c=c, bb=bb, **({"cast_fns": tuple(fn for *_, fn in casts)} if casts else {}))
    return pl.pallas_call(
        body,
        grid=(nbb, nchunk),
        in_specs=in_specs + [pl.BlockSpec(memory_space=pl.ANY)] + cast_in_specs,
        out_specs=out_specs + cast_out_specs,
        out_shape=out_shape + cast_shapes,
        scratch_shapes=scratch,
        input_output_aliases={len(args): 0},
        compiler_params=pltpu.CompilerParams(dimension_semantics=("parallel", "arbitrary"), vmem_limit_bytes=VMEM_LIMIT),
        name=f"{name}_c{c}",
    )(*args, ybuf, *[a for a, *_ in casts])


C1_Z_END = QKV_C + NH * DH
C1_QD = C1_Z_END + 2 * NH
C1_LR = C1_QD + 2 * NH * DK_D + 2 * NH * DH


def _regroup_w_in1(blk):
    rows, cols = blk.shape
    n_small = cols - C1_LR + 2 * NH
    gaps = [jnp.zeros((rows, n), blk.dtype) for n in (P1_PAD - LANES - P1_MAIN, LANES - n_small)]
    return jnp.concatenate([blk[:, :C1_Z_END], blk[:, C1_QD:C1_LR], gaps[0], blk[:, C1_LR:], blk[:, C1_Z_END:C1_QD], gaps[1]], axis=1)


def _side_cast(a, n_blocks, out_cols=None, fn=None):
    return a, n_blocks, out_cols or a.shape[1], fn


def _mixer0(source, ybuf, cosf, sinf, gb_row, norm_g, init, **group):
    c = group["c"]
    dec, vec = _retention_tables(c)
    by_chunk = pl.BlockSpec((c, DH), lambda i, ci: (ci, 0))
    consts = [(cosf, by_chunk), (sinf, by_chunk), _whole(dec), _whole(vec), _whole(gb_row), _whole(norm_g)]
    return _mixer_call("mixer0", (_mixer0_kernel, _mixer0_stream_kernel), source, N_PIECE0, consts, init, ybuf, **group)


def _unit_lower_solve(a, rhs, c):
    bs = min(SOLVE_BLOCK, c)
    t, s = _tri3(c)
    if c > bs:
        shift = bs.bit_length() - 1
        same = jnp.right_shift(t, shift) == jnp.right_shift(s, shift)
        d = jnp.where(same, a, 0.0)
    else:
        d = a
    inv = jnp.where(t == s, 1.0, 0.0) - d
    p = d
    span = 2
    while span < bs:
        p = _bmm_x3(p, p)
        inv = inv + _bmm_x3(inv, p)
        span *= 2
    y = _bmm(inv, rhs)
    if c == bs:
        return y
    b = _bmm(inv, jnp.where(same, 0.0, a))
    y = y - _bmm(b, y)
    p = b
    span = 2
    while span < c // bs:
        p = _bmm(p, p)
        y = y + _bmm(p, y)
        span *= 2
    return y


def _gdn_chunk(q, k, v, beta_col, g_col, g_row, s_prev, c):
    t, s = _tri3(c)
    gc_col, gc_row = _cumsum_col_row(g_col, g_row, c)
    dec_incl = jnp.exp(jnp.where(t >= s, gc_col - gc_row, -jnp.inf))
    dec_strict = jnp.where(t > s, dec_incl, 0.0)
    e_col = jnp.exp(gc_col)
    a = beta_col * _bmm_nt(k, k) * dec_strict
    rhs = jnp.concatenate([beta_col * v, (beta_col * e_col) * k], axis=-1)
    sol = _unit_lower_solve(a, rhs, c)
    u = sol[:, :, :DH] - _bmm(sol[:, :, DH:], s_prev)
    qk = _bmm_nt(q, k) * dec_incl
    o = e_col * _bmm(q, s_prev) + _bmm(qk, u)
    gl = gc_col[:, c - 1 : c, :]
    s_new = jnp.exp(gl) * s_prev + _bmm_tn(k * jnp.exp(gl - gc_col), u)
    return o, s_new


GLA_SUB = 8


def _gla_chunk(q, k, v, bc, s_prev, c):
    t, s = _tri3(c)
    sub = min(GLA_SUB, c)
    atts = []
    for lo in range(0, c, sub):
        hi = lo + sub
        ref = bc[:, lo - 1 : lo, :] if lo else jnp.zeros_like(bc[:, 0:1, :])
        att = _bmm_nt(q[:, lo:hi] * jnp.exp(bc[:, lo:hi] - ref), k[:, :hi] * jnp.exp(ref - bc[:, :hi]))
        atts.append(att if hi == c else jnp.concatenate([att, jnp.zeros((att.shape[0], sub, c - hi), _f32)], axis=2))
    att = jnp.where(t >= s, jnp.concatenate(atts, axis=1), 0.0)
    o = _bmm(att, v) + _bmm(q * jnp.exp(bc), s_prev)
    bl = bc[:, c - 1 : c, :]
    ti, si = _tri3(DK_D)
    el_col = jnp.sum(jnp.where(ti == si, jnp.exp(bl), 0.0), axis=2, keepdims=True)
    s_new = el_col * s_prev + _bmm_tn(k * jnp.exp(bl - bc), v)
    return o, s_new


N_PIECE1 = 7


def _mixer1_body(
    load, small, between,
    cw_ref, gp_ref, wa_ref, ba_ref, ng_ref, s0_ref, cv0_ref, d0_ref,
    y_ref, s_ref, cv_ref, d_ref, *, c, bb,
):
    n = bb * NH
    w4 = NH * DH

    @pl.when(pl.program_id(1) == 0)
    def _():
        s_ref[...] = jnp.broadcast_to(s0_ref[...], s_ref.shape)
        cv_ref[...] = jnp.broadcast_to(cv0_ref[...], cv_ref.shape)
        d_ref[...] = jnp.broadcast_to(d0_ref[...], d_ref.shape)

    beta_all = _sigmoid(small)
    g_all = -jnp.exp(gp_ref[0:1, :]) * _softplus(small + gp_ref[1:2, :])
    g_all_t = jnp.transpose(g_all)
    log_alpha = -_softplus(-(_dot_hi(small[:, 0:GLA_RANK], wa_ref[...]) + ba_ref[...])) * (1.0 / GLA_TAU)
    cw = cw_ref[...]

    acts = []
    qkv = jnp.concatenate([load(0), load(1), load(2)], axis=1)
    between(N_PIECE1 + 1)
    for j in range(bb):
        ext = jnp.concatenate([cv_ref[j], qkv[j * c : (j + 1) * c, :]], axis=0)
        conv = cw[3:4] * ext[8 : 8 + c] + cw[2:3] * ext[7 : 7 + c] + cw[1:2] * ext[6 : 6 + c] + cw[0:1] * ext[5 : 5 + c]
        cv_ref[j] = ext[c : c + 8]
        acts.append(_silu(conv))

    def act_chains(off):
        return jnp.stack([acts[j][:, off + h * DH : off + (h + 1) * DH] for j in range(bb) for h in range(NH)])

    qc = act_chains(0)
    kc = act_chains(w4)
    qc = qc * lax.rsqrt(jnp.sum(qc * qc, axis=-1, keepdims=True) + NORM_EPS) * DH**-0.5
    kc = kc * lax.rsqrt(jnp.sum(kc * kc, axis=-1, keepdims=True) + NORM_EPS)
    o, s_new = _gdn_chunk(
        qc, kc, act_chains(2 * w4),
        _chain_cols(beta_all, c, bb, GLA_RANK), _chain_cols(g_all, c, bb, GLA_RANK + NH),
        _chain_rows(g_all_t, c, bb, GLA_RANK + NH), s_ref[...].reshape(n, DH, DH), c,
    )
    s_ref[...] = s_new.reshape(s_ref.shape)
    y_c = _group_norm(o, _head_rows(ng_ref, 0, bb), rms=True) * _silu(_chains(load(3), c, bb, DH))
    _store_chains(y_ref, y_c, c, bb, 0)

    tt = lax.broadcasted_iota(jnp.int32, (c, c), 0)
    ss = lax.broadcasted_iota(jnp.int32, (c, c), 1)
    ones_lt = jnp.where(tt >= ss, 1.0, 0.0).astype(_bf16)
    bcs = []
    for j in range(bb):
        la = log_alpha[j * c : (j + 1) * c, :]
        hi = la.astype(_bf16)
        r1 = la - hi.astype(_f32)
        mid = r1.astype(_bf16)
        lo = (r1 - mid.astype(_f32)).astype(_bf16)
        cum = functools.partial(jnp.dot, ones_lt, preferred_element_type=_f32)
        bcs.append(cum(hi) + (cum(mid) + cum(lo)))
    bc = jnp.stack([bcs[j][:, h * DK_D : (h + 1) * DK_D] for j in range(bb) for h in range(NH)])
    qkd = load(4)
    o, d_new = _gla_chunk(
        _chains(qkd, c, bb, DK_D) * DK_D**-0.5, _chains(qkd, c, bb, DK_D, off=NH * DK_D),
        _chains(load(5), c, bb, DH), bc, d_ref[...].reshape(n, DK_D, DH), c,
    )
    d_ref[...] = d_new.reshape(d_ref.shape)
    y_d = _group_norm(o, _head_rows(ng_ref, 1, bb), rms=False) * _silu(_chains(load(6), c, bb, DH))
    _store_chains(y_ref, y_d, c, bb, w4)


def _mixer1_kernel(*refs, c, bb):
    pieces, small_ref, rest = refs[:N_PIECE1], refs[N_PIECE1], refs[N_PIECE1 + 1 :]
    consts, outs = rest[:8], rest[9:]
    _mixer1_body(lambda p: pieces[p][...].astype(_f32), small_ref[...], _no_op, *consts, *outs, c=c, bb=bb)


def _mixer1_stream_kernel(xn_ref, xc_ref, w_ref, *rest, c, bb, cast_fns=()):
    n_cast = len(cast_fns)
    consts, rest = rest[:8], rest[9:]
    cast_in, outs, cast_out, scratch = rest[:n_cast], rest[n_cast : n_cast + 4], rest[n_cast + 4 : 2 * n_cast + 4], rest[2 * n_cast + 4 :]
    _cast_blocks(cast_in, cast_out, cast_fns)
    load, small, between = _stream_projection(xn_ref, xc_ref, w_ref, *scratch, N_PIECE1)
    _mixer1_body(load, small, between, *consts, *outs, c=c, bb=bb)


def _mixer1(source, ybuf, conv_w, gdn_par, w_alpha, b_alpha, norm_g, init, **group):
    consts = [_whole(a) for a in (conv_w, gdn_par, w_alpha, b_alpha, norm_g)]
    return _mixer_call("mixer1", (_mixer1_kernel, _mixer1_stream_kernel), source, N_PIECE1, consts, init, ybuf, **group)


_GROUP_META = dict(row_off=ROW_META - ROW_SAMPLE, y_row_off=ROW_META, nb=1, nchunk=1, c=N_META, bb=1, bcast_init=True)
_GROUP_PROMPT = dict(row_off=0, y_row_off=0, nb=BATCH, nchunk=SEQ // CHUNK, c=CHUNK, bb=4, bcast_init=True, stream=True)
_GROUP_SAMPLE = dict(row_off=0, y_row_off=ROW_SAMPLE, nb=DEC_BATCH, nchunk=1, c=DEC_SEQ, bb=16, bcast_init=False)


def _rows_to_batch_major_kernel(x_ref, o_ref):
    o_ref[...] = x_ref[...].reshape(o_ref.shape)


def _prompt_rows_to_batch_major(x):
    nchunk = SEQ // CHUNK
    out = pl.pallas_call(
        _rows_to_batch_major_kernel,
        grid=(nchunk,),
        in_specs=[pl.BlockSpec((BATCH * CHUNK, D_MODEL), lambda ci: (ci, 0))],
        out_specs=pl.BlockSpec((BATCH, None, CHUNK, D_MODEL), lambda ci: (0, ci, 0, 0)),
        out_shape=jax.ShapeDtypeStruct((BATCH, nchunk, CHUNK, D_MODEL), x.dtype),
        compiler_params=pltpu.CompilerParams(dimension_semantics=("parallel",)),
        name="rows_to_batch_major",
    )(x)
    return out.reshape(BATCH, SEQ, D_MODEL)


def _rotary_tables(pos):
    half = DH // 2
    inv = ROPE_BASE ** (-jnp.arange(half, dtype=_f32) / half)
    ang = pos.astype(_f32)[:, None] * inv[None, :]
    cos, sin = jnp.cos(ang), jnp.sin(ang)
    return jnp.concatenate([cos, cos], -1), jnp.concatenate([-sin, sin], -1)


def _lanes(m):
    return jnp.broadcast_to(m.astype(_f32)[..., None], m.shape + (LANES,))


def _conv_rows(s):
    return jnp.pad(s.astype(_f32), ((0, 0), (8 - (CONV_W - 1), 0), (0, 0)))


def kernel(x_prompt, x_sample, state_ret, state_mlstm_c, state_mlstm_n, state_mlstm_m, state_gdn, state_gdn_conv, state_gla, meta_tokens, w_in0, ret_norm_g, mlstm_gate_bias, mlstm_norm_g, w_out0, ln0_mix_g, ln0_mix_b, ffn0_w_gate, ffn0_w_up, ffn0_w_down, ln0_ffn_g, ln0_ffn_b, w_in1, gdn_conv_w, gdn_a_log, gdn_dt_bias, gdn_norm_g, gla_w_alpha, gla_b_alpha, gla_norm_g, w_out1, ln1_mix_g, ln1_mix_b, moe_w_router, moe_b_router, moe_w_gate, moe_w_up, moe_w_down, ln1_ffn_g, ln1_ffn_b):
    w4 = NH * DH
    nchunk = SEQ // CHUNK
    xp = x_prompt.reshape(BATCH, nchunk, CHUNK, D_MODEL).transpose(1, 0, 2, 3).reshape(N_PROMPT, D_MODEL)
    x = jnp.concatenate(
        [
            xp,
            x_sample.reshape(N_SAMPLE, D_MODEL),
            meta_tokens.astype(x_prompt.dtype),
            jnp.zeros((R_ROWS - ROW_META - N_META, D_MODEL), x_prompt.dtype),
        ],
        0,
    )

    w_in0_p = jnp.pad(w_in0.astype(_bf16), ((0, 0), (0, P0_PAD - w_in0.shape[1])))
    gb_row = jnp.pad(mlstm_gate_bias.astype(_f32), (0, LANES - 2 * NH))[None]
    norm0 = jnp.stack([ret_norm_g, mlstm_norm_g]).astype(_f32)
    norm1 = jnp.stack([gdn_norm_g, gla_norm_g]).astype(_f32)
    lo = GLA_RANK + NH
    gdn_par = jnp.stack(
        [
            jnp.pad(gdn_a_log.astype(_f32), (lo, LANES - lo - NH)),
            jnp.pad(gdn_dt_bias.astype(_f32), (lo, LANES - lo - NH)),
        ]
    )
    w_router_t = jnp.pad(moe_w_router.astype(_f32).T, ((0, ROUTE_ROWS - N_EXPERTS), (0, 0)))
    b_router = jnp.pad(moe_b_router.astype(_f32), (0, ROUTE_ROWS - N_EXPERTS), constant_values=-jnp.inf)
    b_router_col = jnp.broadcast_to(b_router[:, None], (ROUTE_ROWS, LANES))
    moe_w_flat = [w.reshape(-1, w.shape[-1]) for w in (moe_w_gate, moe_w_up, moe_w_down)]

    def row(v):
        return v.astype(_f32)[None]

    proj0 = _proj(x[ROW_SAMPLE:], w_in0_p, tn=P0_PAD // 3)
    zeros_even = (
        jnp.zeros((1, NH, DH, DH), _f32), jnp.zeros((1, NH, DH, DH), _f32),
        jnp.zeros((1, NH, DH), _f32), jnp.zeros((1, NH, LANES), _f32),
    )
    cos_m, sin_m = _rotary_tables(jnp.arange(N_META))
    cos_p, sin_p = _rotary_tables(N_META + jnp.arange(SEQ))
    cos_s, sin_s = _rotary_tables(PAST_LEN + jnp.arange(DEC_SEQ))
    ybuf = jnp.zeros((R_ROWS, 2 * w4), _bf16)
    ybuf, *meta_even = _mixer0(proj0, ybuf, cos_m, sin_m, gb_row, norm0, zeros_even, **_GROUP_META)
    steps = _GROUP_PROMPT["nb"] // _GROUP_PROMPT["bb"] * _GROUP_PROMPT["nchunk"]
    casts0 = [
        _side_cast(ffn0_w_gate, steps), _side_cast(ffn0_w_up, steps), _side_cast(ffn0_w_down, D_FF // 64),
        _side_cast(w_out0, steps), _side_cast(w_in1, steps, P1_PAD, _regroup_w_in1),
    ]
    ybuf, p_ret, p_mc, p_mn, p_mm, wg0, wu0, wd0, wo0, w_in1_p = _mixer0(
        (x, w_in0_p), ybuf, cos_p, sin_p, gb_row, norm0, meta_even, casts=casts0, **_GROUP_PROMPT)
    init_s = (state_ret.astype(_f32), state_mlstm_c.astype(_f32), state_mlstm_n.astype(_f32), _lanes(state_mlstm_m))
    ybuf, s_ret, s_mc, s_mn, s_mm = _mixer0(proj0, ybuf, cos_s, sin_s, gb_row, norm0, init_s, **_GROUP_SAMPLE)
    x = _out_ln(ybuf, wo0, x, row(ln0_mix_g), row(ln0_mix_b))
    x = _ffn_ln(x, wg0, wu0, wd0, row(ln0_ffn_g), row(ln0_ffn_b))

    proj1 = _proj(x[ROW_SAMPLE:], w_in1_p, tn=P1_PAD // 3)
    zeros_odd = (jnp.zeros((1, NH, DH, DH), _f32), jnp.zeros((1, 8, QKV_C), _f32), jnp.zeros((1, NH, DK_D, DH), _f32))
    m1_par = (gdn_conv_w.astype(_f32), gdn_par, gla_w_alpha.astype(_f32), row(gla_b_alpha), norm1)
    ybuf, *meta_odd = _mixer1(proj1, ybuf, *m1_par, zeros_odd, **_GROUP_META)
    casts1 = [_side_cast(w, steps) for w in moe_w_flat + [w_out1]]
    ybuf, p_gdn, p_conv, p_gla, *moe_w, wo1 = _mixer1((x, w_in1_p), ybuf, *m1_par, meta_odd, casts=casts1, **_GROUP_PROMPT)
    moe_w = [w.reshape(w32.shape) for w, w32 in zip(moe_w, (moe_w_gate, moe_w_up, moe_w_down))]
    init_s = (state_gdn.astype(_f32), _conv_rows(state_gdn_conv), state_gla.astype(_f32))
    ybuf, s_gdn, s_conv, s_gla = _mixer1(proj1, ybuf, *m1_par, init_s, **_GROUP_SAMPLE)
    x, gate_t, code_t, counts = _out_ln_route(ybuf, wo1, x, row(ln1_mix_g), row(ln1_mix_b), w_router_t, b_router_col)
    x = _moe_ln(x, gate_t, code_t, counts, *moe_w, row(ln1_ffn_g), row(ln1_ffn_b))

    y_prompt = _prompt_rows_to_batch_major(x)
    y_sample = x[ROW_SAMPLE:ROW_META].reshape(DEC_BATCH, DEC_SEQ, D_MODEL)
    tail = 8 - (CONV_W - 1)
    return (
        y_prompt, y_sample,
        p_ret, p_mc, p_mn, p_mm[..., 0], p_gdn, p_conv[:, tail:], p_gla,
        s_ret, s_mc, s_mn, s_mm[..., 0], s_gdn, s_conv[:, tail:], s_gla,
    )
```

```python
import functools

import jax
import jax.numpy as jnp
import numpy as np
from jax import lax
from jax.experimental import pallas as pl
from jax.experimental.pallas import tpu as pltpu

D_MODEL = 1024
BATCH = 8
SEQ = 2048
DEC_BATCH = 128
DEC_SEQ = 4
PAST_LEN = 16384
N_META = 16
CHUNK = 64
NH = 4
DH = 128
DK_D = 64
CONV_W = 4
GLA_RANK = 16
GLA_TAU = 16.0
D_FF = 2816
N_EXPERTS = 8
MOE_FF = 1408
ROPE_BASE = 10000.0
LN_EPS = 1e-5
NORM_EPS = 1e-6
DEPTH = 2
ALPHA = (2 * DEPTH) ** 0.25
QKV_C = 3 * NH * DH

LANES = 128
N_PROMPT = BATCH * SEQ
N_SAMPLE = DEC_BATCH * DEC_SEQ
ROW_SAMPLE = N_PROMPT
ROW_META = N_PROMPT + N_SAMPLE
TM_MOE = 448
MOE_BLOCK = 128
MOE_GROUP = 2
TM = TM_MOE * MOE_GROUP
R_ROWS = ((ROW_META + N_META + TM - 1) // TM) * TM

P0_PAD = 33 * LANES
P1_MAIN = 28 * LANES
P1_PAD = 30 * LANES
SOLVE_BLOCK = 16

VMEM_LIMIT = 56 * 1024 * 1024

_bf16 = jnp.bfloat16
_f32 = jnp.float32
_HI = lax.Precision.HIGHEST


def _dot_hi(a, b):
    return jnp.dot(a, b, preferred_element_type=_f32, precision=_HI)


def _bmm(a, b):
    return jnp.einsum("nmk,nkp->nmp", a.astype(_bf16), b.astype(_bf16), preferred_element_type=_f32)


def _bmm_nt(a, b):
    return jnp.einsum("nmk,npk->nmp", a.astype(_bf16), b.astype(_bf16), preferred_element_type=_f32)


def _bmm_tn(a, b):
    return jnp.einsum("nkm,nkp->nmp", a.astype(_bf16), b.astype(_bf16), preferred_element_type=_f32)


def _split2(a):
    hi = a.astype(_bf16)
    lo = (a - hi.astype(_f32)).astype(_bf16)
    return hi, lo


def _bmm_x3(a, b):
    ah, al = _split2(a)
    bh, bl = _split2(b)
    mm = functools.partial(jnp.einsum, "nmk,nkp->nmp", preferred_element_type=_f32)
    return mm(ah, bh) + (mm(ah, bl) + mm(al, bh))


def _sigmoid(x):
    return 1.0 / (1.0 + jnp.exp(-x))


def _silu(x):
    return x * _sigmoid(x)


def _softplus(x):
    return jnp.maximum(x, 0.0) + jnp.log1p(jnp.exp(-jnp.abs(x)))


def _group_norm(o, g_row, rms):
    if not rms:
        o = o - jnp.mean(o, axis=-1, keepdims=True)
    return o * lax.rsqrt(jnp.mean(o * o, axis=-1, keepdims=True) + NORM_EPS) * g_row


def _layer_norm(x, g_row, b_row):
    mu = jnp.mean(x, axis=-1, keepdims=True)
    xc = x - mu
    var = jnp.mean(xc * xc, axis=-1, keepdims=True)
    return xc * lax.rsqrt(var + LN_EPS) * g_row + b_row


def _tri3(c):
    t = lax.broadcasted_iota(jnp.int32, (1, c, c), 1)
    s = lax.broadcasted_iota(jnp.int32, (1, c, c), 2)
    return t, s


def _cumsum_col_row(x_col, x_row, c):
    t, s = _tri3(c)
    col = jnp.sum(jnp.where(t >= s, x_row, 0.0), axis=2, keepdims=True)
    row = jnp.sum(jnp.where(t <= s, x_col, 0.0), axis=1, keepdims=True)
    return col, row


def _proj_kernel(x_ref, w_ref, o_ref, tail_ref, xb_ref):
    j = pl.program_id(1)

    @pl.when(j == 0)
    def _():
        xb_ref[...] = x_ref[...].astype(_bf16)

    acc = jnp.dot(xb_ref[...], w_ref[...], preferred_element_type=_f32)
    o_ref[...] = acc.astype(o_ref.dtype)

    @pl.when(j == pl.num_programs(1) - 1)
    def _():
        tail_ref[...] = acc[:, acc.shape[1] - LANES :]


def _proj(x, w, tn):
    rows, k = x.shape
    n = w.shape[1]
    tm = min(TM, rows)
    return pl.pallas_call(
        _proj_kernel,
        grid=(rows // tm, n // tn),
        in_specs=[pl.BlockSpec((tm, k), lambda i, j: (i, 0)), pl.BlockSpec((k, tn), lambda i, j: (0, j))],
        out_specs=[pl.BlockSpec((tm, tn), lambda i, j: (i, j)), pl.BlockSpec((tm, LANES), lambda i, j: (i, 0))],
        out_shape=[jax.ShapeDtypeStruct((rows, n), _bf16), jax.ShapeDtypeStruct((rows, LANES), _f32)],
        scratch_shapes=[pltpu.VMEM((tm, k), _bf16)],
        compiler_params=pltpu.CompilerParams(dimension_semantics=("parallel", "arbitrary"), vmem_limit_bytes=VMEM_LIMIT),
        name="in_proj",
    )(x, w)


def _out_ln_kernel(y_ref, w_ref, x_ref, g_ref, b_ref, o_ref):
    h = jnp.dot(y_ref[...], w_ref[...], preferred_element_type=_f32)
    o_ref[...] = _layer_norm(ALPHA * x_ref[...] + h, g_ref[...], b_ref[...])


def _out_ln(y, w, x, g, b):
    rows, k = y.shape
    d = w.shape[1]
    return pl.pallas_call(
        _out_ln_kernel,
        grid=(rows // TM,),
        in_specs=[
            pl.BlockSpec((TM, k), lambda i: (i, 0)),
            pl.BlockSpec((k, d), lambda i: (0, 0)),
            pl.BlockSpec((TM, d), lambda i: (i, 0)),
            pl.BlockSpec((1, d), lambda i: (0, 0)),
            pl.BlockSpec((1, d), lambda i: (0, 0)),
        ],
        out_specs=pl.BlockSpec((TM, d), lambda i: (i, 0)),
        out_shape=jax.ShapeDtypeStruct((rows, d), _f32),
        compiler_params=pltpu.CompilerParams(dimension_semantics=("parallel",), vmem_limit_bytes=VMEM_LIMIT),
        name="out_proj_ln",
    )(y, w, x, g, b)


def _swiglu_partial(xb, wg_ref, wu_ref, wd_ref):
    hg = jnp.dot(xb, wg_ref[...], preferred_element_type=_f32)
    hu = jnp.dot(xb, wu_ref[...], preferred_element_type=_f32)
    return jnp.dot((_silu(hg) * hu).astype(_bf16), wd_ref[...], preferred_element_type=_f32)


def _ffn_ln_kernel(x_ref, wg_ref, wu_ref, wd_ref, g_ref, b_ref, o_ref, xb_ref, acc_ref):
    f = pl.program_id(1)

    @pl.when(f == 0)
    def _():
        xb_ref[...] = x_ref[...].astype(_bf16)
        acc_ref[...] = jnp.zeros_like(acc_ref)

    acc_ref[...] += _swiglu_partial(xb_ref[...], wg_ref, wu_ref, wd_ref)

    @pl.when(f == pl.num_programs(1) - 1)
    def _():
        o_ref[...] = _layer_norm(ALPHA * x_ref[...] + acc_ref[...], g_ref[...], b_ref[...])


def _ffn_ln(x, wg, wu, wd, g, b):
    rows, d = x.shape
    ff = wg.shape[1]
    tf = MOE_FF
    return pl.pallas_call(
        _ffn_ln_kernel,
        grid=(rows // TM, ff // tf),
        in_specs=[
            pl.BlockSpec((TM, d), lambda i, f: (i, 0)),
            pl.BlockSpec((d, tf), lambda i, f: (0, f)),
            pl.BlockSpec((d, tf), lambda i, f: (0, f)),
            pl.BlockSpec((tf, d), lambda i, f: (f, 0)),
            pl.BlockSpec((1, d), lambda i, f: (0, 0)),
            pl.BlockSpec((1, d), lambda i, f: (0, 0)),
        ],
        out_specs=pl.BlockSpec((TM, d), lambda i, f: (i, 0)),
        out_shape=jax.ShapeDtypeStruct((rows, d), _f32),
        scratch_shapes=[pltpu.VMEM((TM, d), _bf16), pltpu.VMEM((TM, d), _f32)],
        compiler_params=pltpu.CompilerParams(dimension_semantics=("parallel", "arbitrary"), vmem_limit_bytes=VMEM_LIMIT),
        name="ffn_ln",
    )(x, wg, wu, wd, g, b)


ROUTE_ROWS = 16


def _route_tile(x, wr_t, bias_col, before):
    xh, xl = _split2(x)
    wh, wl = _split2(wr_t)
    nt = functools.partial(lax.dot_general, dimension_numbers=(((1,), (1,)), ((), ())), preferred_element_type=_f32)
    logits = nt(wh, xh) + (nt(wh, xl) + nt(wl, xh)) + bias_col
    row = lax.broadcasted_iota(jnp.int32, logits.shape, 0)
    ex = jnp.exp(logits - jnp.max(logits, axis=0, keepdims=True))
    probs = ex / jnp.sum(ex, axis=0, keepdims=True)
    p1 = jnp.max(probs, axis=0, keepdims=True)
    i1 = jnp.min(jnp.where(probs == p1, row, ROUTE_ROWS), axis=0, keepdims=True)
    rest = jnp.where(row == i1, -1.0, probs)
    p2 = jnp.max(rest, axis=0, keepdims=True)
    i2 = jnp.min(jnp.where(rest == p2, row, ROUTE_ROWS), axis=0, keepdims=True)
    tot = p1 + p2
    gate_t = jnp.where(row == i1, p1 / tot, 0.0) + jnp.where(row == i2, p2 / tot, 0.0)
    sel_t = jnp.where(row == i1, 1.0, jnp.where(row == i2, 1.0, 0.0))
    rank_t = jnp.dot(sel_t.astype(_bf16), before, preferred_element_type=_f32)
    code_t = jnp.where(sel_t > 0.0, rank_t, -1.0)
    return gate_t[:N_EXPERTS], code_t[:N_EXPERTS], jnp.sum(sel_t, axis=1, keepdims=True)


def _out_ln_route_kernel(y_ref, w_ref, x_ref, g_ref, b_ref, wr_ref, br_ref, before_ref, o_ref, gate_t_ref, code_t_ref, cnt_ref):
    h = jnp.dot(y_ref[...], w_ref[...], preferred_element_type=_f32)
    x_new = _layer_norm(ALPHA * x_ref[...] + h, g_ref[...], b_ref[...])
    o_ref[...] = x_new
    for i in range(MOE_GROUP):
        gate_t, code_t, cnt = _route_tile(x_new[i * TM_MOE : (i + 1) * TM_MOE], wr_ref[...], br_ref[:, 0:1], before_ref[...])
        gate_t_ref[i] = gate_t
        code_t_ref[i] = code_t
        cnt_ref[i] = jnp.broadcast_to(cnt, (ROUTE_ROWS, LANES)).astype(jnp.int32)


def _out_ln_route(y, w, x, g, b, wr_t, br_col):
    rows, k = y.shape
    d = w.shape[1]
    nt = rows // TM_MOE
    t = np.arange(TM_MOE)
    before = jnp.asarray(t[:, None] < t[None, :], _bf16)

    def tile_spec(shape):
        return pl.BlockSpec((MOE_GROUP,) + shape, lambda i: (i, 0, 0))

    x_new, gate_t, code_t, cnt = pl.pallas_call(
        _out_ln_route_kernel,
        grid=(rows // TM,),
        in_specs=[
            pl.BlockSpec((TM, k), lambda i: (i, 0)),
            pl.BlockSpec((k, d), lambda i: (0, 0)),
            pl.BlockSpec((TM, d), lambda i: (i, 0)),
            pl.BlockSpec((1, d), lambda i: (0, 0)),
            pl.BlockSpec((1, d), lambda i: (0, 0)),
            pl.BlockSpec((ROUTE_ROWS, d), lambda i: (0, 0)),
            pl.BlockSpec((ROUTE_ROWS, LANES), lambda i: (0, 0)),
            pl.BlockSpec((TM_MOE, TM_MOE), lambda i: (0, 0)),
        ],
        out_specs=[
            pl.BlockSpec((TM, d), lambda i: (i, 0)),
            tile_spec((N_EXPERTS, TM_MOE)),
            tile_spec((N_EXPERTS, TM_MOE)),
            tile_spec((ROUTE_ROWS, LANES)),
        ],
        out_shape=[
            jax.ShapeDtypeStruct((rows, d), _f32),
            jax.ShapeDtypeStruct((nt, N_EXPERTS, TM_MOE), _f32),
            jax.ShapeDtypeStruct((nt, N_EXPERTS, TM_MOE), _f32),
            jax.ShapeDtypeStruct((nt, ROUTE_ROWS, LANES), jnp.int32),
        ],
        compiler_params=pltpu.CompilerParams(dimension_semantics=("parallel",), vmem_limit_bytes=VMEM_LIMIT),
        name="out_proj_ln_route",
    )(y, w, x, g, b, wr_t, br_col, before)
    return x_new, gate_t, code_t, cnt[:, :N_EXPERTS, 0].reshape(-1)


def _moe_ln_kernel(cnt_ref, x_ref, gate_t_ref, code_t_ref, wg_ref, wu_ref, wd_ref, g_ref, b_ref, o_ref, xb_ref, acc_ref, first_ref):
    grp = pl.program_id(0)
    e = pl.program_id(1)

    @pl.when(e == 0)
    def _():
        xb_ref[...] = x_ref[...].astype(_bf16)
        acc_ref[...] = jnp.zeros_like(acc_ref)

    slot_col = lax.broadcasted_iota(jnp.int32, (MOE_BLOCK, 1), 0)
    counts = [cnt_ref[(grp * MOE_GROUP + i) * N_EXPERTS + e] for i in range(MOE_GROUP)]
    code_rows = [code_t_ref[i, pl.ds(e, 1), :] for i in range(MOE_GROUP)]
    gate_rows = [gate_t_ref[i, pl.ds(e, 1), :] for i in range(MOE_GROUP)]
    first_rows = pl.ds(pl.multiple_of(e * MOE_BLOCK, MOE_BLOCK), MOE_BLOCK)
    scatter_dims = (((0,), (0,)), ((), ()))

    def gather_rows(i, blk):
        hit = code_rows[i] == (slot_col + blk * MOE_BLOCK).astype(_f32)
        onehot = jnp.where(hit, 1.0, 0.0).astype(_bf16)
        xs = jnp.dot(onehot, xb_ref[i * TM_MOE : (i + 1) * TM_MOE, :], preferred_element_type=_f32).astype(_bf16)
        gate = jnp.sum(jnp.where(hit, gate_rows[i], 0.0), axis=1, keepdims=True)
        return onehot, xs, gate

    def first_pass(i):
        _, xs, gate = gather_rows(i, 0)
        first_ref[i, first_rows, :] = (_swiglu_partial(xs, wg_ref, wu_ref, wd_ref) * gate).astype(_bf16)

    def no_pass(i):
        first_ref[i, first_rows, :] = jnp.zeros((MOE_BLOCK, first_ref.shape[2]), _bf16)

    assert MOE_GROUP == 2
    has0 = counts[0] > 0
    has1 = counts[1] > 0

    @pl.when(jnp.logical_and(has0, has1))
    def _():
        _, xs0, gate0 = gather_rows(0, 0)
        _, xs1, gate1 = gather_rows(1, 0)
        out = _swiglu_partial(jnp.concatenate([xs0, xs1], axis=0), wg_ref, wu_ref, wd_ref)
        first_ref[0, first_rows, :] = (out[:MOE_BLOCK] * gate0).astype(_bf16)
        first_ref[1, first_rows, :] = (out[MOE_BLOCK:] * gate1).astype(_bf16)

    @pl.when(jnp.logical_and(has0, jnp.logical_not(has1)))
    def _():
        first_pass(0)
        no_pass(1)

    @pl.when(jnp.logical_and(has1, jnp.logical_not(has0)))
    def _():
        no_pass(0)
        first_pass(1)

    @pl.when(jnp.logical_and(jnp.logical_not(has0), jnp.logical_not(has1)))
    def _():
        no_pass(0)
        no_pass(1)

    for blk in range(1, -(-TM_MOE // MOE_BLOCK)):
        for i in range(MOE_GROUP):

            @pl.when(counts[i] > blk * MOE_BLOCK)
            def _():
                onehot, xs, gate = gather_rows(i, blk)
                out = (_swiglu_partial(xs, wg_ref, wu_ref, wd_ref) * gate).astype(_bf16)
                acc_ref[i * TM_MOE : (i + 1) * TM_MOE, :] += lax.dot_general(onehot, out, scatter_dims, preferred_element_type=_f32)

    @pl.when(e == pl.num_programs(1) - 1)
    def _():
        slot = lax.broadcasted_iota(jnp.int32, (1, MOE_BLOCK, 1), 1).astype(_f32)
        for i in range(MOE_GROUP):
            hit = code_t_ref[i][:, None, :] == slot
            onehot = jnp.where(hit, 1.0, 0.0).astype(_bf16).reshape(N_EXPERTS * MOE_BLOCK, TM_MOE)
            rows = slice(i * TM_MOE, (i + 1) * TM_MOE)
            moe = acc_ref[rows, :] + lax.dot_general(onehot, first_ref[i], scatter_dims, preferred_element_type=_f32)
            o_ref[rows, :] = _layer_norm(ALPHA * x_ref[rows, :] + moe, g_ref[...], b_ref[...])


def _moe_ln(x, gate_t, code_t, counts, wg, wu, wd, g, b):
    rows, d = x.shape
    ne, _, ff = wg.shape
    grid_spec = pltpu.PrefetchScalarGridSpec(
        num_scalar_prefetch=1,
        grid=(rows // TM, ne),
        in_specs=[
            pl.BlockSpec((TM, d), lambda i, e, cnt: (i, 0)),
            pl.BlockSpec((MOE_GROUP, N_EXPERTS, TM_MOE), lambda i, e, cnt: (i, 0, 0)),
            pl.BlockSpec((MOE_GROUP, N_EXPERTS, TM_MOE), lambda i, e, cnt: (i, 0, 0)),
            pl.BlockSpec((None, d, ff), lambda i, e, cnt: (e, 0, 0)),
            pl.BlockSpec((None, d, ff), lambda i, e, cnt: (e, 0, 0)),
            pl.BlockSpec((None, ff, d), lambda i, e, cnt: (e, 0, 0)),
            pl.BlockSpec((1, d), lambda i, e, cnt: (0, 0)),
            pl.BlockSpec((1, d), lambda i, e, cnt: (0, 0)),
        ],
        out_specs=pl.BlockSpec((TM, d), lambda i, e, cnt: (i, 0)),
        scratch_shapes=[
            pltpu.VMEM((TM, d), _bf16),
            pltpu.VMEM((TM, d), _f32),
            pltpu.VMEM((MOE_GROUP, ne * MOE_BLOCK, d), _bf16),
        ],
    )
    return pl.pallas_call(
        _moe_ln_kernel,
        grid_spec=grid_spec,
        out_shape=jax.ShapeDtypeStruct((rows, d), _f32),
        compiler_params=pltpu.CompilerParams(dimension_semantics=("parallel", "arbitrary"), vmem_limit_bytes=VMEM_LIMIT),
        name="moe_ln",
    )(counts, x, gate_t, code_t, wg, wu, wd, g, b)


def _chains(a, c, bb, width, off=0):
    return jnp.stack([a[j * c : (j + 1) * c, off + h * width : off + (h + 1) * width] for j in range(bb) for h in range(NH)])


def _chain_cols(a, c, bb, lane0):
    return jnp.stack([a[j * c : (j + 1) * c, lane0 + h : lane0 + h + 1] for j in range(bb) for h in range(NH)])


def _chain_rows(a_t, c, bb, lane0):
    return jnp.stack([a_t[lane0 + h : lane0 + h + 1, j * c : (j + 1) * c] for j in range(bb) for h in range(NH)])


def _head_rows(ref, row, bb):
    return jnp.stack([ref[row : row + 1, h * DH : (h + 1) * DH] for _ in range(bb) for h in range(NH)])


def _store_chains(y_ref, y, c, bb, off):
    for j in range(bb):
        for h in range(NH):
            y_ref[j * c : (j + 1) * c, off + h * DH : off + (h + 1) * DH] = y[j * NH + h].astype(y_ref.dtype)


PIECE = NH * DH


def _stream_projection(xn_ref, xc_ref, w_ref, main_sc, tail_sc, xb_sc, n_piece):
    rows = xn_ref.shape[0]
    ci = pl.program_id(1)
    tail_col = w_ref.shape[1] - LANES

    def project(xb, row0, p):
        if p < n_piece:
            cols = slice(p * PIECE, (p + 1) * PIECE)
            main_sc[pl.ds(row0, rows), cols] = jnp.dot(xb, w_ref[:, cols], preferred_element_type=_f32).astype(_bf16)
        else:
            tail_sc[pl.ds(row0, rows), :] = jnp.dot(xb, w_ref[:, tail_col:], preferred_element_type=_f32)

    @pl.when(ci == 0)
    def _():
        xb0 = xc_ref[...].astype(_bf16)
        for p in range(n_piece + 1):
            project(xb0, 0, p)

    cur = pl.multiple_of((ci % 2) * rows, rows)
    nxt = pl.multiple_of(((ci + 1) % 2) * rows, rows)
    xb_sc[...] = xn_ref[...].astype(_bf16)
    pending = iter(range(n_piece + 1))

    def between(count=1):
        for _ in range(count):
            p = next(pending, None)
            if p is not None:
                project(xb_sc[...], nxt, p)

    def load(p):
        return main_sc[pl.ds(cur, rows), p * PIECE : (p + 1) * PIECE].astype(_f32)

    return load, tail_sc[pl.ds(cur, rows), :], between


def _retention_chunk(q, k, v, s_prev, dec, vec):
    att = _bmm_nt(q, k) * dec
    o = _bmm(att, v) + vec[:, :, 0:1] * _bmm(q, s_prev)
    s_new = vec[:, 0:1, 2:3] * s_prev + _bmm_tn(k * vec[:, :, 1:2], v)
    return o, s_new


def _mlstm_chunk(q, k, v, it_col, it_row, lf_col, lf_row, c_prev, n_prev, m_prev, c):
    t, s = _tri3(c)
    b_col, b_row = _cumsum_col_row(lf_col, lf_row, c)
    logw = jnp.where(t >= s, b_col - b_row + it_row, -jnp.inf)
    m_t = jnp.maximum(b_col + m_prev, jnp.max(logw, axis=2, keepdims=True))
    w = jnp.exp(logw - m_t)
    carry = jnp.exp(b_col + m_prev - m_t)
    qk = _bmm_nt(q, k) * w
    num = _bmm(qk, v) + carry * _bmm(q, c_prev)
    den = jnp.sum(qk, axis=2, keepdims=True) + carry * jnp.sum(q * n_prev, axis=2, keepdims=True)
    h = num / jnp.maximum(jnp.abs(den), jnp.exp(-m_t))
    m_new = m_t[:, c - 1 : c, :]
    b_last = b_col[:, c - 1 : c, :]
    w_last = jnp.exp(b_last - b_col + it_col - m_new)
    decay = jnp.exp(b_last + m_prev - m_new)
    kw = k * w_last
    c_new = decay * c_prev + _bmm_tn(kw, v)
    n_new = decay * n_prev + jnp.sum(kw, axis=1, keepdims=True)
    return h, c_new, n_new, m_new


N_PIECE0 = 8


def _mixer0_body(
    load, gate_tail, between,
    cos_ref, sin_ref, dec_ref, vec_ref, gb_ref, ng_ref, s0_ref, c0_ref, n0_ref, m0_ref,
    y_ref, s_ref, c_ref, n_ref, m_ref, *, c, bb,
):
    n = bb * NH

    @pl.when(pl.program_id(1) == 0)
    def _():
        s_ref[...] = jnp.broadcast_to(s0_ref[...], s_ref.shape)
        c_ref[...] = jnp.broadcast_to(c0_ref[...], c_ref.shape)
        n_ref[...] = jnp.broadcast_to(n0_ref[...], n_ref.shape)
        m_ref[...] = jnp.broadcast_to(m0_ref[...], m_ref.shape)

    def chains(p):
        return _chains(load(p), c, bb, DH)

    cosf = cos_ref[...][None]
    sinf = sin_ref[...][None]
    gates = gate_tail + gb_ref[...]
    lane = lax.broadcasted_iota(jnp.int32, gates.shape, 1)
    gates = jnp.where(lane < NH, gates, -_softplus(-gates))
    gates_t = jnp.transpose(gates)

    q = chains(0)
    k = chains(1)
    q = q * cosf + pltpu.roll(q, DH // 2, axis=2) * sinf
    k = (k * cosf + pltpu.roll(k, DH // 2, axis=2) * sinf) * DH**-0.5
    between(1)
    dec = jnp.concatenate([dec_ref[...]] * bb, axis=0)
    vec = jnp.concatenate([vec_ref[...]] * bb, axis=0)
    o, s_new = _retention_chunk(q, k, chains(2), s_ref[...].reshape(n, DH, DH), dec, vec)
    between(2)
    s_ref[...] = s_new.reshape(s_ref.shape)
    y_a = _group_norm(o, _head_rows(ng_ref, 0, bb), rms=False) * _silu(chains(3))
    _store_chains(y_ref, y_a, c, bb, 0)
    between(1)

    n_prev = jnp.stack([n_ref[j, h : h + 1, :] for j in range(bb) for h in range(NH)])
    m_prev = jnp.stack([m_ref[j, h : h + 1, 0:1] for j in range(bb) for h in range(NH)])
    h_b, c_new, n_new, m_new = _mlstm_chunk(
        chains(4), chains(5) * DH**-0.5, chains(6),
        _chain_cols(gates, c, bb, 0), _chain_rows(gates_t, c, bb, 0),
        _chain_cols(gates, c, bb, NH), _chain_rows(gates_t, c, bb, NH),
        c_ref[...].reshape(n, DH, DH), n_prev, m_prev, c,
    )
    between(2)
    c_ref[...] = c_new.reshape(c_ref.shape)
    for j in range(bb):
        for h in range(NH):
            n_ref[j, h : h + 1, :] = n_new[j * NH + h]
            m_ref[j, h : h + 1, :] = jnp.broadcast_to(m_new[j * NH + h], (1, LANES))
    h_b = _sigmoid(chains(7)) * h_b
    between(1)
    y_b = _group_norm(h_b, _head_rows(ng_ref, 1, bb), rms=False)
    _store_chains(y_ref, y_b, c, bb, NH * DH)
    between(N_PIECE0 + 1)


def _no_op(count=1):
    del count


def _mixer0_kernel(*refs, c, bb):
    pieces, gate_ref, rest = refs[:N_PIECE0], refs[N_PIECE0], refs[N_PIECE0 + 1 :]
    consts, outs = rest[:10], rest[11:]
    _mixer0_body(lambda p: pieces[p][...].astype(_f32), gate_ref[...], _no_op, *consts, *outs, c=c, bb=bb)


def _cast_blocks(cast_in, cast_out, cast_fns):
    for src, dst, fn in zip(cast_in, cast_out, cast_fns):
        blk = src[...]
        dst[...] = (blk if fn is None else fn(blk)).astype(dst.dtype)


def _mixer0_stream_kernel(xn_ref, xc_ref, w_ref, *rest, c, bb, cast_fns=()):
    n_cast = len(cast_fns)
    consts, rest = rest[:10], rest[11:]
    cast_in, outs, cast_out, scratch = rest[:n_cast], rest[n_cast : n_cast + 5], rest[n_cast + 5 : 2 * n_cast + 5], rest[2 * n_cast + 5 :]
    _cast_blocks(cast_in, cast_out, cast_fns)
    load, gate_tail, between = _stream_projection(xn_ref, xc_ref, w_ref, *scratch, N_PIECE0)
    _mixer0_body(load, gate_tail, between, *consts, *outs, c=c, bb=bb)


def _retention_tables(c):
    lg = np.log(1.0 - 2.0 ** (-5.0 - np.arange(NH, dtype=np.float64)))[:, None, None]
    t = np.arange(c, dtype=np.float64)
    diff = t[None, :, None] - t[None, None, :]
    dec = np.where(diff >= 0, np.exp(np.maximum(diff, 0.0) * lg), 0.0)
    vec = np.zeros((NH, c, LANES))
    vec[:, :, 0] = np.exp((t[None, :] + 1.0) * lg[:, 0])
    vec[:, :, 1] = np.exp((c - 1.0 - t[None, :]) * lg[:, 0])
    vec[:, :, 2] = np.exp(c * lg[:, 0])
    return jnp.asarray(dec, _f32), jnp.asarray(vec, _f32)


def _whole(a):
    return a, pl.BlockSpec(a.shape, lambda i, ci: (0,) * a.ndim)


def _mixer_call(name, kernels, source, n_piece, consts, init, ybuf, *, row_off, y_row_off, nb, nchunk, c, bb, bcast_init, stream=False, casts=()):
    rows = bb * c
    nbb = nb // bb
    blk0 = row_off // rows
    ib = 1 if bcast_init else bb
    y_spec = pl.BlockSpec((rows, ybuf.shape[1]), lambda i, ci: (y_row_off // rows + ci * nbb + i, 0))

    def row_block(col):
        return lambda i, ci: (blk0 + ci * nbb + i, col)

    def batch_block(nd, lead):
        return lambda i, ci: ((i if lead else 0),) + (0,) * nd

    if stream:
        x, w = source
        src_args = [x, x, w]
        src_specs = [
            pl.BlockSpec((rows, D_MODEL), lambda i, ci: (blk0 + jnp.minimum(ci + 1, nchunk - 1) * nbb + i, 0)),
            pl.BlockSpec((rows, D_MODEL), lambda i, ci: (blk0 + i, 0)),
            pl.BlockSpec(w.shape, lambda i, ci: (0, 0)),
        ]
        scratch = [pltpu.VMEM((2 * rows, n_piece * PIECE), _bf16), pltpu.VMEM((2 * rows, LANES), _f32), pltpu.VMEM((rows, D_MODEL), _bf16)]
    else:
        proj, tail = source
        src_args = [proj] * n_piece + [tail]
        src_specs = [pl.BlockSpec((rows, PIECE), row_block(p)) for p in range(n_piece)] + [pl.BlockSpec((rows, LANES), row_block(0))]
        scratch = []
    in_specs = src_specs + [spec for _, spec in consts]
    in_specs += [pl.BlockSpec((ib,) + a.shape[1:], batch_block(a.ndim - 1, not bcast_init)) for a in init]
    out_specs = [y_spec] + [pl.BlockSpec((bb,) + a.shape[1:], batch_block(a.ndim - 1, True)) for a in init]
    out_shape = [jax.ShapeDtypeStruct(ybuf.shape, ybuf.dtype)] + [jax.ShapeDtypeStruct((nb,) + a.shape[1:], _f32) for a in init]
    args = src_args + [a for a, _ in consts] + list(init)
    cast_in_specs, cast_out_specs, cast_shapes = [], [], []
    for a, n_blocks, out_cols, _ in casts:
        blk_rows = a.shape[0] // n_blocks
        assert n_blocks <= nbb * nchunk and blk_rows * n_blocks == a.shape[0] and blk_rows % 16 == 0
        index = (lambda last: lambda i, ci: (jnp.minimum(i * nchunk + ci, last), 0))(n_blocks - 1)
        cast_in_specs.append(pl.BlockSpec((blk_rows, a.shape[1]), index))
        cast_out_specs.append(pl.BlockSpec((blk_rows, out_cols), index))
        cast_shapes.append(jax.ShapeDtypeStruct((a.shape[0], out_cols), _bf16))
    body = functools.partial(kernels[int(stream)], c=c, bb=bb, **({"cast_fns": tuple(fn for *_, fn in casts)} if casts else {}))
    return pl.pallas_call(
        body,
        grid=(nbb, nchunk),
        in_specs=in_specs + [pl.BlockSpec(memory_space=pl.ANY)] + cast_in_specs,
        out_specs=out_specs + cast_out_specs,
        out_shape=out_shape + cast_shapes,
        scratch_shapes=scratch,
        input_output_aliases={len(args): 0},
        compiler_params=pltpu.CompilerParams(dimension_semantics=("parallel", "arbitrary"), vmem_limit_bytes=VMEM_LIMIT),
        name=f"{name}_c{c}",
    )(*args, ybuf, *[a for a, *_ in casts])


C1_Z_END = QKV_C + NH * DH
C1_QD = C1_Z_END + 2 * NH
C1_LR = C1_QD + 2 * NH * DK_D + 2 * NH * DH


def _regroup_w_in1(blk):
    rows, cols = blk.shape
    n_small = cols - C1_LR + 2 * NH
    gaps = [jnp.zeros((rows, n), blk.dtype) for n in (P1_PAD - LANES - P1_MAIN, LANES - n_small)]
    return jnp.concatenate([blk[:, :C1_Z_END], blk[:, C1_QD:C1_LR], gaps[0], blk[:, C1_LR:], blk[:, C1_Z_END:C1_QD], gaps[1]], axis=1)


def _side_cast(a, n_blocks, out_cols=None, fn=None):
    return a, n_blocks, out_cols or a.shape[1], fn


def _mixer0(source, ybuf, cosf, sinf, gb_row, norm_g, init, **group):
    c = group["c"]
    dec, vec = _retention_tables(c)
    by_chunk = pl.BlockSpec((c, DH), lambda i, ci: (ci, 0))
    consts = [(cosf, by_chunk), (sinf, by_chunk), _whole(dec), _whole(vec), _whole(gb_row), _whole(norm_g)]
    return _mixer_call("mixer0", (_mixer0_kernel, _mixer0_stream_kernel), source, N_PIECE0, consts, init, ybuf, **group)


def _unit_lower_solve(a, rhs, c):
    bs = min(SOLVE_BLOCK, c)
    t, s = _tri3(c)
    if c > bs:
        shift = bs.bit_length() - 1
        same = jnp.right_shift(t, shift) == jnp.right_shift(s, shift)
        d = jnp.where(same, a, 0.0)
    else:
        d = a
    inv = jnp.where(t == s, 1.0, 0.0) - d
    p = d
    span = 2
    while span < bs:
        p = _bmm_x3(p, p)
        inv = inv + _bmm_x3(inv, p)
        span *= 2
    y = _bmm(inv, rhs)
    if c == bs:
        return y
    b = _bmm(inv, jnp.where(same, 0.0, a))
    y = y - _bmm(b, y)
    p = b
    span = 2
    while span < c // bs:
        p = _bmm(p, p)
        y = y + _bmm(p, y)
        span *= 2
    return y


def _gdn_chunk(q, k, v, beta_col, g_col, g_row, s_prev, c):
    t, s = _tri3(c)
    gc_col, gc_row = _cumsum_col_row(g_col, g_row, c)
    dec_incl = jnp.exp(jnp.where(t >= s, gc_col - gc_row, -jnp.inf))
    dec_strict = jnp.where(t > s, dec_incl, 0.0)
    e_col = jnp.exp(gc_col)
    a = beta_col * _bmm_nt(k, k) * dec_strict
    rhs = jnp.concatenate([beta_col * v, (beta_col * e_col) * k], axis=-1)
    sol = _unit_lower_solve(a, rhs, c)
    u = sol[:, :, :DH] - _bmm(sol[:, :, DH:], s_prev)
    qk = _bmm_nt(q, k) * dec_incl
    o = e_col * _bmm(q, s_prev) + _bmm(qk, u)
    gl = gc_col[:, c - 1 : c, :]
    s_new = jnp.exp(gl) * s_prev + _bmm_tn(k * jnp.exp(gl - gc_col), u)
    return o, s_new


GLA_SUB = 8


def _gla_chunk(q, k, v, bc, s_prev, c):
    t, s = _tri3(c)
    sub = min(GLA_SUB, c)
    atts = []
    for lo in range(0, c, sub):
        hi = lo + sub
        ref = bc[:, lo - 1 : lo, :] if lo else jnp.zeros_like(bc[:, 0:1, :])
        att = _bmm_nt(q[:, lo:hi] * jnp.exp(bc[:, lo:hi] - ref), k[:, :hi] * jnp.exp(ref - bc[:, :hi]))
        atts.append(att if hi == c else jnp.concatenate([att, jnp.zeros((att.shape[0], sub, c - hi), _f32)], axis=2))
    att = jnp.where(t >= s, jnp.concatenate(atts, axis=1), 0.0)
    o = _bmm(att, v) + _bmm(q * jnp.exp(bc), s_prev)
    bl = bc[:, c - 1 : c, :]
    ti, si = _tri3(DK_D)
    el_col = jnp.sum(jnp.where(ti == si, jnp.exp(bl), 0.0), axis=2, keepdims=True)
    s_new = el_col * s_prev + _bmm_tn(k * jnp.exp(bl - bc), v)
    return o, s_new


N_PIECE1 = 7


def _mixer1_body(
    load, small, between,
    cw_ref, gp_ref, wa_ref, ba_ref, ng_ref, s0_ref, cv0_ref, d0_ref,
    y_ref, s_ref, cv_ref, d_ref, *, c, bb,
):
    n = bb * NH
    w4 = NH * DH

    @pl.when(pl.program_id(1) == 0)
    def _():
        s_ref[...] = jnp.broadcast_to(s0_ref[...], s_ref.shape)
        cv_ref[...] = jnp.broadcast_to(cv0_ref[...], cv_ref.shape)
        d_ref[...] = jnp.broadcast_to(d0_ref[...], d_ref.shape)

    beta_all = _sigmoid(small)
    g_all = -jnp.exp(gp_ref[0:1, :]) * _softplus(small + gp_ref[1:2, :])
    g_all_t = jnp.transpose(g_all)
    log_alpha = -_softplus(-(_dot_hi(small[:, 0:GLA_RANK], wa_ref[...]) + ba_ref[...])) * (1.0 / GLA_TAU)
    cw = cw_ref[...]

    acts = []
    qkv = jnp.concatenate([load(0), load(1), load(2)], axis=1)
    between(N_PIECE1 + 1)
    for j in range(bb):
        ext = jnp.concatenate([cv_ref[j], qkv[j * c : (j + 1) * c, :]], axis=0)
        conv = cw[3:4] * ext[8 : 8 + c] + cw[2:3] * ext[7 : 7 + c] + cw[1:2] * ext[6 : 6 + c] + cw[0:1] * ext[5 : 5 + c]
        cv_ref[j] = ext[c : c + 8]
        acts.append(_silu(conv))

    def act_chains(off):
        return jnp.stack([acts[j][:, off + h * DH : off + (h + 1) * DH] for j in range(bb) for h in range(NH)])

    qc = act_chains(0)
    kc = act_chains(w4)
    qc = qc * lax.rsqrt(jnp.sum(qc * qc, axis=-1, keepdims=True) + NORM_EPS) * DH**-0.5
    kc = kc * lax.rsqrt(jnp.sum(kc * kc, axis=-1, keepdims=True) + NORM_EPS)
    o, s_new = _gdn_chunk(
        qc, kc, act_chains(2 * w4),
        _chain_cols(beta_all, c, bb, GLA_RANK), _chain_cols(g_all, c, bb, GLA_RANK + NH),
        _chain_rows(g_all_t, c, bb, GLA_RANK + NH), s_ref[...].reshape(n, DH, DH), c,
    )
    s_ref[...] = s_new.reshape(s_ref.shape)
    y_c = _group_norm(o, _head_rows(ng_ref, 0, bb), rms=True) * _silu(_chains(load(3), c, bb, DH))
    _store_chains(y_ref, y_c, c, bb, 0)

    tt = lax.broadcasted_iota(jnp.int32, (c, c), 0)
    ss = lax.broadcasted_iota(jnp.int32, (c, c), 1)
    ones_lt = jnp.where(tt >= ss, 1.0, 0.0).astype(_bf16)
    bcs = []
    for j in range(bb):
        la = log_alpha[j * c : (j + 1) * c, :]
        hi = la.astype(_bf16)
        r1 = la - hi.astype(_f32)
        mid = r1.astype(_bf16)
        lo = (r1 - mid.astype(_f32)).astype(_bf16)
        cum = functools.partial(jnp.dot, ones_lt, preferred_element_type=_f32)
        bcs.append(cum(hi) + (cum(mid) + cum(lo)))
    bc = jnp.stack([bcs[j][:, h * DK_D : (h + 1) * DK_D] for j in range(bb) for h in range(NH)])
    qkd = load(4)
    o, d_new = _gla_chunk(
        _chains(qkd, c, bb, DK_D) * DK_D**-0.5, _chains(qkd, c, bb, DK_D, off=NH * DK_D),
        _chains(load(5), c, bb, DH), bc, d_ref[...].reshape(n, DK_D, DH), c,
    )
    d_ref[...] = d_new.reshape(d_ref.shape)
    y_d = _group_norm(o, _head_rows(ng_ref, 1, bb), rms=False) * _silu(_chains(load(6), c, bb, DH))
    _store_chains(y_ref, y_d, c, bb, w4)


def _mixer1_kernel(*refs, c, bb):
    pieces, small_ref, rest = refs[:N_PIECE1], refs[N_PIECE1], refs[N_PIECE1 + 1 :]
    consts, outs = rest[:8], rest[9:]
    _mixer1_body(lambda p: pieces[p][...].astype(_f32), small_ref[...], _no_op, *consts, *outs, c=c, bb=bb)


def _mixer1_stream_kernel(xn_ref, xc_ref, w_ref, *rest, c, bb, cast_fns=()):
    n_cast = len(cast_fns)
    consts, rest = rest[:8], rest[9:]
    cast_in, outs, cast_out, scratch = rest[:n_cast], rest[n_cast : n_cast + 4], rest[n_cast + 4 : 2 * n_cast + 4], rest[2 * n_cast + 4 :]
    _cast_blocks(cast_in, cast_out, cast_fns)
    load, small, between = _stream_projection(xn_ref, xc_ref, w_ref, *scratch, N_PIECE1)
    _mixer1_body(load, small, between, *consts, *outs, c=c, bb=bb)


def _mixer1(source, ybuf, conv_w, gdn_par, w_alpha, b_alpha, norm_g, init, **group):
    consts = [_whole(a) for a in (conv_w, gdn_par, w_alpha, b_alpha, norm_g)]
    return _mixer_call("mixer1", (_mixer1_kernel, _mixer1_stream_kernel), source, N_PIECE1, consts, init, ybuf, **group)


_GROUP_META = dict(row_off=ROW_META - ROW_SAMPLE, y_row_off=ROW_META, nb=1, nchunk=1, c=N_META, bb=1, bcast_init=True)
_GROUP_PROMPT = dict(row_off=0, y_row_off=0, nb=BATCH, nchunk=SEQ // CHUNK, c=CHUNK, bb=4, bcast_init=True, stream=True)
_GROUP_SAMPLE = dict(row_off=0, y_row_off=ROW_SAMPLE, nb=DEC_BATCH, nchunk=1, c=DEC_SEQ, bb=16, bcast_init=False)


def _rows_to_batch_major_kernel(x_ref, o_ref):
    o_ref[...] = x_ref[...].reshape(o_ref.shape)


def _prompt_rows_to_batch_major(x):
    nchunk = SEQ // CHUNK
    out = pl.pallas_call(
        _rows_to_batch_major_kernel,
        grid=(nchunk,),
        in_specs=[pl.BlockSpec((BATCH * CHUNK, D_MODEL), lambda ci: (ci, 0))],
        out_specs=pl.BlockSpec((BATCH, None, CHUNK, D_MODEL), lambda ci: (0, ci, 0, 0)),
        out_shape=jax.ShapeDtypeStruct((BATCH, nchunk, CHUNK, D_MODEL), x.dtype),
        compiler_params=pltpu.CompilerParams(dimension_semantics=("parallel",)),
        name="rows_to_batch_major",
    )(x)
    return out.reshape(BATCH, SEQ, D_MODEL)


def _rotary_tables(pos):
    half = DH // 2
    inv = ROPE_BASE ** (-jnp.arange(half, dtype=_f32) / half)
    ang = pos.astype(_f32)[:, None] * inv[None, :]
    cos, sin = jnp.cos(ang), jnp.sin(ang)
    return jnp.concatenate([cos, cos], -1), jnp.concatenate([-sin, sin], -1)


def _lanes(m):
    return jnp.broadcast_to(m.astype(_f32)[..., None], m.shape + (LANES,))


def _conv_rows(s):
    return jnp.pad(s.astype(_f32), ((0, 0), (8 - (CONV_W - 1), 0), (0, 0)))


def kernel(x_prompt, x_sample, state_ret, state_mlstm_c, state_mlstm_n, state_mlstm_m, state_gdn, state_gdn_conv, state_gla, meta_tokens, w_in0, ret_norm_g, mlstm_gate_bias, mlstm_norm_g, w_out0, ln0_mix_g, ln0_mix_b, ffn0_w_gate, ffn0_w_up, ffn0_w_down, ln0_ffn_g, ln0_ffn_b, w_in1, gdn_conv_w, gdn_a_log, gdn_dt_bias, gdn_norm_g, gla_w_alpha, gla_b_alpha, gla_norm_g, w_out1, ln1_mix_g, ln1_mix_b, moe_w_router, moe_b_router, moe_w_gate, moe_w_up, moe_w_down, ln1_ffn_g, ln1_ffn_b):
    w4 = NH * DH
    nchunk = SEQ // CHUNK
    xp = x_prompt.reshape(BATCH, nchunk, CHUNK, D_MODEL).transpose(1, 0, 2, 3).reshape(N_PROMPT, D_MODEL)
    x = jnp.concatenate(
        [
            xp,
            x_sample.reshape(N_SAMPLE, D_MODEL),
            meta_tokens.astype(x_prompt.dtype),
            jnp.zeros((R_ROWS - ROW_META - N_META, D_MODEL), x_prompt.dtype),
        ],
        0,
    )

    w_in0_p = jnp.pad(w_in0.astype(_bf16), ((0, 0), (0, P0_PAD - w_in0.shape[1])))
    gb_row = jnp.pad(mlstm_gate_bias.astype(_f32), (0, LANES - 2 * NH))[None]
    norm0 = jnp.stack([ret_norm_g, mlstm_norm_g]).astype(_f32)
    norm1 = jnp.stack([gdn_norm_g, gla_norm_g]).astype(_f32)
    lo = GLA_RANK + NH
    gdn_par = jnp.stack(
        [
            jnp.pad(gdn_a_log.astype(_f32), (lo, LANES - lo - NH)),
            jnp.pad(gdn_dt_bias.astype(_f32), (lo, LANES - lo - NH)),
        ]
    )
    w_router_t = jnp.pad(moe_w_router.astype(_f32).T, ((0, ROUTE_ROWS - N_EXPERTS), (0, 0)))
    b_router = jnp.pad(moe_b_router.astype(_f32), (0, ROUTE_ROWS - N_EXPERTS), constant_values=-jnp.inf)
    b_router_col = jnp.broadcast_to(b_router[:, None], (ROUTE_ROWS, LANES))
    moe_w_flat = [w.reshape(-1, w.shape[-1]) for w in (moe_w_gate, moe_w_up, moe_w_down)]

    def row(v):
        return v.astype(_f32)[None]

    proj0 = _proj(x[ROW_SAMPLE:], w_in0_p, tn=P0_PAD // 3)
    zeros_even = (
        jnp.zeros((1, NH, DH, DH), _f32), jnp.zeros((1, NH, DH, DH), _f32),
        jnp.zeros((1, NH, DH), _f32), jnp.zeros((1, NH, LANES), _f32),
    )
    cos_m, sin_m = _rotary_tables(jnp.arange(N_META))
    cos_p, sin_p = _rotary_tables(N_META + jnp.arange(SEQ))
    cos_s, sin_s = _rotary_tables(PAST_LEN + jnp.arange(DEC_SEQ))
    ybuf = jnp.zeros((R_ROWS, 2 * w4), _bf16)
    ybuf, *meta_even = _mixer0(proj0, ybuf, cos_m, sin_m, gb_row, norm0, zeros_even, **_GROUP_META)
    steps = _GROUP_PROMPT["nb"] // _GROUP_PROMPT["bb"] * _GROUP_PROMPT["nchunk"]
    casts0 = [
        _side_cast(ffn0_w_gate, steps), _side_cast(ffn0_w_up, steps), _side_cast(ffn0_w_down, D_FF // 64),
        _side_cast(w_out0, steps), _side_cast(w_in1, steps, P1_PAD, _regroup_w_in1),
    ]
    ybuf, p_ret, p_mc, p_mn, p_mm, wg0, wu0, wd0, wo0, w_in1_p = _mixer0(
        (x, w_in0_p), ybuf, cos_p, sin_p, gb_row, norm0, meta_even, casts=casts0, **_GROUP_PROMPT)
    init_s = (state_ret.astype(_f32), state_mlstm_c.astype(_f32), state_mlstm_n.astype(_f32), _lanes(state_mlstm_m))
    ybuf, s_ret, s_mc, s_mn, s_mm = _mixer0(proj0, ybuf, cos_s, sin_s, gb_row, norm0, init_s, **_GROUP_SAMPLE)
    x = _out_ln(ybuf, wo0, x, row(ln0_mix_g), row(ln0_mix_b))
    x = _ffn_ln(x, wg0, wu0, wd0, row(ln0_ffn_g), row(ln0_ffn_b))

    proj1 = _proj(x[ROW_SAMPLE:], w_in1_p, tn=P1_PAD // 3)
    zeros_odd = (jnp.zeros((1, NH, DH, DH), _f32), jnp.zeros((1, 8, QKV_C), _f32), jnp.zeros((1, NH, DK_D, DH), _f32))
    m1_par = (gdn_conv_w.astype(_f32), gdn_par, gla_w_alpha.astype(_f32), row(gla_b_alpha), norm1)
    ybuf, *meta_odd = _mixer1(proj1, ybuf, *m1_par, zeros_odd, **_GROUP_META)
    casts1 = [_side_cast(w, steps) for w in moe_w_flat + [w_out1]]
    ybuf, p_gdn, p_conv, p_gla, *moe_w, wo1 = _mixer1((x, w_in1_p), ybuf, *m1_par, meta_odd, casts=casts1, **_GROUP_PROMPT)
    moe_w = [w.reshape(w32.shape) for w, w32 in zip(moe_w, (moe_w_gate, moe_w_up, moe_w_down))]
    init_s = (state_gdn.astype(_f32), _conv_rows(state_gdn_conv), state_gla.astype(_f32))
    ybuf, s_gdn, s_conv, s_gla = _mixer1(proj1, ybuf, *m1_par, init_s, **_GROUP_SAMPLE)
    x, gate_t, code_t, counts = _out_ln_route(ybuf, wo1, x, row(ln1_mix_g), row(ln1_mix_b), w_router_t, b_router_col)
    x = _moe_ln(x, gate_t, code_t, counts, *moe_w, row(ln1_ffn_g), row(ln1_ffn_b))

    y_prompt = _prompt_rows_to_batch_major(x)
    y_sample = x[ROW_SAMPLE:ROW_META].reshape(DEC_BATCH, DEC_SEQ, D_MODEL)
    tail = 8 - (CONV_W - 1)
    return (
        y_prompt, y_sample,
        p_ret, p_mc, p_mn, p_mm[..., 0], p_gdn, p_conv[:, tail:], p_gla,
        s_ret, s_mc, s_mn, s_mm[..., 0], s_gdn, s_conv[:, tail:], s_gla,
    )
```
